```python
import math
import jax
import jax.numpy as jnp
from jax import lax
import numpy as np

D_MODEL = 1024
BATCH = 8
SEQ = 2048
DEPTH = 4

ROPE_THETA = 500000.0
EPS = 1e-6
Q_BLOCK = 128
D_FF = 4 * D_MODEL
POS_OFFSET_MAX = 4096
NEG = -1e30
FORCED = 1e9
N_EVEN = (DEPTH + 1) // 2
N_ODD = DEPTH // 2

MLA_HEADS = 8
MLA_NOPE = 64
MLA_ROPE = 32
MLA_V = 64
MLA_Q_RANK = 256
MLA_KV_RANK = 256

NSA_HEADS = 8
NSA_GROUPS = 2
NSA_HPG = NSA_HEADS // NSA_GROUPS
NSA_DH = 64
NSA_ROT = NSA_DH // 4
NSA_CMP_LEN = 32
NSA_CMP_STRIDE = 16
NSA_CMP_HIDDEN = 128
NSA_SLC_LEN = 64
NSA_TOP_N = 16
NSA_WINDOW = 512
NSA_SLC_Q_BLOCK = 64

DIFF_HEADS = 8
DIFF_DH = 64
DIFF_ROT = DIFF_DH // 4

MLA_IN = MLA_Q_RANK + MLA_KV_RANK + MLA_ROPE
NSA_Q_W = NSA_HEADS * NSA_DH
NSA_KV_W = 3 * 2 * NSA_GROUPS * NSA_DH
NSA_GATE_W = 3 * NSA_HEADS
HY_IN = MLA_IN + NSA_Q_W + NSA_KV_W + NSA_GATE_W
HY_OUT = MLA_HEADS * MLA_V + NSA_HEADS * NSA_DH
HY_SPLITS = [MLA_Q_RANK, MLA_Q_RANK + MLA_KV_RANK, MLA_IN, MLA_IN + NSA_Q_W, MLA_IN + NSA_Q_W + NSA_KV_W]
DIFF_W = DIFF_HEADS * 2 * DIFF_DH

kernel_name = "hybrid_mla_nsa_diffattn_trunk"


def _rms(x, gain):
    xf = x.astype(jnp.float32)
    y = xf * lax.rsqrt(jnp.mean(xf * xf, axis=-1, keepdims=True) + EPS)
    return (y * gain.astype(jnp.float32)).astype(x.dtype)


def _rope(x, pos, rot):
    half = rot // 2
    inv_freq = ROPE_THETA ** (-jnp.arange(half, dtype=jnp.float32) / half)
    ang = pos.astype(jnp.float32)[..., None] * inv_freq
    cos, sin = jnp.cos(ang), jnp.sin(ang)
    xr = x[..., :rot].astype(jnp.float32)
    x1, x2 = xr[..., :half], xr[..., half:]
    xr = jnp.concatenate([x1 * cos - x2 * sin, x1 * sin + x2 * cos], axis=-1)
    return jnp.concatenate([xr.astype(x.dtype), x[..., rot:]], axis=-1)


def _causal_attention(q, k, v, map_w, scale):
    B, H, M, S, dk = q.shape
    dv = v.shape[-1]
    nb = S // Q_BLOCK
    qb = jnp.moveaxis(q.reshape(B, H, M, nb, Q_BLOCK, dk), 3, 0)
    kpos = jnp.arange(S)
    wf = map_w.astype(jnp.float32)

    def one(args):
        qblk, s0 = args
        s = jnp.einsum('bhmqd,bhmkd->bhmqk', qblk, k, preferred_element_type=jnp.float32) * scale
        mask = kpos[None, :] <= (s0 + jnp.arange(Q_BLOCK))[:, None]
        p = jax.nn.softmax(jnp.where(mask, s, NEG), axis=-1)
        a = jnp.einsum('m,bhmqk->bhqk', wf, p)
        return jnp.einsum('bhqk,bhkd->bhqd', a.astype(v.dtype), v)

    o = lax.map(one, (qb, jnp.arange(nb) * Q_BLOCK))
    return jnp.moveaxis(o, 0, 2).reshape(B, H, S, dv)


def _mla_mixer(cq_raw, ckv_raw, kpe_raw, pos, q_norm, w_uq, kv_norm, w_ukv, q_gain, k_gain):
    B, S, _ = cq_raw.shape
    H = MLA_HEADS
    q = (_rms(cq_raw, q_norm) @ w_uq).reshape(B, S, H, MLA_ROPE + MLA_NOPE).transpose(0, 2, 1, 3)
    kv = (_rms(ckv_raw, kv_norm) @ w_ukv).reshape(B, S, H, MLA_NOPE + MLA_V).transpose(0, 2, 1, 3)
    k_nope, v = kv[..., :MLA_NOPE], kv[..., MLA_NOPE:]
    k_pe = jnp.broadcast_to(kpe_raw[:, None], (B, H, S, MLA_ROPE))
    k = jnp.concatenate([k_pe, k_nope], axis=-1)
    p = pos[:, None, :]
    q = _rope(_rms(q, q_gain), p, MLA_ROPE)
    k = _rope(_rms(k, k_gain), p, MLA_ROPE)
    o = _causal_attention(q[:, :, None], k[:, :, None], v, jnp.ones((1,), jnp.float32),
                          (MLA_ROPE + MLA_NOPE) ** -0.5)
    return o.transpose(0, 2, 1, 3).reshape(B, S, H * MLA_V)


def _nsa_mixer(q_raw, kv_raw, gate_raw, pos, q_gain, k_gain, cmp_pos, cmp_w1, cmp_w2):
    B, S, _ = q_raw.shape
    G, HG, DH = NSA_GROUPS, NSA_HPG, NSA_DH
    scale = DH ** -0.5
    tpos = jnp.arange(S)
    q = q_raw.reshape(B, S, G, HG, DH).transpose(0, 2, 3, 1, 4)
    q = _rope(_rms(q, q_gain), pos[:, None, None, :], NSA_ROT)
    kv = kv_raw.reshape(B, S, 3, 2, G, DH).transpose(2, 3, 0, 4, 1, 5)

    nc = (S - NSA_CMP_LEN) // NSA_CMP_STRIDE + 1
    c_start = jnp.arange(nc) * NSA_CMP_STRIDE
    c_end = c_start + NSA_CMP_LEN - 1
    blk_idx = c_start[:, None] + jnp.arange(NSA_CMP_LEN)[None, :]

    def compress(tok, j):
        blocks = (tok[:, :, blk_idx] + cmp_pos[j]).reshape(B, G, nc, NSA_CMP_LEN * DH)
        return jax.nn.gelu(blocks @ cmp_w1[j]) @ cmp_w2[j]

    kc = _rope(_rms(compress(kv[0, 0], 0), k_gain[0]), pos[:, c_end][:, None, :], NSA_ROT)
    vc = compress(kv[0, 1], 1)
    s = jnp.einsum('bghqd,bgcd->bghqc', q, kc, preferred_element_type=jnp.float32) * scale
    cmask = c_end[None, :] <= tpos[:, None]
    p_cmp = jax.nn.softmax(jnp.where(cmask, s, NEG), axis=-1) * cmask
    o_cmp = jnp.einsum('bghqc,bgcd->bghqd', p_cmp.astype(vc.dtype), vc)

    ns = S // NSA_SLC_LEN
    n_sel = min(NSA_TOP_N, ns)
    j_start = jnp.arange(ns) * NSA_SLC_LEN
    cover = ((c_start[:, None] <= j_start[None, :] + NSA_SLC_LEN - 1)
             & (c_end[:, None] >= j_start[None, :])).astype(jnp.float32)
    imp = jnp.einsum('bghqc,cj->bgqj', p_cmp, cover)
    jt = tpos // NSA_SLC_LEN
    jj = jnp.arange(ns)
    allowed = jj[None, :] <= jt[:, None]
    forced = allowed & ((jj[None, :] == 0) | (jj[None, :] >= jt[:, None] - 1))
    imp = jnp.where(forced, FORCED, jnp.where(allowed, imp, NEG))
    top_val, top_idx = lax.top_k(imp, n_sel)
    top_ok = top_val > 0.5 * NEG
    ks = _rope(_rms(kv[1, 0], k_gain[1]), pos[:, None, :], NSA_ROT)
    k_blk = ks.reshape(B, G, ns, NSA_SLC_LEN, DH)
    v_blk = kv[1, 1].reshape(B, G, ns, NSA_SLC_LEN, DH)
    bi = jnp.arange(B)[:, None, None, None]
    gi = jnp.arange(G)[None, :, None, None]
    qb_n = NSA_SLC_Q_BLOCK

    def sel_block(args):
        qb, ib, okb, s0 = args
        kg = k_blk[bi, gi, ib]
        vg = v_blk[bi, gi, ib]
        sc = jnp.einsum('bghqd,bgqnld->bghqnl', qb, kg, preferred_element_type=jnp.float32) * scale
        qpos = s0 + jnp.arange(qb_n)
        kpos = ib[..., None] * NSA_SLC_LEN + jnp.arange(NSA_SLC_LEN)
        m = ((kpos <= qpos[:, None, None]) & okb[..., None])[:, :, None]
        sc = jnp.where(m, sc, NEG).reshape(B, G, HG, qb_n, n_sel * NSA_SLC_LEN)
        pr = jax.nn.softmax(sc, axis=-1).reshape(B, G, HG, qb_n, n_sel, NSA_SLC_LEN) * m
        return jnp.einsum('bghqnl,bgqnld->bghqd', pr.astype(vg.dtype), vg)

    nqb = S // qb_n
    o_slc = lax.map(sel_block, (
        jnp.moveaxis(q.reshape(B, G, HG, nqb, qb_n, DH), 3, 0),
        jnp.moveaxis(top_idx.reshape(B, G, nqb, qb_n, n_sel), 2, 0),
        jnp.moveaxis(top_ok.reshape(B, G, nqb, qb_n, n_sel), 2, 0),
        jnp.arange(nqb) * qb_n))
    o_slc = jnp.moveaxis(o_slc, 0, 3).reshape(B, G, HG, S, DH)

    kw = _rope(_rms(kv[2, 0], k_gain[2]), pos[:, None, :], NSA_ROT)
    pad = ((0, 0), (0, 0), (NSA_WINDOW, 0), (0, 0))
    kp, vp = jnp.pad(kw, pad), jnp.pad(kv[2, 1], pad)
    span = NSA_WINDOW + Q_BLOCK

    def win_block(args):
        qb, s0 = args
        kb = lax.dynamic_slice_in_dim(kp, s0, span, axis=2)
        vb = lax.dynamic_slice_in_dim(vp, s0, span, axis=2)
        sc = jnp.einsum('bghqd,bgkd->bghqk', qb, kb, preferred_element_type=jnp.float32) * scale
        qpos = (s0 + jnp.arange(Q_BLOCK))[:, None]
        kpos = (s0 - NSA_WINDOW + jnp.arange(span))[None, :]
        m = (kpos <= qpos) & (kpos > qpos - NSA_WINDOW) & (kpos >= 0)
        pr = jax.nn.softmax(jnp.where(m, sc, NEG), axis=-1)
        return jnp.einsum('bghqk,bgkd->bghqd', pr.astype(vb.dtype), vb)

    nb = S // Q_BLOCK
    o_win = lax.map(win_block, (jnp.moveaxis(q.reshape(B, G, HG, nb, Q_BLOCK, DH), 3, 0),
                                jnp.arange(nb) * Q_BLOCK))
    o_win = jnp.moveaxis(o_win, 0, 3).reshape(B, G, HG, S, DH)

    g = jax.nn.sigmoid(gate_raw).reshape(B, S, 3, G, HG).transpose(2, 0, 3, 4, 1)[..., None]
    o = g[0] * o_cmp.astype(g.dtype) + g[1] * o_slc.astype(g.dtype) + g[2] * o_win.astype(g.dtype)
    return o.transpose(0, 3, 1, 2, 4).reshape(B, S, NSA_HEADS * DH)


def _diff_mixer(h, w_qkv, w_out, q_gain, k_gain, lam, sub_gain, pos, lam_init):
    B, S, _ = h.shape
    H, d = DIFF_HEADS, DIFF_DH
    q, k, v = jnp.split(h @ w_qkv, 3, axis=-1)
    q = q.reshape(B, S, H, 2, d).transpose(0, 2, 3, 1, 4)
    k = k.reshape(B, S, H, 2, d).transpose(0, 2, 3, 1, 4)
    v = v.reshape(B, S, H, 2 * d).transpose(0, 2, 1, 3)
    p = pos[:, None, None, :]
    q = _rope(_rms(q, q_gain), p, DIFF_ROT)
    k = _rope(_rms(k, k_gain), p, DIFF_ROT)
    lf = lam.astype(jnp.float32)
    lmb = jnp.exp(jnp.sum(lf[0] * lf[1])) - jnp.exp(jnp.sum(lf[2] * lf[3])) + lam_init
    o = _causal_attention(q, k, v, jnp.stack([jnp.ones_like(lmb), -lmb]), d ** -0.5)
    o = _rms(o, sub_gain) * (1.0 - lam_init)
    return o.transpose(0, 2, 1, 3).reshape(B, S, DIFF_W) @ w_out


def setup_inputs(seed: int = 0) -> dict:
    key = jax.random.key(seed)
    ks = jax.random.split(key, 28)
    f32 = jnp.float32
    D = D_MODEL

    def nrm(k, shape, scale):
        return jax.random.normal(k, shape, f32) * scale

    def gain(k, shape):
        return 1.0 + 0.02 * jax.random.normal(k, shape, f32)

    positions = (jax.random.randint(ks[2], (BATCH, 1), 0, POS_OFFSET_MAX, dtype=jnp.int32)
                 + jnp.arange(SEQ, dtype=jnp.int32)[None, :])
    return {
        "x": nrm(ks[0], (BATCH, SEQ, D), 1.0),
        "c": nrm(ks[1], (BATCH, D), 1.0),
        "positions": positions,
        "ada_w": nrm(ks[3], (DEPTH, D, 6 * D), 0.5 * D ** -0.5),
        "ada_b": nrm(ks[4], (DEPTH, 6 * D), 0.02),
        "norm_mix": gain(ks[5], (DEPTH, D)),
        "norm_mlp": gain(ks[6], (DEPTH, D)),
        "mlp_w1": nrm(ks[7], (DEPTH, D, D_FF), D ** -0.5),
        "mlp_w2": nrm(ks[8], (DEPTH, D_FF, D), D_FF ** -0.5),
        "hy_w_in": nrm(ks[9], (N_EVEN, D, HY_IN), D ** -0.5),
        "hy_w_out": nrm(ks[10], (N_EVEN, HY_OUT, D), HY_OUT ** -0.5),
        "mla_q_norm": gain(ks[11], (N_EVEN, MLA_Q_RANK)),
        "mla_w_uq": nrm(ks[12], (N_EVEN, MLA_Q_RANK, MLA_HEADS * (MLA_ROPE + MLA_NOPE)), MLA_Q_RANK ** -0.5),
        "mla_kv_norm": gain(ks[13], (N_EVEN, MLA_KV_RANK)),
        "mla_w_ukv": nrm(ks[14], (N_EVEN, MLA_KV_RANK, MLA_HEADS * (MLA_NOPE + MLA_V)), MLA_KV_RANK ** -0.5),
        "mla_q_gain": gain(ks[15], (N_EVEN, MLA_ROPE + MLA_NOPE)),
        "mla_k_gain": gain(ks[16], (N_EVEN, MLA_ROPE + MLA_NOPE)),
        "nsa_q_gain": gain(ks[17], (N_EVEN, NSA_DH)),
        "nsa_k_gain": gain(ks[18], (N_EVEN, 3, NSA_DH)),
        "nsa_cmp_pos": nrm(ks[19], (N_EVEN, 2, NSA_CMP_LEN, NSA_DH), 0.1),
        "nsa_cmp_w1": nrm(ks[20], (N_EVEN, 2, NSA_CMP_LEN * NSA_DH, NSA_CMP_HIDDEN), (NSA_CMP_LEN * NSA_DH) ** -0.5),
        "nsa_cmp_w2": nrm(ks[21], (N_EVEN, 2, NSA_CMP_HIDDEN, NSA_DH), NSA_CMP_HIDDEN ** -0.5),
        "diff_w_qkv": nrm(ks[22], (N_ODD, D, 3 * DIFF_W), D ** -0.5),
        "diff_w_out": nrm(ks[23], (N_ODD, DIFF_W, D), DIFF_W ** -0.5),
        "diff_q_gain": gain(ks[24], (N_ODD, DIFF_DH)),
        "diff_k_gain": gain(ks[25], (N_ODD, DIFF_DH)),
        "diff_lambda": nrm(ks[26], (N_ODD, 4, DIFF_DH), 0.1),
        "diff_sub_gain": gain(ks[27], (N_ODD, 2 * DIFF_DH)),
    }


def reference(x, c, positions, ada_w, ada_b, norm_mix, norm_mlp, mlp_w1, mlp_w2,
              hy_w_in, hy_w_out, mla_q_norm, mla_w_uq, mla_kv_norm, mla_w_ukv, mla_q_gain, mla_k_gain,
              nsa_q_gain, nsa_k_gain, nsa_cmp_pos, nsa_cmp_w1, nsa_cmp_w2,
              diff_w_qkv, diff_w_out, diff_q_gain, diff_k_gain, diff_lambda, diff_sub_gain):
    cond = jax.nn.silu(c)
    for i in range(DEPTH):
        mod = (cond @ ada_w[i] + ada_b[i])[:, None, :]
        sh1, sc1, g1, sh2, sc2, g2 = jnp.split(mod, 6, axis=-1)
        h = _rms(x, norm_mix[i]) * (1 + sc1) + sh1
        j = i // 2
        if i % 2 == 0:
            u = h @ hy_w_in[j]
            cq, ckv, kpe, nq, nkv, ngate = jnp.split(u, HY_SPLITS, axis=-1)
            y_mla = _mla_mixer(cq, ckv, kpe, positions, mla_q_norm[j], mla_w_uq[j], mla_kv_norm[j],
                               mla_w_ukv[j], mla_q_gain[j], mla_k_gain[j])
            y_nsa = _nsa_mixer(nq, nkv, ngate, positions, nsa_q_gain[j], nsa_k_gain[j],
                               nsa_cmp_pos[j], nsa_cmp_w1[j], nsa_cmp_w2[j])
            y = jnp.concatenate([y_mla, y_nsa.astype(y_mla.dtype)], axis=-1) @ hy_w_out[j]
        else:
            lam_init = 0.8 - 0.6 * math.exp(-0.3 * i)
            y = _diff_mixer(h, diff_w_qkv[j], diff_w_out[j], diff_q_gain[j], diff_k_gain[j],
                            diff_lambda[j], diff_sub_gain[j], positions, lam_init)
        x = x + g1 * y
        h = _rms(x, norm_mlp[i]) * (1 + sc2) + sh2
        x = x + g2 * (jnp.square(jax.nn.relu(h @ mlp_w1[i])) @ mlp_w2[i])
    return x
```

```python
import functools
import math

import numpy as np
import jax
import jax.numpy as jnp
from jax import lax
from jax.experimental import pallas as pl
from jax.experimental.pallas import tpu as pltpu

F32 = jnp.float32
BF16 = jnp.bfloat16

LANES = 128
VMEM_LIMIT = 52 * 1024 * 1024

ROPE_THETA = 500000.0
EPS = 1e-6
NEG = -1e30
FORCED = 1e9

MLA_HEADS, MLA_NOPE, MLA_ROPE, MLA_V = 8, 64, 32, 64
MLA_Q_RANK, MLA_KV_RANK = 256, 256
MLA_DK = MLA_ROPE + MLA_NOPE
NSA_HEADS, NSA_GROUPS, NSA_DH = 8, 2, 64
NSA_HPG = NSA_HEADS // NSA_GROUPS
NSA_ROT = NSA_DH // 4
NSA_CMP_LEN, NSA_CMP_STRIDE, NSA_CMP_HIDDEN = 32, 16, 128
NSA_SLC_LEN, NSA_TOP_N, NSA_WINDOW = 64, 16, 512
DIFF_HEADS, DIFF_DH = 8, 64
DIFF_ROT = DIFF_DH // 4

ROW_TILE = 512
ATT_TILE = 256


def _cparams(*sem):
    return pltpu.CompilerParams(dimension_semantics=sem, vmem_limit_bytes=VMEM_LIMIT)


def _split_bf16(x):
    hi = x.astype(BF16)
    lo = (x - hi.astype(F32)).astype(BF16)
    return hi, lo


def _dot(a, b):
    return jnp.dot(a, b, preferred_element_type=F32)


def _dot_nt(a, b):
    return lax.dot_general(a, b, (((1,), (1,)), ((), ())), preferred_element_type=F32)


def _sigmoid(x):
    return 1.0 / (1.0 + jnp.exp(-x))


def _lane(shape):
    return lax.broadcasted_iota(jnp.int32, shape, 1)


def _adaln_kernel(c_ref, w_ref, b_ref, o_ref):
    c = c_ref[...]
    cond = c * _sigmoid(c)
    c_hi, c_lo = _split_bf16(cond)
    w_hi, w_lo = _split_bf16(w_ref[0])
    o_ref[0] = _dot(c_hi, w_hi) + _dot(c_hi, w_lo) + _dot(c_lo, w_hi) + b_ref[0]


def _adaln(c, ada_w, ada_b):
    depth, d, n = ada_w.shape
    b = c.shape[0]
    tn = 1536
    return pl.pallas_call(
        _adaln_kernel,
        out_shape=jax.ShapeDtypeStruct((depth, b, n), F32),
        grid=(depth, n // tn),
        in_specs=[pl.BlockSpec((b, d), lambda i, j: (0, 0)),
                  pl.BlockSpec((1, d, tn), lambda i, j: (i, 0, j)),
                  pl.BlockSpec((1, 1, tn), lambda i, j: (i, 0, j))],
        out_specs=pl.BlockSpec((1, b, tn), lambda i, j: (i, 0, j)),
        compiler_params=_cparams("parallel", "parallel"),
        name="adaln",
    )(c, ada_w, ada_b.reshape(depth, 1, n))


def _rope_table_kernel(pos_ref, fa_ref, fb_ref, ca_ref, sa_ref, cb_ref, sb_ref):
    pos = pos_ref[...].astype(F32)
    ang_a = pos * fa_ref[...]
    ang_b = pos * fb_ref[...]
    ca_ref[...] = jnp.cos(ang_a)
    sa_ref[...] = jnp.sin(ang_a)
    cb_ref[...] = jnp.cos(ang_b)
    sb_ref[...] = jnp.sin(ang_b)


def _lane_freqs(rot, seg):
    half = rot // 2
    lane = np.arange(LANES)
    inv = ROPE_THETA ** (-(np.arange(half, dtype=np.float32) / np.float32(half)))
    f = np.where(lane % seg < rot, inv.astype(np.float32)[lane % half], 0.0)
    return jnp.asarray(f.reshape(1, LANES), F32)


def _rope_masks(rot, seg):
    half = rot // 2
    lane = np.arange(LANES) % seg
    up = ((lane >= half) & (lane < rot)).astype(np.float32)
    dn = -(lane < half).astype(np.float32)
    return jnp.asarray(up.reshape(1, LANES)), jnp.asarray(dn.reshape(1, LANES))


def _rope_tables(pos_col):
    rows = pos_col.shape[0]
    tr = min(rows, 2048)
    spec = pl.BlockSpec((tr, LANES), lambda i: (i, 0))
    vec = pl.BlockSpec((1, LANES), lambda i: (0, 0))
    shp = jax.ShapeDtypeStruct((rows, LANES), F32)
    return pl.pallas_call(
        _rope_table_kernel,
        out_shape=(shp, shp, shp, shp),
        grid=(rows // tr,),
        in_specs=[pl.BlockSpec((tr, 1), lambda i: (i, 0)), vec, vec],
        out_specs=(spec, spec, spec, spec),
        compiler_params=_cparams("parallel"),
        name="rope_tables",
    )(pos_col, _lane_freqs(MLA_ROPE, LANES), _lane_freqs(NSA_ROT, NSA_DH))


def _rope(y, c, s, m_up, m_dn, half):
    up = pltpu.roll(y, half, axis=1)
    dn = pltpu.roll(y, LANES - half, axis=1)
    return y * c + (up * m_up + dn * m_dn) * s


def _rms64(x, gain):
    lo = _lane(x.shape) < 64
    x2 = x * x
    s_lo = jnp.sum(jnp.where(lo, x2, 0.0), axis=-1, keepdims=True)
    s_hi = jnp.sum(jnp.where(lo, 0.0, x2), axis=-1, keepdims=True)
    r = jnp.where(lo, lax.rsqrt(s_lo * (1.0 / 64) + EPS), lax.rsqrt(s_hi * (1.0 / 64) + EPS))
    return x * r * gain


def _norm_mod(x, gain, sc, sh):
    ms = jnp.mean(x * x, axis=-1, keepdims=True)
    return (x * lax.rsqrt(ms + EPS) * gain) * (1.0 + sc) + sh


def _proj_kernel(x_ref, gain_ref, sc_ref, sh_ref, w_ref, o_ref, h_scr):
    @pl.when(pl.program_id(1) == 0)
    def _():
        h_scr[...] = _norm_mod(x_ref[...], gain_ref[...], sc_ref[0], sh_ref[0]).astype(BF16)

    o_ref[...] = _dot(h_scr[...], w_ref[...]).astype(o_ref.dtype)


def _mod_spec(layer, k, nb, rows_per_batch, tm, d):
    per = rows_per_batch // tm
    return pl.BlockSpec((1, 1, d), lambda i, *_: (layer * nb + i // per, 0, k))


def _proj(x2, gain, mod, layer, k_sc, k_sh, w, nb, tn):
    m, d = x2.shape
    n = w.shape[1]
    tm = 1024
    s = m // nb
    return pl.pallas_call(
        _proj_kernel,
        out_shape=jax.ShapeDtypeStruct((m, n), F32),
        grid=(m // tm, n // tn),
        in_specs=[pl.BlockSpec((tm, d), lambda i, j: (i, 0)),
                  pl.BlockSpec((1, d), lambda i, j: (0, 0)),
                  _mod_spec(layer, k_sc, nb, s, tm, d),
                  _mod_spec(layer, k_sh, nb, s, tm, d),
                  pl.BlockSpec((d, tn), lambda i, j: (0, j))],
        out_specs=pl.BlockSpec((tm, tn), lambda i, j: (i, j)),
        scratch_shapes=[pltpu.VMEM((tm, d), BF16)],
        compiler_params=_cparams("parallel", "arbitrary"),
        name="norm_mod_proj",
    )(x2, gain.reshape(1, d), mod, mod, w)


def _mlp_kernel(x_ref, gain_ref, sc_ref, sh_ref, g_ref, w1_ref, w2_ref, o_ref, h_scr, acc_scr):
    j = pl.program_id(1)

    @pl.when(j == 0)
    def _():
        h_scr[...] = _norm_mod(x_ref[...], gain_ref[...], sc_ref[0], sh_ref[0]).astype(BF16)
        acc_scr[...] = jnp.zeros_like(acc_scr)

    a = jnp.maximum(_dot(h_scr[...], w1_ref[...]), 0.0)
    acc_scr[...] += _dot((a * a).astype(BF16), w2_ref[...])

    @pl.when(j == pl.num_programs(1) - 1)
    def _():
        o_ref[...] = x_ref[...] + g_ref[0] * acc_scr[...]


def _mlp(x2, gain, mod, layer, w1, w2, nb):
    m, d = x2.shape
    ff = w1.shape[1]
    tm, tf = 1024, 512
    s = m // nb
    return pl.pallas_call(
        _mlp_kernel,
        out_shape=jax.ShapeDtypeStruct((m, d), F32),
        grid=(m // tm, ff // tf),
        in_specs=[pl.BlockSpec((tm, d), lambda i, j: (i, 0)),
                  pl.BlockSpec((1, d), lambda i, j: (0, 0)),
                  _mod_spec(layer, 4, nb, s, tm, d),
                  _mod_spec(layer, 3, nb, s, tm, d),
                  _mod_spec(layer, 5, nb, s, tm, d),
                  pl.BlockSpec((d, tf), lambda i, j: (0, j)),
                  pl.BlockSpec((tf, d), lambda i, j: (j, 0))],
        out_specs=pl.BlockSpec((tm, d), lambda i, j: (i, 0)),
        scratch_shapes=[pltpu.VMEM((tm, d), BF16), pltpu.VMEM((tm, d), F32)],
        compiler_params=_cparams("parallel", "arbitrary"),
        name="relu2_mlp",
    )(x2, gain.reshape(1, d), mod, mod, mod, w1, w2)


def _out_proj_kernel(n_in, *refs):
    x_ref, g_ref = refs[0], refs[1]
    y_refs = refs[2:2 + n_in]
    w_refs = refs[2 + n_in:2 + 2 * n_in]
    o_ref = refs[2 + 2 * n_in]
    y = _dot(y_refs[0][...], w_refs[0][...])
    for y_ref, w_ref in zip(y_refs[1:], w_refs[1:]):
        y = y + _dot(y_ref[...], w_ref[...])
    o_ref[...] = x_ref[...] + g_ref[0] * y


def _out_proj(x2, mod, layer, ys, ws, nb):
    m, d = x2.shape
    tm = ROW_TILE
    s = m // nb
    in_specs = [pl.BlockSpec((tm, d), lambda i: (i, 0)), _mod_spec(layer, 2, nb, s, tm, d)]
    in_specs += [pl.BlockSpec((tm, y.shape[1]), lambda i: (i, 0)) for y in ys]
    in_specs += [pl.BlockSpec(w.shape, lambda i: (0, 0)) for w in ws]
    return pl.pallas_call(
        functools.partial(_out_proj_kernel, len(ys)),
        out_shape=jax.ShapeDtypeStruct((m, d), F32),
        grid=(m // tm,),
        in_specs=in_specs,
        out_specs=pl.BlockSpec((tm, d), lambda i: (i, 0)),
        compiler_params=_cparams("parallel"),
        name="out_proj_residual",
    )(x2, mod, *ys, *ws)


def _flash_init(rows):
    return (jnp.full((rows, 1), NEG, F32), jnp.zeros((rows, 1), F32), jnp.zeros((rows, LANES), F32))


def _flash_tile(state, q, k_t, v_t, mask=None):
    m, l, acc = state
    s = _dot_nt(q, k_t)
    if mask is not None:
        s = jnp.where(mask, s, NEG)
    m_new = jnp.maximum(m, jnp.max(s, axis=-1, keepdims=True))
    alpha = jnp.exp(m - m_new)
    p = jnp.exp(s - m_new)
    if mask is not None:
        p = jnp.where(mask, p, 0.0)
    l = alpha * l + jnp.sum(p, axis=-1, keepdims=True)
    acc = alpha * acc + _dot(p.astype(BF16), v_t)
    return m_new, l, acc


def _causal_flash(q, k_at, v_at, i, t, rows_per_map):
    rows = q.shape[0]

    def body(j, state):
        return _flash_tile(state, q, k_at(j), v_at(j))

    state = lax.fori_loop(0, i, body, _flash_init(rows))
    r = lax.broadcasted_iota(jnp.int32, (rows, t), 0)
    if rows != rows_per_map:
        r = jnp.where(r >= rows_per_map, r - rows_per_map, r)
    mask = lax.broadcasted_iota(jnp.int32, (rows, t), 1) <= r
    m, l, acc = _flash_tile(state, q, k_at(i), v_at(i), mask)
    return acc / l


def _mla_prep_kernel(u_ref, qn_ref, kvn_ref, wq_ref, wk_ref, wv_ref, qg_ref, kg_ref,
                     c_ref, s_ref, mu_ref, md_ref, q_ref, k_ref, v_ref):
    def rms(x, g):
        return x * lax.rsqrt(jnp.mean(x * x, axis=-1, keepdims=True) + EPS) * g

    cq = rms(u_ref[:, 0:256], qn_ref[...]).astype(BF16)
    ckv = rms(u_ref[:, 256:512], kvn_ref[...]).astype(BF16)
    kpe = u_ref[:, 512:640]
    v_ref[...] = _dot(ckv, wv_ref[...]).astype(BF16)
    q_all = _dot(cq, wq_ref[...])
    k_all = _dot(ckv, wk_ref[...])
    c, s, mu, md = c_ref[...], s_ref[...], mu_ref[...], md_ref[...]
    scale = MLA_DK ** -0.5

    def head(x, g):
        r = lax.rsqrt(jnp.sum(x * x, axis=-1, keepdims=True) * (1.0 / MLA_DK) + EPS)
        return _rope(x * r * g, c, s, mu, md, MLA_ROPE // 2)

    for h in range(MLA_HEADS):
        sl = slice(LANES * h, LANES * (h + 1))
        q_ref[:, sl] = (head(q_all[:, sl], qg_ref[...]) * scale).astype(BF16)
        k_ref[:, sl] = head(k_all[:, sl] + kpe, kg_ref[...]).astype(BF16)


def _mla_prep(u, q_norm, kv_norm, wq, wk, wv, q_gain, k_gain, tab_c, tab_s):
    m = u.shape[0]
    tm = ROW_TILE
    mu, md = _rope_masks(MLA_ROPE, LANES)
    full = lambda a: pl.BlockSpec(a.shape, lambda i: (0, 0))
    row = lambda w: pl.BlockSpec((tm, w), lambda i: (i, 0))
    args = (u, q_norm, kv_norm, wq, wk, wv, q_gain, k_gain, tab_c, tab_s, mu, md)
    in_specs = [row(1024)] + [full(a) for a in args[1:8]] + [row(LANES), row(LANES), full(mu), full(md)]
    return pl.pallas_call(
        _mla_prep_kernel,
        out_shape=(jax.ShapeDtypeStruct((m, 1024), BF16), jax.ShapeDtypeStruct((m, 1024), BF16),
                   jax.ShapeDtypeStruct((m, 512), BF16)),
        grid=(m // tm,),
        in_specs=in_specs,
        out_specs=(row(1024), row(1024), row(512)),
        compiler_params=_cparams("parallel"),
        name="mla_prep",
    )(*args)


def _mla_attn_kernel(q_ref, k_ref, v_ref, o_ref):
    i = pl.program_id(1)
    t = ATT_TILE
    lo = _lane((t, LANES)) < 64

    def rows(j):
        return pl.ds(pl.multiple_of(j * t, t), t)

    for p in range(MLA_HEADS // 2):
        outs = []
        for e in range(2):
            h = 2 * p + e
            hs = slice(LANES * h, LANES * (h + 1))
            ps = slice(LANES * p, LANES * (p + 1))
            outs.append(_causal_flash(q_ref[0, :, hs],
                                      lambda j: k_ref[0, rows(j), hs],
                                      lambda j: v_ref[0, rows(j), ps], i, t, t))
        o_ref[0, :, LANES * p:LANES * (p + 1)] = jnp.where(lo, outs[0], outs[1]).astype(BF16)


def _mla_attn(q, k, v):
    b, s, _ = q.shape
    t = ATT_TILE
    return pl.pallas_call(
        _mla_attn_kernel,
        out_shape=jax.ShapeDtypeStruct((b, s, 512), BF16),
        grid=(b, s // t),
        in_specs=[pl.BlockSpec((1, t, 1024), lambda bi, i: (bi, i, 0)),
                  pl.BlockSpec((1, s, 1024), lambda bi, i: (bi, 0, 0)),
                  pl.BlockSpec((1, s, 512), lambda bi, i: (bi, 0, 0))],
        out_specs=pl.BlockSpec((1, t, 512), lambda bi, i: (bi, i, 0)),
        compiler_params=_cparams("parallel", "arbitrary"),
        name="mla_attention",
    )(q, k, v)


def _diff_prep_kernel(u_ref, g_ref, c_ref, s_ref, mu_ref, md_ref, q_ref, k_ref, v_ref):
    c, s, mu, md = c_ref[...], s_ref[...], mu_ref[...], md_ref[...]
    n = DIFF_HEADS * LANES
    for which, out in ((0, q_ref), (1, k_ref)):
        g = g_ref[which]
        for h in range(DIFF_HEADS):
            sl = slice(LANES * h, LANES * (h + 1))
            x = u_ref[:, which * n + LANES * h:which * n + LANES * (h + 1)]
            out[:, sl] = _rope(_rms64(x, g), c, s, mu, md, DIFF_ROT // 2).astype(BF16)
    v_ref[...] = u_ref[:, 2 * n:3 * n].astype(BF16)


def _diff_prep(u, gains, tab_c, tab_s):
    m = u.shape[0]
    tm = ROW_TILE
    mu, md = _rope_masks(DIFF_ROT, DIFF_DH)
    row = lambda w: pl.BlockSpec((tm, w), lambda i: (i, 0))
    vec = pl.BlockSpec((1, LANES), lambda i: (0, 0))
    shp = jax.ShapeDtypeStruct((m, 1024), BF16)
    return pl.pallas_call(
        _diff_prep_kernel,
        out_shape=(shp, shp, shp),
        grid=(m // tm,),
        in_specs=[row(3072), pl.BlockSpec((2, 1, LANES), lambda i: (0, 0, 0)), row(LANES), row(LANES), vec, vec],
        out_specs=(row(1024), row(1024), row(1024)),
        compiler_params=_cparams("parallel"),
        name="diff_prep",
    )(u, gains, tab_c, tab_s, mu, md)


def _diff_attn_kernel(lam_init, q_ref, k_ref, v_ref, lam_ref, sg_ref, o_ref):
    i = pl.program_id(1)
    t = ATT_TILE
    lo = _lane((t, LANES)) < 64
    lam = lam_ref[...]
    lmb = (jnp.exp(jnp.sum(lam[0:1] * lam[1:2], axis=-1, keepdims=True))
           - jnp.exp(jnp.sum(lam[2:3] * lam[3:4], axis=-1, keepdims=True)) + lam_init)

    def rows(j):
        return pl.ds(pl.multiple_of(j * t, t), t)

    for h in range(DIFF_HEADS):
        hs = slice(LANES * h, LANES * (h + 1))
        qt = q_ref[0, :, hs]
        zero = jnp.zeros_like(qt)
        q2 = jnp.concatenate([jnp.where(lo, qt, zero), jnp.where(lo, zero, qt)], axis=0)
        o = _causal_flash(q2, lambda j: k_ref[0, rows(j), hs], lambda j: v_ref[0, rows(j), hs], i, t, t)
        o = o[:t] - lmb * o[t:]
        o = o * lax.rsqrt(jnp.mean(o * o, axis=-1, keepdims=True) + EPS) * sg_ref[...]
        o_ref[0, :, hs] = (o * (1.0 - lam_init)).astype(BF16)


def _diff_attn(q, k, v, lam, sub_gain, lam_init):
    b, s, n = q.shape
    t = ATT_TILE
    full = pl.BlockSpec((1, s, n), lambda bi, i: (bi, 0, 0))
    return pl.pallas_call(
        functools.partial(_diff_attn_kernel, lam_init),
        out_shape=jax.ShapeDtypeStruct((b, s, n), BF16),
        grid=(b, s // t),
        in_specs=[pl.BlockSpec((1, t, n), lambda bi, i: (bi, i, 0)), full, full,
                  pl.BlockSpec(lam.shape, lambda bi, i: (0, 0)),
                  pl.BlockSpec((1, LANES), lambda bi, i: (0, 0))],
        out_specs=pl.BlockSpec((1, t, n), lambda bi, i: (bi, i, 0)),
        compiler_params=_cparams("parallel", "arbitrary"),
        name="diff_attention",
    )(q, k, v, lam, sub_gain.reshape(1, LANES))


def _nsa_prep_kernel(u_ref, qg_ref, kg_ref, c_ref, s_ref, mu_ref, md_ref,
                     q_ref, kslc_ref, kwin_ref, vslc_ref, vwin_ref, kcmp_ref, vcmp_ref):
    c, s, mu, md = c_ref[...], s_ref[...], mu_ref[...], md_ref[...]

    def prep(x, g):
        return _rope(_rms64(x, g), c, s, mu, md, NSA_ROT // 2)

    for hg in range(NSA_HPG):
        sl = slice(LANES * hg, LANES * (hg + 1))
        q_ref[:, sl] = prep(u_ref[:, 768 + LANES * hg:768 + LANES * (hg + 1)], qg_ref[...]).astype(BF16)
    kcmp_ref[...] = u_ref[:, 1280:1408].astype(BF16)
    vcmp_ref[...] = u_ref[:, 1408:1536].astype(BF16)
    kslc_ref[...] = prep(u_ref[:, 1536:1664], kg_ref[1]).astype(BF16)
    vslc_ref[...] = u_ref[:, 1664:1792].astype(BF16)
    kwin_ref[...] = prep(u_ref[:, 1792:1920], kg_ref[2]).astype(BF16)
    vwin_ref[...] = u_ref[:, 1920:2048].astype(BF16)


def _nsa_prep(u, q_gain, k_gain, tab_c, tab_s):
    m = u.shape[0]
    tm = ROW_TILE
    mu, md = _rope_masks(NSA_ROT, NSA_DH)
    row = lambda w: pl.BlockSpec((tm, w), lambda i: (i, 0))
    vec = pl.BlockSpec((1, LANES), lambda i: (0, 0))
    t128 = jax.ShapeDtypeStruct((m, LANES), BF16)
    return pl.pallas_call(
        _nsa_prep_kernel,
        out_shape=(jax.ShapeDtypeStruct((m, 512), BF16),) + (t128,) * 6,
        grid=(m // tm,),
        in_specs=[row(2048), vec, pl.BlockSpec((3, 1, LANES), lambda i: (0, 0, 0)), row(LANES), row(LANES), vec, vec],
        out_specs=(row(512),) + (row(LANES),) * 6,
        compiler_params=_cparams("parallel"),
        name="nsa_prep",
    )(u, q_gain, k_gain, tab_c, tab_s, mu, md)


def _nsa_compress_kernel(tk_ref, tv_ref, pos_ref, w1a_ref, w1b_ref, w2_ref, kg_ref,
                         c_ref, s_ref, mu_ref, md_ref, kc_ref, vc_ref):
    n_rows = tk_ref.shape[1]
    for j, (t_ref, out) in enumerate(((tk_ref, kc_ref), (tv_ref, vc_ref))):
        tok = t_ref[0]
        p_hi, p_lo = _split_bf16(pos_ref[j])
        w1a, w1b = w1a_ref[j], w1b_ref[j]
        bias = (_dot(p_hi[0], w1a) + _dot(p_lo[0], w1a) + _dot(p_hi[1], w1b) + _dot(p_lo[1], w1b))[0:1]
        hid = _dot(tok, w1a) + pltpu.roll(_dot(tok, w1b), n_rows - 1, axis=0) + bias
        act = jax.nn.gelu(hid, approximate=True)
        cmp = _dot(act.astype(BF16), w2_ref[j])
        if j == 0:
            cmp = _rope(_rms64(cmp, kg_ref[...]), c_ref[0], s_ref[0], mu_ref[...], md_ref[...], NSA_ROT // 2)
        out[0] = cmp.astype(BF16)


def _nsa_compress(tk, tv, pos, w1a, w1b, w2, k_gain0, tab_c, tab_s):
    b, nr, w = tk.shape
    mu, md = _rope_masks(NSA_ROT, NSA_DH)
    full = lambda a: pl.BlockSpec(a.shape, lambda bi: (0,) * a.ndim)
    per_b = lambda a: pl.BlockSpec((1,) + a.shape[1:], lambda bi: (bi,) + (0,) * (a.ndim - 1))
    shp = jax.ShapeDtypeStruct((b, nr, LANES), BF16)
    return pl.pallas_call(
        _nsa_compress_kernel,
        out_shape=(shp, shp),
        grid=(b,),
        in_specs=[per_b(tk), per_b(tv), full(pos), full(w1a), full(w1b), full(w2), full(k_gain0),
                  per_b(tab_c), per_b(tab_s), full(mu), full(md)],
        out_specs=(pl.BlockSpec((1, nr, LANES), lambda bi: (bi, 0, 0)),) * 2,
        compiler_params=_cparams("parallel"),
        name="nsa_compress",
    )(tk, tv, pos, w1a, w1b, w2, k_gain0, tab_c, tab_s, mu, md)


def _nsa_attn_kernel(q_ref, kc_ref, vc_ref, ks_ref, vs_ref, kw_ref, vw_ref, gate_ref, cov_ref, o_ref):
    i = pl.program_id(1)
    t = ATT_TILE
    n_blk = 32
    q0 = i * t
    lo = _lane((t, LANES)) < 64
    qpos = q0 + lax.broadcasted_iota(jnp.int32, (t, LANES), 0)
    cmp_ok = NSA_CMP_STRIDE * _lane((t, LANES)) + (NSA_CMP_LEN - 1) <= qpos
    gates = _sigmoid(gate_ref[0])

    def rows(j):
        return pl.ds(pl.multiple_of(j * t, t), t)

    def tile_pos(j):
        kpos = j * t + lax.broadcasted_iota(jnp.int32, (t, t), 1)
        return kpos, q0 + lax.broadcasted_iota(jnp.int32, (t, t), 0)

    kc, vc = kc_ref[0], vc_ref[0]
    outs = [[None] * NSA_GROUPS for _ in range(NSA_HPG)]
    for g in range(NSA_GROUPS):
        qs = []
        for hg in range(NSA_HPG):
            qt = q_ref[0, :, LANES * hg:LANES * (hg + 1)]
            zero = jnp.zeros_like(qt)
            qs.append(jnp.where(lo, qt, zero) if g == 0 else jnp.where(lo, zero, qt))

        o_cmp, p_sum = [], jnp.zeros((t, LANES), F32)
        for hg in range(NSA_HPG):
            sc = jnp.where(cmp_ok, _dot_nt(qs[hg], kc), NEG)
            p = jnp.where(cmp_ok, jnp.exp(sc - jnp.max(sc, axis=-1, keepdims=True)), 0.0)
            l = jnp.sum(p, axis=-1, keepdims=True)
            p = p * jnp.where(l > 0.0, 1.0 / l, 0.0)
            p_sum = p_sum + p
            o_cmp.append(_dot(p.astype(BF16), vc))
        p_hi, p_lo = _split_bf16(p_sum)
        imp = (_dot_nt(cov_ref[...], p_hi) + _dot_nt(cov_ref[...], p_lo))[0:n_blk]
        blk = lax.broadcasted_iota(jnp.int32, (n_blk, t), 0)
        jt = (q0 + lax.broadcasted_iota(jnp.int32, (n_blk, t), 1)) >> 6
        allowed = blk <= jt
        forced = allowed & ((blk == 0) | (blk >= jt - 1))
        imp = jnp.where(forced, FORCED, jnp.where(allowed, imp, NEG))
        rank = jnp.zeros((n_blk, t), jnp.int32)
        for jp in range(n_blk):
            other = imp[jp:jp + 1, :]
            ahead = (other > imp) | ((other == imp) & (blk > jp))
            rank = rank + ahead.astype(jnp.int32)
        sel_t = jnp.where((rank < NSA_TOP_N) & allowed, 1.0, 0.0)
        sel_t = jnp.concatenate([sel_t, jnp.zeros((LANES - n_blk, t), F32)], axis=0)
        sel = jnp.transpose(sel_t).astype(BF16)

        def slc_body(j, states):
            kpos, qp = tile_pos(j)
            blk_of_key = (j * t + lax.broadcasted_iota(jnp.int32, (LANES, t), 1)) >> 6
            expand = jnp.where(blk_of_key == lax.broadcasted_iota(jnp.int32, (LANES, t), 0), 1.0, 0.0).astype(BF16)
            mask = (_dot(sel, expand) > 0.5) & (kpos <= qp)
            k_t, v_t = ks_ref[0, rows(j), :], vs_ref[0, rows(j), :]
            return tuple(_flash_tile(states[hg], qs[hg], k_t, v_t, mask) for hg in range(NSA_HPG))

        slc = lax.fori_loop(0, i + 1, slc_body, tuple(_flash_init(t) for _ in range(NSA_HPG)))

        def win_body(j, states):
            kpos, qp = tile_pos(j)
            mask = (kpos <= qp) & (kpos > qp - NSA_WINDOW)
            k_t, v_t = kw_ref[0, rows(j), :], vw_ref[0, rows(j), :]
            return tuple(_flash_tile(states[hg], qs[hg], k_t, v_t, mask) for hg in range(NSA_HPG))

        first = jnp.maximum(i - NSA_WINDOW // t, 0)
        win = lax.fori_loop(first, i + 1, win_body, tuple(_flash_init(t) for _ in range(NSA_HPG)))

        for hg in range(NSA_HPG):
            head = g * NSA_HPG + hg
            g_cmp = gates[:, head:head + 1]
            g_slc = gates[:, NSA_HEADS + head:NSA_HEADS + head + 1]
            g_win = gates[:, 2 * NSA_HEADS + head:2 * NSA_HEADS + head + 1]
            outs[hg][g] = (g_cmp * o_cmp[hg] + g_slc * (slc[hg][2] / slc[hg][1])
                           + g_win * (win[hg][2] / win[hg][1]))

    for hg in range(NSA_HPG):
        o_ref[0, :, LANES * hg:LANES * (hg + 1)] = jnp.where(lo, outs[hg][0], outs[hg][1]).astype(BF16)


def _nsa_attn(q, kc, vc, kslc, vslc, kwin, vwin, u3, cover_t):
    b, s, _ = q.shape
    t = ATT_TILE
    seq = pl.BlockSpec((1, s, LANES), lambda bi, i: (bi, 0, 0))
    cmp = pl.BlockSpec((1, LANES, LANES), lambda bi, i: (bi, 0, 0))
    return pl.pallas_call(
        _nsa_attn_kernel,
        out_shape=jax.ShapeDtypeStruct((b, s, 512), BF16),
        grid=(b, s // t),
        in_specs=[pl.BlockSpec((1, t, 512), lambda bi, i: (bi, i, 0)), cmp, cmp, seq, seq, seq, seq,
                  pl.BlockSpec((1, t, LANES), lambda bi, i: (bi, i, 5)),
                  pl.BlockSpec((LANES, LANES), lambda bi, i: (0, 0))],
        out_specs=pl.BlockSpec((1, t, 512), lambda bi, i: (bi, i, 0)),
        compiler_params=_cparams("parallel", "arbitrary"),
        name="nsa_attention",
    )(q, kc, vc, kslc, vslc, kwin, vwin, u3, cover_t)


def _pad_lanes(a, width):
    return jnp.pad(a, [(0, 0)] * (a.ndim - 1) + [(0, width - a.shape[-1])])


def _hy_in_weight(w):
    d = w.shape[0]
    cq, ckv, kpe, nq, nkv, gate = jnp.split(w, [256, 512, 544, 1056, 1824], axis=1)
    nq = nq.reshape(d, NSA_GROUPS, NSA_HPG, NSA_DH).transpose(0, 2, 1, 3).reshape(d, 512)
    return jnp.concatenate([cq, ckv, _pad_lanes(kpe, LANES), _pad_lanes(gate, LANES), nq, nkv], axis=1).astype(BF16)


def _cover_t():
    nc, ns = 127, 32
    c_start = np.arange(nc) * NSA_CMP_STRIDE
    c_end = c_start + NSA_CMP_LEN - 1
    j_start = np.arange(ns) * NSA_SLC_LEN
    cover = ((c_start[:, None] <= j_start[None, :] + NSA_SLC_LEN - 1) & (c_end[:, None] >= j_start[None, :]))
    out = np.zeros((LANES, LANES), np.float32)
    out[:ns, :nc] = cover.T
    return jnp.asarray(out, BF16)


def _compress_weights(cmp_pos, w1, w2):
    half = NSA_CMP_STRIDE
    w1 = w1.reshape(2, 2, half, NSA_DH, NSA_CMP_HIDDEN)
    eye = jnp.eye(NSA_GROUPS, dtype=F32)
    w1e = jnp.einsum('jcldn,gh->jclgdhn', w1, eye).reshape(2, 2, half * LANES, NSA_GROUPS * NSA_CMP_HIDDEN)
    w2e = jnp.einsum('jnd,gh->jgnhd', w2, eye).reshape(2, NSA_GROUPS * NSA_CMP_HIDDEN, LANES)
    pos = cmp_pos.reshape(2, 2, half, 1, NSA_DH)
    pos = jnp.broadcast_to(pos, (2, 2, half, NSA_GROUPS, NSA_DH)).reshape(2, 2, 1, half * LANES)
    pos = jnp.broadcast_to(pos, (2, 2, 8, half * LANES))
    return pos, w1e[:, 0].astype(BF16), w1e[:, 1].astype(BF16), w2e.astype(BF16)


def kernel(x, c, positions, ada_w, ada_b, norm_mix, norm_mlp, mlp_w1, mlp_w2, hy_w_in, hy_w_out, mla_q_norm, mla_w_uq, mla_kv_norm, mla_w_ukv, mla_q_gain, mla_k_gain, nsa_q_gain, nsa_k_gain, nsa_cmp_pos, nsa_cmp_w1, nsa_cmp_w2, diff_w_qkv, diff_w_out, diff_q_gain, diff_k_gain, diff_lambda, diff_sub_gain):
    nb, seq, d = x.shape
    depth = ada_w.shape[0]
    m = nb * seq
    n_cmp = seq // NSA_CMP_STRIDE

    mod = _adaln(c, ada_w, ada_b).reshape(depth * nb, 1, 6 * d)
    tab_ca, tab_sa, tab_cb, tab_sb = _rope_tables(positions.reshape(m, 1))
    pos_c = jnp.pad(positions[:, NSA_CMP_LEN - 1::NSA_CMP_STRIDE], ((0, 0), (0, 1)))
    _, _, tab_cc, tab_sc = _rope_tables(pos_c.reshape(nb * n_cmp, 1))
    tab_cc, tab_sc = tab_cc.reshape(nb, n_cmp, LANES), tab_sc.reshape(nb, n_cmp, LANES)
    cover_t = _cover_t()

    x2 = x.reshape(m, d)
    for i in range(depth):
        j = i // 2
        if i % 2 == 0:
            u = _proj(x2, norm_mix[i], mod, i, 1, 0, _hy_in_weight(hy_w_in[j]), nb, 1024)

            wq = _pad_lanes(mla_w_uq[j].reshape(MLA_Q_RANK, MLA_HEADS, MLA_DK), LANES).reshape(MLA_Q_RANK, -1)
            wkv = mla_w_ukv[j].reshape(MLA_KV_RANK, MLA_HEADS, MLA_NOPE + MLA_V)
            wk = jnp.pad(wkv[..., :MLA_NOPE], ((0, 0), (0, 0), (MLA_ROPE, LANES - MLA_DK))).reshape(MLA_KV_RANK, -1)
            wv = wkv[..., MLA_NOPE:].reshape(MLA_KV_RANK, -1)
            q_mla, k_mla, v_mla = _mla_prep(
                u, mla_q_norm[j].reshape(1, -1), mla_kv_norm[j].reshape(1, -1),
                wq.astype(BF16), wk.astype(BF16), wv.astype(BF16),
                _pad_lanes(mla_q_gain[j], LANES).reshape(1, LANES), _pad_lanes(mla_k_gain[j], LANES).reshape(1, LANES),
                tab_ca, tab_sa)
            y_mla = _mla_attn(q_mla.reshape(nb, seq, -1), k_mla.reshape(nb, seq, -1), v_mla.reshape(nb, seq, -1))

            qg = jnp.tile(nsa_q_gain[j] * NSA_DH ** -0.5, 2).reshape(1, LANES)
            kg = jnp.tile(nsa_k_gain[j], (1, 2)).reshape(3, 1, LANES)
            q_nsa, kslc, kwin, vslc, vwin, kcmp, vcmp = _nsa_prep(u, qg, kg, tab_cb, tab_sb)
            pos_e, w1a, w1b, w2e = _compress_weights(nsa_cmp_pos[j], nsa_cmp_w1[j], nsa_cmp_w2[j])
            chunks = lambda a: a.reshape(nb, n_cmp, NSA_CMP_STRIDE * LANES)
            kc, vc = _nsa_compress(chunks(kcmp), chunks(vcmp), pos_e, w1a, w1b, w2e, kg[0], tab_cc, tab_sc)
            seq3 = lambda a: a.reshape(nb, seq, -1)
            y_nsa = _nsa_attn(seq3(q_nsa), kc, vc, seq3(kslc), seq3(vslc), seq3(kwin), seq3(vwin),
                              seq3(u), cover_t)

            w_out = hy_w_out[j]
            w_nsa = w_out[512:].reshape(NSA_GROUPS, NSA_HPG, NSA_DH, d).transpose(1, 0, 2, 3).reshape(512, d)
            x2 = _out_proj(x2, mod, i, [y_mla.reshape(m, -1), y_nsa.reshape(m, -1)],
                           [w_out[:512].astype(BF16), w_nsa.astype(BF16)], nb)
        else:
            lam_init = 0.8 - 0.6 * math.exp(-0.3 * i)
            u = _proj(x2, norm_mix[i], mod, i, 1, 0, diff_w_qkv[j].astype(BF16), nb, 1024)
            gains = jnp.stack([jnp.tile(diff_q_gain[j] * DIFF_DH ** -0.5, 2), jnp.tile(diff_k_gain[j], 2)])
            q, k, v = _diff_prep(u, gains.reshape(2, 1, LANES), tab_cb, tab_sb)
            seq3 = lambda a: a.reshape(nb, seq, -1)
            y = _diff_attn(seq3(q), seq3(k), seq3(v), diff_lambda[j], diff_sub_gain[j], lam_init)
            x2 = _out_proj(x2, mod, i, [y.reshape(m, -1)], [diff_w_out[j].astype(BF16)], nb)
        x2 = _mlp(x2, norm_mlp[i], mod, i, mlp_w1[i].astype(BF16), mlp_w2[i].astype(BF16), nb)
    return x2.reshape(nb, seq, d)
```

```python
import functools
import math

import numpy as np
import jax
import jax.numpy as jnp
from jax import lax
from jax.experimental import pallas as pl
from jax.experimental.pallas import tpu as pltpu

F32 = jnp.float32
BF16 = jnp.bfloat16

LANES = 128
VMEM_LIMIT = 52 * 1024 * 1024

ROPE_THETA = 500000.0
EPS = 1e-6
NEG = -1e30
FORCED = 1e9

MLA_HEADS, MLA_NOPE, MLA_ROPE, MLA_V = 8, 64, 32, 64
MLA_Q_RANK, MLA_KV_RANK = 256, 256
MLA_DK = MLA_ROPE + MLA_NOPE
NSA_HEADS, NSA_GROUPS, NSA_DH = 8, 2, 64
NSA_HPG = NSA_HEADS // NSA_GROUPS
NSA_ROT = NSA_DH // 4
NSA_CMP_LEN, NSA_CMP_STRIDE, NSA_CMP_HIDDEN = 32, 16, 128
NSA_SLC_LEN, NSA_TOP_N, NSA_WINDOW = 64, 16, 512
DIFF_HEADS, DIFF_DH = 8, 64
DIFF_ROT = DIFF_DH // 4

ROW_TILE = 512
ATT_TILE = 256


def _cparams(*sem):
    return pltpu.CompilerParams(dimension_semantics=sem, vmem_limit_bytes=VMEM_LIMIT)


def _split_bf16(x):
    hi = x.astype(BF16)
    lo = (x - hi.astype(F32)).astype(BF16)
    return hi, lo


def _dot(a, b):
    return jnp.dot(a, b, preferred_element_type=F32)


def _dot_nt(a, b):
    return lax.dot_general(a, b, (((1,), (1,)), ((), ())), preferred_element_type=F32)


def _sigmoid(x):
    return 1.0 / (1.0 + jnp.exp(-x))


def _lane(shape):
    return lax.broadcasted_iota(jnp.int32, shape, 1)


def _adaln_kernel(c_ref, w_ref, b_ref, o_ref):
    c = c_ref[...]
    cond = c * _sigmoid(c)
    c_hi, c_lo = _split_bf16(cond)
    w_hi, w_lo = _split_bf16(w_ref[0])
    o_ref[0] = _dot(c_hi, w_hi) + _dot(c_hi, w_lo) + _dot(c_lo, w_hi) + b_ref[0]


def _adaln(c, ada_w, ada_b):
    depth, d, n = ada_w.shape
    b = c.shape[0]
    tn = 1536
    return pl.pallas_call(
        _adaln_kernel,
        out_shape=jax.ShapeDtypeStruct((depth, b, n), F32),
        grid=(depth, n // tn),
        in_specs=[pl.BlockSpec((b, d), lambda i, j: (0, 0)),
                  pl.BlockSpec((1, d, tn), lambda i, j: (i, 0, j)),
                  pl.BlockSpec((1, 1, tn), lambda i, j: (i, 0, j))],
        out_specs=pl.BlockSpec((1, b, tn), lambda i, j: (i, 0, j)),
        compiler_params=_cparams("parallel", "parallel"),
        name="adaln",
    )(c, ada_w, ada_b.reshape(depth, 1, n))


def _rope_table_kernel(pos_ref, fa_ref, fb_ref, ca_ref, sa_ref, cb_ref, sb_ref):
    pos = pos_ref[...].astype(F32)
    ang_a = pos * fa_ref[...]
    ang_b = pos * fb_ref[...]
    ca_ref[...] = jnp.cos(ang_a)
    sa_ref[...] = jnp.sin(ang_a)
    cb_ref[...] = jnp.cos(ang_b)
    sb_ref[...] = jnp.sin(ang_b)


def _lane_freqs(rot, seg):
    half = rot // 2
    lane = np.arange(LANES)
    inv = ROPE_THETA ** (-(np.arange(half, dtype=np.float32) / np.float32(half)))
    f = np.where(lane % seg < rot, inv.astype(np.float32)[lane % half], 0.0)
    return jnp.asarray(f.reshape(1, LANES), F32)


def _rope_masks(rot, seg):
    half = rot // 2
    lane = np.arange(LANES) % seg
    up = ((lane >= half) & (lane < rot)).astype(np.float32)
    dn = -(lane < half).astype(np.float32)
    return jnp.asarray(up.reshape(1, LANES)), jnp.asarray(dn.reshape(1, LANES))


def _rope_tables(pos_col):
    rows = pos_col.shape[0]
    tr = min(rows, 2048)
    spec = pl.BlockSpec((tr, LANES), lambda i: (i, 0))
    vec = pl.BlockSpec((1, LANES), lambda i: (0, 0))
    shp = jax.ShapeDtypeStruct((rows, LANES), F32)
    return pl.pallas_call(
        _rope_table_kernel,
        out_shape=(shp, shp, shp, shp),
        grid=(rows // tr,),
        in_specs=[pl.BlockSpec((tr, 1), lambda i: (i, 0)), vec, vec],
        out_specs=(spec, spec, spec, spec),
        compiler_params=_cparams("parallel"),
        name="rope_tables",
    )(pos_col, _lane_freqs(MLA_ROPE, LANES), _lane_freqs(NSA_ROT, NSA_DH))


def _rope(y, c, s, m_up, m_dn, half):
    up = pltpu.roll(y, half, axis=1)
    dn = pltpu.roll(y, LANES - half, axis=1)
    return y * c + (up * m_up + dn * m_dn) * s


def _rms64(x, gain):
    lo = _lane(x.shape) < 64
    x2 = x * x
    s_lo = jnp.sum(jnp.where(lo, x2, 0.0), axis=-1, keepdims=True)
    s_hi = jnp.sum(jnp.where(lo, 0.0, x2), axis=-1, keepdims=True)
    r = jnp.where(lo, lax.rsqrt(s_lo * (1.0 / 64) + EPS), lax.rsqrt(s_hi * (1.0 / 64) + EPS))
    return x * r * gain


def _norm_mod(x, gain, sc, sh):
    ms = jnp.mean(x * x, axis=-1, keepdims=True)
    return (x * lax.rsqrt(ms + EPS) * gain) * (1.0 + sc) + sh


def _proj_kernel(x_ref, gain_ref, sc_ref, sh_ref, w_ref, o_ref, h_scr):
    @pl.when(pl.program_id(1) == 0)
    def _():
        h_scr[...] = _norm_mod(x_ref[...], gain_ref[...], sc_ref[0], sh_ref[0]).astype(BF16)

    o_ref[...] = _dot(h_scr[...], w_ref[...]).astype(o_ref.dtype)


def _mod_spec(layer, k, nb, rows_per_batch, tm, d):
    per = rows_per_batch // tm
    return pl.BlockSpec((1, 1, d), lambda i, *_: (layer * nb + i // per, 0, k))


def _proj(x2, gain, mod, layer, k_sc, k_sh, w, nb, tn):
    m, d = x2.shape
    n = w.shape[1]
    tm = 1024
    s = m // nb
    return pl.pallas_call(
        _proj_kernel,
        out_shape=jax.ShapeDtypeStruct((m, n), F32),
        grid=(m // tm, n // tn),
        in_specs=[pl.BlockSpec((tm, d), lambda i, j: (i, 0)),
                  pl.BlockSpec((1, d), lambda i, j: (0, 0)),
                  _mod_spec(layer, k_sc, nb, s, tm, d),
                  _mod_spec(layer, k_sh, nb, s, tm, d),
                  pl.BlockSpec((d, tn), lambda i, j: (0, j))],
        out_specs=pl.BlockSpec((tm, tn), lambda i, j: (i, j)),
        scratch_shapes=[pltpu.VMEM((tm, d), BF16)],
        compiler_params=_cparams("parallel", "arbitrary"),
        name="norm_mod_proj",
    )(x2, gain.reshape(1, d), mod, mod, w)


def _mlp_kernel(x_ref, gain_ref, sc_ref, sh_ref, g_ref, w1_ref, w2_ref, o_ref, h_scr, acc_scr):
    j = pl.program_id(1)

    @pl.when(j == 0)
    def _():
        h_scr[...] = _norm_mod(x_ref[...], gain_ref[...], sc_ref[0], sh_ref[0]).astype(BF16)
        acc_scr[...] = jnp.zeros_like(acc_scr)

    a = jnp.maximum(_dot(h_scr[...], w1_ref[...]), 0.0)
    acc_scr[...] += _dot((a * a).astype(BF16), w2_ref[...])

    @pl.when(j == pl.num_programs(1) - 1)
    def _():
        o_ref[...] = x_ref[...] + g_ref[0] * acc_scr[...]


def _mlp(x2, gain, mod, layer, w1, w2, nb):
    m, d = x2.shape
    ff = w1.shape[1]
    tm, tf = 1024, 512
    s = m // nb
    return pl.pallas_call(
        _mlp_kernel,
        out_shape=jax.ShapeDtypeStruct((m, d), F32),
        grid=(m // tm, ff // tf),
        in_specs=[pl.BlockSpec((tm, d), lambda i, j: (i, 0)),
                  pl.BlockSpec((1, d), lambda i, j: (0, 0)),
                  _mod_spec(layer, 4, nb, s, tm, d),
                  _mod_spec(layer, 3, nb, s, tm, d),
                  _mod_spec(layer, 5, nb, s, tm, d),
                  pl.BlockSpec((d, tf), lambda i, j: (0, j)),
                  pl.BlockSpec((tf, d), lambda i, j: (j, 0))],
        out_specs=pl.BlockSpec((tm, d), lambda i, j: (i, 0)),
        scratch_shapes=[pltpu.VMEM((tm, d), BF16), pltpu.VMEM((tm, d), F32)],
        compiler_params=_cparams("parallel", "arbitrary"),
        name="relu2_mlp",
    )(x2, gain.reshape(1, d), mod, mod, mod, w1, w2)


def _out_proj_kernel(n_in, *refs):
    x_ref, g_ref = refs[0], refs[1]
    y_refs = refs[2:2 + n_in]
    w_refs = refs[2 + n_in:2 + 2 * n_in]
    o_ref = refs[2 + 2 * n_in]
    y = _dot(y_refs[0][...], w_refs[0][...])
    for y_ref, w_ref in zip(y_refs[1:], w_refs[1:]):
        y = y + _dot(y_ref[...], w_ref[...])
    o_ref[...] = x_ref[...] + g_ref[0] * y


def _out_proj(x2, mod, layer, ys, ws, nb):
    m, d = x2.shape
    tm = ROW_TILE
    s = m // nb
    in_specs = [pl.BlockSpec((tm, d), lambda i: (i, 0)), _mod_spec(layer, 2, nb, s, tm, d)]
    in_specs += [pl.BlockSpec((tm, y.shape[1]), lambda i: (i, 0)) for y in ys]
    in_specs += [pl.BlockSpec(w.shape, lambda i: (0, 0)) for w in ws]
    return pl.pallas_call(
        functools.partial(_out_proj_kernel, len(ys)),
        out_shape=jax.ShapeDtypeStruct((m, d), F32),
        grid=(m // tm,),
        in_specs=in_specs,
        out_specs=pl.BlockSpec((tm, d), lambda i: (i, 0)),
        compiler_params=_cparams("parallel"),
        name="out_proj_residual",
    )(x2, mod, *ys, *ws)


LOG2E = math.log2(math.e)


def _flash_init(rows):
    return (jnp.full((rows, 1), NEG, F32), jnp.zeros((rows, LANES), F32), jnp.zeros((rows, LANES), F32))


def _flash_tile(state, q, k_t, v_t, mask=None, may_be_empty=False):
    m, l, acc = state
    s = _dot_nt(q, k_t)
    if mask is not None:
        s = jnp.where(mask, s, NEG)
    m_new = jnp.maximum(m, jnp.max(s, axis=-1, keepdims=True))
    alpha = jnp.exp2(m - m_new)
    p = jnp.exp2(s - m_new)
    if may_be_empty:
        p = jnp.where(mask, p, 0.0)
    p_lanes = p[:, 0:LANES]
    for c in range(1, p.shape[1] // LANES):
        p_lanes = p_lanes + p[:, LANES * c:LANES * (c + 1)]
    l = alpha * l + p_lanes
    acc = alpha * acc + _dot(p.astype(BF16), v_t)
    return m_new, l, acc


def _flash_out(state):
    m, l, acc = state
    return acc / jnp.sum(l, axis=-1, keepdims=True)


def _lane_tiles(x):
    return [x[:, LANES * c:LANES * (c + 1)] for c in range(x.shape[1] // LANES)]


def _causal_flash(qs, k_at, v_at, i, t, rows_per_map):
    n = len(qs)
    rows = qs[0].shape[0]
    r = lax.broadcasted_iota(jnp.int32, (rows, t), 0)
    if rows != rows_per_map:
        r = jnp.where(r >= rows_per_map, r - rows_per_map, r)
    mask = lax.broadcasted_iota(jnp.int32, (rows, t), 1) <= r

    def scores(h, j, diagonal):
        s = _dot_nt(qs[h], k_at(h, j))
        return jnp.where(mask, s, NEG) if diagonal else s

    def max_step(ms, j, diagonal):
        out = []
        for h in range(n):
            m = ms[h]
            for s_c in _lane_tiles(scores(h, j, diagonal)):
                m = jnp.maximum(m, s_c)
            out.append(m)
        return tuple(out)

    ms = tuple(jnp.full((rows, LANES), NEG, F32) for _ in range(n))
    ms = max_step(lax.fori_loop(0, i, lambda j, st: max_step(st, j, False), ms), i, True)
    ms = [jnp.broadcast_to(jnp.max(m, axis=-1, keepdims=True), (rows, LANES)) for m in ms]

    def sum_step(states, j, diagonal):
        out = []
        for h in range(n):
            l, acc = states[h]
            ps = [jnp.exp2(s_c - ms[h]) for s_c in _lane_tiles(scores(h, j, diagonal))]
            for p_c in ps:
                l = l + p_c
            acc = acc + _dot(jnp.concatenate(ps, axis=1).astype(BF16), v_at(h, j))
            out.append((l, acc))
        return tuple(out)

    zero = jnp.zeros((rows, LANES), F32)
    states = tuple((zero, zero) for _ in range(n))
    states = sum_step(lax.fori_loop(0, i, lambda j, st: sum_step(st, j, False), states), i, True)
    return [acc / jnp.sum(l, axis=-1, keepdims=True) for l, acc in states]


def _mla_prep_kernel(u_ref, qn_ref, kvn_ref, wq_ref, wk_ref, wv_ref, qg_ref, kg_ref,
                     c_ref, s_ref, mu_ref, md_ref, q_ref, k_ref, v_ref):
    def rms(x, g):
        return x * lax.rsqrt(jnp.mean(x * x, axis=-1, keepdims=True) + EPS) * g

    cq = rms(u_ref[:, 0:256], qn_ref[...]).astype(BF16)
    ckv = rms(u_ref[:, 256:512], kvn_ref[...]).astype(BF16)
    kpe = u_ref[:, 512:640]
    v_ref[...] = _dot(ckv, wv_ref[...]).astype(BF16)
    q_all = _dot(cq, wq_ref[...])
    k_all = _dot(ckv, wk_ref[...])
    c, s, mu, md = c_ref[...], s_ref[...], mu_ref[...], md_ref[...]
    scale = MLA_DK ** -0.5 * LOG2E

    def head(x, g):
        r = lax.rsqrt(jnp.sum(x * x, axis=-1, keepdims=True) * (1.0 / MLA_DK) + EPS)
        return _rope(x * r * g, c, s, mu, md, MLA_ROPE // 2)

    for h in range(MLA_HEADS):
        sl = slice(LANES * h, LANES * (h + 1))
        q_ref[:, sl] = (head(q_all[:, sl], qg_ref[...]) * scale).astype(BF16)
        k_ref[:, sl] = head(k_all[:, sl] + kpe, kg_ref[...]).astype(BF16)


def _mla_prep(u, q_norm, kv_norm, wq, wk, wv, q_gain, k_gain, tab_c, tab_s):
    m = u.shape[0]
    tm = ROW_TILE
    mu, md = _rope_masks(MLA_ROPE, LANES)
    full = lambda a: pl.BlockSpec(a.shape, lambda i: (0, 0))
    row = lambda w: pl.BlockSpec((tm, w), lambda i: (i, 0))
    args = (u, q_norm, kv_norm, wq, wk, wv, q_gain, k_gain, tab_c, tab_s, mu, md)
    in_specs = [row(1024)] + [full(a) for a in args[1:8]] + [row(LANES), row(LANES), full(mu), full(md)]
    return pl.pallas_call(
        _mla_prep_kernel,
        out_shape=(jax.ShapeDtypeStruct((m, 1024), BF16), jax.ShapeDtypeStruct((m, 1024), BF16),
                   jax.ShapeDtypeStruct((m, 512), BF16)),
        grid=(m // tm,),
        in_specs=in_specs,
        out_specs=(row(1024), row(1024), row(512)),
        compiler_params=_cparams("parallel"),
        name="mla_prep",
    )(*args)


def _mla_attn_kernel(q_ref, k_ref, v_ref, o_ref):
    i = pl.program_id(1)
    t = ATT_TILE
    lo = _lane((t, LANES)) < 64

    def rows(j):
        return pl.ds(pl.multiple_of(j * t, t), t)

    def tile(h):
        return slice(LANES * h, LANES * (h + 1))

    outs = _causal_flash([q_ref[0, :, tile(h)] for h in range(MLA_HEADS)],
                         lambda h, j: k_ref[0, rows(j), tile(h)],
                         lambda h, j: v_ref[0, rows(j), tile(h // 2)], i, t, t)
    for p in range(MLA_HEADS // 2):
        o_ref[0, :, tile(p)] = jnp.where(lo, outs[2 * p], outs[2 * p + 1]).astype(BF16)


def _mla_attn(q, k, v):
    b, s, _ = q.shape
    t = ATT_TILE
    return pl.pallas_call(
        _mla_attn_kernel,
        out_shape=jax.ShapeDtypeStruct((b, s, 512), BF16),
        grid=(b, s // t),
        in_specs=[pl.BlockSpec((1, t, 1024), lambda bi, i: (bi, i, 0)),
                  pl.BlockSpec((1, s, 1024), lambda bi, i: (bi, 0, 0)),
                  pl.BlockSpec((1, s, 512), lambda bi, i: (bi, 0, 0))],
        out_specs=pl.BlockSpec((1, t, 512), lambda bi, i: (bi, i, 0)),
        compiler_params=_cparams("parallel", "arbitrary"),
        name="mla_attention",
    )(q, k, v)


def _diff_prep_kernel(u_ref, g_ref, c_ref, s_ref, mu_ref, md_ref, q_ref, k_ref, v_ref):
    c, s, mu, md = c_ref[...], s_ref[...], mu_ref[...], md_ref[...]
    n = DIFF_HEADS * LANES
    for which, out in ((0, q_ref), (1, k_ref)):
        g = g_ref[which]
        for h in range(DIFF_HEADS):
            sl = slice(LANES * h, LANES * (h + 1))
            x = u_ref[:, which * n + LANES * h:which * n + LANES * (h + 1)]
            out[:, sl] = _rope(_rms64(x, g), c, s, mu, md, DIFF_ROT // 2).astype(BF16)
    v_ref[...] = u_ref[:, 2 * n:3 * n].astype(BF16)


def _diff_prep(u, gains, tab_c, tab_s):
    m = u.shape[0]
    tm = ROW_TILE
    mu, md = _rope_masks(DIFF_ROT, DIFF_DH)
    row = lambda w: pl.BlockSpec((tm, w), lambda i: (i, 0))
    vec = pl.BlockSpec((1, LANES), lambda i: (0, 0))
    shp = jax.ShapeDtypeStruct((m, 1024), BF16)
    return pl.pallas_call(
        _diff_prep_kernel,
        out_shape=(shp, shp, shp),
        grid=(m // tm,),
        in_specs=[row(3072), pl.BlockSpec((2, 1, LANES), lambda i: (0, 0, 0)), row(LANES), row(LANES), vec, vec],
        out_specs=(row(1024), row(1024), row(1024)),
        compiler_params=_cparams("parallel"),
        name="diff_prep",
    )(u, gains, tab_c, tab_s, mu, md)


def _diff_attn_kernel(lam_init, q_ref, k_ref, v_ref, lam_ref, sg_ref, o_ref):
    i = pl.program_id(1)
    t = ATT_TILE
    lo = _lane((t, LANES)) < 64
    lam = lam_ref[...]
    lmb = (jnp.exp(jnp.sum(lam[0:1] * lam[1:2], axis=-1, keepdims=True))
           - jnp.exp(jnp.sum(lam[2:3] * lam[3:4], axis=-1, keepdims=True)) + lam_init)

    def rows(j):
        return pl.ds(pl.multiple_of(j * t, t), t)

    def tile(h):
        return slice(LANES * h, LANES * (h + 1))

    def both_maps(h):
        qt = q_ref[0, :, tile(h)]
        zero = jnp.zeros_like(qt)
        return jnp.concatenate([jnp.where(lo, qt, zero), jnp.where(lo, zero, qt)], axis=0)

    group = DIFF_HEADS // 2
    for h0 in range(0, DIFF_HEADS, group):
        outs = _causal_flash([both_maps(h0 + e) for e in range(group)],
                             lambda e, j: k_ref[0, rows(j), tile(h0 + e)],
                             lambda e, j: v_ref[0, rows(j), tile(h0 + e)], i, t, t)
        for e in range(group):
            o = outs[e][:t] - lmb * outs[e][t:]
            o = o * lax.rsqrt(jnp.mean(o * o, axis=-1, keepdims=True) + EPS) * sg_ref[...]
            o_ref[0, :, tile(h0 + e)] = (o * (1.0 - lam_init)).astype(BF16)


def _diff_attn(q, k, v, lam, sub_gain, lam_init):
    b, s, n = q.shape
    t = ATT_TILE
    full = pl.BlockSpec((1, s, n), lambda bi, i: (bi, 0, 0))
    return pl.pallas_call(
        functools.partial(_diff_attn_kernel, lam_init),
        out_shape=jax.ShapeDtypeStruct((b, s, n), BF16),
        grid=(b, s // t),
        in_specs=[pl.BlockSpec((1, t, n), lambda bi, i: (bi, i, 0)), full, full,
                  pl.BlockSpec(lam.shape, lambda bi, i: (0, 0)),
                  pl.BlockSpec((1, LANES), lambda bi, i: (0, 0))],
        out_specs=pl.BlockSpec((1, t, n), lambda bi, i: (bi, i, 0)),
        compiler_params=_cparams("parallel", "arbitrary"),
        name="diff_attention",
    )(q, k, v, lam, sub_gain.reshape(1, LANES))


def _nsa_prep_kernel(u_ref, qg_ref, kg_ref, c_ref, s_ref, mu_ref, md_ref,
                     q_ref, kslc_ref, kwin_ref, vslc_ref, vwin_ref, kcmp_ref, vcmp_ref):
    c, s, mu, md = c_ref[...], s_ref[...], mu_ref[...], md_ref[...]

    def prep(x, g):
        return _rope(_rms64(x, g), c, s, mu, md, NSA_ROT // 2)

    for hg in range(NSA_HPG):
        sl = slice(LANES * hg, LANES * (hg + 1))
        q_ref[:, sl] = prep(u_ref[:, 768 + LANES * hg:768 + LANES * (hg + 1)], qg_ref[...]).astype(BF16)
    kcmp_ref[...] = u_ref[:, 1280:1408].astype(BF16)
    vcmp_ref[...] = u_ref[:, 1408:1536].astype(BF16)
    kslc_ref[...] = prep(u_ref[:, 1536:1664], kg_ref[1]).astype(BF16)
    vslc_ref[...] = u_ref[:, 1664:1792].astype(BF16)
    kwin_ref[...] = prep(u_ref[:, 1792:1920], kg_ref[2]).astype(BF16)
    vwin_ref[...] = u_ref[:, 1920:2048].astype(BF16)


def _nsa_prep(u, q_gain, k_gain, tab_c, tab_s):
    m = u.shape[0]
    tm = ROW_TILE
    mu, md = _rope_masks(NSA_ROT, NSA_DH)
    row = lambda w: pl.BlockSpec((tm, w), lambda i: (i, 0))
    vec = pl.BlockSpec((1, LANES), lambda i: (0, 0))
    t128 = jax.ShapeDtypeStruct((m, LANES), BF16)
    return pl.pallas_call(
        _nsa_prep_kernel,
        out_shape=(jax.ShapeDtypeStruct((m, 512), BF16),) + (t128,) * 6,
        grid=(m // tm,),
        in_specs=[row(2048), vec, pl.BlockSpec((3, 1, LANES), lambda i: (0, 0, 0)), row(LANES), row(LANES), vec, vec],
        out_specs=(row(512),) + (row(LANES),) * 6,
        compiler_params=_cparams("parallel"),
        name="nsa_prep",
    )(u, q_gain, k_gain, tab_c, tab_s, mu, md)


def _nsa_compress_kernel(tk_ref, tv_ref, pos_ref, w1a_ref, w1b_ref, w2_ref, kg_ref,
                         c_ref, s_ref, mu_ref, md_ref, kc_ref, vc_ref):
    n_rows = tk_ref.shape[1]
    for j, (t_ref, out) in enumerate(((tk_ref, kc_ref), (tv_ref, vc_ref))):
        tok = t_ref[0]
        p_hi, p_lo = _split_bf16(pos_ref[j])
        w1a, w1b = w1a_ref[j], w1b_ref[j]
        bias = (_dot(p_hi[0], w1a) + _dot(p_lo[0], w1a) + _dot(p_hi[1], w1b) + _dot(p_lo[1], w1b))[0:1]
        hid = _dot(tok, w1a) + pltpu.roll(_dot(tok, w1b), n_rows - 1, axis=0) + bias
        act = jax.nn.gelu(hid, approximate=True)
        cmp = _dot(act.astype(BF16), w2_ref[j])
        if j == 0:
            cmp = _rope(_rms64(cmp, kg_ref[...]), c_ref[0], s_ref[0], mu_ref[...], md_ref[...], NSA_ROT // 2)
        out[0] = cmp.astype(BF16)


def _nsa_compress(tk, tv, pos, w1a, w1b, w2, k_gain0, tab_c, tab_s):
    b, nr, w = tk.shape
    mu, md = _rope_masks(NSA_ROT, NSA_DH)
    full = lambda a: pl.BlockSpec(a.shape, lambda bi: (0,) * a.ndim)
    per_b = lambda a: pl.BlockSpec((1,) + a.shape[1:], lambda bi: (bi,) + (0,) * (a.ndim - 1))
    shp = jax.ShapeDtypeStruct((b, nr, LANES), BF16)
    return pl.pallas_call(
        _nsa_compress_kernel,
        out_shape=(shp, shp),
        grid=(b,),
        in_specs=[per_b(tk), per_b(tv), full(pos), full(w1a), full(w1b), full(w2), full(k_gain0),
                  per_b(tab_c), per_b(tab_s), full(mu), full(md)],
        out_specs=(pl.BlockSpec((1, nr, LANES), lambda bi: (bi, 0, 0)),) * 2,
        compiler_params=_cparams("parallel"),
        name="nsa_compress",
    )(tk, tv, pos, w1a, w1b, w2, k_gain0, tab_c, tab_s, mu, md)


def _nsa_attn_kernel(q_ref, kc_ref, vc_ref, ks_ref, vs_ref, kw_ref, vw_ref, gate_ref, cov_ref, o_ref):
    i = pl.program_id(1)
    t = ATT_TILE
    n_blk = 32
    q0 = i * t
    lo = _lane((t, LANES)) < 64
    qpos = q0 + lax.broadcasted_iota(jnp.int32, (t, LANES), 0)
    cmp_ok = NSA_CMP_STRIDE * _lane((t, LANES)) + (NSA_CMP_LEN - 1) <= qpos
    gates = _sigmoid(gate_ref[0])

    def rows(j):
        return pl.ds(pl.multiple_of(j * t, t), t)

    def tile_pos(j):
        kpos = j * t + lax.broadcasted_iota(jnp.int32, (t, t), 1)
        return kpos, q0 + lax.broadcasted_iota(jnp.int32, (t, t), 0)

    kc, vc = kc_ref[0], vc_ref[0]
    outs = [[None] * NSA_GROUPS for _ in range(NSA_HPG)]
    for g in range(NSA_GROUPS):
        qs = []
        for hg in range(NSA_HPG):
            qt = q_ref[0, :, LANES * hg:LANES * (hg + 1)]
            zero = jnp.zeros_like(qt)
            qs.append(jnp.where(lo, qt, zero) if g == 0 else jnp.where(lo, zero, qt))

        o_cmp, p_sum = [], jnp.zeros((t, LANES), F32)
        for hg in range(NSA_HPG):
            sc = jnp.where(cmp_ok, _dot_nt(qs[hg], kc), NEG)
            p = jnp.where(cmp_ok, jnp.exp2(sc - jnp.max(sc, axis=-1, keepdims=True)), 0.0)
            l = jnp.sum(p, axis=-1, keepdims=True)
            p = p * jnp.where(l > 0.0, 1.0 / l, 0.0)
            p_sum = p_sum + p
            o_cmp.append(_dot(p.astype(BF16), vc))
        p_hi, p_lo = _split_bf16(p_sum)
        imp = (_dot_nt(cov_ref[...], p_hi) + _dot_nt(cov_ref[...], p_lo))[0:n_blk]
        blk = lax.broadcasted_iota(jnp.int32, (n_blk, t), 0)
        jt = (q0 + lax.broadcasted_iota(jnp.int32, (n_blk, t), 1)) >> 6
        allowed = blk <= jt
        forced = allowed & ((blk == 0) | (blk >= jt - 1))
        imp = jnp.where(forced, FORCED, jnp.where(allowed, imp, NEG))
        rank = jnp.zeros((n_blk, t), jnp.int32)
        for jp in range(n_blk):
            other = imp[jp:jp + 1, :]
            ahead = (other > imp) | ((other == imp) & (blk > jp))
            rank = rank + ahead.astype(jnp.int32)
        sel_t = jnp.where((rank < NSA_TOP_N) & allowed, 1.0, 0.0)
        sel_t = jnp.concatenate([sel_t, jnp.zeros((LANES - n_blk, t), F32)], axis=0)
        sel = jnp.transpose(sel_t).astype(BF16)

        def slc_body(j, states):
            kpos, qp = tile_pos(j)
            blk_of_key = (j * t + lax.broadcasted_iota(jnp.int32, (LANES, t), 1)) >> 6
            expand = jnp.where(blk_of_key == lax.broadcasted_iota(jnp.int32, (LANES, t), 0), 1.0, 0.0).astype(BF16)
            mask = (_dot(sel, expand) > 0.5) & (kpos <= qp)
            k_t, v_t = ks_ref[0, rows(j), :], vs_ref[0, rows(j), :]
            return tuple(_flash_tile(states[hg], qs[hg], k_t, v_t, mask, True) for hg in range(NSA_HPG))

        slc = lax.fori_loop(0, i + 1, slc_body, tuple(_flash_init(t) for _ in range(NSA_HPG)))

        def win_body(j, states):
            kpos, qp = tile_pos(j)
            mask = (kpos <= qp) & (kpos > qp - NSA_WINDOW)
            k_t, v_t = kw_ref[0, rows(j), :], vw_ref[0, rows(j), :]
            return tuple(_flash_tile(states[hg], qs[hg], k_t, v_t, mask, True) for hg in range(NSA_HPG))

        first = jnp.maximum(i - NSA_WINDOW // t, 0)
        win = lax.fori_loop(first, i + 1, win_body, tuple(_flash_init(t) for _ in range(NSA_HPG)))

        for hg in range(NSA_HPG):
            head = g * NSA_HPG + hg
            g_cmp = gates[:, head:head + 1]
            g_slc = gates[:, NSA_HEADS + head:NSA_HEADS + head + 1]
            g_win = gates[:, 2 * NSA_HEADS + head:2 * NSA_HEADS + head + 1]
            outs[hg][g] = g_cmp * o_cmp[hg] + g_slc * _flash_out(slc[hg]) + g_win * _flash_out(win[hg])

    for hg in range(NSA_HPG):
        o_ref[0, :, LANES * hg:LANES * (hg + 1)] = jnp.where(lo, outs[hg][0], outs[hg][1]).astype(BF16)


def _nsa_attn(q, kc, vc, kslc, vslc, kwin, vwin, u3, cover_t):
    b, s, _ = q.shape
    t = ATT_TILE
    seq = pl.BlockSpec((1, s, LANES), lambda bi, i: (bi, 0, 0))
    cmp = pl.BlockSpec((1, LANES, LANES), lambda bi, i: (bi, 0, 0))
    return pl.pallas_call(
        _nsa_attn_kernel,
        out_shape=jax.ShapeDtypeStruct((b, s, 512), BF16),
        grid=(b, s // t),
        in_specs=[pl.BlockSpec((1, t, 512), lambda bi, i: (bi, i, 0)), cmp, cmp, seq, seq, seq, seq,
                  pl.BlockSpec((1, t, LANES), lambda bi, i: (bi, i, 5)),
                  pl.BlockSpec((LANES, LANES), lambda bi, i: (0, 0))],
        out_specs=pl.BlockSpec((1, t, 512), lambda bi, i: (bi, i, 0)),
        compiler_params=_cparams("parallel", "arbitrary"),
        name="nsa_attention",
    )(q, kc, vc, kslc, vslc, kwin, vwin, u3, cover_t)


def _pad_lanes(a, width):
    return jnp.pad(a, [(0, 0)] * (a.ndim - 1) + [(0, width - a.shape[-1])])


def _hy_in_weight(w):
    d = w.shape[0]
    cq, ckv, kpe, nq, nkv, gate = jnp.split(w, [256, 512, 544, 1056, 1824], axis=1)
    nq = nq.reshape(d, NSA_GROUPS, NSA_HPG, NSA_DH).transpose(0, 2, 1, 3).reshape(d, 512)
    return jnp.concatenate([cq, ckv, _pad_lanes(kpe, LANES), _pad_lanes(gate, LANES), nq, nkv], axis=1).astype(BF16)


def _cover_t():
    nc, ns = 127, 32
    c_start = np.arange(nc) * NSA_CMP_STRIDE
    c_end = c_start + NSA_CMP_LEN - 1
    j_start = np.arange(ns) * NSA_SLC_LEN
    cover = ((c_start[:, None] <= j_start[None, :] + NSA_SLC_LEN - 1) & (c_end[:, None] >= j_start[None, :]))
    out = np.zeros((LANES, LANES), np.float32)
    out[:ns, :nc] = cover.T
    return jnp.asarray(out, BF16)


def _compress_weights(cmp_pos, w1, w2):
    half = NSA_CMP_STRIDE
    w1 = w1.reshape(2, 2, half, NSA_DH, NSA_CMP_HIDDEN)
    eye = jnp.eye(NSA_GROUPS, dtype=F32)
    w1e = jnp.einsum('jcldn,gh->jclgdhn', w1, eye).reshape(2, 2, half * LANES, NSA_GROUPS * NSA_CMP_HIDDEN)
    w2e = jnp.einsum('jnd,gh->jgnhd', w2, eye).reshape(2, NSA_GROUPS * NSA_CMP_HIDDEN, LANES)
    pos = cmp_pos.reshape(2, 2, half, 1, NSA_DH)
    pos = jnp.broadcast_to(pos, (2, 2, half, NSA_GROUPS, NSA_DH)).reshape(2, 2, 1, half * LANES)
    pos = jnp.broadcast_to(pos, (2, 2, 8, half * LANES))
    return pos, w1e[:, 0].astype(BF16), w1e[:, 1].astype(BF16), w2e.astype(BF16)


def kernel(x, c, positions, ada_w, ada_b, norm_mix, norm_mlp, mlp_w1, mlp_w2, hy_w_in, hy_w_out, mla_q_norm, mla_w_uq, mla_kv_norm, mla_w_ukv, mla_q_gain, mla_k_gain, nsa_q_gain, nsa_k_gain, nsa_cmp_pos, nsa_cmp_w1, nsa_cmp_w2, diff_w_qkv, diff_w_out, diff_q_gain, diff_k_gain, diff_lambda, diff_sub_gain):
    nb, seq, d = x.shape
    depth = ada_w.shape[0]
    m = nb * seq
    n_cmp = seq // NSA_CMP_STRIDE

    mod = _adaln(c, ada_w, ada_b).reshape(depth * nb, 1, 6 * d)
    tab_ca, tab_sa, tab_cb, tab_sb = _rope_tables(positions.reshape(m, 1))
    pos_c = jnp.pad(positions[:, NSA_CMP_LEN - 1::NSA_CMP_STRIDE], ((0, 0), (0, 1)))
    _, _, tab_cc, tab_sc = _rope_tables(pos_c.reshape(nb * n_cmp, 1))
    tab_cc, tab_sc = tab_cc.reshape(nb, n_cmp, LANES), tab_sc.reshape(nb, n_cmp, LANES)
    cover_t = _cover_t()

    x2 = x.reshape(m, d)
    for i in range(depth):
        j = i // 2
        if i % 2 == 0:
            u = _proj(x2, norm_mix[i], mod, i, 1, 0, _hy_in_weight(hy_w_in[j]), nb, 1024)

            wq = _pad_lanes(mla_w_uq[j].reshape(MLA_Q_RANK, MLA_HEADS, MLA_DK), LANES).reshape(MLA_Q_RANK, -1)
            wkv = mla_w_ukv[j].reshape(MLA_KV_RANK, MLA_HEADS, MLA_NOPE + MLA_V)
            wk = jnp.pad(wkv[..., :MLA_NOPE], ((0, 0), (0, 0), (MLA_ROPE, LANES - MLA_DK))).reshape(MLA_KV_RANK, -1)
            wv = wkv[..., MLA_NOPE:].reshape(MLA_KV_RANK, -1)
            q_mla, k_mla, v_mla = _mla_prep(
                u, mla_q_norm[j].reshape(1, -1), mla_kv_norm[j].reshape(1, -1),
                wq.astype(BF16), wk.astype(BF16), wv.astype(BF16),
                _pad_lanes(mla_q_gain[j], LANES).reshape(1, LANES), _pad_lanes(mla_k_gain[j], LANES).reshape(1, LANES),
                tab_ca, tab_sa)
            y_mla = _mla_attn(q_mla.reshape(nb, seq, -1), k_mla.reshape(nb, seq, -1), v_mla.reshape(nb, seq, -1))

            qg = jnp.tile(nsa_q_gain[j] * (NSA_DH ** -0.5 * LOG2E), 2).reshape(1, LANES)
            kg = jnp.tile(nsa_k_gain[j], (1, 2)).reshape(3, 1, LANES)
            q_nsa, kslc, kwin, vslc, vwin, kcmp, vcmp = _nsa_prep(u, qg, kg, tab_cb, tab_sb)
            pos_e, w1a, w1b, w2e = _compress_weights(nsa_cmp_pos[j], nsa_cmp_w1[j], nsa_cmp_w2[j])
            chunks = lambda a: a.reshape(nb, n_cmp, NSA_CMP_STRIDE * LANES)
            kc, vc = _nsa_compress(chunks(kcmp), chunks(vcmp), pos_e, w1a, w1b, w2e, kg[0], tab_cc, tab_sc)
            seq3 = lambda a: a.reshape(nb, seq, -1)
            y_nsa = _nsa_attn(seq3(q_nsa), kc, vc, seq3(kslc), seq3(vslc), seq3(kwin), seq3(vwin),
                              seq3(u), cover_t)

            w_out = hy_w_out[j]
            w_nsa = w_out[512:].reshape(NSA_GROUPS, NSA_HPG, NSA_DH, d).transpose(1, 0, 2, 3).reshape(512, d)
            x2 = _out_proj(x2, mod, i, [y_mla.reshape(m, -1), y_nsa.reshape(m, -1)],
                           [w_out[:512].astype(BF16), w_nsa.astype(BF16)], nb)
        else:
            lam_init = 0.8 - 0.6 * math.exp(-0.3 * i)
            u = _proj(x2, norm_mix[i], mod, i, 1, 0, diff_w_qkv[j].astype(BF16), nb, 1024)
            gains = jnp.stack([jnp.tile(diff_q_gain[j] * (DIFF_DH ** -0.5 * LOG2E), 2), jnp.tile(diff_k_gain[j], 2)])
            q, k, v = _diff_prep(u, gains.reshape(2, 1, LANES), tab_cb, tab_sb)
            seq3 = lambda a: a.reshape(nb, seq, -1)
            y = _diff_attn(seq3(q), seq3(k), seq3(v), diff_lambda[j], diff_sub_gain[j], lam_init)
            x2 = _out_proj(x2, mod, i, [y.reshape(m, -1)], [diff_w_out[j].astype(BF16)], nb)
        x2 = _mlp(x2, norm_mlp[i], mod, i, mlp_w1[i].astype(BF16), mlp_w2[i].astype(BF16), nb)
    return x2.reshape(nb, seq, d)
```

```python
import functools
import math

import numpy as np
import jax
import jax.numpy as jnp
from jax import lax
from jax.experimental import pallas as pl
from jax.experimental.pallas import tpu as pltpu

F32 = jnp.float32
BF16 = jnp.bfloat16

LANES = 128
VMEM_LIMIT = 52 * 1024 * 1024

ROPE_THETA = 500000.0
EPS = 1e-6
NEG = -1e30
FORCED = 1e9

MLA_HEADS, MLA_NOPE, MLA_ROPE, MLA_V = 8, 64, 32, 64
MLA_Q_RANK, MLA_KV_RANK = 256, 256
MLA_DK = MLA_ROPE + MLA_NOPE
NSA_HEADS, NSA_GROUPS, NSA_DH = 8, 2, 64
NSA_HPG = NSA_HEADS // NSA_GROUPS
NSA_ROT = NSA_DH // 4
NSA_CMP_LEN, NSA_CMP_STRIDE, NSA_CMP_HIDDEN = 32, 16, 128
NSA_SLC_LEN, NSA_TOP_N, NSA_WINDOW = 64, 16, 512
DIFF_HEADS, DIFF_DH = 8, 64
DIFF_ROT = DIFF_DH // 4

ROW_TILE = 512
ATT_TILE = 256


def _cparams(*sem):
    return pltpu.CompilerParams(dimension_semantics=sem, vmem_limit_bytes=VMEM_LIMIT)


def _split_bf16(x):
    hi = x.astype(BF16)
    lo = (x - hi.astype(F32)).astype(BF16)
    return hi, lo


def _dot(a, b):
    return jnp.dot(a, b, preferred_element_type=F32)


def _dot_nt(a, b):
    return lax.dot_general(a, b, (((1,), (1,)), ((), ())), preferred_element_type=F32)


def _sigmoid(x):
    return 1.0 / (1.0 + jnp.exp(-x))


def _lane(shape):
    return lax.broadcasted_iota(jnp.int32, shape, 1)


def _adaln_kernel(c_ref, w_ref, b_ref, o_ref):
    c = c_ref[...]
    cond = c * _sigmoid(c)
    c_hi, c_lo = _split_bf16(cond)
    w_hi, w_lo = _split_bf16(w_ref[0])
    o_ref[0] = _dot(c_hi, w_hi) + _dot(c_hi, w_lo) + _dot(c_lo, w_hi) + b_ref[0]


def _adaln(c, ada_w, ada_b):
    depth, d, n = ada_w.shape
    b = c.shape[0]
    tn = 1536
    return pl.pallas_call(
        _adaln_kernel,
        out_shape=jax.ShapeDtypeStruct((depth, b, n), F32),
        grid=(depth, n // tn),
        in_specs=[pl.BlockSpec((b, d), lambda i, j: (0, 0)),
                  pl.BlockSpec((1, d, tn), lambda i, j: (i, 0, j)),
                  pl.BlockSpec((1, 1, tn), lambda i, j: (i, 0, j))],
        out_specs=pl.BlockSpec((1, b, tn), lambda i, j: (i, 0, j)),
        compiler_params=_cparams("parallel", "parallel"),
        name="adaln",
    )(c, ada_w, ada_b.reshape(depth, 1, n))


def _rope_table_kernel(pos_ref, fa_ref, fb_ref, ca_ref, sa_ref, cb_ref, sb_ref):
    pos = pos_ref[...].astype(F32)
    ang_a = pos * fa_ref[...]
    ang_b = pos * fb_ref[...]
    ca_ref[...] = jnp.cos(ang_a)
    sa_ref[...] = jnp.sin(ang_a)
    cb_ref[...] = jnp.cos(ang_b)
    sb_ref[...] = jnp.sin(ang_b)


def _lane_freqs(rot, seg):
    half = rot // 2
    lane = np.arange(LANES)
    inv = ROPE_THETA ** (-(np.arange(half, dtype=np.float32) / np.float32(half)))
    f = np.where(lane % seg < rot, inv.astype(np.float32)[lane % half], 0.0)
    return jnp.asarray(f.reshape(1, LANES), F32)


def _rope_masks(rot, seg):
    half = rot // 2
    lane = np.arange(LANES) % seg
    up = ((lane >= half) & (lane < rot)).astype(np.float32)
    dn = -(lane < half).astype(np.float32)
    return jnp.asarray(up.reshape(1, LANES)), jnp.asarray(dn.reshape(1, LANES))


def _rope_tables(pos_col):
    rows = pos_col.shape[0]
    tr = min(rows, 2048)
    spec = pl.BlockSpec((tr, LANES), lambda i: (i, 0))
    vec = pl.BlockSpec((1, LANES), lambda i: (0, 0))
    shp = jax.ShapeDtypeStruct((rows, LANES), F32)
    return pl.pallas_call(
        _rope_table_kernel,
        out_shape=(shp, shp, shp, shp),
        grid=(rows // tr,),
        in_specs=[pl.BlockSpec((tr, 1), lambda i: (i, 0)), vec, vec],
        out_specs=(spec, spec, spec, spec),
        compiler_params=_cparams("parallel"),
        name="rope_tables",
    )(pos_col, _lane_freqs(MLA_ROPE, LANES), _lane_freqs(NSA_ROT, NSA_DH))


def _rope(y, c, s, m_up, m_dn, half):
    up = pltpu.roll(y, half, axis=1)
    dn = pltpu.roll(y, LANES - half, axis=1)
    return y * c + (up * m_up + dn * m_dn) * s


def _rms64(x, gain):
    lo = _lane(x.shape) < 64
    x2 = x * x
    s_lo = jnp.sum(jnp.where(lo, x2, 0.0), axis=-1, keepdims=True)
    s_hi = jnp.sum(jnp.where(lo, 0.0, x2), axis=-1, keepdims=True)
    r = jnp.where(lo, lax.rsqrt(s_lo * (1.0 / 64) + EPS), lax.rsqrt(s_hi * (1.0 / 64) + EPS))
    return x * r * gain


def _norm_mod(x, gain, sc, sh):
    ms = jnp.mean(x * x, axis=-1, keepdims=True)
    return (x * lax.rsqrt(ms + EPS) * gain) * (1.0 + sc) + sh


def _proj_kernel(x_ref, gain_ref, sc_ref, sh_ref, w_ref, o_ref, h_scr):
    @pl.when(pl.program_id(1) == 0)
    def _():
        h_scr[...] = _norm_mod(x_ref[...], gain_ref[...], sc_ref[0], sh_ref[0]).astype(BF16)

    o_ref[...] = _dot(h_scr[...], w_ref[...]).astype(o_ref.dtype)


def _mod_spec(layer, k, nb, rows_per_batch, tm, d):
    per = rows_per_batch // tm
    return pl.BlockSpec((1, 1, d), lambda i, *_: (layer * nb + i // per, 0, k))


def _proj(x2, gain, mod, layer, k_sc, k_sh, w, nb, tn):
    m, d = x2.shape
    n = w.shape[1]
    tm = 1024
    s = m // nb
    return pl.pallas_call(
        _proj_kernel,
        out_shape=jax.ShapeDtypeStruct((m, n), F32),
        grid=(m // tm, n // tn),
        in_specs=[pl.BlockSpec((tm, d), lambda i, j: (i, 0)),
                  pl.BlockSpec((1, d), lambda i, j: (0, 0)),
                  _mod_spec(layer, k_sc, nb, s, tm, d),
                  _mod_spec(layer, k_sh, nb, s, tm, d),
                  pl.BlockSpec((d, tn), lambda i, j: (0, j))],
        out_specs=pl.BlockSpec((tm, tn), lambda i, j: (i, j)),
        scratch_shapes=[pltpu.VMEM((tm, d), BF16)],
        compiler_params=_cparams("parallel", "arbitrary"),
        name="norm_mod_proj",
    )(x2, gain.reshape(1, d), mod, mod, w)


def _mlp_kernel(x_ref, gain_ref, sc_ref, sh_ref, g_ref, w1_ref, w2_ref, o_ref, h_scr, acc_scr):
    j = pl.program_id(1)

    @pl.when(j == 0)
    def _():
        h_scr[...] = _norm_mod(x_ref[...], gain_ref[...], sc_ref[0], sh_ref[0]).astype(BF16)
        acc_scr[...] = jnp.zeros_like(acc_scr)

    a = jnp.maximum(_dot(h_scr[...], w1_ref[...]), 0.0)
    acc_scr[...] += _dot((a * a).astype(BF16), w2_ref[...])

    @pl.when(j == pl.num_programs(1) - 1)
    def _():
        o_ref[...] = x_ref[...] + g_ref[0] * acc_scr[...]


def _mlp(x2, gain, mod, layer, w1, w2, nb):
    m, d = x2.shape
    ff = w1.shape[1]
    tm, tf = 1024, 512
    s = m // nb
    return pl.pallas_call(
        _mlp_kernel,
        out_shape=jax.ShapeDtypeStruct((m, d), F32),
        grid=(m // tm, ff // tf),
        in_specs=[pl.BlockSpec((tm, d), lambda i, j: (i, 0)),
                  pl.BlockSpec((1, d), lambda i, j: (0, 0)),
                  _mod_spec(layer, 4, nb, s, tm, d),
                  _mod_spec(layer, 3, nb, s, tm, d),
                  _mod_spec(layer, 5, nb, s, tm, d),
                  pl.BlockSpec((d, tf), lambda i, j: (0, j)),
                  pl.BlockSpec((tf, d), lambda i, j: (j, 0))],
        out_specs=pl.BlockSpec((tm, d), lambda i, j: (i, 0)),
        scratch_shapes=[pltpu.VMEM((tm, d), BF16), pltpu.VMEM((tm, d), F32)],
        compiler_params=_cparams("parallel", "arbitrary"),
        name="relu2_mlp",
    )(x2, gain.reshape(1, d), mod, mod, mod, w1, w2)


def _out_proj_kernel(n_in, *refs):
    x_ref, g_ref = refs[0], refs[1]
    y_refs = refs[2:2 + n_in]
    w_refs = refs[2 + n_in:2 + 2 * n_in]
    o_ref = refs[2 + 2 * n_in]
    y = _dot(y_refs[0][...], w_refs[0][...])
    for y_ref, w_ref in zip(y_refs[1:], w_refs[1:]):
        y = y + _dot(y_ref[...], w_ref[...])
    o_ref[...] = x_ref[...] + g_ref[0] * y


def _out_proj(x2, mod, layer, ys, ws, nb):
    m, d = x2.shape
    tm = ROW_TILE
    s = m // nb
    in_specs = [pl.BlockSpec((tm, d), lambda i: (i, 0)), _mod_spec(layer, 2, nb, s, tm, d)]
    in_specs += [pl.BlockSpec((tm, y.shape[1]), lambda i: (i, 0)) for y in ys]
    in_specs += [pl.BlockSpec(w.shape, lambda i: (0, 0)) for w in ws]
    return pl.pallas_call(
        functools.partial(_out_proj_kernel, len(ys)),
        out_shape=jax.ShapeDtypeStruct((m, d), F32),
        grid=(m // tm,),
        in_specs=in_specs,
        out_specs=pl.BlockSpec((tm, d), lambda i: (i, 0)),
        compiler_params=_cparams("parallel"),
        name="out_proj_residual",
    )(x2, mod, *ys, *ws)


LOG2E = math.log2(math.e)


def _flash_init(rows):
    return (jnp.full((rows, 1), NEG, F32), jnp.zeros((rows, LANES), F32), jnp.zeros((rows, LANES), F32))


def _flash_tile(state, q, k_t, v_t, mask=None, may_be_empty=False):
    m, l, acc = state
    s = _dot_nt(q, k_t)
    if mask is not None:
        s = jnp.where(mask, s, NEG)
    m_new = jnp.maximum(m, jnp.max(s, axis=-1, keepdims=True))
    alpha = jnp.exp2(m - m_new)
    p = jnp.exp2(s - m_new)
    if may_be_empty:
        p = jnp.where(mask, p, 0.0)
    p_lanes = p[:, 0:LANES]
    for c in range(1, p.shape[1] // LANES):
        p_lanes = p_lanes + p[:, LANES * c:LANES * (c + 1)]
    l = alpha * l + p_lanes
    acc = alpha * acc + _dot(p.astype(BF16), v_t)
    return m_new, l, acc


def _flash_out(state):
    m, l, acc = state
    return acc / jnp.sum(l, axis=-1, keepdims=True)


def _lane_tiles(x):
    return [x[:, LANES * c:LANES * (c + 1)] for c in range(x.shape[1] // LANES)]


def _causal_flash(qs, k_at, v_at, i, t, rows_per_map):
    n = len(qs)
    rows = qs[0].shape[0]
    r = lax.broadcasted_iota(jnp.int32, (rows, t), 0)
    if rows != rows_per_map:
        r = jnp.where(r >= rows_per_map, r - rows_per_map, r)
    mask = lax.broadcasted_iota(jnp.int32, (rows, t), 1) <= r

    def scores(h, j, diagonal):
        s = _dot_nt(qs[h], k_at(h, j))
        return jnp.where(mask, s, NEG) if diagonal else s

    def max_step(ms, j, diagonal):
        out = []
        for h in range(n):
            m = ms[h]
            for s_c in _lane_tiles(scores(h, j, diagonal)):
                m = jnp.maximum(m, s_c)
            out.append(m)
        return tuple(out)

    ms = tuple(jnp.full((rows, LANES), NEG, F32) for _ in range(n))
    ms = max_step(lax.fori_loop(0, i, lambda j, st: max_step(st, j, False), ms), i, True)
    ms = [jnp.broadcast_to(jnp.max(m, axis=-1, keepdims=True), (rows, LANES)) for m in ms]

    def sum_step(states, j, diagonal):
        out = []
        for h in range(n):
            l, acc = states[h]
            ps = [jnp.exp2(s_c - ms[h]) for s_c in _lane_tiles(scores(h, j, diagonal))]
            for p_c in ps:
                l = l + p_c
            acc = acc + _dot(jnp.concatenate(ps, axis=1).astype(BF16), v_at(h, j))
            out.append((l, acc))
        return tuple(out)

    zero = jnp.zeros((rows, LANES), F32)
    states = tuple((zero, zero) for _ in range(n))
    states = sum_step(lax.fori_loop(0, i, lambda j, st: sum_step(st, j, False), states), i, True)
    return [acc / jnp.sum(l, axis=-1, keepdims=True) for l, acc in states]


def _mla_prep_kernel(u_ref, qn_ref, kvn_ref, wq_ref, wk_ref, wv_ref, qg_ref, kg_ref,
                     c_ref, s_ref, mu_ref, md_ref, q_ref, k_ref, v_ref):
    def rms(x, g):
        return x * lax.rsqrt(jnp.mean(x * x, axis=-1, keepdims=True) + EPS) * g

    cq = rms(u_ref[:, 0:256], qn_ref[...]).astype(BF16)
    ckv = rms(u_ref[:, 256:512], kvn_ref[...]).astype(BF16)
    kpe = u_ref[:, 512:640]
    v_ref[...] = _dot(ckv, wv_ref[...]).astype(BF16)
    q_all = _dot(cq, wq_ref[...])
    k_all = _dot(ckv, wk_ref[...])
    c, s, mu, md = c_ref[...], s_ref[...], mu_ref[...], md_ref[...]
    scale = MLA_DK ** -0.5 * LOG2E

    def head(x, g):
        r = lax.rsqrt(jnp.sum(x * x, axis=-1, keepdims=True) * (1.0 / MLA_DK) + EPS)
        return _rope(x * r * g, c, s, mu, md, MLA_ROPE // 2)

    for h in range(MLA_HEADS):
        sl = slice(LANES * h, LANES * (h + 1))
        q_ref[:, sl] = (head(q_all[:, sl], qg_ref[...]) * scale).astype(BF16)
        k_ref[:, sl] = head(k_all[:, sl] + kpe, kg_ref[...]).astype(BF16)


def _mla_prep(u, q_norm, kv_norm, wq, wk, wv, q_gain, k_gain, tab_c, tab_s):
    m = u.shape[0]
    tm = ROW_TILE
    mu, md = _rope_masks(MLA_ROPE, LANES)
    full = lambda a: pl.BlockSpec(a.shape, lambda i: (0, 0))
    row = lambda w: pl.BlockSpec((tm, w), lambda i: (i, 0))
    args = (u, q_norm, kv_norm, wq, wk, wv, q_gain, k_gain, tab_c, tab_s, mu, md)
    in_specs = [row(1024)] + [full(a) for a in args[1:8]] + [row(LANES), row(LANES), full(mu), full(md)]
    return pl.pallas_call(
        _mla_prep_kernel,
        out_shape=(jax.ShapeDtypeStruct((m, 1024), BF16), jax.ShapeDtypeStruct((m, 1024), BF16),
                   jax.ShapeDtypeStruct((m, 512), BF16)),
        grid=(m // tm,),
        in_specs=in_specs,
        out_specs=(row(1024), row(1024), row(512)),
        compiler_params=_cparams("parallel"),
        name="mla_prep",
    )(*args)


def _mla_attn_kernel(q_ref, k_ref, v_ref, o_ref):
    i = pl.program_id(1)
    t = ATT_TILE
    lo = _lane((t, LANES)) < 64

    def rows(j):
        return pl.ds(pl.multiple_of(j * t, t), t)

    def tile(h):
        return slice(LANES * h, LANES * (h + 1))

    outs = _causal_flash([q_ref[0, :, tile(h)] for h in range(MLA_HEADS)],
                         lambda h, j: k_ref[0, rows(j), tile(h)],
                         lambda h, j: v_ref[0, rows(j), tile(h // 2)], i, t, t)
    for p in range(MLA_HEADS // 2):
        o_ref[0, :, tile(p)] = jnp.where(lo, outs[2 * p], outs[2 * p + 1]).astype(BF16)


def _mla_attn(q, k, v):
    b, s, _ = q.shape
    t = ATT_TILE
    return pl.pallas_call(
        _mla_attn_kernel,
        out_shape=jax.ShapeDtypeStruct((b, s, 512), BF16),
        grid=(b, s // t),
        in_specs=[pl.BlockSpec((1, t, 1024), lambda bi, i: (bi, i, 0)),
                  pl.BlockSpec((1, s, 1024), lambda bi, i: (bi, 0, 0)),
                  pl.BlockSpec((1, s, 512), lambda bi, i: (bi, 0, 0))],
        out_specs=pl.BlockSpec((1, t, 512), lambda bi, i: (bi, i, 0)),
        compiler_params=_cparams("parallel", "arbitrary"),
        name="mla_attention",
    )(q, k, v)


def _diff_prep_kernel(u_ref, g_ref, c_ref, s_ref, mu_ref, md_ref, q_ref, k_ref, v_ref):
    c, s, mu, md = c_ref[...], s_ref[...], mu_ref[...], md_ref[...]
    n = DIFF_HEADS * LANES
    for which, out in ((0, q_ref), (1, k_ref)):
        g = g_ref[which]
        for h in range(DIFF_HEADS):
            sl = slice(LANES * h, LANES * (h + 1))
            x = u_ref[:, which * n + LANES * h:which * n + LANES * (h + 1)]
            out[:, sl] = _rope(_rms64(x, g), c, s, mu, md, DIFF_ROT // 2).astype(BF16)
    v_ref[...] = u_ref[:, 2 * n:3 * n].astype(BF16)


def _diff_prep(u, gains, tab_c, tab_s):
    m = u.shape[0]
    tm = ROW_TILE
    mu, md = _rope_masks(DIFF_ROT, DIFF_DH)
    row = lambda w: pl.BlockSpec((tm, w), lambda i: (i, 0))
    vec = pl.BlockSpec((1, LANES), lambda i: (0, 0))
    shp = jax.ShapeDtypeStruct((m, 1024), BF16)
    return pl.pallas_call(
        _diff_prep_kernel,
        out_shape=(shp, shp, shp),
        grid=(m // tm,),
        in_specs=[row(3072), pl.BlockSpec((2, 1, LANES), lambda i: (0, 0, 0)), row(LANES), row(LANES), vec, vec],
        out_specs=(row(1024), row(1024), row(1024)),
        compiler_params=_cparams("parallel"),
        name="diff_prep",
    )(u, gains, tab_c, tab_s, mu, md)


def _diff_attn_kernel(lam_init, q_ref, k_ref, v_ref, lam_ref, sg_ref, o_ref):
    i = pl.program_id(1)
    t = ATT_TILE
    lo = _lane((t, LANES)) < 64
    lam = lam_ref[...]
    lmb = (jnp.exp(jnp.sum(lam[0:1] * lam[1:2], axis=-1, keepdims=True))
           - jnp.exp(jnp.sum(lam[2:3] * lam[3:4], axis=-1, keepdims=True)) + lam_init)

    def rows(j):
        return pl.ds(pl.multiple_of(j * t, t), t)

    def tile(h):
        return slice(LANES * h, LANES * (h + 1))

    def both_maps(h):
        qt = q_ref[0, :, tile(h)]
        zero = jnp.zeros_like(qt)
        return jnp.concatenate([jnp.where(lo, qt, zero), jnp.where(lo, zero, qt)], axis=0)

    group = DIFF_HEADS // 2
    for h0 in range(0, DIFF_HEADS, group):
        outs = _causal_flash([both_maps(h0 + e) for e in range(group)],
                             lambda e, j: k_ref[0, rows(j), tile(h0 + e)],
                             lambda e, j: v_ref[0, rows(j), tile(h0 + e)], i, t, t)
        for e in range(group):
            o = outs[e][:t] - lmb * outs[e][t:]
            o = o * lax.rsqrt(jnp.mean(o * o, axis=-1, keepdims=True) + EPS) * sg_ref[...]
            o_ref[0, :, tile(h0 + e)] = (o * (1.0 - lam_init)).astype(BF16)


def _diff_attn(q, k, v, lam, sub_gain, lam_init):
    b, s, n = q.shape
    t = ATT_TILE
    full = pl.BlockSpec((1, s, n), lambda bi, i: (bi, 0, 0))
    return pl.pallas_call(
        functools.partial(_diff_attn_kernel, lam_init),
        out_shape=jax.ShapeDtypeStruct((b, s, n), BF16),
        grid=(b, s // t),
        in_specs=[pl.BlockSpec((1, t, n), lambda bi, i: (bi, i, 0)), full, full,
                  pl.BlockSpec(lam.shape, lambda bi, i: (0, 0)),
                  pl.BlockSpec((1, LANES), lambda bi, i: (0, 0))],
        out_specs=pl.BlockSpec((1, t, n), lambda bi, i: (bi, i, 0)),
        compiler_params=_cparams("parallel", "arbitrary"),
        name="diff_attention",
    )(q, k, v, lam, sub_gain.reshape(1, LANES))


def _nsa_prep_kernel(seq, u_ref, qg_ref, kg_ref, c_ref, s_ref, mu_ref, md_ref,
                     q_ref, kslc0_ref, kslc1_ref, kwin_ref, vslc_ref, vwin_ref, kcmp_ref, vcmp_ref):
    c, s, mu, md = c_ref[...], s_ref[...], mu_ref[...], md_ref[...]
    tm = u_ref.shape[0]

    def prep(x, g):
        return _rope(_rms64(x, g), c, s, mu, md, NSA_ROT // 2)

    for hg in range(NSA_HPG):
        sl = slice(LANES * hg, LANES * (hg + 1))
        q_ref[:, sl] = prep(u_ref[:, 768 + LANES * hg:768 + LANES * (hg + 1)], qg_ref[...]).astype(BF16)
    kcmp_ref[...] = u_ref[:, 1280:1408].astype(BF16)
    vcmp_ref[...] = u_ref[:, 1408:1536].astype(BF16)
    kslc = prep(u_ref[:, 1536:1664], kg_ref[1])
    lane = _lane((tm, LANES))
    t_idx = (pl.program_id(0) * tm + lax.broadcasted_iota(jnp.int32, (tm, LANES), 0)) & (seq - 1)
    blk = t_idx >> 6
    kslc0_ref[...] = jnp.where(lane < 64, kslc, jnp.where(lane - 64 == blk, 1.0, 0.0)).astype(BF16)
    kslc1_ref[...] = jnp.where(lane >= 64, kslc, jnp.where(lane == blk, 1.0, 0.0)).astype(BF16)
    vslc_ref[...] = u_ref[:, 1664:1792].astype(BF16)
    kwin_ref[...] = prep(u_ref[:, 1792:1920], kg_ref[2]).astype(BF16)
    vwin_ref[...] = u_ref[:, 1920:2048].astype(BF16)


def _nsa_prep(u, q_gain, k_gain, tab_c, tab_s, seq):
    m = u.shape[0]
    tm = ROW_TILE
    assert seq & (seq - 1) == 0 and seq // NSA_SLC_LEN == 32 and seq % tm == 0
    mu, md = _rope_masks(NSA_ROT, NSA_DH)
    row = lambda w: pl.BlockSpec((tm, w), lambda i: (i, 0))
    vec = pl.BlockSpec((1, LANES), lambda i: (0, 0))
    t128 = jax.ShapeDtypeStruct((m, LANES), BF16)
    return pl.pallas_call(
        functools.partial(_nsa_prep_kernel, seq),
        out_shape=(jax.ShapeDtypeStruct((m, 512), BF16),) + (t128,) * 7,
        grid=(m // tm,),
        in_specs=[row(2048), vec, pl.BlockSpec((3, 1, LANES), lambda i: (0, 0, 0)), row(LANES), row(LANES), vec, vec],
        out_specs=(row(512),) + (row(LANES),) * 7,
        compiler_params=_cparams("parallel"),
        name="nsa_prep",
    )(u, q_gain, k_gain, tab_c, tab_s, mu, md)


def _nsa_compress_kernel(tk_ref, tv_ref, pos_ref, w1a_ref, w1b_ref, w2_ref, kg_ref,
                         c_ref, s_ref, mu_ref, md_ref, kc_ref, vc_ref):
    n_rows = tk_ref.shape[1]
    for j, (t_ref, out) in enumerate(((tk_ref, kc_ref), (tv_ref, vc_ref))):
        tok = t_ref[0]
        p_hi, p_lo = _split_bf16(pos_ref[j])
        w1a, w1b = w1a_ref[j], w1b_ref[j]
        bias = (_dot(p_hi[0], w1a) + _dot(p_lo[0], w1a) + _dot(p_hi[1], w1b) + _dot(p_lo[1], w1b))[0:1]
        hid = _dot(tok, w1a) + pltpu.roll(_dot(tok, w1b), n_rows - 1, axis=0) + bias
        act = jax.nn.gelu(hid, approximate=True)
        cmp = _dot(act.astype(BF16), w2_ref[j])
        if j == 0:
            cmp = _rope(_rms64(cmp, kg_ref[...]), c_ref[0], s_ref[0], mu_ref[...], md_ref[...], NSA_ROT // 2)
        out[0] = cmp.astype(BF16)


def _nsa_compress(tk, tv, pos, w1a, w1b, w2, k_gain0, tab_c, tab_s):
    b, nr, w = tk.shape
    mu, md = _rope_masks(NSA_ROT, NSA_DH)
    full = lambda a: pl.BlockSpec(a.shape, lambda bi: (0,) * a.ndim)
    per_b = lambda a: pl.BlockSpec((1,) + a.shape[1:], lambda bi: (bi,) + (0,) * (a.ndim - 1))
    shp = jax.ShapeDtypeStruct((b, nr, LANES), BF16)
    return pl.pallas_call(
        _nsa_compress_kernel,
        out_shape=(shp, shp),
        grid=(b,),
        in_specs=[per_b(tk), per_b(tv), full(pos), full(w1a), full(w1b), full(w2), full(k_gain0),
                  per_b(tab_c), per_b(tab_s), full(mu), full(md)],
        out_specs=(pl.BlockSpec((1, nr, LANES), lambda bi: (bi, 0, 0)),) * 2,
        compiler_params=_cparams("parallel"),
        name="nsa_compress",
    )(tk, tv, pos, w1a, w1b, w2, k_gain0, tab_c, tab_s, mu, md)


def _nsa_attn_kernel(q_ref, kc_ref, vc_ref, ks0_ref, ks1_ref, vs_ref, kw_ref, vw_ref, gate_ref, cov_ref, o_ref):
    i = pl.program_id(1)
    t = ATT_TILE
    n_blk = 32
    q0 = i * t
    lane = _lane((t, LANES))
    lo = lane < 64
    qpos = q0 + lax.broadcasted_iota(jnp.int32, (t, LANES), 0)
    cmp_ok = NSA_CMP_STRIDE * lane + (NSA_CMP_LEN - 1) <= qpos
    gates = _sigmoid(gate_ref[0])

    def rows(j):
        return pl.ds(pl.multiple_of(j * t, t), t)

    assert NSA_WINDOW == 2 * t
    r_i = lax.broadcasted_iota(jnp.int32, (t, t), 0)
    c_i = lax.broadcasted_iota(jnp.int32, (t, t), 1)
    win_masks = (c_i <= r_i, (i - 1) * t + c_i >= 0, (c_i > r_i) & ((i - 2) * t + c_i >= 0))
    win_tiles = (i, jnp.maximum(i - 1, 0), jnp.maximum(i - 2, 0))
    win_k = [kw_ref[0, rows(j), :] for j in win_tiles]
    win_v = [vw_ref[0, rows(j), :] for j in win_tiles]

    def window(q):
        ss = [jnp.where(mk, _dot_nt(q, k_t), NEG) for mk, k_t in zip(win_masks, win_k)]
        m = jnp.full((t, LANES), NEG, F32)
        for s_t in ss:
            for s_c in _lane_tiles(s_t):
                m = jnp.maximum(m, s_c)
        m = jnp.broadcast_to(jnp.max(m, axis=-1, keepdims=True), (t, LANES))
        l, acc = jnp.zeros((t, LANES), F32), jnp.zeros((t, LANES), F32)
        for s_t, v_t in zip(ss, win_v):
            ps = [jnp.exp2(s_c - m) for s_c in _lane_tiles(s_t)]
            for p_c in ps:
                l = l + p_c
            acc = acc + _dot(jnp.concatenate(ps, axis=1).astype(BF16), v_t)
        return acc / jnp.sum(l, axis=-1, keepdims=True)

    kc, vc = kc_ref[0], vc_ref[0]
    outs = [[None] * NSA_GROUPS for _ in range(NSA_HPG)]
    for g in range(NSA_GROUPS):
        own = lo if g == 0 else jnp.logical_not(lo)
        qts = [q_ref[0, :, LANES * hg:LANES * (hg + 1)] for hg in range(NSA_HPG)]
        qs = [jnp.where(own, qt, jnp.zeros_like(qt)) for qt in qts]

        o_cmp, p_sum = [], jnp.zeros((t, LANES), F32)
        for hg in range(NSA_HPG):
            sc = jnp.where(cmp_ok, _dot_nt(qs[hg], kc), NEG)
            p = jnp.where(cmp_ok, jnp.exp2(sc - jnp.max(sc, axis=-1, keepdims=True)), 0.0)
            l = jnp.sum(p, axis=-1, keepdims=True)
            p = p * jnp.where(l > 0.0, 1.0 / l, 0.0)
            p_sum = p_sum + p
            o_cmp.append(_dot(p.astype(BF16), vc))
        p_hi, p_lo = _split_bf16(p_sum)
        imp = (_dot_nt(cov_ref[...], p_hi) + _dot_nt(cov_ref[...], p_lo))[0:n_blk]
        blk = lax.broadcasted_iota(jnp.int32, (n_blk, t), 0)
        jt = (q0 + lax.broadcasted_iota(jnp.int32, (n_blk, t), 1)) >> 6
        allowed = blk <= jt
        forced = allowed & ((blk == 0) | (blk >= jt - 1))
        imp = jnp.where(forced, FORCED, jnp.where(allowed, imp, NEG))
        rank = jnp.zeros((n_blk, t), jnp.int32)
        for jp in range(n_blk):
            other = imp[jp:jp + 1, :]
            ahead = (other > imp) | ((other == imp) & (blk > jp))
            rank = rank + ahead.astype(jnp.int32)
        pen_t = jnp.where((rank < NSA_TOP_N) & allowed, 0.0, NEG)
        pen_t = jnp.concatenate([pen_t, jnp.zeros((LANES - n_blk, t), F32)], axis=0)
        pen = jnp.transpose(pen_t)
        if g == 0:
            pen = pltpu.roll(pen, 64, axis=1)
        pen = pen.astype(BF16)

        ks_ref = ks0_ref if g == 0 else ks1_ref
        slc = _causal_flash([jnp.where(own, qt, pen) for qt in qts],
                            lambda hg, j: ks_ref[0, rows(j), :], lambda hg, j: vs_ref[0, rows(j), :], i, t, t)

        for hg in range(NSA_HPG):
            head = g * NSA_HPG + hg
            g_cmp = gates[:, head:head + 1]
            g_slc = gates[:, NSA_HEADS + head:NSA_HEADS + head + 1]
            g_win = gates[:, 2 * NSA_HEADS + head:2 * NSA_HEADS + head + 1]
            outs[hg][g] = g_cmp * o_cmp[hg] + g_slc * slc[hg] + g_win * window(qs[hg])

    for hg in range(NSA_HPG):
        o_ref[0, :, LANES * hg:LANES * (hg + 1)] = jnp.where(lo, outs[hg][0], outs[hg][1]).astype(BF16)


def _nsa_attn(q, kc, vc, kslc0, kslc1, vslc, kwin, vwin, u3, cover_t):
    b, s, _ = q.shape
    t = ATT_TILE
    seq = pl.BlockSpec((1, s, LANES), lambda bi, i: (bi, 0, 0))
    cmp = pl.BlockSpec((1, LANES, LANES), lambda bi, i: (bi, 0, 0))
    return pl.pallas_call(
        _nsa_attn_kernel,
        out_shape=jax.ShapeDtypeStruct((b, s, 512), BF16),
        grid=(b, s // t),
        in_specs=[pl.BlockSpec((1, t, 512), lambda bi, i: (bi, i, 0)), cmp, cmp, seq, seq, seq, seq, seq,
                  pl.BlockSpec((1, t, LANES), lambda bi, i: (bi, i, 5)),
                  pl.BlockSpec((LANES, LANES), lambda bi, i: (0, 0))],
        out_specs=pl.BlockSpec((1, t, 512), lambda bi, i: (bi, i, 0)),
        compiler_params=_cparams("parallel", "arbitrary"),
        name="nsa_attention",
    )(q, kc, vc, kslc0, kslc1, vslc, kwin, vwin, u3, cover_t)


def _pad_lanes(a, width):
    return jnp.pad(a, [(0, 0)] * (a.ndim - 1) + [(0, width - a.shape[-1])])


def _hy_in_weight(w):
    d = w.shape[0]
    cq, ckv, kpe, nq, nkv, gate = jnp.split(w, [256, 512, 544, 1056, 1824], axis=1)
    nq = nq.reshape(d, NSA_GROUPS, NSA_HPG, NSA_DH).transpose(0, 2, 1, 3).reshape(d, 512)
    return jnp.concatenate([cq, ckv, _pad_lanes(kpe, LANES), _pad_lanes(gate, LANES), nq, nkv], axis=1).astype(BF16)


def _cover_t():
    nc, ns = 127, 32
    c_start = np.arange(nc) * NSA_CMP_STRIDE
    c_end = c_start + NSA_CMP_LEN - 1
    j_start = np.arange(ns) * NSA_SLC_LEN
    cover = ((c_start[:, None] <= j_start[None, :] + NSA_SLC_LEN - 1) & (c_end[:, None] >= j_start[None, :]))
    out = np.zeros((LANES, LANES), np.float32)
    out[:ns, :nc] = cover.T
    return jnp.asarray(out, BF16)


def _compress_weights(cmp_pos, w1, w2):
    half = NSA_CMP_STRIDE
    w1 = w1.reshape(2, 2, half, NSA_DH, NSA_CMP_HIDDEN)
    eye = jnp.eye(NSA_GROUPS, dtype=F32)
    w1e = jnp.einsum('jcldn,gh->jclgdhn', w1, eye).reshape(2, 2, half * LANES, NSA_GROUPS * NSA_CMP_HIDDEN)
    w2e = jnp.einsum('jnd,gh->jgnhd', w2, eye).reshape(2, NSA_GROUPS * NSA_CMP_HIDDEN, LANES)
    pos = cmp_pos.reshape(2, 2, half, 1, NSA_DH)
    pos = jnp.broadcast_to(pos, (2, 2, half, NSA_GROUPS, NSA_DH)).reshape(2, 2, 1, half * LANES)
    pos = jnp.broadcast_to(pos, (2, 2, 8, half * LANES))
    return pos, w1e[:, 0].astype(BF16), w1e[:, 1].astype(BF16), w2e.astype(BF16)


def kernel(x, c, positions, ada_w, ada_b, norm_mix, norm_mlp, mlp_w1, mlp_w2, hy_w_in, hy_w_out, mla_q_norm, mla_w_uq, mla_kv_norm, mla_w_ukv, mla_q_gain, mla_k_gain, nsa_q_gain, nsa_k_gain, nsa_cmp_pos, nsa_cmp_w1, nsa_cmp_w2, diff_w_qkv, diff_w_out, diff_q_gain, diff_k_gain, diff_lambda, diff_sub_gain):
    nb, seq, d = x.shape
    depth = ada_w.shape[0]
    m = nb * seq
    n_cmp = seq // NSA_CMP_STRIDE

    mod = _adaln(c, ada_w, ada_b).reshape(depth * nb, 1, 6 * d)
    tab_ca, tab_sa, tab_cb, tab_sb = _rope_tables(positions.reshape(m, 1))
    pos_c = jnp.pad(positions[:, NSA_CMP_LEN - 1::NSA_CMP_STRIDE], ((0, 0), (0, 1)))
    _, _, tab_cc, tab_sc = _rope_tables(pos_c.reshape(nb * n_cmp, 1))
    tab_cc, tab_sc = tab_cc.reshape(nb, n_cmp, LANES), tab_sc.reshape(nb, n_cmp, LANES)
    cover_t = _cover_t()

    x2 = x.reshape(m, d)
    for i in range(depth):
        j = i // 2
        if i % 2 == 0:
            u = _proj(x2, norm_mix[i], mod, i, 1, 0, _hy_in_weight(hy_w_in[j]), nb, 1024)

            wq = _pad_lanes(mla_w_uq[j].reshape(MLA_Q_RANK, MLA_HEADS, MLA_DK), LANES).reshape(MLA_Q_RANK, -1)
            wkv = mla_w_ukv[j].reshape(MLA_KV_RANK, MLA_HEADS, MLA_NOPE + MLA_V)
            wk = jnp.pad(wkv[..., :MLA_NOPE], ((0, 0), (0, 0), (MLA_ROPE, LANES - MLA_DK))).reshape(MLA_KV_RANK, -1)
            wv = wkv[..., MLA_NOPE:].reshape(MLA_KV_RANK, -1)
            q_mla, k_mla, v_mla = _mla_prep(
                u, mla_q_norm[j].reshape(1, -1), mla_kv_norm[j].reshape(1, -1),
                wq.astype(BF16), wk.astype(BF16), wv.astype(BF16),
                _pad_lanes(mla_q_gain[j], LANES).reshape(1, LANES), _pad_lanes(mla_k_gain[j], LANES).reshape(1, LANES),
                tab_ca, tab_sa)
            y_mla = _mla_attn(q_mla.reshape(nb, seq, -1), k_mla.reshape(nb, seq, -1), v_mla.reshape(nb, seq, -1))

            qg = jnp.tile(nsa_q_gain[j] * (NSA_DH ** -0.5 * LOG2E), 2).reshape(1, LANES)
            kg = jnp.tile(nsa_k_gain[j], (1, 2)).reshape(3, 1, LANES)
            q_nsa, kslc0, kslc1, kwin, vslc, vwin, kcmp, vcmp = _nsa_prep(u, qg, kg, tab_cb, tab_sb, seq)
            pos_e, w1a, w1b, w2e = _compress_weights(nsa_cmp_pos[j], nsa_cmp_w1[j], nsa_cmp_w2[j])
            chunks = lambda a: a.reshape(nb, n_cmp, NSA_CMP_STRIDE * LANES)
            kc, vc = _nsa_compress(chunks(kcmp), chunks(vcmp), pos_e, w1a, w1b, w2e, kg[0], tab_cc, tab_sc)
            seq3 = lambda a: a.reshape(nb, seq, -1)
            y_nsa = _nsa_attn(seq3(q_nsa), kc, vc, seq3(kslc0), seq3(kslc1), seq3(vslc), seq3(kwin), seq3(vwin),
                              seq3(u), cover_t)

            w_out = hy_w_out[j]
            w_nsa = w_out[512:].reshape(NSA_GROUPS, NSA_HPG, NSA_DH, d).transpose(1, 0, 2, 3).reshape(512, d)
            x2 = _out_proj(x2, mod, i, [y_mla.reshape(m, -1), y_nsa.reshape(m, -1)],
                           [w_out[:512].astype(BF16), w_nsa.astype(BF16)], nb)
        else:
            lam_init = 0.8 - 0.6 * math.exp(-0.3 * i)
            u = _proj(x2, norm_mix[i], mod, i, 1, 0, diff_w_qkv[j].astype(BF16), nb, 1024)
            gains = jnp.stack([jnp.tile(diff_q_gain[j] * (DIFF_DH ** -0.5 * LOG2E), 2), jnp.tile(diff_k_gain[j], 2)])
            q, k, v = _diff_prep(u, gains.reshape(2, 1, LANES), tab_cb, tab_sb)
            seq3 = lambda a: a.reshape(nb, seq, -1)
            y = _diff_attn(seq3(q), seq3(k), seq3(v), diff_lambda[j], diff_sub_gain[j], lam_init)
            x2 = _out_proj(x2, mod, i, [y.reshape(m, -1)], [diff_w_out[j].astype(BF16)], nb)
        x2 = _mlp(x2, norm_mlp[i], mod, i, mlp_w1[i].astype(BF16), mlp_w2[i].astype(BF16), nb)
    return x2.reshape(nb, seq, d)
```

```python
import functools
import math

import numpy as np
import jax
import jax.numpy as jnp
from jax import lax
from jax.experimental import pallas as pl
from jax.experimental.pallas import tpu as pltpu

F32 = jnp.float32
BF16 = jnp.bfloat16

LANES = 128
VMEM_LIMIT = 52 * 1024 * 1024

ROPE_THETA = 500000.0
EPS = 1e-6
NEG = -1e30
FORCED = 1e9

MLA_HEADS, MLA_NOPE, MLA_ROPE, MLA_V = 8, 64, 32, 64
MLA_Q_RANK, MLA_KV_RANK = 256, 256
MLA_DK = MLA_ROPE + MLA_NOPE
NSA_HEADS, NSA_GROUPS, NSA_DH = 8, 2, 64
NSA_HPG = NSA_HEADS // NSA_GROUPS
NSA_ROT = NSA_DH // 4
NSA_CMP_LEN, NSA_CMP_STRIDE, NSA_CMP_HIDDEN = 32, 16, 128
NSA_SLC_LEN, NSA_TOP_N, NSA_WINDOW = 64, 16, 512
DIFF_HEADS, DIFF_DH = 8, 64
DIFF_ROT = DIFF_DH // 4

ROW_TILE = 512
ATT_TILE = 256


def _cparams(*sem):
    return pltpu.CompilerParams(dimension_semantics=sem, vmem_limit_bytes=VMEM_LIMIT)


def _split_bf16(x):
    hi = x.astype(BF16)
    lo = (x - hi.astype(F32)).astype(BF16)
    return hi, lo


def _dot(a, b):
    return jnp.dot(a, b, preferred_element_type=F32)


def _dot_nt(a, b):
    return lax.dot_general(a, b, (((1,), (1,)), ((), ())), preferred_element_type=F32)


def _sigmoid(x):
    return 1.0 / (1.0 + jnp.exp(-x))


def _lane(shape):
    return lax.broadcasted_iota(jnp.int32, shape, 1)


def _adaln_kernel(c_ref, w_ref, b_ref, o_ref):
    c = c_ref[...]
    cond = c * _sigmoid(c)
    c_hi, c_lo = _split_bf16(cond)
    w_hi, w_lo = _split_bf16(w_ref[0])
    o_ref[0] = _dot(c_hi, w_hi) + _dot(c_hi, w_lo) + _dot(c_lo, w_hi) + b_ref[0]


def _adaln(c, ada_w, ada_b):
    depth, d, n = ada_w.shape
    b = c.shape[0]
    tn = 1536
    return pl.pallas_call(
        _adaln_kernel,
        out_shape=jax.ShapeDtypeStruct((depth, b, n), F32),
        grid=(depth, n // tn),
        in_specs=[pl.BlockSpec((b, d), lambda i, j: (0, 0)),
                  pl.BlockSpec((1, d, tn), lambda i, j: (i, 0, j)),
                  pl.BlockSpec((1, 1, tn), lambda i, j: (i, 0, j))],
        out_specs=pl.BlockSpec((1, b, tn), lambda i, j: (i, 0, j)),
        compiler_params=_cparams("parallel", "parallel"),
        name="adaln",
    )(c, ada_w, ada_b.reshape(depth, 1, n))


def _rope_table_kernel(pos_ref, fa_ref, fb_ref, ca_ref, sa_ref, cb_ref, sb_ref):
    pos = pos_ref[...].astype(F32)
    ang_a = pos * fa_ref[...]
    ang_b = pos * fb_ref[...]
    ca_ref[...] = jnp.cos(ang_a)
    sa_ref[...] = jnp.sin(ang_a)
    cb_ref[...] = jnp.cos(ang_b)
    sb_ref[...] = jnp.sin(ang_b)


def _lane_freqs(rot, seg):
    half = rot // 2
    lane = np.arange(LANES)
    inv = ROPE_THETA ** (-(np.arange(half, dtype=np.float32) / np.float32(half)))
    f = np.where(lane % seg < rot, inv.astype(np.float32)[lane % half], 0.0)
    return jnp.asarray(f.reshape(1, LANES), F32)


def _rope_masks(rot, seg):
    half = rot // 2
    lane = np.arange(LANES) % seg
    up = ((lane >= half) & (lane < rot)).astype(np.float32)
    dn = -(lane < half).astype(np.float32)
    return jnp.asarray(up.reshape(1, LANES)), jnp.asarray(dn.reshape(1, LANES))


def _rope_tables(pos_col):
    rows = pos_col.shape[0]
    tr = min(rows, 2048)
    spec = pl.BlockSpec((tr, LANES), lambda i: (i, 0))
    vec = pl.BlockSpec((1, LANES), lambda i: (0, 0))
    shp = jax.ShapeDtypeStruct((rows, LANES), F32)
    return pl.pallas_call(
        _rope_table_kernel,
        out_shape=(shp, shp, shp, shp),
        grid=(rows // tr,),
        in_specs=[pl.BlockSpec((tr, 1), lambda i: (i, 0)), vec, vec],
        out_specs=(spec, spec, spec, spec),
        compiler_params=_cparams("parallel"),
        name="rope_tables",
    )(pos_col, _lane_freqs(MLA_ROPE, LANES), _lane_freqs(NSA_ROT, NSA_DH))


def _rope(y, c, s, m_up, m_dn, half):
    up = pltpu.roll(y, half, axis=1)
    dn = pltpu.roll(y, LANES - half, axis=1)
    return y * c + (up * m_up + dn * m_dn) * s


def _rms64(x, gain):
    lo = _lane(x.shape) < 64
    x2 = x * x
    s_lo = jnp.sum(jnp.where(lo, x2, 0.0), axis=-1, keepdims=True)
    s_hi = jnp.sum(jnp.where(lo, 0.0, x2), axis=-1, keepdims=True)
    r = jnp.where(lo, lax.rsqrt(s_lo * (1.0 / 64) + EPS), lax.rsqrt(s_hi * (1.0 / 64) + EPS))
    return x * r * gain


def _norm_mod(x, gain, sc, sh):
    ms = jnp.mean(x * x, axis=-1, keepdims=True)
    return (x * lax.rsqrt(ms + EPS) * gain) * (1.0 + sc) + sh


def _proj_kernel(x_ref, gain_ref, sc_ref, sh_ref, w_ref, o_ref, h_scr):
    @pl.when(pl.program_id(1) == 0)
    def _():
        h_scr[...] = _norm_mod(x_ref[...], gain_ref[...], sc_ref[0], sh_ref[0]).astype(BF16)

    o_ref[...] = _dot(h_scr[...], w_ref[...]).astype(o_ref.dtype)


def _mod_spec(layer, k, nb, rows_per_batch, tm, d):
    per = rows_per_batch // tm
    return pl.BlockSpec((1, 1, d), lambda i, *_: (layer * nb + i // per, 0, k))


def _proj(x2, gain, mod, layer, k_sc, k_sh, w, nb, tn):
    m, d = x2.shape
    n = w.shape[1]
    tm = 1024
    s = m // nb
    return pl.pallas_call(
        _proj_kernel,
        out_shape=jax.ShapeDtypeStruct((m, n), F32),
        grid=(m // tm, n // tn),
        in_specs=[pl.BlockSpec((tm, d), lambda i, j: (i, 0)),
                  pl.BlockSpec((1, d), lambda i, j: (0, 0)),
                  _mod_spec(layer, k_sc, nb, s, tm, d),
                  _mod_spec(layer, k_sh, nb, s, tm, d),
                  pl.BlockSpec((d, tn), lambda i, j: (0, j))],
        out_specs=pl.BlockSpec((tm, tn), lambda i, j: (i, j)),
        scratch_shapes=[pltpu.VMEM((tm, d), BF16)],
        compiler_params=_cparams("parallel", "arbitrary"),
        name="norm_mod_proj",
    )(x2, gain.reshape(1, d), mod, mod, w)


def _mlp_kernel(x_ref, gain_ref, sc_ref, sh_ref, g_ref, w1_ref, w2_ref, o_ref, h_scr, acc_scr):
    j = pl.program_id(1)

    @pl.when(j == 0)
    def _():
        h_scr[...] = _norm_mod(x_ref[...], gain_ref[...], sc_ref[0], sh_ref[0]).astype(BF16)
        acc_scr[...] = jnp.zeros_like(acc_scr)

    a = jnp.maximum(_dot(h_scr[...], w1_ref[...]), 0.0)
    acc_scr[...] += _dot((a * a).astype(BF16), w2_ref[...])

    @pl.when(j == pl.num_programs(1) - 1)
    def _():
        o_ref[...] = x_ref[...] + g_ref[0] * acc_scr[...]


def _mlp(x2, gain, mod, layer, w1, w2, nb):
    m, d = x2.shape
    ff = w1.shape[1]
    tm, tf = 1024, 512
    s = m // nb
    return pl.pallas_call(
        _mlp_kernel,
        out_shape=jax.ShapeDtypeStruct((m, d), F32),
        grid=(m // tm, ff // tf),
        in_specs=[pl.BlockSpec((tm, d), lambda i, j: (i, 0)),
                  pl.BlockSpec((1, d), lambda i, j: (0, 0)),
                  _mod_spec(layer, 4, nb, s, tm, d),
                  _mod_spec(layer, 3, nb, s, tm, d),
                  _mod_spec(layer, 5, nb, s, tm, d),
                  pl.BlockSpec((d, tf), lambda i, j: (0, j)),
                  pl.BlockSpec((tf, d), lambda i, j: (j, 0))],
        out_specs=pl.BlockSpec((tm, d), lambda i, j: (i, 0)),
        scratch_shapes=[pltpu.VMEM((tm, d), BF16), pltpu.VMEM((tm, d), F32)],
        compiler_params=_cparams("parallel", "arbitrary"),
        name="relu2_mlp",
    )(x2, gain.reshape(1, d), mod, mod, mod, w1, w2)


def _out_proj_kernel(n_in, *refs):
    x_ref, g_ref = refs[0], refs[1]
    y_refs = refs[2:2 + n_in]
    w_refs = refs[2 + n_in:2 + 2 * n_in]
    o_ref = refs[2 + 2 * n_in]
    y = _dot(y_refs[0][...], w_refs[0][...])
    for y_ref, w_ref in zip(y_refs[1:], w_refs[1:]):
        y = y + _dot(y_ref[...], w_ref[...])
    o_ref[...] = x_ref[...] + g_ref[0] * y


def _out_proj(x2, mod, layer, ys, ws, nb):
    m, d = x2.shape
    tm = ROW_TILE
    s = m // nb
    in_specs = [pl.BlockSpec((tm, d), lambda i: (i, 0)), _mod_spec(layer, 2, nb, s, tm, d)]
    in_specs += [pl.BlockSpec((tm, y.shape[1]), lambda i: (i, 0)) for y in ys]
    in_specs += [pl.BlockSpec(w.shape, lambda i: (0, 0)) for w in ws]
    return pl.pallas_call(
        functools.partial(_out_proj_kernel, len(ys)),
        out_shape=jax.ShapeDtypeStruct((m, d), F32),
        grid=(m // tm,),
        in_specs=in_specs,
        out_specs=pl.BlockSpec((tm, d), lambda i: (i, 0)),
        compiler_params=_cparams("parallel"),
        name="out_proj_residual",
    )(x2, mod, *ys, *ws)


LOG2E = math.log2(math.e)


SCORE_BOUND = 60.0
BOUND_MARGIN = 1.02


def _needs_shift(dk, q_gain, k_gain):
    bound = dk * jnp.max(jnp.abs(q_gain)) * jnp.max(jnp.abs(k_gain)) * BOUND_MARGIN
    return jnp.logical_not(bound <= SCORE_BOUND)


def _shift_dispatch(needs_shift, attn, *args):
    return lax.cond(needs_shift, functools.partial(attn, True), functools.partial(attn, False), *args)


def _lane_tiles(x):
    return [x[:, LANES * c:LANES * (c + 1)] for c in range(x.shape[1] // LANES)]


def _causal_flash(qs, k_at, v_at, i, t, rows_per_map, shift):
    n = len(qs)
    rows = qs[0].shape[0]
    r = lax.broadcasted_iota(jnp.int32, (rows, t), 0)
    if rows != rows_per_map:
        r = jnp.where(r >= rows_per_map, r - rows_per_map, r)
    mask = lax.broadcasted_iota(jnp.int32, (rows, t), 1) <= r

    def scores(h, j, diagonal):
        s = _dot_nt(qs[h], k_at(h, j))
        return jnp.where(mask, s, NEG) if diagonal else s

    def max_step(ms, j, diagonal):
        out = []
        for h in range(n):
            m = ms[h]
            for s_c in _lane_tiles(scores(h, j, diagonal)):
                m = jnp.maximum(m, s_c)
            out.append(m)
        return tuple(out)

    if shift:
        ms = tuple(jnp.full((rows, LANES), NEG, F32) for _ in range(n))
        ms = max_step(lax.fori_loop(0, i, lambda j, st: max_step(st, j, False), ms), i, True)
        ms = [jnp.broadcast_to(jnp.max(m, axis=-1, keepdims=True), (rows, LANES)) for m in ms]

    def sum_step(states, j, diagonal):
        out = []
        for h in range(n):
            l, acc, p_prev = states[h]
            acc = acc + _dot(p_prev, v_at(h, jnp.maximum(j - 1, 0)))
            ps = _lane_tiles(scores(h, j, diagonal))
            ps = [jnp.exp2(s_c - ms[h]) if shift else jnp.exp2(s_c) for s_c in ps]
            for p_c in ps:
                l = l + p_c
            out.append((l, acc, jnp.concatenate(ps, axis=1).astype(BF16)))
        return tuple(out)

    zero = jnp.zeros((rows, LANES), F32)
    states = tuple((zero, zero, jnp.zeros((rows, t), BF16)) for _ in range(n))
    states = sum_step(lax.fori_loop(0, i, lambda j, st: sum_step(st, j, False), states), i, True)
    return [(acc + _dot(p_last, v_at(h, i))) / jnp.sum(l, axis=-1, keepdims=True)
            for h, (l, acc, p_last) in enumerate(states)]


def _mla_prep_kernel(u_ref, qn_ref, kvn_ref, wq_ref, wk_ref, wv_ref, qg_ref, kg_ref,
                     c_ref, s_ref, mu_ref, md_ref, q_ref, k_ref, v_ref):
    def rms(x, g):
        return x * lax.rsqrt(jnp.mean(x * x, axis=-1, keepdims=True) + EPS) * g

    cq = rms(u_ref[:, 0:256], qn_ref[...]).astype(BF16)
    ckv = rms(u_ref[:, 256:512], kvn_ref[...]).astype(BF16)
    kpe = u_ref[:, 512:640]
    v_ref[...] = _dot(ckv, wv_ref[...]).astype(BF16)
    q_all = _dot(cq, wq_ref[...])
    k_all = _dot(ckv, wk_ref[...])
    c, s, mu, md = c_ref[...], s_ref[...], mu_ref[...], md_ref[...]
    scale = MLA_DK ** -0.5 * LOG2E

    def head(x, g):
        r = lax.rsqrt(jnp.sum(x * x, axis=-1, keepdims=True) * (1.0 / MLA_DK) + EPS)
        return _rope(x * r * g, c, s, mu, md, MLA_ROPE // 2)

    for h in range(MLA_HEADS):
        sl = slice(LANES * h, LANES * (h + 1))
        q_ref[:, sl] = (head(q_all[:, sl], qg_ref[...]) * scale).astype(BF16)
        k_ref[:, sl] = head(k_all[:, sl] + kpe, kg_ref[...]).astype(BF16)


def _mla_prep(u, q_norm, kv_norm, wq, wk, wv, q_gain, k_gain, tab_c, tab_s):
    m = u.shape[0]
    tm = ROW_TILE
    mu, md = _rope_masks(MLA_ROPE, LANES)
    full = lambda a: pl.BlockSpec(a.shape, lambda i: (0, 0))
    row = lambda w: pl.BlockSpec((tm, w), lambda i: (i, 0))
    args = (u, q_norm, kv_norm, wq, wk, wv, q_gain, k_gain, tab_c, tab_s, mu, md)
    in_specs = [row(1024)] + [full(a) for a in args[1:8]] + [row(LANES), row(LANES), full(mu), full(md)]
    return pl.pallas_call(
        _mla_prep_kernel,
        out_shape=(jax.ShapeDtypeStruct((m, 1024), BF16), jax.ShapeDtypeStruct((m, 1024), BF16),
                   jax.ShapeDtypeStruct((m, 512), BF16)),
        grid=(m // tm,),
        in_specs=in_specs,
        out_specs=(row(1024), row(1024), row(512)),
        compiler_params=_cparams("parallel"),
        name="mla_prep",
    )(*args)


def _mla_attn_kernel(shift, q_ref, k_ref, v_ref, o_ref):
    i = pl.program_id(1)
    t = ATT_TILE
    lo = _lane((t, LANES)) < 64

    def rows(j):
        return pl.ds(pl.multiple_of(j * t, t), t)

    def tile(h):
        return slice(LANES * h, LANES * (h + 1))

    outs = _causal_flash([q_ref[0, :, tile(h)] for h in range(MLA_HEADS)],
                         lambda h, j: k_ref[0, rows(j), tile(h)],
                         lambda h, j: v_ref[0, rows(j), tile(h // 2)], i, t, t, shift)
    for p in range(MLA_HEADS // 2):
        o_ref[0, :, tile(p)] = jnp.where(lo, outs[2 * p], outs[2 * p + 1]).astype(BF16)


def _mla_attn(shift, q, k, v):
    b, s, _ = q.shape
    t = ATT_TILE
    return pl.pallas_call(
        functools.partial(_mla_attn_kernel, shift),
        out_shape=jax.ShapeDtypeStruct((b, s, 512), BF16),
        grid=(b, s // t),
        in_specs=[pl.BlockSpec((1, t, 1024), lambda bi, i: (bi, i, 0)),
                  pl.BlockSpec((1, s, 1024), lambda bi, i: (bi, 0, 0)),
                  pl.BlockSpec((1, s, 512), lambda bi, i: (bi, 0, 0))],
        out_specs=pl.BlockSpec((1, t, 512), lambda bi, i: (bi, i, 0)),
        compiler_params=_cparams("parallel", "arbitrary"),
        name="mla_attention",
    )(q, k, v)


def _diff_prep_kernel(u_ref, g_ref, c_ref, s_ref, mu_ref, md_ref, q_ref, k_ref, v_ref):
    c, s, mu, md = c_ref[...], s_ref[...], mu_ref[...], md_ref[...]
    n = DIFF_HEADS * LANES
    for which, out in ((0, q_ref), (1, k_ref)):
        g = g_ref[which]
        for h in range(DIFF_HEADS):
            sl = slice(LANES * h, LANES * (h + 1))
            x = u_ref[:, which * n + LANES * h:which * n + LANES * (h + 1)]
            out[:, sl] = _rope(_rms64(x, g), c, s, mu, md, DIFF_ROT // 2).astype(BF16)
    v_ref[...] = u_ref[:, 2 * n:3 * n].astype(BF16)


def _diff_prep(u, gains, tab_c, tab_s):
    m = u.shape[0]
    tm = ROW_TILE
    mu, md = _rope_masks(DIFF_ROT, DIFF_DH)
    row = lambda w: pl.BlockSpec((tm, w), lambda i: (i, 0))
    vec = pl.BlockSpec((1, LANES), lambda i: (0, 0))
    shp = jax.ShapeDtypeStruct((m, 1024), BF16)
    return pl.pallas_call(
        _diff_prep_kernel,
        out_shape=(shp, shp, shp),
        grid=(m // tm,),
        in_specs=[row(3072), pl.BlockSpec((2, 1, LANES), lambda i: (0, 0, 0)), row(LANES), row(LANES), vec, vec],
        out_specs=(row(1024), row(1024), row(1024)),
        compiler_params=_cparams("parallel"),
        name="diff_prep",
    )(u, gains, tab_c, tab_s, mu, md)


def _diff_attn_kernel(lam_init, shift, q_ref, k_ref, v_ref, lam_ref, sg_ref, o_ref):
    i = pl.program_id(1)
    t = ATT_TILE
    lo = _lane((t, LANES)) < 64
    lam = lam_ref[...]
    lmb = (jnp.exp(jnp.sum(lam[0:1] * lam[1:2], axis=-1, keepdims=True))
           - jnp.exp(jnp.sum(lam[2:3] * lam[3:4], axis=-1, keepdims=True)) + lam_init)

    def rows(j):
        return pl.ds(pl.multiple_of(j * t, t), t)

    def tile(h):
        return slice(LANES * h, LANES * (h + 1))

    def both_maps(h):
        qt = q_ref[0, :, tile(h)]
        zero = jnp.zeros_like(qt)
        return jnp.concatenate([jnp.where(lo, qt, zero), jnp.where(lo, zero, qt)], axis=0)

    group = DIFF_HEADS // 2
    for h0 in range(0, DIFF_HEADS, group):
        outs = _causal_flash([both_maps(h0 + e) for e in range(group)],
                             lambda e, j: k_ref[0, rows(j), tile(h0 + e)],
                             lambda e, j: v_ref[0, rows(j), tile(h0 + e)], i, t, t, shift)
        for e in range(group):
            o = outs[e][:t] - lmb * outs[e][t:]
            o = o * lax.rsqrt(jnp.mean(o * o, axis=-1, keepdims=True) + EPS) * sg_ref[...]
            o_ref[0, :, tile(h0 + e)] = (o * (1.0 - lam_init)).astype(BF16)


def _diff_attn(lam_init, shift, q, k, v, lam, sub_gain):
    b, s, n = q.shape
    t = ATT_TILE
    full = pl.BlockSpec((1, s, n), lambda bi, i: (bi, 0, 0))
    return pl.pallas_call(
        functools.partial(_diff_attn_kernel, lam_init, shift),
        out_shape=jax.ShapeDtypeStruct((b, s, n), BF16),
        grid=(b, s // t),
        in_specs=[pl.BlockSpec((1, t, n), lambda bi, i: (bi, i, 0)), full, full,
                  pl.BlockSpec(lam.shape, lambda bi, i: (0, 0)),
                  pl.BlockSpec((1, LANES), lambda bi, i: (0, 0))],
        out_specs=pl.BlockSpec((1, t, n), lambda bi, i: (bi, i, 0)),
        compiler_params=_cparams("parallel", "arbitrary"),
        name="diff_attention",
    )(q, k, v, lam, sub_gain.reshape(1, LANES))


def _nsa_prep_kernel(seq, u_ref, qg_ref, kg_ref, c_ref, s_ref, mu_ref, md_ref,
                     q_ref, kslc0_ref, kslc1_ref, kwin_ref, vslc_ref, vwin_ref, kcmp_ref, vcmp_ref):
    c, s, mu, md = c_ref[...], s_ref[...], mu_ref[...], md_ref[...]
    tm = u_ref.shape[0]

    def prep(x, g):
        return _rope(_rms64(x, g), c, s, mu, md, NSA_ROT // 2)

    for hg in range(NSA_HPG):
        sl = slice(LANES * hg, LANES * (hg + 1))
        q_ref[:, sl] = prep(u_ref[:, 768 + LANES * hg:768 + LANES * (hg + 1)], qg_ref[...]).astype(BF16)
    kcmp_ref[...] = u_ref[:, 1280:1408].astype(BF16)
    vcmp_ref[...] = u_ref[:, 1408:1536].astype(BF16)
    kslc = prep(u_ref[:, 1536:1664], kg_ref[1])
    lane = _lane((tm, LANES))
    t_idx = (pl.program_id(0) * tm + lax.broadcasted_iota(jnp.int32, (tm, LANES), 0)) & (seq - 1)
    blk = t_idx >> 6
    kslc0_ref[...] = jnp.where(lane < 64, kslc, jnp.where(lane - 64 == blk, 1.0, 0.0)).astype(BF16)
    kslc1_ref[...] = jnp.where(lane >= 64, kslc, jnp.where(lane == blk, 1.0, 0.0)).astype(BF16)
    vslc_ref[...] = u_ref[:, 1664:1792].astype(BF16)
    kwin_ref[...] = prep(u_ref[:, 1792:1920], kg_ref[2]).astype(BF16)
    vwin_ref[...] = u_ref[:, 1920:2048].astype(BF16)


def _nsa_prep(u, q_gain, k_gain, tab_c, tab_s, seq):
    m = u.shape[0]
    tm = ROW_TILE
    assert seq & (seq - 1) == 0 and seq // NSA_SLC_LEN == 32 and seq % tm == 0
    mu, md = _rope_masks(NSA_ROT, NSA_DH)
    row = lambda w: pl.BlockSpec((tm, w), lambda i: (i, 0))
    vec = pl.BlockSpec((1, LANES), lambda i: (0, 0))
    t128 = jax.ShapeDtypeStruct((m, LANES), BF16)
    return pl.pallas_call(
        functools.partial(_nsa_prep_kernel, seq),
        out_shape=(jax.ShapeDtypeStruct((m, 512), BF16),) + (t128,) * 7,
        grid=(m // tm,),
        in_specs=[row(2048), vec, pl.BlockSpec((3, 1, LANES), lambda i: (0, 0, 0)), row(LANES), row(LANES), vec, vec],
        out_specs=(row(512),) + (row(LANES),) * 7,
        compiler_params=_cparams("parallel"),
        name="nsa_prep",
    )(u, q_gain, k_gain, tab_c, tab_s, mu, md)


def _nsa_compress_kernel(tk_ref, tv_ref, pos_ref, w1a_ref, w1b_ref, w2_ref, kg_ref,
                         c_ref, s_ref, mu_ref, md_ref, kc_ref, vc_ref):
    n_rows = tk_ref.shape[1]
    for j, (t_ref, out) in enumerate(((tk_ref, kc_ref), (tv_ref, vc_ref))):
        tok = t_ref[0]
        p_hi, p_lo = _split_bf16(pos_ref[j])
        w1a, w1b = w1a_ref[j], w1b_ref[j]
        bias = (_dot(p_hi[0], w1a) + _dot(p_lo[0], w1a) + _dot(p_hi[1], w1b) + _dot(p_lo[1], w1b))[0:1]
        hid = _dot(tok, w1a) + pltpu.roll(_dot(tok, w1b), n_rows - 1, axis=0) + bias
        act = jax.nn.gelu(hid, approximate=True)
        cmp = _dot(act.astype(BF16), w2_ref[j])
        if j == 0:
            cmp = _rope(_rms64(cmp, kg_ref[...]), c_ref[0], s_ref[0], mu_ref[...], md_ref[...], NSA_ROT // 2)
        out[0] = cmp.astype(BF16)


def _nsa_compress(tk, tv, pos, w1a, w1b, w2, k_gain0, tab_c, tab_s):
    b, nr, w = tk.shape
    mu, md = _rope_masks(NSA_ROT, NSA_DH)
    full = lambda a: pl.BlockSpec(a.shape, lambda bi: (0,) * a.ndim)
    per_b = lambda a: pl.BlockSpec((1,) + a.shape[1:], lambda bi: (bi,) + (0,) * (a.ndim - 1))
    shp = jax.ShapeDtypeStruct((b, nr, LANES), BF16)
    return pl.pallas_call(
        _nsa_compress_kernel,
        out_shape=(shp, shp),
        grid=(b,),
        in_specs=[per_b(tk), per_b(tv), full(pos), full(w1a), full(w1b), full(w2), full(k_gain0),
                  per_b(tab_c), per_b(tab_s), full(mu), full(md)],
        out_specs=(pl.BlockSpec((1, nr, LANES), lambda bi: (bi, 0, 0)),) * 2,
        compiler_params=_cparams("parallel"),
        name="nsa_compress",
    )(tk, tv, pos, w1a, w1b, w2, k_gain0, tab_c, tab_s, mu, md)


def _nsa_attn_kernel(shift, q_ref, kc_ref, vc_ref, ks0_ref, ks1_ref, vs_ref, kw_ref, vw_ref, gate_ref, cov_ref,
                     o_ref):
    i = pl.program_id(1)
    t = ATT_TILE
    n_blk = 32
    q0 = i * t
    lane = _lane((t, LANES))
    lo = lane < 64
    qpos = q0 + lax.broadcasted_iota(jnp.int32, (t, LANES), 0)
    cmp_ok = NSA_CMP_STRIDE * lane + (NSA_CMP_LEN - 1) <= qpos
    gates = _sigmoid(gate_ref[0])

    def rows(j):
        return pl.ds(pl.multiple_of(j * t, t), t)

    assert NSA_WINDOW == 2 * t
    r_i = lax.broadcasted_iota(jnp.int32, (t, t), 0)
    c_i = lax.broadcasted_iota(jnp.int32, (t, t), 1)
    win_masks = (c_i <= r_i, (i - 1) * t + c_i >= 0, (c_i > r_i) & ((i - 2) * t + c_i >= 0))
    win_tiles = (i, jnp.maximum(i - 1, 0), jnp.maximum(i - 2, 0))
    win_k = [kw_ref[0, rows(j), :] for j in win_tiles]
    win_v = [vw_ref[0, rows(j), :] for j in win_tiles]

    def window(q):
        ss = [jnp.where(mk, _dot_nt(q, k_t), NEG) for mk, k_t in zip(win_masks, win_k)]
        if shift:
            m = jnp.full((t, LANES), NEG, F32)
            for s_t in ss:
                for s_c in _lane_tiles(s_t):
                    m = jnp.maximum(m, s_c)
            m = jnp.broadcast_to(jnp.max(m, axis=-1, keepdims=True), (t, LANES))
        l, acc = jnp.zeros((t, LANES), F32), jnp.zeros((t, LANES), F32)
        for s_t, v_t in zip(ss, win_v):
            ps = [jnp.exp2(s_c - m) if shift else jnp.exp2(s_c) for s_c in _lane_tiles(s_t)]
            for p_c in ps:
                l = l + p_c
            acc = acc + _dot(jnp.concatenate(ps, axis=1).astype(BF16), v_t)
        return acc / jnp.sum(l, axis=-1, keepdims=True)

    kc, vc = kc_ref[0], vc_ref[0]
    outs = [[None] * NSA_GROUPS for _ in range(NSA_HPG)]
    for g in range(NSA_GROUPS):
        own = lo if g == 0 else jnp.logical_not(lo)
        qts = [q_ref[0, :, LANES * hg:LANES * (hg + 1)] for hg in range(NSA_HPG)]
        qs = [jnp.where(own, qt, jnp.zeros_like(qt)) for qt in qts]

        o_cmp, p_sum = [], jnp.zeros((t, LANES), F32)
        for hg in range(NSA_HPG):
            sc = jnp.where(cmp_ok, _dot_nt(qs[hg], kc), NEG)
            p = jnp.where(cmp_ok, jnp.exp2(sc - jnp.max(sc, axis=-1, keepdims=True)), 0.0)
            l = jnp.sum(p, axis=-1, keepdims=True)
            p = p * jnp.where(l > 0.0, 1.0 / l, 0.0)
            p_sum = p_sum + p
            o_cmp.append(_dot(p.astype(BF16), vc))
        p_hi, p_lo = _split_bf16(p_sum)
        imp = (_dot_nt(cov_ref[...], p_hi) + _dot_nt(cov_ref[...], p_lo))[0:n_blk]
        blk = lax.broadcasted_iota(jnp.int32, (n_blk, t), 0)
        jt = (q0 + lax.broadcasted_iota(jnp.int32, (n_blk, t), 1)) >> 6
        allowed = blk <= jt
        forced = allowed & ((blk == 0) | (blk >= jt - 1))
        imp = jnp.where(forced, FORCED, jnp.where(allowed, imp, NEG))
        rank = jnp.zeros((n_blk, t), jnp.int32)
        for jp in range(n_blk):
            other = imp[jp:jp + 1, :]
            ahead = (other > imp) | ((other == imp) & (blk > jp))
            rank = rank + ahead.astype(jnp.int32)
        pen_t = jnp.where((rank < NSA_TOP_N) & allowed, 0.0, NEG)
        pen_t = jnp.concatenate([pen_t, jnp.zeros((LANES - n_blk, t), F32)], axis=0)
        pen = jnp.transpose(pen_t)
        if g == 0:
            pen = pltpu.roll(pen, 64, axis=1)
        pen = pen.astype(BF16)

        ks_ref = ks0_ref if g == 0 else ks1_ref
        slc = _causal_flash([jnp.where(own, qt, pen) for qt in qts],
                            lambda hg, j: ks_ref[0, rows(j), :], lambda hg, j: vs_ref[0, rows(j), :], i, t, t, shift)

        for hg in range(NSA_HPG):
            head = g * NSA_HPG + hg
            g_cmp = gates[:, head:head + 1]
            g_slc = gates[:, NSA_HEADS + head:NSA_HEADS + head + 1]
            g_win = gates[:, 2 * NSA_HEADS + head:2 * NSA_HEADS + head + 1]
            outs[hg][g] = g_cmp * o_cmp[hg] + g_slc * slc[hg] + g_win * window(qs[hg])

    for hg in range(NSA_HPG):
        o_ref[0, :, LANES * hg:LANES * (hg + 1)] = jnp.where(lo, outs[hg][0], outs[hg][1]).astype(BF16)


def _nsa_attn(shift, q, kc, vc, kslc0, kslc1, vslc, kwin, vwin, u3, cover_t):
    b, s, _ = q.shape
    t = ATT_TILE
    seq = pl.BlockSpec((1, s, LANES), lambda bi, i: (bi, 0, 0))
    cmp = pl.BlockSpec((1, LANES, LANES), lambda bi, i: (bi, 0, 0))
    return pl.pallas_call(
        functools.partial(_nsa_attn_kernel, shift),
        out_shape=jax.ShapeDtypeStruct((b, s, 512), BF16),
        grid=(b, s // t),
        in_specs=[pl.BlockSpec((1, t, 512), lambda bi, i: (bi, i, 0)), cmp, cmp, seq, seq, seq, seq, seq,
                  pl.BlockSpec((1, t, LANES), lambda bi, i: (bi, i, 5)),
                  pl.BlockSpec((LANES, LANES), lambda bi, i: (0, 0))],
        out_specs=pl.BlockSpec((1, t, 512), lambda bi, i: (bi, i, 0)),
        compiler_params=_cparams("parallel", "arbitrary"),
        name="nsa_attention",
    )(q, kc, vc, kslc0, kslc1, vslc, kwin, vwin, u3, cover_t)


def _pad_lanes(a, width):
    return jnp.pad(a, [(0, 0)] * (a.ndim - 1) + [(0, width - a.shape[-1])])


def _hy_in_weight(w):
    d = w.shape[0]
    cq, ckv, kpe, nq, nkv, gate = jnp.split(w, [256, 512, 544, 1056, 1824], axis=1)
    nq = nq.reshape(d, NSA_GROUPS, NSA_HPG, NSA_DH).transpose(0, 2, 1, 3).reshape(d, 512)
    return jnp.concatenate([cq, ckv, _pad_lanes(kpe, LANES), _pad_lanes(gate, LANES), nq, nkv], axis=1).astype(BF16)


def _cover_t():
    nc, ns = 127, 32
    c_start = np.arange(nc) * NSA_CMP_STRIDE
    c_end = c_start + NSA_CMP_LEN - 1
    j_start = np.arange(ns) * NSA_SLC_LEN
    cover = ((c_start[:, None] <= j_start[None, :] + NSA_SLC_LEN - 1) & (c_end[:, None] >= j_start[None, :]))
    out = np.zeros((LANES, LANES), np.float32)
    out[:ns, :nc] = cover.T
    return jnp.asarray(out, BF16)


def _compress_weights(cmp_pos, w1, w2):
    half = NSA_CMP_STRIDE
    w1 = w1.reshape(2, 2, half, NSA_DH, NSA_CMP_HIDDEN)
    eye = jnp.eye(NSA_GROUPS, dtype=F32)
    w1e = jnp.einsum('jcldn,gh->jclgdhn', w1, eye).reshape(2, 2, half * LANES, NSA_GROUPS * NSA_CMP_HIDDEN)
    w2e = jnp.einsum('jnd,gh->jgnhd', w2, eye).reshape(2, NSA_GROUPS * NSA_CMP_HIDDEN, LANES)
    pos = cmp_pos.reshape(2, 2, half, 1, NSA_DH)
    pos = jnp.broadcast_to(pos, (2, 2, half, NSA_GROUPS, NSA_DH)).reshape(2, 2, 1, half * LANES)
    pos = jnp.broadcast_to(pos, (2, 2, 8, half * LANES))
    return pos, w1e[:, 0].astype(BF16), w1e[:, 1].astype(BF16), w2e.astype(BF16)


def kernel(x, c, positions, ada_w, ada_b, norm_mix, norm_mlp, mlp_w1, mlp_w2, hy_w_in, hy_w_out, mla_q_norm, mla_w_uq, mla_kv_norm, mla_w_ukv, mla_q_gain, mla_k_gain, nsa_q_gain, nsa_k_gain, nsa_cmp_pos, nsa_cmp_w1, nsa_cmp_w2, diff_w_qkv, diff_w_out, diff_q_gain, diff_k_gain, diff_lambda, diff_sub_gain):
    nb, seq, d = x.shape
    depth = ada_w.shape[0]
    m = nb * seq
    n_cmp = seq // NSA_CMP_STRIDE

    mod = _adaln(c, ada_w, ada_b).reshape(depth * nb, 1, 6 * d)
    tab_ca, tab_sa, tab_cb, tab_sb = _rope_tables(positions.reshape(m, 1))
    pos_c = jnp.pad(positions[:, NSA_CMP_LEN - 1::NSA_CMP_STRIDE], ((0, 0), (0, 1)))
    _, _, tab_cc, tab_sc = _rope_tables(pos_c.reshape(nb * n_cmp, 1))
    tab_cc, tab_sc = tab_cc.reshape(nb, n_cmp, LANES), tab_sc.reshape(nb, n_cmp, LANES)
    cover_t = _cover_t()

    x2 = x.reshape(m, d)
    for i in range(depth):
        j = i // 2
        if i % 2 == 0:
            u = _proj(x2, norm_mix[i], mod, i, 1, 0, _hy_in_weight(hy_w_in[j]), nb, 1024)

            wq = _pad_lanes(mla_w_uq[j].reshape(MLA_Q_RANK, MLA_HEADS, MLA_DK), LANES).reshape(MLA_Q_RANK, -1)
            wkv = mla_w_ukv[j].reshape(MLA_KV_RANK, MLA_HEADS, MLA_NOPE + MLA_V)
            wk = jnp.pad(wkv[..., :MLA_NOPE], ((0, 0), (0, 0), (MLA_ROPE, LANES - MLA_DK))).reshape(MLA_KV_RANK, -1)
            wv = wkv[..., MLA_NOPE:].reshape(MLA_KV_RANK, -1)
            q_mla, k_mla, v_mla = _mla_prep(
                u, mla_q_norm[j].reshape(1, -1), mla_kv_norm[j].reshape(1, -1),
                wq.astype(BF16), wk.astype(BF16), wv.astype(BF16),
                _pad_lanes(mla_q_gain[j], LANES).reshape(1, LANES), _pad_lanes(mla_k_gain[j], LANES).reshape(1, LANES),
                tab_ca, tab_sa)
            y_mla = _shift_dispatch(
                _needs_shift(MLA_DK, mla_q_gain[j] * (MLA_DK ** -0.5 * LOG2E), mla_k_gain[j]), _mla_attn,
                q_mla.reshape(nb, seq, -1), k_mla.reshape(nb, seq, -1), v_mla.reshape(nb, seq, -1))

            qg = jnp.tile(nsa_q_gain[j] * (NSA_DH ** -0.5 * LOG2E), 2).reshape(1, LANES)
            kg = jnp.tile(nsa_k_gain[j], (1, 2)).reshape(3, 1, LANES)
            q_nsa, kslc0, kslc1, kwin, vslc, vwin, kcmp, vcmp = _nsa_prep(u, qg, kg, tab_cb, tab_sb, seq)
            pos_e, w1a, w1b, w2e = _compress_weights(nsa_cmp_pos[j], nsa_cmp_w1[j], nsa_cmp_w2[j])
            chunks = lambda a: a.reshape(nb, n_cmp, NSA_CMP_STRIDE * LANES)
            kc, vc = _nsa_compress(chunks(kcmp), chunks(vcmp), pos_e, w1a, w1b, w2e, kg[0], tab_cc, tab_sc)
            seq3 = lambda a: a.reshape(nb, seq, -1)
            y_nsa = _shift_dispatch(
                _needs_shift(NSA_DH, qg, nsa_k_gain[j][1:]), _nsa_attn,
                seq3(q_nsa), kc, vc, seq3(kslc0), seq3(kslc1), seq3(vslc), seq3(kwin), seq3(vwin), seq3(u), cover_t)

            w_out = hy_w_out[j]
            w_nsa = w_out[512:].reshape(NSA_GROUPS, NSA_HPG, NSA_DH, d).transpose(1, 0, 2, 3).reshape(512, d)
            x2 = _out_proj(x2, mod, i, [y_mla.reshape(m, -1), y_nsa.reshape(m, -1)],
                           [w_out[:512].astype(BF16), w_nsa.astype(BF16)], nb)
        else:
            lam_init = 0.8 - 0.6 * math.exp(-0.3 * i)
            u = _proj(x2, norm_mix[i], mod, i, 1, 0, diff_w_qkv[j].astype(BF16), nb, 1024)
            gains = jnp.stack([jnp.tile(diff_q_gain[j] * (DIFF_DH ** -0.5 * LOG2E), 2), jnp.tile(diff_k_gain[j], 2)])
            q, k, v = _diff_prep(u, gains.reshape(2, 1, LANES), tab_cb, tab_sb)
            seq3 = lambda a: a.reshape(nb, seq, -1)
            y = _shift_dispatch(
                _needs_shift(DIFF_DH, gains[0], gains[1]), functools.partial(_diff_attn, lam_init),
                seq3(q), seq3(k), seq3(v), diff_lambda[j], diff_sub_gain[j])
            x2 = _out_proj(x2, mod, i, [y.reshape(m, -1)], [diff_w_out[j].astype(BF16)], nb)
        x2 = _mlp(x2, norm_mlp[i], mod, i, mlp_w1[i].astype(BF16), mlp_w2[i].astype(BF16), nb)
    return x2.reshape(nb, seq, d)
```

```python
import functools
import math

import numpy as np
import jax
import jax.numpy as jnp
from jax import lax
from jax.experimental import pallas as pl
from jax.experimental.pallas import tpu as pltpu

F32 = jnp.float32
BF16 = jnp.bfloat16

LANES = 128
VMEM_LIMIT = 52 * 1024 * 1024

ROPE_THETA = 500000.0
EPS = 1e-6
NEG = -1e30
FORCED = 1e9

MLA_HEADS, MLA_NOPE, MLA_ROPE, MLA_V = 8, 64, 32, 64
MLA_Q_RANK, MLA_KV_RANK = 256, 256
MLA_DK = MLA_ROPE + MLA_NOPE
NSA_HEADS, NSA_GROUPS, NSA_DH = 8, 2, 64
NSA_HPG = NSA_HEADS // NSA_GROUPS
NSA_ROT = NSA_DH // 4
NSA_CMP_LEN, NSA_CMP_STRIDE, NSA_CMP_HIDDEN = 32, 16, 128
NSA_SLC_LEN, NSA_TOP_N, NSA_WINDOW = 64, 16, 512
DIFF_HEADS, DIFF_DH = 8, 64
DIFF_ROT = DIFF_DH // 4

ROW_TILE = 512
ATT_TILE = 256


def _cparams(*sem):
    return pltpu.CompilerParams(dimension_semantics=sem, vmem_limit_bytes=VMEM_LIMIT)


def _split_bf16(x):
    hi = x.astype(BF16)
    lo = (x - hi.astype(F32)).astype(BF16)
    return hi, lo


def _dot(a, b):
    return jnp.dot(a, b, preferred_element_type=F32)


def _dot_nt(a, b):
    return lax.dot_general(a, b, (((1,), (1,)), ((), ())), preferred_element_type=F32)


def _sigmoid(x):
    return 1.0 / (1.0 + jnp.exp(-x))


def _lane(shape):
    return lax.broadcasted_iota(jnp.int32, shape, 1)


def _adaln_kernel(c_ref, w_ref, b_ref, o_ref):
    c = c_ref[...]
    cond = c * _sigmoid(c)
    c_hi, c_lo = _split_bf16(cond)
    w_hi, w_lo = _split_bf16(w_ref[0])
    o_ref[0] = _dot(c_hi, w_hi) + _dot(c_hi, w_lo) + _dot(c_lo, w_hi) + b_ref[0]


def _adaln(c, ada_w, ada_b):
    depth, d, n = ada_w.shape
    b = c.shape[0]
    tn = 1536
    return pl.pallas_call(
        _adaln_kernel,
        out_shape=jax.ShapeDtypeStruct((depth, b, n), F32),
        grid=(depth, n // tn),
        in_specs=[pl.BlockSpec((b, d), lambda i, j: (0, 0)),
                  pl.BlockSpec((1, d, tn), lambda i, j: (i, 0, j)),
                  pl.BlockSpec((1, 1, tn), lambda i, j: (i, 0, j))],
        out_specs=pl.BlockSpec((1, b, tn), lambda i, j: (i, 0, j)),
        compiler_params=_cparams("parallel", "parallel"),
        name="adaln",
    )(c, ada_w, ada_b.reshape(depth, 1, n))


def _rope_table_kernel(pos_ref, fa_ref, fb_ref, ca_ref, sa_ref, cb_ref, sb_ref):
    pos = pos_ref[...].astype(F32)
    ang_a = pos * fa_ref[...]
    ang_b = pos * fb_ref[...]
    ca_ref[...] = jnp.cos(ang_a)
    sa_ref[...] = jnp.sin(ang_a)
    cb_ref[...] = jnp.cos(ang_b)
    sb_ref[...] = jnp.sin(ang_b)


def _lane_freqs(rot, seg):
    half = rot // 2
    lane = np.arange(LANES)
    inv = ROPE_THETA ** (-(np.arange(half, dtype=np.float32) / np.float32(half)))
    f = np.where(lane % seg < rot, inv.astype(np.float32)[lane % half], 0.0)
    return jnp.asarray(f.reshape(1, LANES), F32)


def _rope_masks(rot, seg):
    half = rot // 2
    lane = np.arange(LANES) % seg
    up = ((lane >= half) & (lane < rot)).astype(np.float32)
    dn = -(lane < half).astype(np.float32)
    return jnp.asarray(up.reshape(1, LANES)), jnp.asarray(dn.reshape(1, LANES))


def _rope_tables(pos_col):
    rows = pos_col.shape[0]
    tr = min(rows, 2048)
    spec = pl.BlockSpec((tr, LANES), lambda i: (i, 0))
    vec = pl.BlockSpec((1, LANES), lambda i: (0, 0))
    shp = jax.ShapeDtypeStruct((rows, LANES), F32)
    return pl.pallas_call(
        _rope_table_kernel,
        out_shape=(shp, shp, shp, shp),
        grid=(rows // tr,),
        in_specs=[pl.BlockSpec((tr, 1), lambda i: (i, 0)), vec, vec],
        out_specs=(spec, spec, spec, spec),
        compiler_params=_cparams("parallel"),
        name="rope_tables",
    )(pos_col, _lane_freqs(MLA_ROPE, LANES), _lane_freqs(NSA_ROT, NSA_DH))


def _rope(y, c, s, m_up, m_dn, half):
    up = pltpu.roll(y, half, axis=1)
    dn = pltpu.roll(y, LANES - half, axis=1)
    return y * c + (up * m_up + dn * m_dn) * s


def _rms64(x, gain):
    lo = _lane(x.shape) < 64
    x2 = x * x
    s_lo = jnp.sum(jnp.where(lo, x2, 0.0), axis=-1, keepdims=True)
    s_hi = jnp.sum(jnp.where(lo, 0.0, x2), axis=-1, keepdims=True)
    r = jnp.where(lo, lax.rsqrt(s_lo * (1.0 / 64) + EPS), lax.rsqrt(s_hi * (1.0 / 64) + EPS))
    return x * r * gain


def _norm_mod(x, gain, sc, sh):
    ms = jnp.mean(x * x, axis=-1, keepdims=True)
    return (x * lax.rsqrt(ms + EPS) * gain) * (1.0 + sc) + sh


def _proj_kernel(x_ref, gain_ref, sc_ref, sh_ref, w_ref, o_ref, h_scr):
    @pl.when(pl.program_id(1) == 0)
    def _():
        h_scr[...] = _norm_mod(x_ref[...], gain_ref[...], sc_ref[0], sh_ref[0]).astype(BF16)

    o_ref[...] = _dot(h_scr[...], w_ref[...]).astype(o_ref.dtype)


def _mod_spec(layer, k, nb, rows_per_batch, tm, d):
    per = rows_per_batch // tm
    return pl.BlockSpec((1, 1, d), lambda i, *_: (layer * nb + i // per, 0, k))


def _proj(x2, gain, mod, layer, k_sc, k_sh, w, nb, tn):
    m, d = x2.shape
    n = w.shape[1]
    tm = 1024
    s = m // nb
    return pl.pallas_call(
        _proj_kernel,
        out_shape=jax.ShapeDtypeStruct((m, n), F32),
        grid=(m // tm, n // tn),
        in_specs=[pl.BlockSpec((tm, d), lambda i, j: (i, 0)),
                  pl.BlockSpec((1, d), lambda i, j: (0, 0)),
                  _mod_spec(layer, k_sc, nb, s, tm, d),
                  _mod_spec(layer, k_sh, nb, s, tm, d),
                  pl.BlockSpec((d, tn), lambda i, j: (0, j))],
        out_specs=pl.BlockSpec((tm, tn), lambda i, j: (i, j)),
        scratch_shapes=[pltpu.VMEM((tm, d), BF16)],
        compiler_params=_cparams("parallel", "arbitrary"),
        name="norm_mod_proj",
    )(x2, gain.reshape(1, d), mod, mod, w)


def _mlp_kernel(x_ref, gain_ref, sc_ref, sh_ref, g_ref, w1_ref, w2_ref, o_ref, h_scr, acc_scr):
    j = pl.program_id(1)

    @pl.when(j == 0)
    def _():
        h_scr[...] = _norm_mod(x_ref[...], gain_ref[...], sc_ref[0], sh_ref[0]).astype(BF16)
        acc_scr[...] = jnp.zeros_like(acc_scr)

    a = jnp.maximum(_dot(h_scr[...], w1_ref[...]), 0.0)
    acc_scr[...] += _dot((a * a).astype(BF16), w2_ref[...])

    @pl.when(j == pl.num_programs(1) - 1)
    def _():
        o_ref[...] = x_ref[...] + g_ref[0] * acc_scr[...]


def _mlp(x2, gain, mod, layer, w1, w2, nb):
    m, d = x2.shape
    ff = w1.shape[1]
    tm, tf = 1024, 512
    s = m // nb
    return pl.pallas_call(
        _mlp_kernel,
        out_shape=jax.ShapeDtypeStruct((m, d), F32),
        grid=(m // tm, ff // tf),
        in_specs=[pl.BlockSpec((tm, d), lambda i, j: (i, 0)),
                  pl.BlockSpec((1, d), lambda i, j: (0, 0)),
                  _mod_spec(layer, 4, nb, s, tm, d),
                  _mod_spec(layer, 3, nb, s, tm, d),
                  _mod_spec(layer, 5, nb, s, tm, d),
                  pl.BlockSpec((d, tf), lambda i, j: (0, j)),
                  pl.BlockSpec((tf, d), lambda i, j: (j, 0))],
        out_specs=pl.BlockSpec((tm, d), lambda i, j: (i, 0)),
        scratch_shapes=[pltpu.VMEM((tm, d), BF16), pltpu.VMEM((tm, d), F32)],
        compiler_params=_cparams("parallel", "arbitrary"),
        name="relu2_mlp",
    )(x2, gain.reshape(1, d), mod, mod, mod, w1, w2)


def _out_proj_kernel(n_in, *refs):
    x_ref, g_ref = refs[0], refs[1]
    y_refs = refs[2:2 + n_in]
    w_refs = refs[2 + n_in:2 + 2 * n_in]
    o_ref = refs[2 + 2 * n_in]
    y = _dot(y_refs[0][...], w_refs[0][...])
    for y_ref, w_ref in zip(y_refs[1:], w_refs[1:]):
        y = y + _dot(y_ref[...], w_ref[...])
    o_ref[...] = x_ref[...] + g_ref[0] * y


def _out_proj(x2, mod, layer, ys, ws, nb):
    m, d = x2.shape
    tm = ROW_TILE
    s = m // nb
    in_specs = [pl.BlockSpec((tm, d), lambda i: (i, 0)), _mod_spec(layer, 2, nb, s, tm, d)]
    in_specs += [pl.BlockSpec((tm, y.shape[1]), lambda i: (i, 0)) for y in ys]
    in_specs += [pl.BlockSpec(w.shape, lambda i: (0, 0)) for w in ws]
    return pl.pallas_call(
        functools.partial(_out_proj_kernel, len(ys)),
        out_shape=jax.ShapeDtypeStruct((m, d), F32),
        grid=(m // tm,),
        in_specs=in_specs,
        out_specs=pl.BlockSpec((tm, d), lambda i: (i, 0)),
        compiler_params=_cparams("parallel"),
        name="out_proj_residual",
    )(x2, mod, *ys, *ws)


LOG2E = math.log2(math.e)


SCORE_BOUND = 60.0
BOUND_MARGIN = 1.02


def _needs_shift(dk, q_gain, k_gain):
    bound = dk * jnp.max(jnp.abs(q_gain)) * jnp.max(jnp.abs(k_gain)) * BOUND_MARGIN
    return jnp.logical_not(bound <= SCORE_BOUND)


def _shift_dispatch(needs_shift, attn, *args):
    return lax.cond(needs_shift, functools.partial(attn, True), functools.partial(attn, False), *args)


def _lane_tiles(x):
    return [x[:, LANES * c:LANES * (c + 1)] for c in range(x.shape[1] // LANES)]


def _flash_scratch(n, rows, t):
    return [pltpu.VMEM((n, rows, LANES), F32), pltpu.VMEM((n, rows, LANES), F32)]


def _flash(qs, k_at, v_at, first, i, t, rows_per_map, shift, scratch, window=None):
    n = len(qs)
    rows = qs[0].shape[0]
    r = lax.broadcasted_iota(jnp.int32, (rows, t), 0)
    if rows != rows_per_map:
        r = jnp.where(r >= rows_per_map, r - rows_per_map, r)
    c_minus_r = lax.broadcasted_iota(jnp.int32, (rows, t), 1) - r
    causal = c_minus_r <= 0
    assert window is None or window >= t

    def scores(h, j, diagonal):
        s = _dot_nt(qs[h], k_at(h, j))
        if diagonal:
            return jnp.where(causal, s, NEG)
        if window is not None:
            return jnp.where(c_minus_r > (i - j) * t - window, s, NEG)
        return s

    def max_step(ms, j, diagonal):
        out = []
        for h in range(n):
            m = ms[h]
            for s_c in _lane_tiles(scores(h, j, diagonal)):
                m = jnp.maximum(m, s_c)
            out.append(m)
        return tuple(out)

    if shift:
        ms = tuple(jnp.full((rows, LANES), NEG, F32) for _ in range(n))
        ms = max_step(lax.fori_loop(first, i, lambda j, st: max_step(st, j, False), ms), i, True)
        ms = [jnp.broadcast_to(jnp.max(m, axis=-1, keepdims=True), (rows, LANES)) for m in ms]

    l_ref, acc_ref = scratch
    for h in range(n):
        l_ref[h] = jnp.zeros((rows, LANES), F32)
        acc_ref[h] = jnp.zeros((rows, LANES), F32)

    def sum_step(j, diagonal, p_prev):
        p_new = []
        for h in range(n):
            acc_ref[h] += _dot(p_prev[h], v_at(h, jnp.maximum(j - 1, 0)))
            ps = _lane_tiles(scores(h, j, diagonal))
            ps = [jnp.exp2(s_c - ms[h]) if shift else jnp.exp2(s_c) for s_c in ps]
            l = l_ref[h]
            for p_c in ps:
                l = l + p_c
            l_ref[h] = l
            p_new.append(jnp.concatenate(ps, axis=1).astype(BF16))
        return tuple(p_new)

    p = tuple(jnp.zeros((rows, t), BF16) for _ in range(n))
    p = sum_step(i, True, lax.fori_loop(first, i, lambda j, pp: sum_step(j, False, pp), p))
    return [(acc_ref[h] + _dot(p[h], v_at(h, i))) / jnp.sum(l_ref[h], axis=-1, keepdims=True)
            for h in range(n)]


def _mla_prep_kernel(u_ref, qn_ref, kvn_ref, wq_ref, wk_ref, wv_ref, qg_ref, kg_ref,
                     c_ref, s_ref, mu_ref, md_ref, q_ref, k_ref, v_ref):
    def rms(x, g):
        return x * lax.rsqrt(jnp.mean(x * x, axis=-1, keepdims=True) + EPS) * g

    cq = rms(u_ref[:, 0:256], qn_ref[...]).astype(BF16)
    ckv = rms(u_ref[:, 256:512], kvn_ref[...]).astype(BF16)
    kpe = u_ref[:, 512:640]
    v_ref[...] = _dot(ckv, wv_ref[...]).astype(BF16)
    q_all = _dot(cq, wq_ref[...])
    k_all = _dot(ckv, wk_ref[...])
    c, s, mu, md = c_ref[...], s_ref[...], mu_ref[...], md_ref[...]
    scale = MLA_DK ** -0.5 * LOG2E

    def head(x, g):
        r = lax.rsqrt(jnp.sum(x * x, axis=-1, keepdims=True) * (1.0 / MLA_DK) + EPS)
        return _rope(x * r * g, c, s, mu, md, MLA_ROPE // 2)

    for h in range(MLA_HEADS):
        sl = slice(LANES * h, LANES * (h + 1))
        q_ref[:, sl] = (head(q_all[:, sl], qg_ref[...]) * scale).astype(BF16)
        k_ref[:, sl] = head(k_all[:, sl] + kpe, kg_ref[...]).astype(BF16)


def _mla_prep(u, q_norm, kv_norm, wq, wk, wv, q_gain, k_gain, tab_c, tab_s):
    m = u.shape[0]
    tm = ROW_TILE
    mu, md = _rope_masks(MLA_ROPE, LANES)
    full = lambda a: pl.BlockSpec(a.shape, lambda i: (0, 0))
    row = lambda w: pl.BlockSpec((tm, w), lambda i: (i, 0))
    args = (u, q_norm, kv_norm, wq, wk, wv, q_gain, k_gain, tab_c, tab_s, mu, md)
    in_specs = [row(1024)] + [full(a) for a in args[1:8]] + [row(LANES), row(LANES), full(mu), full(md)]
    return pl.pallas_call(
        _mla_prep_kernel,
        out_shape=(jax.ShapeDtypeStruct((m, 1024), BF16), jax.ShapeDtypeStruct((m, 1024), BF16),
                   jax.ShapeDtypeStruct((m, 512), BF16)),
        grid=(m // tm,),
        in_specs=in_specs,
        out_specs=(row(1024), row(1024), row(512)),
        compiler_params=_cparams("parallel"),
        name="mla_prep",
    )(*args)


def _mla_attn_kernel(shift, q_ref, k_ref, v_ref, o_ref, *scratch):
    i = pl.program_id(1)
    t = ATT_TILE
    lo = _lane((t, LANES)) < 64

    def rows(j):
        return pl.ds(pl.multiple_of(j * t, t), t)

    def tile(h):
        return slice(LANES * h, LANES * (h + 1))

    outs = _flash([q_ref[0, :, tile(h)] for h in range(MLA_HEADS)],
                  lambda h, j: k_ref[0, rows(j), tile(h)],
                  lambda h, j: v_ref[0, rows(j), tile(h // 2)], 0, i, t, t, shift, scratch)
    for p in range(MLA_HEADS // 2):
        o_ref[0, :, tile(p)] = jnp.where(lo, outs[2 * p], outs[2 * p + 1]).astype(BF16)


def _mla_attn(shift, q, k, v):
    b, s, _ = q.shape
    t = ATT_TILE
    return pl.pallas_call(
        functools.partial(_mla_attn_kernel, shift),
        out_shape=jax.ShapeDtypeStruct((b, s, 512), BF16),
        grid=(b, s // t),
        in_specs=[pl.BlockSpec((1, t, 1024), lambda bi, i: (bi, i, 0)),
                  pl.BlockSpec((1, s, 1024), lambda bi, i: (bi, 0, 0)),
                  pl.BlockSpec((1, s, 512), lambda bi, i: (bi, 0, 0))],
        out_specs=pl.BlockSpec((1, t, 512), lambda bi, i: (bi, i, 0)),
        scratch_shapes=_flash_scratch(MLA_HEADS, t, t),
        compiler_params=_cparams("parallel", "arbitrary"),
        name="mla_attention",
    )(q, k, v)


def _diff_prep_kernel(u_ref, g_ref, c_ref, s_ref, mu_ref, md_ref, q_ref, k_ref, v_ref):
    c, s, mu, md = c_ref[...], s_ref[...], mu_ref[...], md_ref[...]
    n = DIFF_HEADS * LANES
    for which, out in ((0, q_ref), (1, k_ref)):
        g = g_ref[which]
        for h in range(DIFF_HEADS):
            sl = slice(LANES * h, LANES * (h + 1))
            x = u_ref[:, which * n + LANES * h:which * n + LANES * (h + 1)]
            out[:, sl] = _rope(_rms64(x, g), c, s, mu, md, DIFF_ROT // 2).astype(BF16)
    v_ref[...] = u_ref[:, 2 * n:3 * n].astype(BF16)


def _diff_prep(u, gains, tab_c, tab_s):
    m = u.shape[0]
    tm = ROW_TILE
    mu, md = _rope_masks(DIFF_ROT, DIFF_DH)
    row = lambda w: pl.BlockSpec((tm, w), lambda i: (i, 0))
    vec = pl.BlockSpec((1, LANES), lambda i: (0, 0))
    shp = jax.ShapeDtypeStruct((m, 1024), BF16)
    return pl.pallas_call(
        _diff_prep_kernel,
        out_shape=(shp, shp, shp),
        grid=(m // tm,),
        in_specs=[row(3072), pl.BlockSpec((2, 1, LANES), lambda i: (0, 0, 0)), row(LANES), row(LANES), vec, vec],
        out_specs=(row(1024), row(1024), row(1024)),
        compiler_params=_cparams("parallel"),
        name="diff_prep",
    )(u, gains, tab_c, tab_s, mu, md)


def _diff_attn_kernel(lam_init, shift, q_ref, k_ref, v_ref, lam_ref, sg_ref, o_ref, *scratch):
    i = pl.program_id(1)
    t = ATT_TILE
    lo = _lane((t, LANES)) < 64
    lam = lam_ref[...]
    lmb = (jnp.exp(jnp.sum(lam[0:1] * lam[1:2], axis=-1, keepdims=True))
           - jnp.exp(jnp.sum(lam[2:3] * lam[3:4], axis=-1, keepdims=True)) + lam_init)

    def rows(j):
        return pl.ds(pl.multiple_of(j * t, t), t)

    def tile(h):
        return slice(LANES * h, LANES * (h + 1))

    def both_maps(h):
        qt = q_ref[0, :, tile(h)]
        zero = jnp.zeros_like(qt)
        return jnp.concatenate([jnp.where(lo, qt, zero), jnp.where(lo, zero, qt)], axis=0)

    group = DIFF_HEADS // 2
    for h0 in range(0, DIFF_HEADS, group):
        outs = _flash([both_maps(h0 + e) for e in range(group)],
                      lambda e, j: k_ref[0, rows(j), tile(h0 + e)],
                      lambda e, j: v_ref[0, rows(j), tile(h0 + e)], 0, i, t, t, shift, scratch)
        for e in range(group):
            o = outs[e][:t] - lmb * outs[e][t:]
            o = o * lax.rsqrt(jnp.mean(o * o, axis=-1, keepdims=True) + EPS) * sg_ref[...]
            o_ref[0, :, tile(h0 + e)] = (o * (1.0 - lam_init)).astype(BF16)


def _diff_attn(lam_init, shift, q, k, v, lam, sub_gain):
    b, s, n = q.shape
    t = ATT_TILE
    full = pl.BlockSpec((1, s, n), lambda bi, i: (bi, 0, 0))
    return pl.pallas_call(
        functools.partial(_diff_attn_kernel, lam_init, shift),
        out_shape=jax.ShapeDtypeStruct((b, s, n), BF16),
        grid=(b, s // t),
        in_specs=[pl.BlockSpec((1, t, n), lambda bi, i: (bi, i, 0)), full, full,
                  pl.BlockSpec(lam.shape, lambda bi, i: (0, 0)),
                  pl.BlockSpec((1, LANES), lambda bi, i: (0, 0))],
        out_specs=pl.BlockSpec((1, t, n), lambda bi, i: (bi, i, 0)),
        scratch_shapes=_flash_scratch(DIFF_HEADS // 2, 2 * t, t),
        compiler_params=_cparams("parallel", "arbitrary"),
        name="diff_attention",
    )(q, k, v, lam, sub_gain.reshape(1, LANES))


def _nsa_prep_kernel(seq, u_ref, qg_ref, kg_ref, c_ref, s_ref, mu_ref, md_ref,
                     q_ref, kslc0_ref, kslc1_ref, kwin_ref, vslc_ref, vwin_ref, kcmp_ref, vcmp_ref):
    c, s, mu, md = c_ref[...], s_ref[...], mu_ref[...], md_ref[...]
    tm = u_ref.shape[0]

    def prep(x, g):
        return _rope(_rms64(x, g), c, s, mu, md, NSA_ROT // 2)

    for hg in range(NSA_HPG):
        sl = slice(LANES * hg, LANES * (hg + 1))
        q_ref[:, sl] = prep(u_ref[:, 768 + LANES * hg:768 + LANES * (hg + 1)], qg_ref[...]).astype(BF16)
    kcmp_ref[...] = u_ref[:, 1280:1408].astype(BF16)
    vcmp_ref[...] = u_ref[:, 1408:1536].astype(BF16)
    kslc = prep(u_ref[:, 1536:1664], kg_ref[1])
    lane = _lane((tm, LANES))
    t_idx = (pl.program_id(0) * tm + lax.broadcasted_iota(jnp.int32, (tm, LANES), 0)) & (seq - 1)
    blk = t_idx >> 6
    kslc0_ref[...] = jnp.where(lane < 64, kslc, jnp.where(lane - 64 == blk, 1.0, 0.0)).astype(BF16)
    kslc1_ref[...] = jnp.where(lane >= 64, kslc, jnp.where(lane == blk, 1.0, 0.0)).astype(BF16)
    vslc_ref[...] = u_ref[:, 1664:1792].astype(BF16)
    kwin_ref[...] = prep(u_ref[:, 1792:1920], kg_ref[2]).astype(BF16)
    vwin_ref[...] = u_ref[:, 1920:2048].astype(BF16)


def _nsa_prep(u, q_gain, k_gain, tab_c, tab_s, seq):
    m = u.shape[0]
    tm = ROW_TILE
    assert seq & (seq - 1) == 0 and seq // NSA_SLC_LEN == 32 and seq % tm == 0
    mu, md = _rope_masks(NSA_ROT, NSA_DH)
    row = lambda w: pl.BlockSpec((tm, w), lambda i: (i, 0))
    vec = pl.BlockSpec((1, LANES), lambda i: (0, 0))
    t128 = jax.ShapeDtypeStruct((m, LANES), BF16)
    return pl.pallas_call(
        functools.partial(_nsa_prep_kernel, seq),
        out_shape=(jax.ShapeDtypeStruct((m, 512), BF16),) + (t128,) * 7,
        grid=(m // tm,),
        in_specs=[row(2048), vec, pl.BlockSpec((3, 1, LANES), lambda i: (0, 0, 0)), row(LANES), row(LANES), vec, vec],
        out_specs=(row(512),) + (row(LANES),) * 7,
        compiler_params=_cparams("parallel"),
        name="nsa_prep",
    )(u, q_gain, k_gain, tab_c, tab_s, mu, md)


def _nsa_compress_kernel(tk_ref, tv_ref, pos_ref, w1a_ref, w1b_ref, w2_ref, kg_ref,
                         c_ref, s_ref, mu_ref, md_ref, kc_ref, vc_ref):
    n_rows = tk_ref.shape[1]
    for j, (t_ref, out) in enumerate(((tk_ref, kc_ref), (tv_ref, vc_ref))):
        tok = t_ref[0]
        p_hi, p_lo = _split_bf16(pos_ref[j])
        w1a, w1b = w1a_ref[j], w1b_ref[j]
        bias = (_dot(p_hi[0], w1a) + _dot(p_lo[0], w1a) + _dot(p_hi[1], w1b) + _dot(p_lo[1], w1b))[0:1]
        hid = _dot(tok, w1a) + pltpu.roll(_dot(tok, w1b), n_rows - 1, axis=0) + bias
        act = jax.nn.gelu(hid, approximate=True)
        cmp = _dot(act.astype(BF16), w2_ref[j])
        if j == 0:
            cmp = _rope(_rms64(cmp, kg_ref[...]), c_ref[0], s_ref[0], mu_ref[...], md_ref[...], NSA_ROT // 2)
        out[0] = cmp.astype(BF16)


def _nsa_compress(tk, tv, pos, w1a, w1b, w2, k_gain0, tab_c, tab_s):
    b, nr, w = tk.shape
    mu, md = _rope_masks(NSA_ROT, NSA_DH)
    full = lambda a: pl.BlockSpec(a.shape, lambda bi: (0,) * a.ndim)
    per_b = lambda a: pl.BlockSpec((1,) + a.shape[1:], lambda bi: (bi,) + (0,) * (a.ndim - 1))
    shp = jax.ShapeDtypeStruct((b, nr, LANES), BF16)
    return pl.pallas_call(
        _nsa_compress_kernel,
        out_shape=(shp, shp),
        grid=(b,),
        in_specs=[per_b(tk), per_b(tv), full(pos), full(w1a), full(w1b), full(w2), full(k_gain0),
                  per_b(tab_c), per_b(tab_s), full(mu), full(md)],
        out_specs=(pl.BlockSpec((1, nr, LANES), lambda bi: (bi, 0, 0)),) * 2,
        compiler_params=_cparams("parallel"),
        name="nsa_compress",
    )(tk, tv, pos, w1a, w1b, w2, k_gain0, tab_c, tab_s, mu, md)


def _nsa_attn_kernel(shift, q_ref, kc_ref, vc_ref, ks0_ref, ks1_ref, vs_ref, kw_ref, vw_ref, gate_ref, cov_ref,
                     o_ref, *scratch):
    i = pl.program_id(1)
    t = ATT_TILE
    n_blk = 32
    q0 = i * t
    lane = _lane((t, LANES))
    lo = lane < 64
    qpos = q0 + lax.broadcasted_iota(jnp.int32, (t, LANES), 0)
    cmp_ok = NSA_CMP_STRIDE * lane + (NSA_CMP_LEN - 1) <= qpos
    gates = _sigmoid(gate_ref[0])

    def rows(j):
        return pl.ds(pl.multiple_of(j * t, t), t)

    def gate(branch, h):
        col = branch * NSA_HEADS + h
        return gates[:, col:col + 1]

    out_ref, flash_scratch = scratch[0], scratch[1:]
    kc, vc = kc_ref[0], vc_ref[0]
    qts = [q_ref[0, :, LANES * hg:LANES * (hg + 1)] for hg in range(NSA_HPG)]
    qs, q_sel = [], []
    for g in range(NSA_GROUPS):
        own = lo if g == 0 else jnp.logical_not(lo)
        qs += [jnp.where(own, qt, jnp.zeros_like(qt)) for qt in qts]

        p_sum = jnp.zeros((t, LANES), F32)
        for hg in range(NSA_HPG):
            h = g * NSA_HPG + hg
            sc = jnp.where(cmp_ok, _dot_nt(qs[h], kc), NEG)
            p = jnp.where(cmp_ok, jnp.exp2(sc - jnp.max(sc, axis=-1, keepdims=True)), 0.0)
            l = jnp.sum(p, axis=-1, keepdims=True)
            p = p * jnp.where(l > 0.0, 1.0 / l, 0.0)
            p_sum = p_sum + p
            out_ref[h] = gate(0, h) * _dot(p.astype(BF16), vc)
        p_hi, p_lo = _split_bf16(p_sum)
        imp = (_dot_nt(cov_ref[...], p_hi) + _dot_nt(cov_ref[...], p_lo))[0:n_blk]
        blk = lax.broadcasted_iota(jnp.int32, (n_blk, t), 0)
        jt = (q0 + lax.broadcasted_iota(jnp.int32, (n_blk, t), 1)) >> 6
        allowed = blk <= jt
        forced = allowed & ((blk == 0) | (blk >= jt - 1))
        imp = jnp.where(forced, FORCED, jnp.where(allowed, imp, NEG))
        rank = jnp.zeros((n_blk, t), jnp.int32)
        for jp in range(n_blk):
            other = imp[jp:jp + 1, :]
            ahead = (other > imp) | ((other == imp) & (blk > jp))
            rank = rank + ahead.astype(jnp.int32)
        pen_t = jnp.where((rank < NSA_TOP_N) & allowed, 0.0, NEG)
        pen_t = jnp.concatenate([pen_t, jnp.zeros((LANES - n_blk, t), F32)], axis=0)
        pen = jnp.transpose(pen_t)
        if g == 0:
            pen = pltpu.roll(pen, 64, axis=1)
        pen = pen.astype(BF16)
        q_sel += [jnp.where(own, qt, pen) for qt in qts]

    slc = _flash(q_sel, lambda h, j: (ks0_ref if h < NSA_HPG else ks1_ref)[0, rows(j), :],
                 lambda h, j: vs_ref[0, rows(j), :], 0, i, t, t, shift, flash_scratch)
    for h in range(NSA_HEADS):
        out_ref[h] += gate(1, h) * slc[h]

    win = _flash(qs, lambda h, j: kw_ref[0, rows(j), :], lambda h, j: vw_ref[0, rows(j), :],
                 jnp.maximum(i - NSA_WINDOW // t, 0), i, t, t, shift, flash_scratch, window=NSA_WINDOW)
    for hg in range(NSA_HPG):
        h0, h1 = hg, NSA_HPG + hg
        o_ref[0, :, LANES * hg:LANES * (hg + 1)] = jnp.where(
            lo, out_ref[h0] + gate(2, h0) * win[h0], out_ref[h1] + gate(2, h1) * win[h1]).astype(BF16)


def _nsa_attn(shift, q, kc, vc, kslc0, kslc1, vslc, kwin, vwin, u3, cover_t):
    b, s, _ = q.shape
    t = ATT_TILE
    seq = pl.BlockSpec((1, s, LANES), lambda bi, i: (bi, 0, 0))
    cmp = pl.BlockSpec((1, LANES, LANES), lambda bi, i: (bi, 0, 0))
    return pl.pallas_call(
        functools.partial(_nsa_attn_kernel, shift),
        out_shape=jax.ShapeDtypeStruct((b, s, 512), BF16),
        grid=(b, s // t),
        in_specs=[pl.BlockSpec((1, t, 512), lambda bi, i: (bi, i, 0)), cmp, cmp, seq, seq, seq, seq, seq,
                  pl.BlockSpec((1, t, LANES), lambda bi, i: (bi, i, 5)),
                  pl.BlockSpec((LANES, LANES), lambda bi, i: (0, 0))],
        out_specs=pl.BlockSpec((1, t, 512), lambda bi, i: (bi, i, 0)),
        scratch_shapes=[pltpu.VMEM((NSA_HEADS, t, LANES), F32)] + _flash_scratch(NSA_HEADS, t, t),
        compiler_params=_cparams("parallel", "arbitrary"),
        name="nsa_attention",
    )(q, kc, vc, kslc0, kslc1, vslc, kwin, vwin, u3, cover_t)


def _pad_lanes(a, width):
    return jnp.pad(a, [(0, 0)] * (a.ndim - 1) + [(0, width - a.shape[-1])])


def _hy_in_weight(w):
    d = w.shape[0]
    cq, ckv, kpe, nq, nkv, gate = jnp.split(w, [256, 512, 544, 1056, 1824], axis=1)
    nq = nq.reshape(d, NSA_GROUPS, NSA_HPG, NSA_DH).transpose(0, 2, 1, 3).reshape(d, 512)
    return jnp.concatenate([cq, ckv, _pad_lanes(kpe, LANES), _pad_lanes(gate, LANES), nq, nkv], axis=1).astype(BF16)


def _cover_t():
    nc, ns = 127, 32
    c_start = np.arange(nc) * NSA_CMP_STRIDE
    c_end = c_start + NSA_CMP_LEN - 1
    j_start = np.arange(ns) * NSA_SLC_LEN
    cover = ((c_start[:, None] <= j_start[None, :] + NSA_SLC_LEN - 1) & (c_end[:, None] >= j_start[None, :]))
    out = np.zeros((LANES, LANES), np.float32)
    out[:ns, :nc] = cover.T
    return jnp.asarray(out, BF16)


def _compress_weights(cmp_pos, w1, w2):
    half = NSA_CMP_STRIDE
    w1 = w1.reshape(2, 2, half, NSA_DH, NSA_CMP_HIDDEN)
    eye = jnp.eye(NSA_GROUPS, dtype=F32)
    w1e = jnp.einsum('jcldn,gh->jclgdhn', w1, eye).reshape(2, 2, half * LANES, NSA_GROUPS * NSA_CMP_HIDDEN)
    w2e = jnp.einsum('jnd,gh->jgnhd', w2, eye).reshape(2, NSA_GROUPS * NSA_CMP_HIDDEN, LANES)
    pos = cmp_pos.reshape(2, 2, half, 1, NSA_DH)
    pos = jnp.broadcast_to(pos, (2, 2, half, NSA_GROUPS, NSA_DH)).reshape(2, 2, 1, half * LANES)
    pos = jnp.broadcast_to(pos, (2, 2, 8, half * LANES))
    return pos, w1e[:, 0].astype(BF16), w1e[:, 1].astype(BF16), w2e.astype(BF16)


def kernel(x, c, positions, ada_w, ada_b, norm_mix, norm_mlp, mlp_w1, mlp_w2, hy_w_in, hy_w_out, mla_q_norm, mla_w_uq, mla_kv_norm, mla_w_ukv, mla_q_gain, mla_k_gain, nsa_q_gain, nsa_k_gain, nsa_cmp_pos, nsa_cmp_w1, nsa_cmp_w2, diff_w_qkv, diff_w_out, diff_q_gain, diff_k_gain, diff_lambda, diff_sub_gain):
    nb, seq, d = x.shape
    depth = ada_w.shape[0]
    m = nb * seq
    n_cmp = seq // NSA_CMP_STRIDE

    mod = _adaln(c, ada_w, ada_b).reshape(depth * nb, 1, 6 * d)
    tab_ca, tab_sa, tab_cb, tab_sb = _rope_tables(positions.reshape(m, 1))
    pos_c = jnp.pad(positions[:, NSA_CMP_LEN - 1::NSA_CMP_STRIDE], ((0, 0), (0, 1)))
    _, _, tab_cc, tab_sc = _rope_tables(pos_c.reshape(nb * n_cmp, 1))
    tab_cc, tab_sc = tab_cc.reshape(nb, n_cmp, LANES), tab_sc.reshape(nb, n_cmp, LANES)
    cover_t = _cover_t()

    x2 = x.reshape(m, d)
    for i in range(depth):
        j = i // 2
        if i % 2 == 0:
            u = _proj(x2, norm_mix[i], mod, i, 1, 0, _hy_in_weight(hy_w_in[j]), nb, 1024)

            wq = _pad_lanes(mla_w_uq[j].reshape(MLA_Q_RANK, MLA_HEADS, MLA_DK), LANES).reshape(MLA_Q_RANK, -1)
            wkv = mla_w_ukv[j].reshape(MLA_KV_RANK, MLA_HEADS, MLA_NOPE + MLA_V)
            wk = jnp.pad(wkv[..., :MLA_NOPE], ((0, 0), (0, 0), (MLA_ROPE, LANES - MLA_DK))).reshape(MLA_KV_RANK, -1)
            wv = wkv[..., MLA_NOPE:].reshape(MLA_KV_RANK, -1)
            q_mla, k_mla, v_mla = _mla_prep(
                u, mla_q_norm[j].reshape(1, -1), mla_kv_norm[j].reshape(1, -1),
                wq.astype(BF16), wk.astype(BF16), wv.astype(BF16),
                _pad_lanes(mla_q_gain[j], LANES).reshape(1, LANES), _pad_lanes(mla_k_gain[j], LANES).reshape(1, LANES),
                tab_ca, tab_sa)
            y_mla = _shift_dispatch(
                _needs_shift(MLA_DK, mla_q_gain[j] * (MLA_DK ** -0.5 * LOG2E), mla_k_gain[j]), _mla_attn,
                q_mla.reshape(nb, seq, -1), k_mla.reshape(nb, seq, -1), v_mla.reshape(nb, seq, -1))

            qg = jnp.tile(nsa_q_gain[j] * (NSA_DH ** -0.5 * LOG2E), 2).reshape(1, LANES)
            kg = jnp.tile(nsa_k_gain[j], (1, 2)).reshape(3, 1, LANES)
            q_nsa, kslc0, kslc1, kwin, vslc, vwin, kcmp, vcmp = _nsa_prep(u, qg, kg, tab_cb, tab_sb, seq)
            pos_e, w1a, w1b, w2e = _compress_weights(nsa_cmp_pos[j], nsa_cmp_w1[j], nsa_cmp_w2[j])
            chunks = lambda a: a.reshape(nb, n_cmp, NSA_CMP_STRIDE * LANES)
            kc, vc = _nsa_compress(chunks(kcmp), chunks(vcmp), pos_e, w1a, w1b, w2e, kg[0], tab_cc, tab_sc)
            seq3 = lambda a: a.reshape(nb, seq, -1)
            y_nsa = _shift_dispatch(
                _needs_shift(NSA_DH, qg, nsa_k_gain[j][1:]), _nsa_attn,
                seq3(q_nsa), kc, vc, seq3(kslc0), seq3(kslc1), seq3(vslc), seq3(kwin), seq3(vwin), seq3(u), cover_t)

            w_out = hy_w_out[j]
            w_nsa = w_out[512:].reshape(NSA_GROUPS, NSA_HPG, NSA_DH, d).transpose(1, 0, 2, 3).reshape(512, d)
            x2 = _out_proj(x2, mod, i, [y_mla.reshape(m, -1), y_nsa.reshape(m, -1)],
                           [w_out[:512].astype(BF16), w_nsa.astype(BF16)], nb)
        else:
            lam_init = 0.8 - 0.6 * math.exp(-0.3 * i)
            u = _proj(x2, norm_mix[i], mod, i, 1, 0, diff_w_qkv[j].astype(BF16), nb, 1024)
            gains = jnp.stack([jnp.tile(diff_q_gain[j] * (DIFF_DH ** -0.5 * LOG2E), 2), jnp.tile(diff_k_gain[j], 2)])
            q, k, v = _diff_prep(u, gains.reshape(2, 1, LANES), tab_cb, tab_sb)
            seq3 = lambda a: a.reshape(nb, seq, -1)
            y = _shift_dispatch(
                _needs_shift(DIFF_DH, gains[0], gains[1]), functools.partial(_diff_attn, lam_init),
                seq3(q), seq3(k), seq3(v), diff_lambda[j], diff_sub_gain[j])
            x2 = _out_proj(x2, mod, i, [y.reshape(m, -1)], [diff_w_out[j].astype(BF16)], nb)
        x2 = _mlp(x2, norm_mlp[i], mod, i, mlp_w1[i].astype(BF16), mlp_w2[i].astype(BF16), nb)
    return x2.reshape(nb, seq, d)
```

```python
import functools
import math

import numpy as np
import jax
import jax.numpy as jnp
from jax import lax
from jax.experimental import pallas as pl
from jax.experimental.pallas import tpu as pltpu

F32 = jnp.float32
BF16 = jnp.bfloat16

LANES = 128
VMEM_LIMIT = 52 * 1024 * 1024

ROPE_THETA = 500000.0
EPS = 1e-6
NEG = -1e30
FORCED = 1e9

MLA_HEADS, MLA_NOPE, MLA_ROPE, MLA_V = 8, 64, 32, 64
MLA_Q_RANK, MLA_KV_RANK = 256, 256
MLA_DK = MLA_ROPE + MLA_NOPE
NSA_HEADS, NSA_GROUPS, NSA_DH = 8, 2, 64
NSA_HPG = NSA_HEADS // NSA_GROUPS
NSA_ROT = NSA_DH // 4
NSA_CMP_LEN, NSA_CMP_STRIDE, NSA_CMP_HIDDEN = 32, 16, 128
NSA_SLC_LEN, NSA_TOP_N, NSA_WINDOW = 64, 16, 512
DIFF_HEADS, DIFF_DH = 8, 64
DIFF_ROT = DIFF_DH // 4

ROW_TILE = 512
ATT_TILE = 256
MXU_COLS = 256


def _cparams(*sem):
    return pltpu.CompilerParams(dimension_semantics=sem, vmem_limit_bytes=VMEM_LIMIT)


def _split_bf16(x):
    hi = x.astype(BF16)
    lo = (x - hi.astype(F32)).astype(BF16)
    return hi, lo


def _dot(a, b):
    return jnp.dot(a, b, preferred_element_type=F32)


def _dot_nt(a, b):
    return lax.dot_general(a, b, (((1,), (1,)), ((), ())), preferred_element_type=F32)


def _sigmoid(x):
    return 1.0 / (1.0 + jnp.exp(-x))


def _lane(shape):
    return lax.broadcasted_iota(jnp.int32, shape, 1)


def _adaln_kernel(c_ref, w_ref, b_ref, o_ref):
    c = c_ref[...]
    cond = c * _sigmoid(c)
    c_hi, c_lo = _split_bf16(cond)
    w_hi, w_lo = _split_bf16(w_ref[0])
    o_ref[0] = _dot(c_hi, w_hi) + _dot(c_hi, w_lo) + _dot(c_lo, w_hi) + b_ref[0]


def _adaln(c, ada_w, ada_b):
    depth, d, n = ada_w.shape
    b = c.shape[0]
    tn = 1536
    return pl.pallas_call(
        _adaln_kernel,
        out_shape=jax.ShapeDtypeStruct((depth, b, n), F32),
        grid=(depth, n // tn),
        in_specs=[pl.BlockSpec((b, d), lambda i, j: (0, 0)),
                  pl.BlockSpec((1, d, tn), lambda i, j: (i, 0, j)),
                  pl.BlockSpec((1, 1, tn), lambda i, j: (i, 0, j))],
        out_specs=pl.BlockSpec((1, b, tn), lambda i, j: (i, 0, j)),
        compiler_params=_cparams("parallel", "parallel"),
        name="adaln",
    )(c, ada_w, ada_b.reshape(depth, 1, n))


def _rope_table_kernel(pos_ref, fa_ref, ma_ref, fb_ref, mb_ref, ca_ref, ua_ref, da_ref, cb_ref, ub_ref, db_ref):
    pos = pos_ref[...].astype(F32)
    for f_ref, m_ref, c_ref, u_ref, d_ref in ((fa_ref, ma_ref, ca_ref, ua_ref, da_ref),
                                              (fb_ref, mb_ref, cb_ref, ub_ref, db_ref)):
        ang = pos * f_ref[...]
        sin = jnp.sin(ang)
        c_ref[...] = jnp.cos(ang)
        u_ref[...] = sin * m_ref[0:1]
        d_ref[...] = sin * m_ref[1:2]


def _lane_freqs(rot, seg):
    half = rot // 2
    lane = np.arange(LANES)
    inv = ROPE_THETA ** (-(np.arange(half, dtype=np.float32) / np.float32(half)))
    f = np.where(lane % seg < rot, inv.astype(np.float32)[lane % half], 0.0)
    return jnp.asarray(f.reshape(1, LANES), F32)


def _rope_masks(rot, seg):
    half = rot // 2
    lane = np.arange(LANES) % seg
    up = ((lane >= half) & (lane < rot)).astype(np.float32)
    dn = -(lane < half).astype(np.float32)
    return jnp.asarray(np.stack([up, dn]))


def _rope_tables(pos_col):
    rows = pos_col.shape[0]
    tr = min(rows, 2048)
    spec = pl.BlockSpec((tr, LANES), lambda i: (i, 0))
    vec = pl.BlockSpec((1, LANES), lambda i: (0, 0))
    msk = pl.BlockSpec((2, LANES), lambda i: (0, 0))
    shp = jax.ShapeDtypeStruct((rows, LANES), F32)
    out = pl.pallas_call(
        _rope_table_kernel,
        out_shape=(shp,) * 6,
        grid=(rows // tr,),
        in_specs=[pl.BlockSpec((tr, 1), lambda i: (i, 0)), vec, msk, vec, msk],
        out_specs=(spec,) * 6,
        compiler_params=_cparams("parallel"),
        name="rope_tables",
    )(pos_col, _lane_freqs(MLA_ROPE, LANES), _rope_masks(MLA_ROPE, LANES),
      _lane_freqs(NSA_ROT, NSA_DH), _rope_masks(NSA_ROT, NSA_DH))
    return out[:3], out[3:]


def _rope(y, tab, half):
    c, s_up, s_dn = tab
    up = pltpu.roll(y, half, axis=1)
    dn = pltpu.roll(y, LANES - half, axis=1)
    return y * c + up * s_up + dn * s_dn


def _rms64(x, gain):
    lo = _lane(x.shape) < 64
    x2 = x * x
    s_lo = jnp.sum(jnp.where(lo, x2, 0.0), axis=-1, keepdims=True)
    s_hi = jnp.sum(jnp.where(lo, 0.0, x2), axis=-1, keepdims=True)
    r = jnp.where(lo, lax.rsqrt(s_lo * (1.0 / 64) + EPS), lax.rsqrt(s_hi * (1.0 / 64) + EPS))
    return x * (r * gain)


def _rms(x, gain):
    return x * lax.rsqrt(jnp.mean(x * x, axis=-1, keepdims=True) + EPS) * gain


def _norm_mod(x, gain, sc, sh):
    return _rms(x, gain) * (1.0 + sc) + sh


def _mod_spec(layer, k, nb, rows_per_batch, tm, d):
    per = rows_per_batch // tm
    return pl.BlockSpec((1, 1, d), lambda i, *_: (layer * nb + i // per, 0, k))


def _mlp_kernel(x_ref, gain_ref, sc_ref, sh_ref, g_ref, w1_ref, w2_ref, o_ref, h_scr, acc_scr):
    j = pl.program_id(1)

    @pl.when(j == 0)
    def _():
        h_scr[...] = _norm_mod(x_ref[...], gain_ref[...], sc_ref[0], sh_ref[0]).astype(BF16)
        acc_scr[...] = jnp.zeros_like(acc_scr)

    a = jnp.maximum(_dot(h_scr[...], w1_ref[...]), 0.0)
    acc_scr[...] += _dot((a * a).astype(BF16), w2_ref[...])

    @pl.when(j == pl.num_programs(1) - 1)
    def _():
        o_ref[...] = x_ref[...] + g_ref[0] * acc_scr[...]


def _mlp(x2, gain, mod, layer, w1, w2, nb):
    m, d = x2.shape
    ff = w1.shape[1]
    tm, tf = 1024, 512
    s = m // nb
    return pl.pallas_call(
        _mlp_kernel,
        out_shape=jax.ShapeDtypeStruct((m, d), F32),
        grid=(m // tm, ff // tf),
        in_specs=[pl.BlockSpec((tm, d), lambda i, j: (i, 0)),
                  pl.BlockSpec((1, d), lambda i, j: (0, 0)),
                  _mod_spec(layer, 4, nb, s, tm, d),
                  _mod_spec(layer, 3, nb, s, tm, d),
                  _mod_spec(layer, 5, nb, s, tm, d),
                  pl.BlockSpec((d, tf), lambda i, j: (0, j)),
                  pl.BlockSpec((tf, d), lambda i, j: (j, 0))],
        out_specs=pl.BlockSpec((tm, d), lambda i, j: (i, 0)),
        scratch_shapes=[pltpu.VMEM((tm, d), BF16), pltpu.VMEM((tm, d), F32)],
        compiler_params=_cparams("parallel", "arbitrary"),
        name="relu2_mlp",
    )(x2, gain.reshape(1, d), mod, mod, mod, w1, w2)


def _out_proj_kernel(n_in, *refs):
    x_ref, g_ref = refs[0], refs[1]
    y_refs = refs[2:2 + n_in]
    w_refs = refs[2 + n_in:2 + 2 * n_in]
    o_ref = refs[2 + 2 * n_in]
    y = _dot(y_refs[0][...], w_refs[0][...])
    for y_ref, w_ref in zip(y_refs[1:], w_refs[1:]):
        y = y + _dot(y_ref[...], w_ref[...])
    o_ref[...] = x_ref[...] + g_ref[0] * y


def _out_proj(x2, mod, layer, ys, ws, nb):
    m, d = x2.shape
    tm = ROW_TILE
    s = m // nb
    in_specs = [pl.BlockSpec((tm, d), lambda i: (i, 0)), _mod_spec(layer, 2, nb, s, tm, d)]
    in_specs += [pl.BlockSpec((tm, y.shape[1]), lambda i: (i, 0)) for y in ys]
    in_specs += [pl.BlockSpec(w.shape, lambda i: (0, 0)) for w in ws]
    return pl.pallas_call(
        functools.partial(_out_proj_kernel, len(ys)),
        out_shape=jax.ShapeDtypeStruct((m, d), F32),
        grid=(m // tm,),
        in_specs=in_specs,
        out_specs=pl.BlockSpec((tm, d), lambda i: (i, 0)),
        compiler_params=_cparams("parallel"),
        name="out_proj_residual",
    )(x2, mod, *ys, *ws)


LOG2E = math.log2(math.e)


SCORE_BOUND = 60.0
BOUND_MARGIN = 1.02


def _needs_shift(dk, q_gain, k_gain):
    bound = dk * jnp.max(jnp.abs(q_gain)) * jnp.max(jnp.abs(k_gain)) * BOUND_MARGIN
    return jnp.logical_not(bound <= SCORE_BOUND)


def _shift_dispatch(needs_shift, attn, *args):
    return lax.cond(needs_shift, functools.partial(attn, True), functools.partial(attn, False), *args)


def _lane_tiles(x):
    return [x[:, LANES * c:LANES * (c + 1)] for c in range(x.shape[1] // LANES)]


def _flash_scratch(n, rows, t):
    return [pltpu.VMEM((n, rows, LANES), F32), pltpu.VMEM((n, rows, LANES), F32)]


def _flash(qs, k_at, v_at, first, i, t, rows_per_map, shift, scratch, window=None):
    n = len(qs)
    rows = qs[0].shape[0]
    r = lax.broadcasted_iota(jnp.int32, (rows, t), 0)
    if rows != rows_per_map:
        r = jnp.where(r >= rows_per_map, r - rows_per_map, r)
    c_minus_r = lax.broadcasted_iota(jnp.int32, (rows, t), 1) - r
    causal = c_minus_r <= 0
    assert window is None or window >= t

    def scores(h, j, diagonal):
        s = _dot_nt(qs[h], k_at(h, j))
        if diagonal:
            return jnp.where(causal, s, NEG)
        if window is not None:
            return jnp.where(c_minus_r > (i - j) * t - window, s, NEG)
        return s

    def max_step(ms, j, diagonal):
        out = []
        for h in range(n):
            m = ms[h]
            for s_c in _lane_tiles(scores(h, j, diagonal)):
                m = jnp.maximum(m, s_c)
            out.append(m)
        return tuple(out)

    if shift:
        ms = tuple(jnp.full((rows, LANES), NEG, F32) for _ in range(n))
        ms = max_step(lax.fori_loop(first, i, lambda j, st: max_step(st, j, False), ms), i, True)
        ms = [jnp.broadcast_to(jnp.max(m, axis=-1, keepdims=True), (rows, LANES)) for m in ms]

    l_ref, acc_ref = scratch
    for h in range(n):
        l_ref[h] = jnp.zeros((rows, LANES), F32)
        acc_ref[h] = jnp.zeros((rows, LANES), F32)

    def sum_step(j, diagonal, p_prev):
        p_new = []
        for h in range(n):
            acc_ref[h] += _dot(p_prev[h], v_at(h, jnp.maximum(j - 1, 0)))
            ps = _lane_tiles(scores(h, j, diagonal))
            ps = [jnp.exp2(s_c - ms[h]) if shift else jnp.exp2(s_c) for s_c in ps]
            l = l_ref[h]
            for p_c in ps:
                l = l + p_c
            l_ref[h] = l
            p_new.append(jnp.concatenate(ps, axis=1).astype(BF16))
        return tuple(p_new)

    p = tuple(jnp.zeros((rows, t), BF16) for _ in range(n))
    p = sum_step(i, True, lax.fori_loop(first, i, lambda j, pp: sum_step(j, False, pp), p))
    return [(acc_ref[h] + _dot(p[h], v_at(h, i))) / jnp.sum(l_ref[h], axis=-1, keepdims=True)
            for h in range(n)]


def _hy_in_kernel(seq, x_ref, gain_ref, sc_ref, sh_ref, w_ref, qn_ref, kvn_ref, qg_ref, kg_ref,
                  c_ref, su_ref, sd_ref,
                  lat_ref, q_ref, ks0_ref, ks1_ref, kw_ref, pg_ref, kcmp_ref, vcmp_ref, vsw_ref):
    tm = x_ref.shape[0]
    h = _norm_mod(x_ref[...], gain_ref[...], sc_ref[0], sh_ref[0]).astype(BF16)
    tab = (c_ref[...], su_ref[...], sd_ref[...])

    def cols(a, b):
        return _dot(h, w_ref[:, a:b])

    def prep(x, g):
        return _rope(_rms64(x, g), tab, NSA_ROT // 2)

    lat_ref[:, 0:256] = _rms(cols(0, 256), qn_ref[...]).astype(BF16)
    lat_ref[:, 256:512] = _rms(cols(256, 512), kvn_ref[...]).astype(BF16)
    for ch in range(2):
        y = cols(512 + MXU_COLS * ch, 512 + MXU_COLS * (ch + 1))
        for e in range(2):
            q_ref[:, MXU_COLS * ch + LANES * e:MXU_COLS * ch + LANES * (e + 1)] = prep(
                y[:, LANES * e:LANES * (e + 1)], qg_ref[...]).astype(BF16)
    y = cols(1024, 1280)
    kslc = prep(y[:, 0:LANES], kg_ref[1])
    lane = _lane((tm, LANES))
    t_idx = (pl.program_id(0) * tm + lax.broadcasted_iota(jnp.int32, (tm, LANES), 0)) & (seq - 1)
    blk = t_idx >> 6
    ks0_ref[...] = jnp.where(lane < 64, kslc, jnp.where(lane - 64 == blk, 1.0, 0.0)).astype(BF16)
    ks1_ref[...] = jnp.where(lane >= 64, kslc, jnp.where(lane == blk, 1.0, 0.0)).astype(BF16)
    kw_ref[...] = prep(y[:, LANES:2 * LANES], kg_ref[2]).astype(BF16)
    pg_ref[...] = cols(1280, 1536)
    y = cols(1536, 1792)
    kcmp_ref[...] = y[:, 0:LANES].astype(BF16)
    vcmp_ref[...] = y[:, LANES:2 * LANES].astype(BF16)
    vsw_ref[...] = cols(1792, 2048).astype(BF16)


def _hy_in(x2, gain, mod, layer, w, q_norm, kv_norm, q_gain, k_gain, tab, nb):
    m, d = x2.shape
    tm = ROW_TILE
    seq = m // nb
    assert seq & (seq - 1) == 0 and seq // NSA_SLC_LEN == 32 and seq % tm == 0
    row = lambda wd: pl.BlockSpec((tm, wd), lambda i: (i, 0))
    full = lambda a: pl.BlockSpec(a.shape, lambda i: (0,) * a.ndim)
    bf = lambda wd: jax.ShapeDtypeStruct((m, wd), BF16)
    widths = (512, 512, LANES, LANES, LANES, 2 * LANES, LANES, LANES, 2 * LANES)
    shapes = tuple(jax.ShapeDtypeStruct((m, wd), F32) if k == 5 else bf(wd) for k, wd in enumerate(widths))
    gain = gain.reshape(1, d)
    return pl.pallas_call(
        functools.partial(_hy_in_kernel, seq),
        out_shape=shapes,
        grid=(m // tm,),
        in_specs=[row(d), full(gain), _mod_spec(layer, 1, nb, seq, tm, d), _mod_spec(layer, 0, nb, seq, tm, d),
                  full(w), full(q_norm), full(kv_norm), full(q_gain), full(k_gain),
                  row(LANES), row(LANES), row(LANES)],
        out_specs=tuple(row(wd) for wd in widths),
        compiler_params=_cparams("parallel"),
        name="hy_in_proj",
    )(x2, gain, mod, mod, w, q_norm, kv_norm, q_gain, k_gain, *tab)


def _mla_prep_kernel(lat_ref, kpe_ref, wq_ref, wk_ref, wv_ref, qg_ref, kg_ref, c_ref, su_ref, sd_ref,
                     q_ref, k_ref, v_ref):
    cq, ckv = lat_ref[:, 0:256], lat_ref[:, 256:512]
    kpe = kpe_ref[...]
    tab = (c_ref[...], su_ref[...], sd_ref[...])
    scale = MLA_DK ** -0.5 * LOG2E
    v_ref[...] = _dot(ckv, wv_ref[...]).astype(BF16)

    def head(x, g):
        r = lax.rsqrt(jnp.sum(x * x, axis=-1, keepdims=True) * (1.0 / MLA_DK) + EPS)
        return _rope(x * r * g, tab, MLA_ROPE // 2)

    for ch in range(MLA_HEADS // 2):
        cs = slice(MXU_COLS * ch, MXU_COLS * (ch + 1))
        q2, k2 = _dot(cq, wq_ref[:, cs]), _dot(ckv, wk_ref[:, cs])
        for e in range(2):
            sl = slice(MXU_COLS * ch + LANES * e, MXU_COLS * ch + LANES * (e + 1))
            es = slice(LANES * e, LANES * (e + 1))
            q_ref[:, sl] = (head(q2[:, es], qg_ref[...]) * scale).astype(BF16)
            k_ref[:, sl] = head(k2[:, es] + kpe, kg_ref[...]).astype(BF16)


def _mla_prep(lat, pg, wq, wk, wv, q_gain, k_gain, tab):
    m = lat.shape[0]
    tm = ROW_TILE
    full = lambda a: pl.BlockSpec(a.shape, lambda i: (0, 0))
    row = lambda w: pl.BlockSpec((tm, w), lambda i: (i, 0))
    return pl.pallas_call(
        _mla_prep_kernel,
        out_shape=(jax.ShapeDtypeStruct((m, 1024), BF16), jax.ShapeDtypeStruct((m, 1024), BF16),
                   jax.ShapeDtypeStruct((m, 512), BF16)),
        grid=(m // tm,),
        in_specs=[row(512), row(LANES), full(wq), full(wk), full(wv), full(q_gain), full(k_gain),
                  row(LANES), row(LANES), row(LANES)],
        out_specs=(row(1024), row(1024), row(512)),
        compiler_params=_cparams("parallel"),
        name="mla_prep",
    )(lat, pg, wq, wk, wv, q_gain, k_gain, *tab)


def _mla_attn_kernel(shift, q_ref, k_ref, v_ref, o_ref, *scratch):
    i = pl.program_id(1)
    t = ATT_TILE
    lo = _lane((t, LANES)) < 64

    def rows(j):
        return pl.ds(pl.multiple_of(j * t, t), t)

    def tile(h):
        return slice(LANES * h, LANES * (h + 1))

    outs = _flash([q_ref[0, :, tile(h)] for h in range(MLA_HEADS)],
                  lambda h, j: k_ref[0, rows(j), tile(h)],
                  lambda h, j: v_ref[0, rows(j), tile(h // 2)], 0, i, t, t, shift, scratch)
    for p in range(MLA_HEADS // 2):
        o_ref[0, :, tile(p)] = jnp.where(lo, outs[2 * p], outs[2 * p + 1]).astype(BF16)


def _mla_attn(shift, q, k, v):
    b, s, _ = q.shape
    t = ATT_TILE
    return pl.pallas_call(
        functools.partial(_mla_attn_kernel, shift),
        out_shape=jax.ShapeDtypeStruct((b, s, 512), BF16),
        grid=(b, s // t),
        in_specs=[pl.BlockSpec((1, t, 1024), lambda bi, i: (bi, i, 0)),
                  pl.BlockSpec((1, s, 1024), lambda bi, i: (bi, 0, 0)),
                  pl.BlockSpec((1, s, 512), lambda bi, i: (bi, 0, 0))],
        out_specs=pl.BlockSpec((1, t, 512), lambda bi, i: (bi, i, 0)),
        scratch_shapes=_flash_scratch(MLA_HEADS, t, t),
        compiler_params=_cparams("parallel", "arbitrary"),
        name="mla_attention",
    )(q, k, v)


def _diff_in_kernel(x_ref, gain_ref, sc_ref, sh_ref, w_ref, g_ref, c_ref, su_ref, sd_ref, q_ref, k_ref, v_ref):
    h = _norm_mod(x_ref[...], gain_ref[...], sc_ref[0], sh_ref[0]).astype(BF16)
    tab = (c_ref[...], su_ref[...], sd_ref[...])
    n = DIFF_HEADS * LANES
    for which, out in ((0, q_ref), (1, k_ref)):
        g = g_ref[which]
        for ch in range(n // MXU_COLS):
            y = _dot(h, w_ref[:, which * n + MXU_COLS * ch:which * n + MXU_COLS * (ch + 1)])
            for e in range(2):
                out[:, MXU_COLS * ch + LANES * e:MXU_COLS * ch + LANES * (e + 1)] = _rope(
                    _rms64(y[:, LANES * e:LANES * (e + 1)], g), tab, DIFF_ROT // 2).astype(BF16)
    for ch in range(n // MXU_COLS):
        cs = slice(MXU_COLS * ch, MXU_COLS * (ch + 1))
        v_ref[:, cs] = _dot(h, w_ref[:, 2 * n + MXU_COLS * ch:2 * n + MXU_COLS * (ch + 1)]).astype(BF16)


def _diff_in(x2, gain, mod, layer, w, gains, tab, nb):
    m, d = x2.shape
    tm = ROW_TILE
    seq = m // nb
    row = lambda wd: pl.BlockSpec((tm, wd), lambda i: (i, 0))
    full = lambda a: pl.BlockSpec(a.shape, lambda i: (0,) * a.ndim)
    shp = jax.ShapeDtypeStruct((m, DIFF_HEADS * LANES), BF16)
    gain = gain.reshape(1, d)
    return pl.pallas_call(
        _diff_in_kernel,
        out_shape=(shp, shp, shp),
        grid=(m // tm,),
        in_specs=[row(d), full(gain), _mod_spec(layer, 1, nb, seq, tm, d), _mod_spec(layer, 0, nb, seq, tm, d),
                  full(w), full(gains), row(LANES), row(LANES), row(LANES)],
        out_specs=(row(DIFF_HEADS * LANES),) * 3,
        compiler_params=_cparams("parallel"),
        name="diff_in_proj",
    )(x2, gain, mod, mod, w, gains, *tab)


def _diff_attn_kernel(lam_init, shift, q_ref, k_ref, v_ref, lam_ref, sg_ref, o_ref, *scratch):
    i = pl.program_id(1)
    t = ATT_TILE
    lo = _lane((t, LANES)) < 64
    lam = lam_ref[...]
    lmb = (jnp.exp(jnp.sum(lam[0:1] * lam[1:2], axis=-1, keepdims=True))
           - jnp.exp(jnp.sum(lam[2:3] * lam[3:4], axis=-1, keepdims=True)) + lam_init)

    def rows(j):
        return pl.ds(pl.multiple_of(j * t, t), t)

    def tile(h):
        return slice(LANES * h, LANES * (h + 1))

    def both_maps(h):
        qt = q_ref[0, :, tile(h)]
        zero = jnp.zeros_like(qt)
        return jnp.concatenate([jnp.where(lo, qt, zero), jnp.where(lo, zero, qt)], axis=0)

    group = DIFF_HEADS // 2
    for h0 in range(0, DIFF_HEADS, group):
        outs = _flash([both_maps(h0 + e) for e in range(group)],
                      lambda e, j: k_ref[0, rows(j), tile(h0 + e)],
                      lambda e, j: v_ref[0, rows(j), tile(h0 + e)], 0, i, t, t, shift, scratch)
        for e in range(group):
            o = outs[e][:t] - lmb * outs[e][t:]
            o = o * lax.rsqrt(jnp.mean(o * o, axis=-1, keepdims=True) + EPS) * sg_ref[...]
            o_ref[0, :, tile(h0 + e)] = (o * (1.0 - lam_init)).astype(BF16)


def _diff_attn(lam_init, shift, q, k, v, lam, sub_gain):
    b, s, n = q.shape
    t = ATT_TILE
    full = pl.BlockSpec((1, s, n), lambda bi, i: (bi, 0, 0))
    return pl.pallas_call(
        functools.partial(_diff_attn_kernel, lam_init, shift),
        out_shape=jax.ShapeDtypeStruct((b, s, n), BF16),
        grid=(b, s // t),
        in_specs=[pl.BlockSpec((1, t, n), lambda bi, i: (bi, i, 0)), full, full,
                  pl.BlockSpec(lam.shape, lambda bi, i: (0, 0)),
                  pl.BlockSpec((1, LANES), lambda bi, i: (0, 0))],
        out_specs=pl.BlockSpec((1, t, n), lambda bi, i: (bi, i, 0)),
        scratch_shapes=_flash_scratch(DIFF_HEADS // 2, 2 * t, t),
        compiler_params=_cparams("parallel", "arbitrary"),
        name="diff_attention",
    )(q, k, v, lam, sub_gain.reshape(1, LANES))


def _nsa_compress_kernel(tk_ref, tv_ref, pos_ref, w1a_ref, w1b_ref, w2_ref, kg_ref,
                         c_ref, su_ref, sd_ref, kc_ref, vc_ref):
    n_rows = tk_ref.shape[1]
    for j, (t_ref, out) in enumerate(((tk_ref, kc_ref), (tv_ref, vc_ref))):
        tok = t_ref[0]
        p_hi, p_lo = _split_bf16(pos_ref[j])
        w1a, w1b = w1a_ref[j], w1b_ref[j]
        bias = (_dot(p_hi[0], w1a) + _dot(p_lo[0], w1a) + _dot(p_hi[1], w1b) + _dot(p_lo[1], w1b))[0:1]
        hid = _dot(tok, w1a) + pltpu.roll(_dot(tok, w1b), n_rows - 1, axis=0) + bias
        act = jax.nn.gelu(hid, approximate=True)
        cmp = _dot(act.astype(BF16), w2_ref[j])
        if j == 0:
            cmp = _rope(_rms64(cmp, kg_ref[...]), (c_ref[0], su_ref[0], sd_ref[0]), NSA_ROT // 2)
        out[0] = cmp.astype(BF16)


def _nsa_compress(tk, tv, pos, w1a, w1b, w2, k_gain0, tab):
    b, nr, w = tk.shape
    full = lambda a: pl.BlockSpec(a.shape, lambda bi: (0,) * a.ndim)
    per_b = lambda a: pl.BlockSpec((1,) + a.shape[1:], lambda bi: (bi,) + (0,) * (a.ndim - 1))
    shp = jax.ShapeDtypeStruct((b, nr, LANES), BF16)
    return pl.pallas_call(
        _nsa_compress_kernel,
        out_shape=(shp, shp),
        grid=(b,),
        in_specs=[per_b(tk), per_b(tv), full(pos), full(w1a), full(w1b), full(w2), full(k_gain0),
                  per_b(tab[0]), per_b(tab[1]), per_b(tab[2])],
        out_specs=(pl.BlockSpec((1, nr, LANES), lambda bi: (bi, 0, 0)),) * 2,
        compiler_params=_cparams("parallel"),
        name="nsa_compress",
    )(tk, tv, pos, w1a, w1b, w2, k_gain0, *tab)


def _nsa_attn_kernel(shift, q_ref, kc_ref, vc_ref, ks0_ref, ks1_ref, vs_ref, kw_ref, vw_ref, gate_ref, cov_ref,
                     o_ref, *scratch):
    i = pl.program_id(1)
    t = ATT_TILE
    n_blk = 32
    q0 = i * t
    lane = _lane((t, LANES))
    lo = lane < 64
    qpos = q0 + lax.broadcasted_iota(jnp.int32, (t, LANES), 0)
    cmp_ok = NSA_CMP_STRIDE * lane + (NSA_CMP_LEN - 1) <= qpos
    gates = _sigmoid(gate_ref[0])

    def rows(j):
        return pl.ds(pl.multiple_of(j * t, t), t)

    def gate(branch, h):
        col = branch * NSA_HEADS + h
        return gates[:, col:col + 1]

    out_ref, flash_scratch = scratch[0], scratch[1:]
    kc, vc = kc_ref[0], vc_ref[0]
    qts = [q_ref[0, :, LANES * hg:LANES * (hg + 1)] for hg in range(NSA_HPG)]
    qs, q_sel = [], []
    for g in range(NSA_GROUPS):
        own = lo if g == 0 else jnp.logical_not(lo)
        qs += [jnp.where(own, qt, jnp.zeros_like(qt)) for qt in qts]

        p_sum = jnp.zeros((t, LANES), F32)
        for hg in range(NSA_HPG):
            h = g * NSA_HPG + hg
            sc = jnp.where(cmp_ok, _dot_nt(qs[h], kc), NEG)
            p = jnp.where(cmp_ok, jnp.exp2(sc - jnp.max(sc, axis=-1, keepdims=True)), 0.0)
            l = jnp.sum(p, axis=-1, keepdims=True)
            p = p * jnp.where(l > 0.0, 1.0 / l, 0.0)
            p_sum = p_sum + p
            out_ref[h] = gate(0, h) * _dot(p.astype(BF16), vc)
        p_hi, p_lo = _split_bf16(p_sum)
        imp = (_dot_nt(cov_ref[...], p_hi) + _dot_nt(cov_ref[...], p_lo))[0:n_blk]
        blk = lax.broadcasted_iota(jnp.int32, (n_blk, t), 0)
        jt = (q0 + lax.broadcasted_iota(jnp.int32, (n_blk, t), 1)) >> 6
        allowed = blk <= jt
        forced = allowed & ((blk == 0) | (blk >= jt - 1))
        imp = jnp.where(forced, FORCED, jnp.where(allowed, imp, NEG))
        rank = jnp.zeros((n_blk, t), jnp.int32)
        for jp in range(n_blk):
            other = imp[jp:jp + 1, :]
            ahead = (other > imp) | ((other == imp) & (blk > jp))
            rank = rank + ahead.astype(jnp.int32)
        pen_t = jnp.where((rank < NSA_TOP_N) & allowed, 0.0, NEG)
        pen_t = jnp.concatenate([pen_t, jnp.zeros((LANES - n_blk, t), F32)], axis=0)
        pen = jnp.transpose(pen_t)
        if g == 0:
            pen = pltpu.roll(pen, 64, axis=1)
        pen = pen.astype(BF16)
        q_sel += [jnp.where(own, qt, pen) for qt in qts]

    slc = _flash(q_sel, lambda h, j: (ks0_ref if h < NSA_HPG else ks1_ref)[0, rows(j), :],
                 lambda h, j: vs_ref[0, rows(j), :], 0, i, t, t, shift, flash_scratch)
    for h in range(NSA_HEADS):
        out_ref[h] += gate(1, h) * slc[h]

    win = _flash(qs, lambda h, j: kw_ref[0, rows(j), :], lambda h, j: vw_ref[0, rows(j), :],
                 jnp.maximum(i - NSA_WINDOW // t, 0), i, t, t, shift, flash_scratch, window=NSA_WINDOW)
    for hg in range(NSA_HPG):
        h0, h1 = hg, NSA_HPG + hg
        o_ref[0, :, LANES * hg:LANES * (hg + 1)] = jnp.where(
            lo, out_ref[h0] + gate(2, h0) * win[h0], out_ref[h1] + gate(2, h1) * win[h1]).astype(BF16)


def _nsa_attn(shift, q, kc, vc, kslc0, kslc1, kwin, vsw, pg, cover_t):
    b, s, _ = q.shape
    t = ATT_TILE
    seq = pl.BlockSpec((1, s, LANES), lambda bi, i: (bi, 0, 0))
    seq_hi = pl.BlockSpec((1, s, LANES), lambda bi, i: (bi, 0, 1))
    cmp = pl.BlockSpec((1, LANES, LANES), lambda bi, i: (bi, 0, 0))
    return pl.pallas_call(
        functools.partial(_nsa_attn_kernel, shift),
        out_shape=jax.ShapeDtypeStruct((b, s, 512), BF16),
        grid=(b, s // t),
        in_specs=[pl.BlockSpec((1, t, 512), lambda bi, i: (bi, i, 0)), cmp, cmp,
                  seq, seq, seq, seq, seq_hi,
                  pl.BlockSpec((1, t, LANES), lambda bi, i: (bi, i, 1)),
                  pl.BlockSpec((LANES, LANES), lambda bi, i: (0, 0))],
        out_specs=pl.BlockSpec((1, t, 512), lambda bi, i: (bi, i, 0)),
        scratch_shapes=[pltpu.VMEM((NSA_HEADS, t, LANES), F32)] + _flash_scratch(NSA_HEADS, t, t),
        compiler_params=_cparams("parallel", "arbitrary"),
        name="nsa_attention",
    )(q, kc, vc, kslc0, kslc1, vsw, kwin, vsw, pg, cover_t)


def _pad_lanes(a, width):
    return jnp.pad(a, [(0, 0)] * (a.ndim - 1) + [(0, width - a.shape[-1])])


def _hy_in_weight(w):
    d = w.shape[0]
    cq, ckv, kpe, nq, nkv, gate = jnp.split(w, [256, 512, 544, 1056, 1824], axis=1)
    nq = nq.reshape(d, NSA_GROUPS, NSA_HPG, NSA_DH).transpose(0, 2, 1, 3).reshape(d, 512)
    kcmp, vcmp, kslc, vslc, kwin, vwin = jnp.split(nkv, 6, axis=1)
    return jnp.concatenate([cq, ckv, nq, kslc, kwin, _pad_lanes(kpe, LANES), _pad_lanes(gate, LANES),
                            kcmp, vcmp, vslc, vwin], axis=1).astype(BF16)


def _cover_t():
    nc, ns = 127, 32
    c_start = np.arange(nc) * NSA_CMP_STRIDE
    c_end = c_start + NSA_CMP_LEN - 1
    j_start = np.arange(ns) * NSA_SLC_LEN
    cover = ((c_start[:, None] <= j_start[None, :] + NSA_SLC_LEN - 1) & (c_end[:, None] >= j_start[None, :]))
    out = np.zeros((LANES, LANES), np.float32)
    out[:ns, :nc] = cover.T
    return jnp.asarray(out, BF16)


def _compress_weights(cmp_pos, w1, w2):
    half = NSA_CMP_STRIDE
    w1 = w1.reshape(2, 2, half, NSA_DH, NSA_CMP_HIDDEN)
    eye = jnp.eye(NSA_GROUPS, dtype=F32)
    w1e = jnp.einsum('jcldn,gh->jclgdhn', w1, eye).reshape(2, 2, half * LANES, NSA_GROUPS * NSA_CMP_HIDDEN)
    w2e = jnp.einsum('jnd,gh->jgnhd', w2, eye).reshape(2, NSA_GROUPS * NSA_CMP_HIDDEN, LANES)
    pos = cmp_pos.reshape(2, 2, half, 1, NSA_DH)
    pos = jnp.broadcast_to(pos, (2, 2, half, NSA_GROUPS, NSA_DH)).reshape(2, 2, 1, half * LANES)
    pos = jnp.broadcast_to(pos, (2, 2, 8, half * LANES))
    return pos, w1e[:, 0].astype(BF16), w1e[:, 1].astype(BF16), w2e.astype(BF16)


def kernel(x, c, positions, ada_w, ada_b, norm_mix, norm_mlp, mlp_w1, mlp_w2, hy_w_in, hy_w_out, mla_q_norm, mla_w_uq, mla_kv_norm, mla_w_ukv, mla_q_gain, mla_k_gain, nsa_q_gain, nsa_k_gain, nsa_cmp_pos, nsa_cmp_w1, nsa_cmp_w2, diff_w_qkv, diff_w_out, diff_q_gain, diff_k_gain, diff_lambda, diff_sub_gain):
    nb, seq, d = x.shape
    depth = ada_w.shape[0]
    m = nb * seq
    n_cmp = seq // NSA_CMP_STRIDE

    mod = _adaln(c, ada_w, ada_b).reshape(depth * nb, 1, 6 * d)
    tab_mla, tab_64 = _rope_tables(positions.reshape(m, 1))
    pos_c = jnp.pad(positions[:, NSA_CMP_LEN - 1::NSA_CMP_STRIDE], ((0, 0), (0, 1)))
    _, tab_cmp = _rope_tables(pos_c.reshape(nb * n_cmp, 1))
    tab_cmp = tuple(a.reshape(nb, n_cmp, LANES) for a in tab_cmp)
    cover_t = _cover_t()
    seq3 = lambda a: a.reshape(nb, seq, -1)

    x2 = x.reshape(m, d)
    for i in range(depth):
        j = i // 2
        if i % 2 == 0:
            qg = jnp.tile(nsa_q_gain[j] * (NSA_DH ** -0.5 * LOG2E), 2).reshape(1, LANES)
            kg = jnp.tile(nsa_k_gain[j], (1, 2)).reshape(3, 1, LANES)
            lat, q_nsa, kslc0, kslc1, kwin, pg, kcmp, vcmp, vsw = _hy_in(
                x2, norm_mix[i], mod, i, _hy_in_weight(hy_w_in[j]),
                mla_q_norm[j].reshape(1, -1), mla_kv_norm[j].reshape(1, -1), qg, kg, tab_64, nb)

            wq = _pad_lanes(mla_w_uq[j].reshape(MLA_Q_RANK, MLA_HEADS, MLA_DK), LANES).reshape(MLA_Q_RANK, -1)
            wkv = mla_w_ukv[j].reshape(MLA_KV_RANK, MLA_HEADS, MLA_NOPE + MLA_V)
            wk = jnp.pad(wkv[..., :MLA_NOPE], ((0, 0), (0, 0), (MLA_ROPE, LANES - MLA_DK))).reshape(MLA_KV_RANK, -1)
            wv = wkv[..., MLA_NOPE:].reshape(MLA_KV_RANK, -1)
            q_mla, k_mla, v_mla = _mla_prep(
                lat, pg, wq.astype(BF16), wk.astype(BF16), wv.astype(BF16),
                _pad_lanes(mla_q_gain[j], LANES).reshape(1, LANES), _pad_lanes(mla_k_gain[j], LANES).reshape(1, LANES),
                tab_mla)
            y_mla = _shift_dispatch(
                _needs_shift(MLA_DK, mla_q_gain[j] * (MLA_DK ** -0.5 * LOG2E), mla_k_gain[j]), _mla_attn,
                seq3(q_mla), seq3(k_mla), seq3(v_mla))

            pos_e, w1a, w1b, w2e = _compress_weights(nsa_cmp_pos[j], nsa_cmp_w1[j], nsa_cmp_w2[j])
            chunks = lambda a: a.reshape(nb, n_cmp, NSA_CMP_STRIDE * LANES)
            kc, vc = _nsa_compress(chunks(kcmp), chunks(vcmp), pos_e, w1a, w1b, w2e, kg[0], tab_cmp)
            y_nsa = _shift_dispatch(
                _needs_shift(NSA_DH, qg, nsa_k_gain[j][1:]), _nsa_attn,
                seq3(q_nsa), kc, vc, seq3(kslc0), seq3(kslc1), seq3(kwin), seq3(vsw), seq3(pg), cover_t)

            w_out = hy_w_out[j]
            w_nsa = w_out[512:].reshape(NSA_GROUPS, NSA_HPG, NSA_DH, d).transpose(1, 0, 2, 3).reshape(512, d)
            x2 = _out_proj(x2, mod, i, [y_mla.reshape(m, -1), y_nsa.reshape(m, -1)],
                           [w_out[:512].astype(BF16), w_nsa.astype(BF16)], nb)
        else:
            lam_init = 0.8 - 0.6 * math.exp(-0.3 * i)
            gains = jnp.stack([jnp.tile(diff_q_gain[j] * (DIFF_DH ** -0.5 * LOG2E), 2), jnp.tile(diff_k_gain[j], 2)])
            q, k, v = _diff_in(x2, norm_mix[i], mod, i, diff_w_qkv[j].astype(BF16),
                               gains.reshape(2, 1, LANES), tab_64, nb)
            y = _shift_dispatch(
                _needs_shift(DIFF_DH, gains[0], gains[1]), functools.partial(_diff_attn, lam_init),
                seq3(q), seq3(k), seq3(v), diff_lambda[j], diff_sub_gain[j])
            x2 = _out_proj(x2, mod, i, [y.reshape(m, -1)], [diff_w_out[j].astype(BF16)], nb)
        x2 = _mlp(x2, norm_mlp[i], mod, i, mlp_w1[i].astype(BF16), mlp_w2[i].astype(BF16), nb)
    return x2.reshape(nb, seq, d)
```

```python
import functools
import math

import numpy as np
import jax
import jax.numpy as jnp
from jax import lax
from jax.experimental import pallas as pl
from jax.experimental.pallas import tpu as pltpu

F32 = jnp.float32
BF16 = jnp.bfloat16

LANES = 128
VMEM_LIMIT = 52 * 1024 * 1024

ROPE_THETA = 500000.0
EPS = 1e-6
NEG = -1e30
FORCED = 1e9

MLA_HEADS, MLA_NOPE, MLA_ROPE, MLA_V = 8, 64, 32, 64
MLA_Q_RANK, MLA_KV_RANK = 256, 256
MLA_DK = MLA_ROPE + MLA_NOPE
NSA_HEADS, NSA_GROUPS, NSA_DH = 8, 2, 64
NSA_HPG = NSA_HEADS // NSA_GROUPS
NSA_ROT = NSA_DH // 4
NSA_CMP_LEN, NSA_CMP_STRIDE, NSA_CMP_HIDDEN = 32, 16, 128
NSA_SLC_LEN, NSA_TOP_N, NSA_WINDOW = 64, 16, 512
DIFF_HEADS, DIFF_DH = 8, 64
DIFF_ROT = DIFF_DH // 4

ROW_TILE = 512
ATT_TILE = 256
MXU_COLS = 256


def _cparams(*sem):
    return pltpu.CompilerParams(dimension_semantics=sem, vmem_limit_bytes=VMEM_LIMIT)


def _split_bf16(x):
    hi = x.astype(BF16)
    lo = (x - hi.astype(F32)).astype(BF16)
    return hi, lo


def _dot(a, b):
    return jnp.dot(a, b, preferred_element_type=F32)


def _dot_nt(a, b):
    return lax.dot_general(a, b, (((1,), (1,)), ((), ())), preferred_element_type=F32)


def _sigmoid(x):
    return 1.0 / (1.0 + jnp.exp(-x))


def _lane(shape):
    return lax.broadcasted_iota(jnp.int32, shape, 1)


def _adaln_kernel(c_ref, w_ref, b_ref, o_ref):
    c = c_ref[...]
    cond = c * _sigmoid(c)
    c_hi, c_lo = _split_bf16(cond)
    w_hi, w_lo = _split_bf16(w_ref[0])
    o_ref[0] = _dot(c_hi, w_hi) + _dot(c_hi, w_lo) + _dot(c_lo, w_hi) + b_ref[0]


def _adaln(c, ada_w, ada_b):
    depth, d, n = ada_w.shape
    b = c.shape[0]
    tn = 1536
    return pl.pallas_call(
        _adaln_kernel,
        out_shape=jax.ShapeDtypeStruct((depth, b, n), F32),
        grid=(depth, n // tn),
        in_specs=[pl.BlockSpec((b, d), lambda i, j: (0, 0)),
                  pl.BlockSpec((1, d, tn), lambda i, j: (i, 0, j)),
                  pl.BlockSpec((1, 1, tn), lambda i, j: (i, 0, j))],
        out_specs=pl.BlockSpec((1, b, tn), lambda i, j: (i, 0, j)),
        compiler_params=_cparams("parallel", "parallel"),
        name="adaln",
    )(c, ada_w, ada_b.reshape(depth, 1, n))


def _rope_table_kernel(pos_ref, fa_ref, ma_ref, fb_ref, mb_ref, ca_ref, ua_ref, da_ref, cb_ref, ub_ref, db_ref):
    pos = pos_ref[...].astype(F32)
    for f_ref, m_ref, c_ref, u_ref, d_ref in ((fa_ref, ma_ref, ca_ref, ua_ref, da_ref),
                                              (fb_ref, mb_ref, cb_ref, ub_ref, db_ref)):
        ang = pos * f_ref[...]
        sin = jnp.sin(ang)
        c_ref[...] = jnp.cos(ang)
        u_ref[...] = sin * m_ref[0:1]
        d_ref[...] = sin * m_ref[1:2]


def _lane_freqs(rot, seg):
    half = rot // 2
    lane = np.arange(LANES)
    inv = ROPE_THETA ** (-(np.arange(half, dtype=np.float32) / np.float32(half)))
    f = np.where(lane % seg < rot, inv.astype(np.float32)[lane % half], 0.0)
    return jnp.asarray(f.reshape(1, LANES), F32)


def _rope_masks(rot, seg):
    half = rot // 2
    lane = np.arange(LANES) % seg
    up = ((lane >= half) & (lane < rot)).astype(np.float32)
    dn = -(lane < half).astype(np.float32)
    return jnp.asarray(np.stack([up, dn]))


def _rope_tables(pos_col):
    rows = pos_col.shape[0]
    tr = min(rows, 2048)
    spec = pl.BlockSpec((tr, LANES), lambda i: (i, 0))
    vec = pl.BlockSpec((1, LANES), lambda i: (0, 0))
    msk = pl.BlockSpec((2, LANES), lambda i: (0, 0))
    shp = jax.ShapeDtypeStruct((rows, LANES), F32)
    out = pl.pallas_call(
        _rope_table_kernel,
        out_shape=(shp,) * 6,
        grid=(rows // tr,),
        in_specs=[pl.BlockSpec((tr, 1), lambda i: (i, 0)), vec, msk, vec, msk],
        out_specs=(spec,) * 6,
        compiler_params=_cparams("parallel"),
        name="rope_tables",
    )(pos_col, _lane_freqs(MLA_ROPE, LANES), _rope_masks(MLA_ROPE, LANES),
      _lane_freqs(NSA_ROT, NSA_DH), _rope_masks(NSA_ROT, NSA_DH))
    return out[:3], out[3:]


def _rope(y, tab, half):
    c, s_up, s_dn = tab
    up = pltpu.roll(y, half, axis=1)
    dn = pltpu.roll(y, LANES - half, axis=1)
    return y * c + up * s_up + dn * s_dn


def _rms64(x, gain):
    lo = _lane(x.shape) < 64
    x2 = x * x
    s_lo = jnp.sum(jnp.where(lo, x2, 0.0), axis=-1, keepdims=True)
    s_hi = jnp.sum(jnp.where(lo, 0.0, x2), axis=-1, keepdims=True)
    r = jnp.where(lo, lax.rsqrt(s_lo * (1.0 / 64) + EPS), lax.rsqrt(s_hi * (1.0 / 64) + EPS))
    return x * (r * gain)


def _rms(x, gain):
    return x * lax.rsqrt(jnp.mean(x * x, axis=-1, keepdims=True) + EPS) * gain


def _norm_mod(x, gain, sc, sh):
    return _rms(x, gain) * (1.0 + sc) + sh


def _mod_spec(layer, k, nb, rows_per_batch, tm, d):
    per = rows_per_batch // tm
    return pl.BlockSpec((1, 1, d), lambda i, *_: (layer * nb + i // per, 0, k))


def _mlp_kernel(x_ref, gain_ref, sc_ref, sh_ref, g_ref, w1_ref, w2_ref, o_ref, h_scr, acc_scr):
    j = pl.program_id(1)

    @pl.when(j == 0)
    def _():
        h_scr[...] = _norm_mod(x_ref[...], gain_ref[...], sc_ref[0], sh_ref[0]).astype(BF16)
        acc_scr[...] = jnp.zeros_like(acc_scr)

    a = jnp.maximum(_dot(h_scr[...], w1_ref[...]), 0.0)
    acc_scr[...] += _dot((a * a).astype(BF16), w2_ref[...])

    @pl.when(j == pl.num_programs(1) - 1)
    def _():
        o_ref[...] = x_ref[...] + g_ref[0] * acc_scr[...]


def _mlp(x2, gain, mod, layer, w1, w2, nb):
    m, d = x2.shape
    ff = w1.shape[1]
    tm, tf = 1024, 1024
    s = m // nb
    return pl.pallas_call(
        _mlp_kernel,
        out_shape=jax.ShapeDtypeStruct((m, d), F32),
        grid=(m // tm, ff // tf),
        in_specs=[pl.BlockSpec((tm, d), lambda i, j: (i, 0)),
                  pl.BlockSpec((1, d), lambda i, j: (0, 0)),
                  _mod_spec(layer, 4, nb, s, tm, d),
                  _mod_spec(layer, 3, nb, s, tm, d),
                  _mod_spec(layer, 5, nb, s, tm, d),
                  pl.BlockSpec((d, tf), lambda i, j: (0, j)),
                  pl.BlockSpec((tf, d), lambda i, j: (j, 0))],
        out_specs=pl.BlockSpec((tm, d), lambda i, j: (i, 0)),
        scratch_shapes=[pltpu.VMEM((tm, d), BF16), pltpu.VMEM((tm, d), F32)],
        compiler_params=_cparams("parallel", "arbitrary"),
        name="relu2_mlp",
    )(x2, gain.reshape(1, d), mod, mod, mod, w1, w2)


def _out_proj_kernel(n_in, *refs):
    x_ref, g_ref = refs[0], refs[1]
    y_refs = refs[2:2 + n_in]
    w_refs = refs[2 + n_in:2 + 2 * n_in]
    o_ref = refs[2 + 2 * n_in]
    y = _dot(y_refs[0][...], w_refs[0][...])
    for y_ref, w_ref in zip(y_refs[1:], w_refs[1:]):
        y = y + _dot(y_ref[...], w_ref[...])
    o_ref[...] = x_ref[...] + g_ref[0] * y


def _out_proj(x2, mod, layer, ys, ws, nb):
    m, d = x2.shape
    tm = ROW_TILE
    s = m // nb
    in_specs = [pl.BlockSpec((tm, d), lambda i: (i, 0)), _mod_spec(layer, 2, nb, s, tm, d)]
    in_specs += [pl.BlockSpec((tm, y.shape[1]), lambda i: (i, 0)) for y in ys]
    in_specs += [pl.BlockSpec(w.shape, lambda i: (0, 0)) for w in ws]
    return pl.pallas_call(
        functools.partial(_out_proj_kernel, len(ys)),
        out_shape=jax.ShapeDtypeStruct((m, d), F32),
        grid=(m // tm,),
        in_specs=in_specs,
        out_specs=pl.BlockSpec((tm, d), lambda i: (i, 0)),
        compiler_params=_cparams("parallel"),
        name="out_proj_residual",
    )(x2, mod, *ys, *ws)


LOG2E = math.log2(math.e)


SCORE_BOUND = 60.0
BOUND_MARGIN = 1.02


def _needs_shift(dk, q_gain, k_gain):
    bound = dk * jnp.max(jnp.abs(q_gain)) * jnp.max(jnp.abs(k_gain)) * BOUND_MARGIN
    return jnp.logical_not(bound <= SCORE_BOUND)


def _shift_dispatch(needs_shift, attn, *args):
    return lax.cond(needs_shift, functools.partial(attn, True), functools.partial(attn, False), *args)


def _lane_tiles(x):
    return [x[:, LANES * c:LANES * (c + 1)] for c in range(x.shape[1] // LANES)]


def _flash_scratch(n, rows, t):
    return [pltpu.VMEM((n, rows, LANES), F32), pltpu.VMEM((n, rows, LANES), F32)]


def _flash(qs, k_at, v_at, first, i, t, rows_per_map, shift, scratch, window=None, masked_loop=False):
    n = len(qs)
    rows = qs[0].shape[0]
    r = lax.broadcasted_iota(jnp.int32, (rows, t), 0)
    if rows != rows_per_map:
        r = jnp.where(r >= rows_per_map, r - rows_per_map, r)
    c_minus_r = lax.broadcasted_iota(jnp.int32, (rows, t), 1) - r

    assert masked_loop or window is None

    def visible(j):
        ahead = (i - j) * t
        ok = c_minus_r <= ahead
        return ok if window is None else ok & (c_minus_r > ahead - window)

    def scores(h, j, ok):
        s = _dot_nt(qs[h], k_at(h, j))
        return s if ok is None else jnp.where(ok, s, NEG)

    def sweep(step, carry):
        if masked_loop:
            return lax.fori_loop(first, i + 1, lambda j, cr: step(j, cr, visible(j)), carry)
        return step(i, lax.fori_loop(first, i, lambda j, cr: step(j, cr, None), carry), visible(i))

    def max_step(j, ms, ok):
        out = []
        for h in range(n):
            m = ms[h]
            for s_c in _lane_tiles(scores(h, j, ok)):
                m = jnp.maximum(m, s_c)
            out.append(m)
        return tuple(out)

    if shift:
        ms = sweep(max_step, tuple(jnp.full((rows, LANES), NEG, F32) for _ in range(n)))
        ms = [jnp.broadcast_to(jnp.max(m, axis=-1, keepdims=True), (rows, LANES)) for m in ms]

    l_ref, acc_ref = scratch
    for h in range(n):
        l_ref[h] = jnp.zeros((rows, LANES), F32)
        acc_ref[h] = jnp.zeros((rows, LANES), F32)

    def sum_step(j, p_prev, ok):
        p_new = []
        for h in range(n):
            acc_ref[h] += _dot(p_prev[h], v_at(h, jnp.maximum(j - 1, 0)))
            ps = _lane_tiles(scores(h, j, ok))
            ps = [jnp.exp2(s_c - ms[h]) if shift else jnp.exp2(s_c) for s_c in ps]
            l = l_ref[h]
            for p_c in ps:
                l = l + p_c
            l_ref[h] = l
            p_new.append(jnp.concatenate(ps, axis=1).astype(BF16))
        return tuple(p_new)

    p = sweep(sum_step, tuple(jnp.zeros((rows, t), BF16) for _ in range(n)))
    return [(acc_ref[h] + _dot(p[h], v_at(h, i))) / jnp.sum(l_ref[h], axis=-1, keepdims=True)
            for h in range(n)]


def _hy_in_kernel(seq, x_ref, gain_ref, sc_ref, sh_ref, w_ref, qn_ref, kvn_ref, qg_ref, kg_ref,
                  c_ref, su_ref, sd_ref,
                  lat_ref, q_ref, ks0_ref, ks1_ref, kw_ref, pg_ref, kcmp_ref, vcmp_ref, vsw_ref):
    tm = x_ref.shape[0]
    h = _norm_mod(x_ref[...], gain_ref[...], sc_ref[0], sh_ref[0]).astype(BF16)
    tab = (c_ref[...], su_ref[...], sd_ref[...])

    def cols(a, b):
        return _dot(h, w_ref[:, a:b])

    def prep(x, g):
        return _rope(_rms64(x, g), tab, NSA_ROT // 2)

    lat_ref[:, 0:256] = _rms(cols(0, 256), qn_ref[...]).astype(BF16)
    lat_ref[:, 256:512] = _rms(cols(256, 512), kvn_ref[...]).astype(BF16)
    for ch in range(2):
        y = cols(512 + MXU_COLS * ch, 512 + MXU_COLS * (ch + 1))
        for e in range(2):
            q_ref[:, MXU_COLS * ch + LANES * e:MXU_COLS * ch + LANES * (e + 1)] = prep(
                y[:, LANES * e:LANES * (e + 1)], qg_ref[...]).astype(BF16)
    y = cols(1024, 1280)
    kslc = prep(y[:, 0:LANES], kg_ref[1])
    lane = _lane((tm, LANES))
    t_idx = (pl.program_id(0) * tm + lax.broadcasted_iota(jnp.int32, (tm, LANES), 0)) & (seq - 1)
    blk = t_idx >> 6
    ks0_ref[...] = jnp.where(lane < 64, kslc, jnp.where(lane - 64 == blk, 1.0, 0.0)).astype(BF16)
    ks1_ref[...] = jnp.where(lane >= 64, kslc, jnp.where(lane == blk, 1.0, 0.0)).astype(BF16)
    kw_ref[...] = prep(y[:, LANES:2 * LANES], kg_ref[2]).astype(BF16)
    pg_ref[...] = cols(1280, 1536)
    y = cols(1536, 1792)
    kcmp_ref[...] = y[:, 0:LANES].astype(BF16)
    vcmp_ref[...] = y[:, LANES:2 * LANES].astype(BF16)
    vsw_ref[...] = cols(1792, 2048).astype(BF16)


def _hy_in(x2, gain, mod, layer, w, q_norm, kv_norm, q_gain, k_gain, tab, nb):
    m, d = x2.shape
    tm = ROW_TILE
    seq = m // nb
    assert seq & (seq - 1) == 0 and seq // NSA_SLC_LEN == 32 and seq % tm == 0
    row = lambda wd: pl.BlockSpec((tm, wd), lambda i: (i, 0))
    full = lambda a: pl.BlockSpec(a.shape, lambda i: (0,) * a.ndim)
    bf = lambda wd: jax.ShapeDtypeStruct((m, wd), BF16)
    widths = (512, 512, LANES, LANES, LANES, 2 * LANES, LANES, LANES, 2 * LANES)
    shapes = tuple(jax.ShapeDtypeStruct((m, wd), F32) if k == 5 else bf(wd) for k, wd in enumerate(widths))
    gain = gain.reshape(1, d)
    return pl.pallas_call(
        functools.partial(_hy_in_kernel, seq),
        out_shape=shapes,
        grid=(m // tm,),
        in_specs=[row(d), full(gain), _mod_spec(layer, 1, nb, seq, tm, d), _mod_spec(layer, 0, nb, seq, tm, d),
                  full(w), full(q_norm), full(kv_norm), full(q_gain), full(k_gain),
                  row(LANES), row(LANES), row(LANES)],
        out_specs=tuple(row(wd) for wd in widths),
        compiler_params=_cparams("parallel"),
        name="hy_in_proj",
    )(x2, gain, mod, mod, w, q_norm, kv_norm, q_gain, k_gain, *tab)


def _mla_prep_kernel(lat_ref, kpe_ref, wq_ref, wk_ref, wv_ref, qg_ref, kg_ref, c_ref, su_ref, sd_ref,
                     q_ref, k_ref, v_ref):
    cq, ckv = lat_ref[:, 0:256], lat_ref[:, 256:512]
    kpe = kpe_ref[...]
    tab = (c_ref[...], su_ref[...], sd_ref[...])
    scale = MLA_DK ** -0.5 * LOG2E
    v_ref[...] = _dot(ckv, wv_ref[...]).astype(BF16)

    def head(x, g):
        r = lax.rsqrt(jnp.sum(x * x, axis=-1, keepdims=True) * (1.0 / MLA_DK) + EPS)
        return _rope(x * r * g, tab, MLA_ROPE // 2)

    for ch in range(MLA_HEADS // 2):
        cs = slice(MXU_COLS * ch, MXU_COLS * (ch + 1))
        q2, k2 = _dot(cq, wq_ref[:, cs]), _dot(ckv, wk_ref[:, cs])
        for e in range(2):
            sl = slice(MXU_COLS * ch + LANES * e, MXU_COLS * ch + LANES * (e + 1))
            es = slice(LANES * e, LANES * (e + 1))
            q_ref[:, sl] = (head(q2[:, es], qg_ref[...]) * scale).astype(BF16)
            k_ref[:, sl] = head(k2[:, es] + kpe, kg_ref[...]).astype(BF16)


def _mla_prep(lat, pg, wq, wk, wv, q_gain, k_gain, tab):
    m = lat.shape[0]
    tm = ROW_TILE
    full = lambda a: pl.BlockSpec(a.shape, lambda i: (0, 0))
    row = lambda w: pl.BlockSpec((tm, w), lambda i: (i, 0))
    return pl.pallas_call(
        _mla_prep_kernel,
        out_shape=(jax.ShapeDtypeStruct((m, 1024), BF16), jax.ShapeDtypeStruct((m, 1024), BF16),
                   jax.ShapeDtypeStruct((m, 512), BF16)),
        grid=(m // tm,),
        in_specs=[row(512), row(LANES), full(wq), full(wk), full(wv), full(q_gain), full(k_gain),
                  row(LANES), row(LANES), row(LANES)],
        out_specs=(row(1024), row(1024), row(512)),
        compiler_params=_cparams("parallel"),
        name="mla_prep",
    )(lat, pg, wq, wk, wv, q_gain, k_gain, *tab)


def _mla_attn_kernel(shift, q_ref, k_ref, v_ref, o_ref, *scratch):
    i = pl.program_id(1)
    t = ATT_TILE
    lo = _lane((t, LANES)) < 64

    def rows(j):
        return pl.ds(pl.multiple_of(j * t, t), t)

    def tile(h):
        return slice(LANES * h, LANES * (h + 1))

    outs = _flash([q_ref[0, :, tile(h)] for h in range(MLA_HEADS)],
                  lambda h, j: k_ref[0, rows(j), tile(h)],
                  lambda h, j: v_ref[0, rows(j), tile(h // 2)], 0, i, t, t, shift, scratch)
    for p in range(MLA_HEADS // 2):
        o_ref[0, :, tile(p)] = jnp.where(lo, outs[2 * p], outs[2 * p + 1]).astype(BF16)


def _mla_attn(shift, q, k, v):
    b, s, _ = q.shape
    t = ATT_TILE
    return pl.pallas_call(
        functools.partial(_mla_attn_kernel, shift),
        out_shape=jax.ShapeDtypeStruct((b, s, 512), BF16),
        grid=(b, s // t),
        in_specs=[pl.BlockSpec((1, t, 1024), lambda bi, i: (bi, i, 0)),
                  pl.BlockSpec((1, s, 1024), lambda bi, i: (bi, 0, 0)),
                  pl.BlockSpec((1, s, 512), lambda bi, i: (bi, 0, 0))],
        out_specs=pl.BlockSpec((1, t, 512), lambda bi, i: (bi, i, 0)),
        scratch_shapes=_flash_scratch(MLA_HEADS, t, t),
        compiler_params=_cparams("parallel", "arbitrary"),
        name="mla_attention",
    )(q, k, v)


def _diff_in_kernel(x_ref, gain_ref, sc_ref, sh_ref, w_ref, g_ref, c_ref, su_ref, sd_ref, q_ref, k_ref, v_ref):
    h = _norm_mod(x_ref[...], gain_ref[...], sc_ref[0], sh_ref[0]).astype(BF16)
    tab = (c_ref[...], su_ref[...], sd_ref[...])
    n = DIFF_HEADS * LANES
    for which, out in ((0, q_ref), (1, k_ref)):
        g = g_ref[which]
        for ch in range(n // MXU_COLS):
            y = _dot(h, w_ref[:, which * n + MXU_COLS * ch:which * n + MXU_COLS * (ch + 1)])
            for e in range(2):
                out[:, MXU_COLS * ch + LANES * e:MXU_COLS * ch + LANES * (e + 1)] = _rope(
                    _rms64(y[:, LANES * e:LANES * (e + 1)], g), tab, DIFF_ROT // 2).astype(BF16)
    for ch in range(n // MXU_COLS):
        cs = slice(MXU_COLS * ch, MXU_COLS * (ch + 1))
        v_ref[:, cs] = _dot(h, w_ref[:, 2 * n + MXU_COLS * ch:2 * n + MXU_COLS * (ch + 1)]).astype(BF16)


def _diff_in(x2, gain, mod, layer, w, gains, tab, nb):
    m, d = x2.shape
    tm = ROW_TILE
    seq = m // nb
    row = lambda wd: pl.BlockSpec((tm, wd), lambda i: (i, 0))
    full = lambda a: pl.BlockSpec(a.shape, lambda i: (0,) * a.ndim)
    shp = jax.ShapeDtypeStruct((m, DIFF_HEADS * LANES), BF16)
    gain = gain.reshape(1, d)
    return pl.pallas_call(
        _diff_in_kernel,
        out_shape=(shp, shp, shp),
        grid=(m // tm,),
        in_specs=[row(d), full(gain), _mod_spec(layer, 1, nb, seq, tm, d), _mod_spec(layer, 0, nb, seq, tm, d),
                  full(w), full(gains), row(LANES), row(LANES), row(LANES)],
        out_specs=(row(DIFF_HEADS * LANES),) * 3,
        compiler_params=_cparams("parallel"),
        name="diff_in_proj",
    )(x2, gain, mod, mod, w, gains, *tab)


def _diff_attn_kernel(lam_init, shift, q_ref, k_ref, v_ref, lam_ref, sg_ref, o_ref, *scratch):
    i = pl.program_id(1)
    t = ATT_TILE
    lo = _lane((t, LANES)) < 64
    lam = lam_ref[...]
    lmb = (jnp.exp(jnp.sum(lam[0:1] * lam[1:2], axis=-1, keepdims=True))
           - jnp.exp(jnp.sum(lam[2:3] * lam[3:4], axis=-1, keepdims=True)) + lam_init)

    def rows(j):
        return pl.ds(pl.multiple_of(j * t, t), t)

    def tile(h):
        return slice(LANES * h, LANES * (h + 1))

    def both_maps(h):
        qt = q_ref[0, :, tile(h)]
        zero = jnp.zeros_like(qt)
        return jnp.concatenate([jnp.where(lo, qt, zero), jnp.where(lo, zero, qt)], axis=0)

    group = DIFF_HEADS // 2
    for h0 in range(0, DIFF_HEADS, group):
        outs = _flash([both_maps(h0 + e) for e in range(group)],
                      lambda e, j: k_ref[0, rows(j), tile(h0 + e)],
                      lambda e, j: v_ref[0, rows(j), tile(h0 + e)], 0, i, t, t, shift, scratch)
        for e in range(group):
            o = outs[e][:t] - lmb * outs[e][t:]
            o = o * lax.rsqrt(jnp.mean(o * o, axis=-1, keepdims=True) + EPS) * sg_ref[...]
            o_ref[0, :, tile(h0 + e)] = (o * (1.0 - lam_init)).astype(BF16)


def _diff_attn(lam_init, shift, q, k, v, lam, sub_gain):
    b, s, n = q.shape
    t = ATT_TILE
    full = pl.BlockSpec((1, s, n), lambda bi, i: (bi, 0, 0))
    return pl.pallas_call(
        functools.partial(_diff_attn_kernel, lam_init, shift),
        out_shape=jax.ShapeDtypeStruct((b, s, n), BF16),
        grid=(b, s // t),
        in_specs=[pl.BlockSpec((1, t, n), lambda bi, i: (bi, i, 0)), full, full,
                  pl.BlockSpec(lam.shape, lambda bi, i: (0, 0)),
                  pl.BlockSpec((1, LANES), lambda bi, i: (0, 0))],
        out_specs=pl.BlockSpec((1, t, n), lambda bi, i: (bi, i, 0)),
        scratch_shapes=_flash_scratch(DIFF_HEADS // 2, 2 * t, t),
        compiler_params=_cparams("parallel", "arbitrary"),
        name="diff_attention",
    )(q, k, v, lam, sub_gain.reshape(1, LANES))


def _nsa_compress_kernel(tk_ref, tv_ref, pos_ref, w1a_ref, w1b_ref, w2_ref, kg_ref,
                         c_ref, su_ref, sd_ref, kc_ref, vc_ref):
    n_rows = tk_ref.shape[1]
    for j, (t_ref, out) in enumerate(((tk_ref, kc_ref), (tv_ref, vc_ref))):
        tok = t_ref[0]
        p_hi, p_lo = _split_bf16(pos_ref[j])
        w1a, w1b = w1a_ref[j], w1b_ref[j]
        bias = (_dot(p_hi[0], w1a) + _dot(p_lo[0], w1a) + _dot(p_hi[1], w1b) + _dot(p_lo[1], w1b))[0:1]
        hid = _dot(tok, w1a) + pltpu.roll(_dot(tok, w1b), n_rows - 1, axis=0) + bias
        act = jax.nn.gelu(hid, approximate=True)
        cmp = _dot(act.astype(BF16), w2_ref[j])
        if j == 0:
            cmp = _rope(_rms64(cmp, kg_ref[...]), (c_ref[0], su_ref[0], sd_ref[0]), NSA_ROT // 2)
        out[0] = cmp.astype(BF16)


def _nsa_compress(tk, tv, pos, w1a, w1b, w2, k_gain0, tab):
    b, nr, w = tk.shape
    full = lambda a: pl.BlockSpec(a.shape, lambda bi: (0,) * a.ndim)
    per_b = lambda a: pl.BlockSpec((1,) + a.shape[1:], lambda bi: (bi,) + (0,) * (a.ndim - 1))
    shp = jax.ShapeDtypeStruct((b, nr, LANES), BF16)
    return pl.pallas_call(
        _nsa_compress_kernel,
        out_shape=(shp, shp),
        grid=(b,),
        in_specs=[per_b(tk), per_b(tv), full(pos), full(w1a), full(w1b), full(w2), full(k_gain0),
                  per_b(tab[0]), per_b(tab[1]), per_b(tab[2])],
        out_specs=(pl.BlockSpec((1, nr, LANES), lambda bi: (bi, 0, 0)),) * 2,
        compiler_params=_cparams("parallel"),
        name="nsa_compress",
    )(tk, tv, pos, w1a, w1b, w2, k_gain0, *tab)


def _nsa_attn_kernel(shift, q_ref, kc_ref, vc_ref, ks0_ref, ks1_ref, vs_ref, kw_ref, vw_ref, gate_ref, cov_ref,
                     o_ref, *scratch):
    i = pl.program_id(1)
    t = ATT_TILE
    n_blk = 32
    q0 = i * t
    lane = _lane((t, LANES))
    lo = lane < 64
    qpos = q0 + lax.broadcasted_iota(jnp.int32, (t, LANES), 0)
    cmp_ok = NSA_CMP_STRIDE * lane + (NSA_CMP_LEN - 1) <= qpos
    gates = _sigmoid(gate_ref[0])

    def rows(j):
        return pl.ds(pl.multiple_of(j * t, t), t)

    def gate(branch, h):
        col = branch * NSA_HEADS + h
        return gates[:, col:col + 1]

    out_ref, flash_scratch = scratch[0], scratch[1:]
    kc, vc = kc_ref[0], vc_ref[0]
    qts = [q_ref[0, :, LANES * hg:LANES * (hg + 1)] for hg in range(NSA_HPG)]
    owns = [lo, jnp.logical_not(lo)]
    qs = [jnp.where(own, qt, jnp.zeros_like(qt)) for own in owns for qt in qts]

    win = _flash(qs, lambda h, j: kw_ref[0, rows(j), :], lambda h, j: vw_ref[0, rows(j), :],
                 jnp.maximum(i - NSA_WINDOW // t, 0), i, t, t, shift, flash_scratch, window=NSA_WINDOW,
                 masked_loop=True)
    for h in range(NSA_HEADS):
        out_ref[h] = gate(2, h) * win[h]

    q_sel = []
    for g in range(NSA_GROUPS):
        own = owns[g]

        p_sum = jnp.zeros((t, LANES), F32)
        for hg in range(NSA_HPG):
            h = g * NSA_HPG + hg
            sc = jnp.where(cmp_ok, _dot_nt(qs[h], kc), NEG)
            p = jnp.where(cmp_ok, jnp.exp2(sc - jnp.max(sc, axis=-1, keepdims=True)), 0.0)
            l = jnp.sum(p, axis=-1, keepdims=True)
            p = p * jnp.where(l > 0.0, 1.0 / l, 0.0)
            p_sum = p_sum + p
            out_ref[h] += gate(0, h) * _dot(p.astype(BF16), vc)
        p_hi, p_lo = _split_bf16(p_sum)
        imp = (_dot_nt(cov_ref[...], p_hi) + _dot_nt(cov_ref[...], p_lo))[0:n_blk]
        blk = lax.broadcasted_iota(jnp.int32, (n_blk, t), 0)
        jt = (q0 + lax.broadcasted_iota(jnp.int32, (n_blk, t), 1)) >> 6
        allowed = blk <= jt
        forced = allowed & ((blk == 0) | (blk >= jt - 1))
        imp = jnp.where(forced, FORCED, jnp.where(allowed, imp, NEG))
        rank = jnp.zeros((n_blk, t), jnp.int32)
        for jp in range(n_blk):
            other = imp[jp:jp + 1, :]
            ahead = (other > imp) | ((other == imp) & (blk > jp))
            rank = rank + ahead.astype(jnp.int32)
        pen_t = jnp.where((rank < NSA_TOP_N) & allowed, 0.0, NEG)
        pen_t = jnp.concatenate([pen_t, jnp.zeros((LANES - n_blk, t), F32)], axis=0)
        pen = jnp.transpose(pen_t)
        if g == 0:
            pen = pltpu.roll(pen, 64, axis=1)
        pen = pen.astype(BF16)
        q_sel += [jnp.where(own, qt, pen) for qt in qts]

    slc = _flash(q_sel, lambda h, j: (ks0_ref if h < NSA_HPG else ks1_ref)[0, rows(j), :],
                 lambda h, j: vs_ref[0, rows(j), :], 0, i, t, t, shift, flash_scratch, masked_loop=True)
    for hg in range(NSA_HPG):
        h0, h1 = hg, NSA_HPG + hg
        o_ref[0, :, LANES * hg:LANES * (hg + 1)] = jnp.where(
            lo, out_ref[h0] + gate(1, h0) * slc[h0], out_ref[h1] + gate(1, h1) * slc[h1]).astype(BF16)


def _nsa_attn(shift, q, kc, vc, kslc0, kslc1, kwin, vsw, pg, cover_t):
    b, s, _ = q.shape
    t = ATT_TILE
    seq = pl.BlockSpec((1, s, LANES), lambda bi, i: (bi, 0, 0))
    seq_hi = pl.BlockSpec((1, s, LANES), lambda bi, i: (bi, 0, 1))
    cmp = pl.BlockSpec((1, LANES, LANES), lambda bi, i: (bi, 0, 0))
    return pl.pallas_call(
        functools.partial(_nsa_attn_kernel, shift),
        out_shape=jax.ShapeDtypeStruct((b, s, 512), BF16),
        grid=(b, s // t),
        in_specs=[pl.BlockSpec((1, t, 512), lambda bi, i: (bi, i, 0)), cmp, cmp,
                  seq, seq, seq, seq, seq_hi,
                  pl.BlockSpec((1, t, LANES), lambda bi, i: (bi, i, 1)),
                  pl.BlockSpec((LANES, LANES), lambda bi, i: (0, 0))],
        out_specs=pl.BlockSpec((1, t, 512), lambda bi, i: (bi, i, 0)),
        scratch_shapes=[pltpu.VMEM((NSA_HEADS, t, LANES), F32)] + _flash_scratch(NSA_HEADS, t, t),
        compiler_params=_cparams("parallel", "arbitrary"),
        name="nsa_attention",
    )(q, kc, vc, kslc0, kslc1, vsw, kwin, vsw, pg, cover_t)


def _pad_lanes(a, width):
    return jnp.pad(a, [(0, 0)] * (a.ndim - 1) + [(0, width - a.shape[-1])])


def _hy_in_weight(w):
    d = w.shape[0]
    cq, ckv, kpe, nq, nkv, gate = jnp.split(w, [256, 512, 544, 1056, 1824], axis=1)
    nq = nq.reshape(d, NSA_GROUPS, NSA_HPG, NSA_DH).transpose(0, 2, 1, 3).reshape(d, 512)
    kcmp, vcmp, kslc, vslc, kwin, vwin = jnp.split(nkv, 6, axis=1)
    return jnp.concatenate([cq, ckv, nq, kslc, kwin, _pad_lanes(kpe, LANES), _pad_lanes(gate, LANES),
                            kcmp, vcmp, vslc, vwin], axis=1).astype(BF16)


def _cover_t():
    nc, ns = 127, 32
    c_start = np.arange(nc) * NSA_CMP_STRIDE
    c_end = c_start + NSA_CMP_LEN - 1
    j_start = np.arange(ns) * NSA_SLC_LEN
    cover = ((c_start[:, None] <= j_start[None, :] + NSA_SLC_LEN - 1) & (c_end[:, None] >= j_start[None, :]))
    out = np.zeros((LANES, LANES), np.float32)
    out[:ns, :nc] = cover.T
    return jnp.asarray(out, BF16)


def _compress_weights(cmp_pos, w1, w2):
    half = NSA_CMP_STRIDE
    w1 = w1.reshape(2, 2, half, NSA_DH, NSA_CMP_HIDDEN)
    eye = jnp.eye(NSA_GROUPS, dtype=F32)
    w1e = jnp.einsum('jcldn,gh->jclgdhn', w1, eye).reshape(2, 2, half * LANES, NSA_GROUPS * NSA_CMP_HIDDEN)
    w2e = jnp.einsum('jnd,gh->jgnhd', w2, eye).reshape(2, NSA_GROUPS * NSA_CMP_HIDDEN, LANES)
    pos = cmp_pos.reshape(2, 2, half, 1, NSA_DH)
    pos = jnp.broadcast_to(pos, (2, 2, half, NSA_GROUPS, NSA_DH)).reshape(2, 2, 1, half * LANES)
    pos = jnp.broadcast_to(pos, (2, 2, 8, half * LANES))
    return pos, w1e[:, 0].astype(BF16), w1e[:, 1].astype(BF16), w2e.astype(BF16)


def kernel(x, c, positions, ada_w, ada_b, norm_mix, norm_mlp, mlp_w1, mlp_w2, hy_w_in, hy_w_out, mla_q_norm, mla_w_uq, mla_kv_norm, mla_w_ukv, mla_q_gain, mla_k_gain, nsa_q_gain, nsa_k_gain, nsa_cmp_pos, nsa_cmp_w1, nsa_cmp_w2, diff_w_qkv, diff_w_out, diff_q_gain, diff_k_gain, diff_lambda, diff_sub_gain):
    nb, seq, d = x.shape
    depth = ada_w.shape[0]
    m = nb * seq
    n_cmp = seq // NSA_CMP_STRIDE

    mod = _adaln(c, ada_w, ada_b).reshape(depth * nb, 1, 6 * d)
    tab_mla, tab_64 = _rope_tables(positions.reshape(m, 1))
    pos_c = jnp.pad(positions[:, NSA_CMP_LEN - 1::NSA_CMP_STRIDE], ((0, 0), (0, 1)))
    _, tab_cmp = _rope_tables(pos_c.reshape(nb * n_cmp, 1))
    tab_cmp = tuple(a.reshape(nb, n_cmp, LANES) for a in tab_cmp)
    cover_t = _cover_t()
    seq3 = lambda a: a.reshape(nb, seq, -1)

    x2 = x.reshape(m, d)
    for i in range(depth):
        j = i // 2
        if i % 2 == 0:
            qg = jnp.tile(nsa_q_gain[j] * (NSA_DH ** -0.5 * LOG2E), 2).reshape(1, LANES)
            kg = jnp.tile(nsa_k_gain[j], (1, 2)).reshape(3, 1, LANES)
            lat, q_nsa, kslc0, kslc1, kwin, pg, kcmp, vcmp, vsw = _hy_in(
                x2, norm_mix[i], mod, i, _hy_in_weight(hy_w_in[j]),
                mla_q_norm[j].reshape(1, -1), mla_kv_norm[j].reshape(1, -1), qg, kg, tab_64, nb)

            wq = _pad_lanes(mla_w_uq[j].reshape(MLA_Q_RANK, MLA_HEADS, MLA_DK), LANES).reshape(MLA_Q_RANK, -1)
            wkv = mla_w_ukv[j].reshape(MLA_KV_RANK, MLA_HEADS, MLA_NOPE + MLA_V)
            wk = jnp.pad(wkv[..., :MLA_NOPE], ((0, 0), (0, 0), (MLA_ROPE, LANES - MLA_DK))).reshape(MLA_KV_RANK, -1)
            wv = wkv[..., MLA_NOPE:].reshape(MLA_KV_RANK, -1)
            q_mla, k_mla, v_mla = _mla_prep(
                lat, pg, wq.astype(BF16), wk.astype(BF16), wv.astype(BF16),
                _pad_lanes(mla_q_gain[j], LANES).reshape(1, LANES), _pad_lanes(mla_k_gain[j], LANES).reshape(1, LANES),
                tab_mla)
            y_mla = _shift_dispatch(
                _needs_shift(MLA_DK, mla_q_gain[j] * (MLA_DK ** -0.5 * LOG2E), mla_k_gain[j]), _mla_attn,
                seq3(q_mla), seq3(k_mla), seq3(v_mla))

            pos_e, w1a, w1b, w2e = _compress_weights(nsa_cmp_pos[j], nsa_cmp_w1[j], nsa_cmp_w2[j])
            chunks = lambda a: a.reshape(nb, n_cmp, NSA_CMP_STRIDE * LANES)
            kc, vc = _nsa_compress(chunks(kcmp), chunks(vcmp), pos_e, w1a, w1b, w2e, kg[0], tab_cmp)
            y_nsa = _shift_dispatch(
                _needs_shift(NSA_DH, qg, nsa_k_gain[j][1:]), _nsa_attn,
                seq3(q_nsa), kc, vc, seq3(kslc0), seq3(kslc1), seq3(kwin), seq3(vsw), seq3(pg), cover_t)

            w_out = hy_w_out[j]
            w_nsa = w_out[512:].reshape(NSA_GROUPS, NSA_HPG, NSA_DH, d).transpose(1, 0, 2, 3).reshape(512, d)
            x2 = _out_proj(x2, mod, i, [y_mla.reshape(m, -1), y_nsa.reshape(m, -1)],
                           [w_out[:512].astype(BF16), w_nsa.astype(BF16)], nb)
        else:
            lam_init = 0.8 - 0.6 * math.exp(-0.3 * i)
            gains = jnp.stack([jnp.tile(diff_q_gain[j] * (DIFF_DH ** -0.5 * LOG2E), 2), jnp.tile(diff_k_gain[j], 2)])
            q, k, v = _diff_in(x2, norm_mix[i], mod, i, diff_w_qkv[j].astype(BF16),
                               gains.reshape(2, 1, LANES), tab_64, nb)
            y = _shift_dispatch(
                _needs_shift(DIFF_DH, gains[0], gains[1]), functools.partial(_diff_attn, lam_init),
                seq3(q), seq3(k), seq3(v), diff_lambda[j], diff_sub_gain[j])
            x2 = _out_proj(x2, mod, i, [y.reshape(m, -1)], [diff_w_out[j].astype(BF16)], nb)
        x2 = _mlp(x2, norm_mlp[i], mod, i, mlp_w1[i].astype(BF16), mlp_w2[i].astype(BF16), nb)
    return x2.reshape(nb, seq, d)
```

```python
import functools
import math

import numpy as np
import jax
import jax.numpy as jnp
from jax import lax
from jax.experimental import pallas as pl
from jax.experimental.pallas import tpu as pltpu

F32 = jnp.float32
BF16 = jnp.bfloat16

LANES = 128
VMEM_LIMIT = 52 * 1024 * 1024

ROPE_THETA = 500000.0
EPS = 1e-6
NEG = -1e30
FORCED = 1e9

MLA_HEADS, MLA_NOPE, MLA_ROPE, MLA_V = 8, 64, 32, 64
MLA_Q_RANK, MLA_KV_RANK = 256, 256
MLA_DK = MLA_ROPE + MLA_NOPE
NSA_HEADS, NSA_GROUPS, NSA_DH = 8, 2, 64
NSA_HPG = NSA_HEADS // NSA_GROUPS
NSA_ROT = NSA_DH // 4
NSA_CMP_LEN, NSA_CMP_STRIDE, NSA_CMP_HIDDEN = 32, 16, 128
NSA_SLC_LEN, NSA_TOP_N, NSA_WINDOW = 64, 16, 512
DIFF_HEADS, DIFF_DH = 8, 64
DIFF_ROT = DIFF_DH // 4

ROW_TILE = 512
ATT_TILE = 256
MXU_COLS = 256


def _cparams(*sem):
    return pltpu.CompilerParams(dimension_semantics=sem, vmem_limit_bytes=VMEM_LIMIT)


def _split_bf16(x):
    hi = x.astype(BF16)
    lo = (x - hi.astype(F32)).astype(BF16)
    return hi, lo


def _dot(a, b):
    return jnp.dot(a, b, preferred_element_type=F32)


def _dot_nt(a, b):
    return lax.dot_general(a, b, (((1,), (1,)), ((), ())), preferred_element_type=F32)


def _sigmoid(x):
    return 1.0 / (1.0 + jnp.exp(-x))


def _lane(shape):
    return lax.broadcasted_iota(jnp.int32, shape, 1)


def _adaln_kernel(c_ref, w_ref, b_ref, o_ref):
    c = c_ref[...]
    cond = c * _sigmoid(c)
    c_hi, c_lo = _split_bf16(cond)
    w_hi, w_lo = _split_bf16(w_ref[0])
    o_ref[0] = _dot(c_hi, w_hi) + _dot(c_hi, w_lo) + _dot(c_lo, w_hi) + b_ref[0]


def _adaln(c, ada_w, ada_b):
    depth, d, n = ada_w.shape
    b = c.shape[0]
    tn = 1536
    return pl.pallas_call(
        _adaln_kernel,
        out_shape=jax.ShapeDtypeStruct((depth, b, n), F32),
        grid=(depth, n // tn),
        in_specs=[pl.BlockSpec((b, d), lambda i, j: (0, 0)),
                  pl.BlockSpec((1, d, tn), lambda i, j: (i, 0, j)),
                  pl.BlockSpec((1, 1, tn), lambda i, j: (i, 0, j))],
        out_specs=pl.BlockSpec((1, b, tn), lambda i, j: (i, 0, j)),
        compiler_params=_cparams("parallel", "parallel"),
        name="adaln",
    )(c, ada_w, ada_b.reshape(depth, 1, n))


def _rope_table_kernel(pos_ref, fa_ref, ma_ref, fb_ref, mb_ref, ca_ref, ua_ref, da_ref, cb_ref, ub_ref, db_ref):
    pos = pos_ref[...].astype(F32)
    for f_ref, m_ref, c_ref, u_ref, d_ref in ((fa_ref, ma_ref, ca_ref, ua_ref, da_ref),
                                              (fb_ref, mb_ref, cb_ref, ub_ref, db_ref)):
        ang = pos * f_ref[...]
        sin = jnp.sin(ang)
        c_ref[...] = jnp.cos(ang)
        u_ref[...] = sin * m_ref[0:1]
        d_ref[...] = sin * m_ref[1:2]


def _lane_freqs(rot, seg):
    half = rot // 2
    lane = np.arange(LANES)
    inv = ROPE_THETA ** (-(np.arange(half, dtype=np.float32) / np.float32(half)))
    f = np.where(lane % seg < rot, inv.astype(np.float32)[lane % half], 0.0)
    return jnp.asarray(f.reshape(1, LANES), F32)


def _rope_masks(rot, seg):
    half = rot // 2
    lane = np.arange(LANES) % seg
    up = ((lane >= half) & (lane < rot)).astype(np.float32)
    dn = -(lane < half).astype(np.float32)
    return jnp.asarray(np.stack([up, dn]))


def _rope_tables(pos_col):
    rows = pos_col.shape[0]
    tr = min(rows, 2048)
    spec = pl.BlockSpec((tr, LANES), lambda i: (i, 0))
    vec = pl.BlockSpec((1, LANES), lambda i: (0, 0))
    msk = pl.BlockSpec((2, LANES), lambda i: (0, 0))
    shp = jax.ShapeDtypeStruct((rows, LANES), F32)
    out = pl.pallas_call(
        _rope_table_kernel,
        out_shape=(shp,) * 6,
        grid=(rows // tr,),
        in_specs=[pl.BlockSpec((tr, 1), lambda i: (i, 0)), vec, msk, vec, msk],
        out_specs=(spec,) * 6,
        compiler_params=_cparams("parallel"),
        name="rope_tables",
    )(pos_col, _lane_freqs(MLA_ROPE, LANES), _rope_masks(MLA_ROPE, LANES),
      _lane_freqs(NSA_ROT, NSA_DH), _rope_masks(NSA_ROT, NSA_DH))
    return out[:3], out[3:]


def _rope(y, tab, half):
    c, s_up, s_dn = tab
    up = pltpu.roll(y, half, axis=1)
    dn = pltpu.roll(y, LANES - half, axis=1)
    return y * c + up * s_up + dn * s_dn


def _rms64(x, gain):
    lo = _lane(x.shape) < 64
    x2 = x * x
    s_lo = jnp.sum(jnp.where(lo, x2, 0.0), axis=-1, keepdims=True)
    s_hi = jnp.sum(jnp.where(lo, 0.0, x2), axis=-1, keepdims=True)
    r = jnp.where(lo, lax.rsqrt(s_lo * (1.0 / 64) + EPS), lax.rsqrt(s_hi * (1.0 / 64) + EPS))
    return x * (r * gain)


def _seg_ones(seg):
    lane = np.arange(LANES) // seg
    return jnp.asarray(lane[:, None] == lane[None, :], BF16)


def _rms_seg(x, gain, ones, n):
    ss = _dot((x * x).astype(BF16), ones)
    return x * (lax.rsqrt(ss * (1.0 / n) + EPS) * gain)


def _rms(x, gain):
    return x * lax.rsqrt(jnp.mean(x * x, axis=-1, keepdims=True) + EPS) * gain


def _norm_mod(x, gain, sc, sh):
    return _rms(x, gain) * (1.0 + sc) + sh


def _mod_spec(layer, k, nb, rows_per_batch, tm, d):
    per = rows_per_batch // tm
    return pl.BlockSpec((1, 1, d), lambda i, *_: (layer * nb + i // per, 0, k))


def _mlp_kernel(x_ref, gain_ref, sc_ref, sh_ref, g_ref, w1_ref, w2_ref, o_ref, h_scr, acc_scr):
    j = pl.program_id(1)

    @pl.when(j == 0)
    def _():
        h_scr[...] = _norm_mod(x_ref[...], gain_ref[...], sc_ref[0], sh_ref[0]).astype(BF16)
        acc_scr[...] = jnp.zeros_like(acc_scr)

    a = jnp.maximum(_dot(h_scr[...], w1_ref[...]), 0.0)
    acc_scr[...] += _dot((a * a).astype(BF16), w2_ref[...])

    @pl.when(j == pl.num_programs(1) - 1)
    def _():
        o_ref[...] = x_ref[...] + g_ref[0] * acc_scr[...]


def _mlp(x2, gain, mod, layer, w1, w2, nb):
    m, d = x2.shape
    ff = w1.shape[1]
    tm, tf = 1024, 1024
    s = m // nb
    return pl.pallas_call(
        _mlp_kernel,
        out_shape=jax.ShapeDtypeStruct((m, d), F32),
        grid=(m // tm, ff // tf),
        in_specs=[pl.BlockSpec((tm, d), lambda i, j: (i, 0)),
                  pl.BlockSpec((1, d), lambda i, j: (0, 0)),
                  _mod_spec(layer, 4, nb, s, tm, d),
                  _mod_spec(layer, 3, nb, s, tm, d),
                  _mod_spec(layer, 5, nb, s, tm, d),
                  pl.BlockSpec((d, tf), lambda i, j: (0, j)),
                  pl.BlockSpec((tf, d), lambda i, j: (j, 0))],
        out_specs=pl.BlockSpec((tm, d), lambda i, j: (i, 0)),
        scratch_shapes=[pltpu.VMEM((tm, d), BF16), pltpu.VMEM((tm, d), F32)],
        compiler_params=_cparams("parallel", "arbitrary"),
        name="relu2_mlp",
    )(x2, gain.reshape(1, d), mod, mod, mod, w1, w2)


def _out_proj_kernel(n_in, *refs):
    x_ref, g_ref = refs[0], refs[1]
    y_refs = refs[2:2 + n_in]
    w_refs = refs[2 + n_in:2 + 2 * n_in]
    o_ref = refs[2 + 2 * n_in]
    y = _dot(y_refs[0][...], w_refs[0][...])
    for y_ref, w_ref in zip(y_refs[1:], w_refs[1:]):
        y = y + _dot(y_ref[...], w_ref[...])
    o_ref[...] = x_ref[...] + g_ref[0] * y


def _out_proj(x2, mod, layer, ys, ws, nb):
    m, d = x2.shape
    tm = ROW_TILE
    s = m // nb
    in_specs = [pl.BlockSpec((tm, d), lambda i: (i, 0)), _mod_spec(layer, 2, nb, s, tm, d)]
    in_specs += [pl.BlockSpec((tm, y.shape[1]), lambda i: (i, 0)) for y in ys]
    in_specs += [pl.BlockSpec(w.shape, lambda i: (0, 0)) for w in ws]
    return pl.pallas_call(
        functools.partial(_out_proj_kernel, len(ys)),
        out_shape=jax.ShapeDtypeStruct((m, d), F32),
        grid=(m // tm,),
        in_specs=in_specs,
        out_specs=pl.BlockSpec((tm, d), lambda i: (i, 0)),
        compiler_params=_cparams("parallel"),
        name="out_proj_residual",
    )(x2, mod, *ys, *ws)


LOG2E = math.log2(math.e)


SCORE_BOUND = 60.0
BOUND_MARGIN = 1.02


def _needs_shift(dk, q_gain, k_gain):
    bound = dk * jnp.max(jnp.abs(q_gain)) * jnp.max(jnp.abs(k_gain)) * BOUND_MARGIN
    return jnp.logical_not(bound <= SCORE_BOUND)


def _shift_dispatch(needs_shift, attn, *args):
    return lax.cond(needs_shift, functools.partial(attn, True), functools.partial(attn, False), *args)


def _lane_tiles(x):
    return [x[:, LANES * c:LANES * (c + 1)] for c in range(x.shape[1] // LANES)]


def _flash_scratch(n, rows, t):
    return [pltpu.VMEM((n, rows, LANES), F32), pltpu.VMEM((n, rows, LANES), F32)]


def _flash(qs, k_at, v_at, first, i, t, rows_per_map, shift, scratch, window=None, masked_loop=False):
    n = len(qs)
    rows = qs[0].shape[0]
    r = lax.broadcasted_iota(jnp.int32, (rows, t), 0)
    if rows != rows_per_map:
        r = jnp.where(r >= rows_per_map, r - rows_per_map, r)
    c_minus_r = lax.broadcasted_iota(jnp.int32, (rows, t), 1) - r

    assert masked_loop or window is None

    def visible(j):
        ahead = (i - j) * t
        ok = c_minus_r <= ahead
        return ok if window is None else ok & (c_minus_r > ahead - window)

    def scores(h, j, ok):
        s = _dot_nt(qs[h], k_at(h, j))
        return s if ok is None else jnp.where(ok, s, NEG)

    def sweep(step, carry):
        if masked_loop:
            return lax.fori_loop(first, i + 1, lambda j, cr: step(j, cr, visible(j)), carry)
        return step(i, lax.fori_loop(first, i, lambda j, cr: step(j, cr, None), carry), visible(i))

    def max_step(j, ms, ok):
        out = []
        for h in range(n):
            m = ms[h]
            for s_c in _lane_tiles(scores(h, j, ok)):
                m = jnp.maximum(m, s_c)
            out.append(m)
        return tuple(out)

    if shift:
        ms = sweep(max_step, tuple(jnp.full((rows, LANES), NEG, F32) for _ in range(n)))
        ms = [jnp.broadcast_to(jnp.max(m, axis=-1, keepdims=True), (rows, LANES)) for m in ms]

    l_ref, acc_ref = scratch
    for h in range(n):
        l_ref[h] = jnp.zeros((rows, LANES), F32)
        acc_ref[h] = jnp.zeros((rows, LANES), F32)

    def sum_step(j, p_prev, ok):
        p_new = []
        for h in range(n):
            acc_ref[h] += _dot(p_prev[h], v_at(h, jnp.maximum(j - 1, 0)))
            ps = _lane_tiles(scores(h, j, ok))
            ps = [jnp.exp2(s_c - ms[h]) if shift else jnp.exp2(s_c) for s_c in ps]
            l = l_ref[h]
            for p_c in ps:
                l = l + p_c
            l_ref[h] = l
            p_new.append(jnp.concatenate(ps, axis=1).astype(BF16))
        return tuple(p_new)

    p = sweep(sum_step, tuple(jnp.zeros((rows, t), BF16) for _ in range(n)))
    return [(acc_ref[h] + _dot(p[h], v_at(h, i))) / jnp.sum(l_ref[h], axis=-1, keepdims=True)
            for h in range(n)]


def _hy_in_kernel(seq, x_ref, gain_ref, sc_ref, sh_ref, w_ref, qn_ref, kvn_ref, qg_ref, kg_ref,
                  c_ref, su_ref, sd_ref,
                  lat_ref, q_ref, ks0_ref, ks1_ref, kw_ref, pg_ref, kcmp_ref, vcmp_ref, vsw_ref):
    tm = x_ref.shape[0]
    h = _norm_mod(x_ref[...], gain_ref[...], sc_ref[0], sh_ref[0]).astype(BF16)
    tab = (c_ref[...], su_ref[...], sd_ref[...])

    def cols(a, b):
        return _dot(h, w_ref[:, a:b])

    def prep(x, g):
        return _rope(_rms64(x, g), tab, NSA_ROT // 2)

    lat_ref[:, 0:256] = _rms(cols(0, 256), qn_ref[...]).astype(BF16)
    lat_ref[:, 256:512] = _rms(cols(256, 512), kvn_ref[...]).astype(BF16)
    for ch in range(2):
        y = cols(512 + MXU_COLS * ch, 512 + MXU_COLS * (ch + 1))
        for e in range(2):
            q_ref[:, MXU_COLS * ch + LANES * e:MXU_COLS * ch + LANES * (e + 1)] = prep(
                y[:, LANES * e:LANES * (e + 1)], qg_ref[...]).astype(BF16)
    y = cols(1024, 1280)
    kslc = prep(y[:, 0:LANES], kg_ref[1])
    lane = _lane((tm, LANES))
    t_idx = (pl.program_id(0) * tm + lax.broadcasted_iota(jnp.int32, (tm, LANES), 0)) & (seq - 1)
    blk = t_idx >> 6
    ks0_ref[...] = jnp.where(lane < 64, kslc, jnp.where(lane - 64 == blk, 1.0, 0.0)).astype(BF16)
    ks1_ref[...] = jnp.where(lane >= 64, kslc, jnp.where(lane == blk, 1.0, 0.0)).astype(BF16)
    kw_ref[...] = prep(y[:, LANES:2 * LANES], kg_ref[2]).astype(BF16)
    pg_ref[...] = cols(1280, 1536)
    y = cols(1536, 1792)
    kcmp_ref[...] = y[:, 0:LANES].astype(BF16)
    vcmp_ref[...] = y[:, LANES:2 * LANES].astype(BF16)
    vsw_ref[...] = cols(1792, 2048).astype(BF16)


def _hy_in(x2, gain, mod, layer, w, q_norm, kv_norm, q_gain, k_gain, tab, nb):
    m, d = x2.shape
    tm = ROW_TILE
    seq = m // nb
    assert seq & (seq - 1) == 0 and seq // NSA_SLC_LEN == 32 and seq % tm == 0
    row = lambda wd: pl.BlockSpec((tm, wd), lambda i: (i, 0))
    full = lambda a: pl.BlockSpec(a.shape, lambda i: (0,) * a.ndim)
    bf = lambda wd: jax.ShapeDtypeStruct((m, wd), BF16)
    widths = (512, 512, LANES, LANES, LANES, 2 * LANES, LANES, LANES, 2 * LANES)
    shapes = tuple(jax.ShapeDtypeStruct((m, wd), F32) if k == 5 else bf(wd) for k, wd in enumerate(widths))
    gain = gain.reshape(1, d)
    return pl.pallas_call(
        functools.partial(_hy_in_kernel, seq),
        out_shape=shapes,
        grid=(m // tm,),
        in_specs=[row(d), full(gain), _mod_spec(layer, 1, nb, seq, tm, d), _mod_spec(layer, 0, nb, seq, tm, d),
                  full(w), full(q_norm), full(kv_norm), full(q_gain), full(k_gain),
                  row(LANES), row(LANES), row(LANES)],
        out_specs=tuple(row(wd) for wd in widths),
        compiler_params=_cparams("parallel"),
        name="hy_in_proj",
    )(x2, gain, mod, mod, w, q_norm, kv_norm, q_gain, k_gain, *tab)


def _mla_prep_kernel(lat_ref, kpe_ref, wq_ref, wk_ref, wv_ref, qg_ref, kg_ref, c_ref, su_ref, sd_ref, ones_ref,
                     q_ref, k_ref, v_ref):
    cq, ckv = lat_ref[:, 0:256], lat_ref[:, 256:512]
    kpe = kpe_ref[...]
    tab = (c_ref[...], su_ref[...], sd_ref[...])
    v_ref[...] = _dot(ckv, wv_ref[...]).astype(BF16)

    def head(x, g):
        return _rope(_rms_seg(x, g, ones_ref[...], MLA_DK), tab, MLA_ROPE // 2)

    for ch in range(MLA_HEADS // 2):
        cs = slice(MXU_COLS * ch, MXU_COLS * (ch + 1))
        q2, k2 = _dot(cq, wq_ref[:, cs]), _dot(ckv, wk_ref[:, cs])
        for e in range(2):
            sl = slice(MXU_COLS * ch + LANES * e, MXU_COLS * ch + LANES * (e + 1))
            es = slice(LANES * e, LANES * (e + 1))
            q_ref[:, sl] = head(q2[:, es], qg_ref[...]).astype(BF16)
            k_ref[:, sl] = head(k2[:, es] + kpe, kg_ref[...]).astype(BF16)


def _mla_prep(lat, pg, wq, wk, wv, q_gain, k_gain, tab):
    m = lat.shape[0]
    tm = ROW_TILE
    ones = _seg_ones(LANES)
    full = lambda a: pl.BlockSpec(a.shape, lambda i: (0, 0))
    row = lambda w: pl.BlockSpec((tm, w), lambda i: (i, 0))
    return pl.pallas_call(
        _mla_prep_kernel,
        out_shape=(jax.ShapeDtypeStruct((m, 1024), BF16), jax.ShapeDtypeStruct((m, 1024), BF16),
                   jax.ShapeDtypeStruct((m, 512), BF16)),
        grid=(m // tm,),
        in_specs=[row(512), row(LANES), full(wq), full(wk), full(wv), full(q_gain), full(k_gain),
                  row(LANES), row(LANES), row(LANES), full(ones)],
        out_specs=(row(1024), row(1024), row(512)),
        compiler_params=_cparams("parallel"),
        name="mla_prep",
    )(lat, pg, wq, wk, wv, q_gain, k_gain, *tab, ones)


def _mla_attn_kernel(shift, q_ref, k_ref, v_ref, o_ref, *scratch):
    i = pl.program_id(1)
    t = ATT_TILE
    lo = _lane((t, LANES)) < 64

    def rows(j):
        return pl.ds(pl.multiple_of(j * t, t), t)

    def tile(h):
        return slice(LANES * h, LANES * (h + 1))

    outs = _flash([q_ref[0, :, tile(h)] for h in range(MLA_HEADS)],
                  lambda h, j: k_ref[0, rows(j), tile(h)],
                  lambda h, j: v_ref[0, rows(j), tile(h // 2)], 0, i, t, t, shift, scratch)
    for p in range(MLA_HEADS // 2):
        o_ref[0, :, tile(p)] = jnp.where(lo, outs[2 * p], outs[2 * p + 1]).astype(BF16)


def _mla_attn(shift, q, k, v):
    b, s, _ = q.shape
    t = ATT_TILE
    return pl.pallas_call(
        functools.partial(_mla_attn_kernel, shift),
        out_shape=jax.ShapeDtypeStruct((b, s, 512), BF16),
        grid=(b, s // t),
        in_specs=[pl.BlockSpec((1, t, 1024), lambda bi, i: (bi, i, 0)),
                  pl.BlockSpec((1, s, 1024), lambda bi, i: (bi, 0, 0)),
                  pl.BlockSpec((1, s, 512), lambda bi, i: (bi, 0, 0))],
        out_specs=pl.BlockSpec((1, t, 512), lambda bi, i: (bi, i, 0)),
        scratch_shapes=_flash_scratch(MLA_HEADS, t, t),
        compiler_params=_cparams("parallel", "arbitrary"),
        name="mla_attention",
    )(q, k, v)


def _diff_in_kernel(x_ref, gain_ref, sc_ref, sh_ref, w_ref, g_ref, c_ref, su_ref, sd_ref, q_ref, k_ref, v_ref):
    h = _norm_mod(x_ref[...], gain_ref[...], sc_ref[0], sh_ref[0]).astype(BF16)
    tab = (c_ref[...], su_ref[...], sd_ref[...])
    n = DIFF_HEADS * LANES
    for which, out in ((0, q_ref), (1, k_ref)):
        g = g_ref[which]
        for ch in range(n // MXU_COLS):
            y = _dot(h, w_ref[:, which * n + MXU_COLS * ch:which * n + MXU_COLS * (ch + 1)])
            for e in range(2):
                out[:, MXU_COLS * ch + LANES * e:MXU_COLS * ch + LANES * (e + 1)] = _rope(
                    _rms64(y[:, LANES * e:LANES * (e + 1)], g), tab, DIFF_ROT // 2).astype(BF16)
    for ch in range(n // MXU_COLS):
        cs = slice(MXU_COLS * ch, MXU_COLS * (ch + 1))
        v_ref[:, cs] = _dot(h, w_ref[:, 2 * n + MXU_COLS * ch:2 * n + MXU_COLS * (ch + 1)]).astype(BF16)


def _diff_in(x2, gain, mod, layer, w, gains, tab, nb):
    m, d = x2.shape
    tm = ROW_TILE
    seq = m // nb
    row = lambda wd: pl.BlockSpec((tm, wd), lambda i: (i, 0))
    full = lambda a: pl.BlockSpec(a.shape, lambda i: (0,) * a.ndim)
    shp = jax.ShapeDtypeStruct((m, DIFF_HEADS * LANES), BF16)
    gain = gain.reshape(1, d)
    return pl.pallas_call(
        _diff_in_kernel,
        out_shape=(shp, shp, shp),
        grid=(m // tm,),
        in_specs=[row(d), full(gain), _mod_spec(layer, 1, nb, seq, tm, d), _mod_spec(layer, 0, nb, seq, tm, d),
                  full(w), full(gains), row(LANES), row(LANES), row(LANES)],
        out_specs=(row(DIFF_HEADS * LANES),) * 3,
        compiler_params=_cparams("parallel"),
        name="diff_in_proj",
    )(x2, gain, mod, mod, w, gains, *tab)


def _diff_attn_kernel(lam_init, shift, q_ref, k_ref, v_ref, lam_ref, sg_ref, o_ref, *scratch):
    i = pl.program_id(1)
    t = ATT_TILE
    lo = _lane((t, LANES)) < 64
    lam = lam_ref[...]
    lmb = (jnp.exp(jnp.sum(lam[0:1] * lam[1:2], axis=-1, keepdims=True))
           - jnp.exp(jnp.sum(lam[2:3] * lam[3:4], axis=-1, keepdims=True)) + lam_init)

    def rows(j):
        return pl.ds(pl.multiple_of(j * t, t), t)

    def tile(h):
        return slice(LANES * h, LANES * (h + 1))

    def both_maps(h):
        qt = q_ref[0, :, tile(h)]
        zero = jnp.zeros_like(qt)
        return jnp.concatenate([jnp.where(lo, qt, zero), jnp.where(lo, zero, qt)], axis=0)

    group = DIFF_HEADS
    for h0 in range(0, DIFF_HEADS, group):
        outs = _flash([both_maps(h0 + e) for e in range(group)],
                      lambda e, j: k_ref[0, rows(j), tile(h0 + e)],
                      lambda e, j: v_ref[0, rows(j), tile(h0 + e)], 0, i, t, t, shift, scratch)
        for e in range(group):
            o = outs[e][:t] - lmb * outs[e][t:]
            o = o * lax.rsqrt(jnp.mean(o * o, axis=-1, keepdims=True) + EPS) * sg_ref[...]
            o_ref[0, :, tile(h0 + e)] = (o * (1.0 - lam_init)).astype(BF16)


def _diff_attn(lam_init, shift, q, k, v, lam, sub_gain):
    b, s, n = q.shape
    t = ATT_TILE
    full = pl.BlockSpec((1, s, n), lambda bi, i: (bi, 0, 0))
    return pl.pallas_call(
        functools.partial(_diff_attn_kernel, lam_init, shift),
        out_shape=jax.ShapeDtypeStruct((b, s, n), BF16),
        grid=(b, s // t),
        in_specs=[pl.BlockSpec((1, t, n), lambda bi, i: (bi, i, 0)), full, full,
                  pl.BlockSpec(lam.shape, lambda bi, i: (0, 0)),
                  pl.BlockSpec((1, LANES), lambda bi, i: (0, 0))],
        out_specs=pl.BlockSpec((1, t, n), lambda bi, i: (bi, i, 0)),
        scratch_shapes=_flash_scratch(DIFF_HEADS, 2 * t, t),
        compiler_params=_cparams("parallel", "arbitrary"),
        name="diff_attention",
    )(q, k, v, lam, sub_gain.reshape(1, LANES))


def _nsa_compress_kernel(tk_ref, tv_ref, pos_ref, w1a_ref, w1b_ref, w2_ref, kg_ref,
                         c_ref, su_ref, sd_ref, kc_ref, vc_ref):
    n_rows = tk_ref.shape[1]
    for j, (t_ref, out) in enumerate(((tk_ref, kc_ref), (tv_ref, vc_ref))):
        tok = t_ref[0]
        p_hi, p_lo = _split_bf16(pos_ref[j])
        w1a, w1b = w1a_ref[j], w1b_ref[j]
        bias = (_dot(p_hi[0], w1a) + _dot(p_lo[0], w1a) + _dot(p_hi[1], w1b) + _dot(p_lo[1], w1b))[0:1]
        hid = _dot(tok, w1a) + pltpu.roll(_dot(tok, w1b), n_rows - 1, axis=0) + bias
        act = jax.nn.gelu(hid, approximate=True)
        cmp = _dot(act.astype(BF16), w2_ref[j])
        if j == 0:
            cmp = _rope(_rms64(cmp, kg_ref[...]), (c_ref[0], su_ref[0], sd_ref[0]), NSA_ROT // 2)
        out[0] = cmp.astype(BF16)


def _nsa_compress(tk, tv, pos, w1a, w1b, w2, k_gain0, tab):
    b, nr, w = tk.shape
    full = lambda a: pl.BlockSpec(a.shape, lambda bi: (0,) * a.ndim)
    per_b = lambda a: pl.BlockSpec((1,) + a.shape[1:], lambda bi: (bi,) + (0,) * (a.ndim - 1))
    shp = jax.ShapeDtypeStruct((b, nr, LANES), BF16)
    return pl.pallas_call(
        _nsa_compress_kernel,
        out_shape=(shp, shp),
        grid=(b,),
        in_specs=[per_b(tk), per_b(tv), full(pos), full(w1a), full(w1b), full(w2), full(k_gain0),
                  per_b(tab[0]), per_b(tab[1]), per_b(tab[2])],
        out_specs=(pl.BlockSpec((1, nr, LANES), lambda bi: (bi, 0, 0)),) * 2,
        compiler_params=_cparams("parallel"),
        name="nsa_compress",
    )(tk, tv, pos, w1a, w1b, w2, k_gain0, *tab)


def _nsa_attn_kernel(shift, q_ref, kc_ref, vc_ref, ks0_ref, ks1_ref, vs_ref, kw_ref, vw_ref, gate_ref, cov_ref,
                     o_ref, *scratch):
    i = pl.program_id(1)
    t = ATT_TILE
    n_blk = 32
    q0 = i * t
    lane = _lane((t, LANES))
    lo = lane < 64
    qpos = q0 + lax.broadcasted_iota(jnp.int32, (t, LANES), 0)
    cmp_ok = NSA_CMP_STRIDE * lane + (NSA_CMP_LEN - 1) <= qpos
    gates = _sigmoid(gate_ref[0])

    def rows(j):
        return pl.ds(pl.multiple_of(j * t, t), t)

    def gate(branch, h):
        col = branch * NSA_HEADS + h
        return gates[:, col:col + 1]

    out_ref, flash_scratch = scratch[0], scratch[1:]
    kc, vc = kc_ref[0], vc_ref[0]
    qts = [q_ref[0, :, LANES * hg:LANES * (hg + 1)] for hg in range(NSA_HPG)]
    owns = [lo, jnp.logical_not(lo)]
    qs = [jnp.where(own, qt, jnp.zeros_like(qt)) for own in owns for qt in qts]

    win = _flash(qs, lambda h, j: kw_ref[0, rows(j), :], lambda h, j: vw_ref[0, rows(j), :],
                 jnp.maximum(i - NSA_WINDOW // t, 0), i, t, t, shift, flash_scratch, window=NSA_WINDOW,
                 masked_loop=True)
    for h in range(NSA_HEADS):
        out_ref[h] = gate(2, h) * win[h]

    q_sel = []
    for g in range(NSA_GROUPS):
        own = owns[g]

        p_sum = jnp.zeros((t, LANES), F32)
        for hg in range(NSA_HPG):
            h = g * NSA_HPG + hg
            sc = jnp.where(cmp_ok, _dot_nt(qs[h], kc), NEG)
            p = jnp.where(cmp_ok, jnp.exp2(sc - jnp.max(sc, axis=-1, keepdims=True)), 0.0)
            l = jnp.sum(p, axis=-1, keepdims=True)
            p = p * jnp.where(l > 0.0, 1.0 / l, 0.0)
            p_sum = p_sum + p
            out_ref[h] += gate(0, h) * _dot(p.astype(BF16), vc)
        p_hi, p_lo = _split_bf16(p_sum)
        imp = (_dot_nt(cov_ref[...], p_hi) + _dot_nt(cov_ref[...], p_lo))[0:n_blk]
        blk = lax.broadcasted_iota(jnp.int32, (n_blk, t), 0)
        jt = (q0 + lax.broadcasted_iota(jnp.int32, (n_blk, t), 1)) >> 6
        allowed = blk <= jt
        forced = allowed & ((blk == 0) | (blk >= jt - 1))
        imp = jnp.where(forced, FORCED, jnp.where(allowed, imp, NEG))
        rank = jnp.zeros((n_blk, t), jnp.int32)
        for jp in range(n_blk):
            other = imp[jp:jp + 1, :]
            ahead = (other > imp) | ((other == imp) & (blk > jp))
            rank = rank + ahead.astype(jnp.int32)
        pen_t = jnp.where((rank < NSA_TOP_N) & allowed, 0.0, NEG)
        pen_t = jnp.concatenate([pen_t, jnp.zeros((LANES - n_blk, t), F32)], axis=0)
        pen = jnp.transpose(pen_t)
        if g == 0:
            pen = pltpu.roll(pen, 64, axis=1)
        pen = pen.astype(BF16)
        q_sel += [jnp.where(own, qt, pen) for qt in qts]

    slc = _flash(q_sel, lambda h, j: (ks0_ref if h < NSA_HPG else ks1_ref)[0, rows(j), :],
                 lambda h, j: vs_ref[0, rows(j), :], 0, i, t, t, shift, flash_scratch, masked_loop=True)
    for hg in range(NSA_HPG):
        h0, h1 = hg, NSA_HPG + hg
        o_ref[0, :, LANES * hg:LANES * (hg + 1)] = jnp.where(
            lo, out_ref[h0] + gate(1, h0) * slc[h0], out_ref[h1] + gate(1, h1) * slc[h1]).astype(BF16)


def _nsa_attn(shift, q, kc, vc, kslc0, kslc1, kwin, vsw, pg, cover_t):
    b, s, _ = q.shape
    t = ATT_TILE
    seq = pl.BlockSpec((1, s, LANES), lambda bi, i: (bi, 0, 0))
    seq_hi = pl.BlockSpec((1, s, LANES), lambda bi, i: (bi, 0, 1))
    cmp = pl.BlockSpec((1, LANES, LANES), lambda bi, i: (bi, 0, 0))
    return pl.pallas_call(
        functools.partial(_nsa_attn_kernel, shift),
        out_shape=jax.ShapeDtypeStruct((b, s, 512), BF16),
        grid=(b, s // t),
        in_specs=[pl.BlockSpec((1, t, 512), lambda bi, i: (bi, i, 0)), cmp, cmp,
                  seq, seq, seq, seq, seq_hi,
                  pl.BlockSpec((1, t, LANES), lambda bi, i: (bi, i, 1)),
                  pl.BlockSpec((LANES, LANES), lambda bi, i: (0, 0))],
        out_specs=pl.BlockSpec((1, t, 512), lambda bi, i: (bi, i, 0)),
        scratch_shapes=[pltpu.VMEM((NSA_HEADS, t, LANES), F32)] + _flash_scratch(NSA_HEADS, t, t),
        compiler_params=_cparams("parallel", "arbitrary"),
        name="nsa_attention",
    )(q, kc, vc, kslc0, kslc1, vsw, kwin, vsw, pg, cover_t)


def _pad_lanes(a, width):
    return jnp.pad(a, [(0, 0)] * (a.ndim - 1) + [(0, width - a.shape[-1])])


def _hy_in_weight(w):
    d = w.shape[0]
    cq, ckv, kpe, nq, nkv, gate = jnp.split(w, [256, 512, 544, 1056, 1824], axis=1)
    nq = nq.reshape(d, NSA_GROUPS, NSA_HPG, NSA_DH).transpose(0, 2, 1, 3).reshape(d, 512)
    kcmp, vcmp, kslc, vslc, kwin, vwin = jnp.split(nkv, 6, axis=1)
    return jnp.concatenate([cq, ckv, nq, kslc, kwin, _pad_lanes(kpe, LANES), _pad_lanes(gate, LANES),
                            kcmp, vcmp, vslc, vwin], axis=1).astype(BF16)


def _cover_t():
    nc, ns = 127, 32
    c_start = np.arange(nc) * NSA_CMP_STRIDE
    c_end = c_start + NSA_CMP_LEN - 1
    j_start = np.arange(ns) * NSA_SLC_LEN
    cover = ((c_start[:, None] <= j_start[None, :] + NSA_SLC_LEN - 1) & (c_end[:, None] >= j_start[None, :]))
    out = np.zeros((LANES, LANES), np.float32)
    out[:ns, :nc] = cover.T
    return jnp.asarray(out, BF16)


def _compress_weights(cmp_pos, w1, w2):
    half = NSA_CMP_STRIDE
    w1 = w1.reshape(2, 2, half, NSA_DH, NSA_CMP_HIDDEN)
    eye = jnp.eye(NSA_GROUPS, dtype=F32)
    w1e = jnp.einsum('jcldn,gh->jclgdhn', w1, eye).reshape(2, 2, half * LANES, NSA_GROUPS * NSA_CMP_HIDDEN)
    w2e = jnp.einsum('jnd,gh->jgnhd', w2, eye).reshape(2, NSA_GROUPS * NSA_CMP_HIDDEN, LANES)
    pos = cmp_pos.reshape(2, 2, half, 1, NSA_DH)
    pos = jnp.broadcast_to(pos, (2, 2, half, NSA_GROUPS, NSA_DH)).reshape(2, 2, 1, half * LANES)
    pos = jnp.broadcast_to(pos, (2, 2, 8, half * LANES))
    return pos, w1e[:, 0].astype(BF16), w1e[:, 1].astype(BF16), w2e.astype(BF16)


def kernel(x, c, positions, ada_w, ada_b, norm_mix, norm_mlp, mlp_w1, mlp_w2, hy_w_in, hy_w_out, mla_q_norm, mla_w_uq, mla_kv_norm, mla_w_ukv, mla_q_gain, mla_k_gain, nsa_q_gain, nsa_k_gain, nsa_cmp_pos, nsa_cmp_w1, nsa_cmp_w2, diff_w_qkv, diff_w_out, diff_q_gain, diff_k_gain, diff_lambda, diff_sub_gain):
    nb, seq, d = x.shape
    depth = ada_w.shape[0]
    m = nb * seq
    n_cmp = seq // NSA_CMP_STRIDE

    mod = _adaln(c, ada_w, ada_b).reshape(depth * nb, 1, 6 * d)
    tab_mla, tab_64 = _rope_tables(positions.reshape(m, 1))
    pos_c = jnp.pad(positions[:, NSA_CMP_LEN - 1::NSA_CMP_STRIDE], ((0, 0), (0, 1)))
    _, tab_cmp = _rope_tables(pos_c.reshape(nb * n_cmp, 1))
    tab_cmp = tuple(a.reshape(nb, n_cmp, LANES) for a in tab_cmp)
    cover_t = _cover_t()
    seq3 = lambda a: a.reshape(nb, seq, -1)

    x2 = x.reshape(m, d)
    for i in range(depth):
        j = i // 2
        if i % 2 == 0:
            qg = jnp.tile(nsa_q_gain[j] * (NSA_DH ** -0.5 * LOG2E), 2).reshape(1, LANES)
            kg = jnp.tile(nsa_k_gain[j], (1, 2)).reshape(3, 1, LANES)
            lat, q_nsa, kslc0, kslc1, kwin, pg, kcmp, vcmp, vsw = _hy_in(
                x2, norm_mix[i], mod, i, _hy_in_weight(hy_w_in[j]),
                mla_q_norm[j].reshape(1, -1), mla_kv_norm[j].reshape(1, -1), qg, kg, tab_64, nb)

            wq = _pad_lanes(mla_w_uq[j].reshape(MLA_Q_RANK, MLA_HEADS, MLA_DK), LANES).reshape(MLA_Q_RANK, -1)
            wkv = mla_w_ukv[j].reshape(MLA_KV_RANK, MLA_HEADS, MLA_NOPE + MLA_V)
            wk = jnp.pad(wkv[..., :MLA_NOPE], ((0, 0), (0, 0), (MLA_ROPE, LANES - MLA_DK))).reshape(MLA_KV_RANK, -1)
            wv = wkv[..., MLA_NOPE:].reshape(MLA_KV_RANK, -1)
            q_mla, k_mla, v_mla = _mla_prep(
                lat, pg, wq.astype(BF16), wk.astype(BF16), wv.astype(BF16),
                _pad_lanes(mla_q_gain[j] * (MLA_DK ** -0.5 * LOG2E), LANES).reshape(1, LANES),
                _pad_lanes(mla_k_gain[j], LANES).reshape(1, LANES),
                tab_mla)
            y_mla = _shift_dispatch(
                _needs_shift(MLA_DK, mla_q_gain[j] * (MLA_DK ** -0.5 * LOG2E), mla_k_gain[j]), _mla_attn,
                seq3(q_mla), seq3(k_mla), seq3(v_mla))

            pos_e, w1a, w1b, w2e = _compress_weights(nsa_cmp_pos[j], nsa_cmp_w1[j], nsa_cmp_w2[j])
            chunks = lambda a: a.reshape(nb, n_cmp, NSA_CMP_STRIDE * LANES)
            kc, vc = _nsa_compress(chunks(kcmp), chunks(vcmp), pos_e, w1a, w1b, w2e, kg[0], tab_cmp)
            y_nsa = _shift_dispatch(
                _needs_shift(NSA_DH, qg, nsa_k_gain[j][1:]), _nsa_attn,
                seq3(q_nsa), kc, vc, seq3(kslc0), seq3(kslc1), seq3(kwin), seq3(vsw), seq3(pg), cover_t)

            w_out = hy_w_out[j]
            w_nsa = w_out[512:].reshape(NSA_GROUPS, NSA_HPG, NSA_DH, d).transpose(1, 0, 2, 3).reshape(512, d)
            x2 = _out_proj(x2, mod, i, [y_mla.reshape(m, -1), y_nsa.reshape(m, -1)],
                           [w_out[:512].astype(BF16), w_nsa.astype(BF16)], nb)
        else:
            lam_init = 0.8 - 0.6 * math.exp(-0.3 * i)
            gains = jnp.stack([jnp.tile(diff_q_gain[j] * (DIFF_DH ** -0.5 * LOG2E), 2), jnp.tile(diff_k_gain[j], 2)])
            q, k, v = _diff_in(x2, norm_mix[i], mod, i, diff_w_qkv[j].astype(BF16),
                               gains.reshape(2, 1, LANES), tab_64, nb)
            y = _shift_dispatch(
                _needs_shift(DIFF_DH, gains[0], gains[1]), functools.partial(_diff_attn, lam_init),
                seq3(q), seq3(k), seq3(v), diff_lambda[j], diff_sub_gain[j])
            x2 = _out_proj(x2, mod, i, [y.reshape(m, -1)], [diff_w_out[j].astype(BF16)], nb)
        x2 = _mlp(x2, norm_mlp[i], mod, i, mlp_w1[i].astype(BF16), mlp_w2[i].astype(BF16), nb)
    return x2.reshape(nb, seq, d)
```

```python
import functools
import math

import numpy as np
import jax
import jax.numpy as jnp
from jax import lax
from jax.experimental import pallas as pl
from jax.experimental.pallas import tpu as pltpu

F32 = jnp.float32
BF16 = jnp.bfloat16

LANES = 128
VMEM_LIMIT = 52 * 1024 * 1024

ROPE_THETA = 500000.0
EPS = 1e-6
NEG = -1e30
FORCED = 1e9

MLA_HEADS, MLA_NOPE, MLA_ROPE, MLA_V = 8, 64, 32, 64
MLA_Q_RANK, MLA_KV_RANK = 256, 256
MLA_DK = MLA_ROPE + MLA_NOPE
NSA_HEADS, NSA_GROUPS, NSA_DH = 8, 2, 64
NSA_HPG = NSA_HEADS // NSA_GROUPS
NSA_ROT = NSA_DH // 4
NSA_CMP_LEN, NSA_CMP_STRIDE, NSA_CMP_HIDDEN = 32, 16, 128
NSA_SLC_LEN, NSA_TOP_N, NSA_WINDOW = 64, 16, 512
DIFF_HEADS, DIFF_DH = 8, 64
DIFF_ROT = DIFF_DH // 4

ROW_TILE = 512
ATT_TILE = 256
Q_TILE = 256
MXU_COLS = 256


def _cparams(*sem):
    return pltpu.CompilerParams(dimension_semantics=sem, vmem_limit_bytes=VMEM_LIMIT)


def _split_bf16(x):
    hi = x.astype(BF16)
    lo = (x - hi.astype(F32)).astype(BF16)
    return hi, lo


def _dot(a, b):
    return jnp.dot(a, b, preferred_element_type=F32)


def _dot_nt(a, b):
    return lax.dot_general(a, b, (((1,), (1,)), ((), ())), preferred_element_type=F32)


def _sigmoid(x):
    return 1.0 / (1.0 + jnp.exp(-x))


def _lane(shape):
    return lax.broadcasted_iota(jnp.int32, shape, 1)


def _adaln_kernel(c_ref, w_ref, b_ref, o_ref):
    c = c_ref[...]
    cond = c * _sigmoid(c)
    c_hi, c_lo = _split_bf16(cond)
    w_hi, w_lo = _split_bf16(w_ref[0])
    o_ref[0] = _dot(c_hi, w_hi) + _dot(c_hi, w_lo) + _dot(c_lo, w_hi) + b_ref[0]


def _adaln(c, ada_w, ada_b):
    depth, d, n = ada_w.shape
    b = c.shape[0]
    tn = 1536
    return pl.pallas_call(
        _adaln_kernel,
        out_shape=jax.ShapeDtypeStruct((depth, b, n), F32),
        grid=(depth, n // tn),
        in_specs=[pl.BlockSpec((b, d), lambda i, j: (0, 0)),
                  pl.BlockSpec((1, d, tn), lambda i, j: (i, 0, j)),
                  pl.BlockSpec((1, 1, tn), lambda i, j: (i, 0, j))],
        out_specs=pl.BlockSpec((1, b, tn), lambda i, j: (i, 0, j)),
        compiler_params=_cparams("parallel", "parallel"),
        name="adaln",
    )(c, ada_w, ada_b.reshape(depth, 1, n))


def _rope_table_kernel(pos_ref, fa_ref, ma_ref, fb_ref, mb_ref, ca_ref, ua_ref, da_ref, cb_ref, ub_ref, db_ref):
    pos = pos_ref[...].astype(F32)
    for f_ref, m_ref, c_ref, u_ref, d_ref in ((fa_ref, ma_ref, ca_ref, ua_ref, da_ref),
                                              (fb_ref, mb_ref, cb_ref, ub_ref, db_ref)):
        ang = pos * f_ref[...]
        sin = jnp.sin(ang)
        c_ref[...] = jnp.cos(ang)
        u_ref[...] = sin * m_ref[0:1]
        d_ref[...] = sin * m_ref[1:2]


def _lane_freqs(rot, seg):
    half = rot // 2
    lane = np.arange(LANES)
    inv = ROPE_THETA ** (-(np.arange(half, dtype=np.float32) / np.float32(half)))
    f = np.where(lane % seg < rot, inv.astype(np.float32)[lane % half], 0.0)
    return jnp.asarray(f.reshape(1, LANES), F32)


def _rope_masks(rot, seg):
    half = rot // 2
    lane = np.arange(LANES) % seg
    up = ((lane >= half) & (lane < rot)).astype(np.float32)
    dn = -(lane < half).astype(np.float32)
    return jnp.asarray(np.stack([up, dn]))


def _rope_tables(pos_col):
    rows = pos_col.shape[0]
    tr = min(rows, 2048)
    spec = pl.BlockSpec((tr, LANES), lambda i: (i, 0))
    vec = pl.BlockSpec((1, LANES), lambda i: (0, 0))
    msk = pl.BlockSpec((2, LANES), lambda i: (0, 0))
    shp = jax.ShapeDtypeStruct((rows, LANES), F32)
    out = pl.pallas_call(
        _rope_table_kernel,
        out_shape=(shp,) * 6,
        grid=(rows // tr,),
        in_specs=[pl.BlockSpec((tr, 1), lambda i: (i, 0)), vec, msk, vec, msk],
        out_specs=(spec,) * 6,
        compiler_params=_cparams("parallel"),
        name="rope_tables",
    )(pos_col, _lane_freqs(MLA_ROPE, LANES), _rope_masks(MLA_ROPE, LANES),
      _lane_freqs(NSA_ROT, NSA_DH), _rope_masks(NSA_ROT, NSA_DH))
    return out[:3], out[3:]


def _rope(y, tab, half):
    c, s_up, s_dn = tab
    up = pltpu.roll(y, half, axis=1)
    dn = pltpu.roll(y, LANES - half, axis=1)
    return y * c + up * s_up + dn * s_dn


def _rms64(x, gain):
    lo = _lane(x.shape) < 64
    x2 = x * x
    s_lo = jnp.sum(jnp.where(lo, x2, 0.0), axis=-1, keepdims=True)
    s_hi = jnp.sum(jnp.where(lo, 0.0, x2), axis=-1, keepdims=True)
    r = jnp.where(lo, lax.rsqrt(s_lo * (1.0 / 64) + EPS), lax.rsqrt(s_hi * (1.0 / 64) + EPS))
    return x * (r * gain)


def _seg_ones(seg):
    lane = np.arange(LANES) // seg
    return jnp.asarray(lane[:, None] == lane[None, :], BF16)


def _rms_seg(x, gain_root, ones, n):
    ss = _dot((x * x).astype(BF16), ones)
    return x * (lax.rsqrt(ss + n * EPS) * gain_root)


def _rms(x, gain):
    return x * lax.rsqrt(jnp.mean(x * x, axis=-1, keepdims=True) + EPS) * gain


def _norm_mod(x, gain, sc, sh):
    return _rms(x, gain) * (1.0 + sc) + sh


def _mod_spec(layer, k, nb, rows_per_batch, tm, d):
    per = rows_per_batch // tm
    return pl.BlockSpec((1, 1, d), lambda i, *_: (layer * nb + i // per, 0, k))


def _mlp_kernel(x_ref, gain_ref, sc_ref, sh_ref, g_ref, w1_ref, w2_ref, o_ref, h_scr, acc_scr):
    j = pl.program_id(1)

    @pl.when(j == 0)
    def _():
        h_scr[...] = _norm_mod(x_ref[...], gain_ref[...], sc_ref[0], sh_ref[0]).astype(BF16)
        acc_scr[...] = jnp.zeros_like(acc_scr)

    a = jnp.maximum(_dot(h_scr[...], w1_ref[...]), 0.0)
    acc_scr[...] += _dot((a * a).astype(BF16), w2_ref[...])

    @pl.when(j == pl.num_programs(1) - 1)
    def _():
        o_ref[...] = x_ref[...] + g_ref[0] * acc_scr[...]


def _mlp(x2, gain, mod, layer, w1, w2, nb):
    m, d = x2.shape
    ff = w1.shape[1]
    tm, tf = 1024, 1024
    s = m // nb
    return pl.pallas_call(
        _mlp_kernel,
        out_shape=jax.ShapeDtypeStruct((m, d), F32),
        grid=(m // tm, ff // tf),
        in_specs=[pl.BlockSpec((tm, d), lambda i, j: (i, 0)),
                  pl.BlockSpec((1, d), lambda i, j: (0, 0)),
                  _mod_spec(layer, 4, nb, s, tm, d),
                  _mod_spec(layer, 3, nb, s, tm, d),
                  _mod_spec(layer, 5, nb, s, tm, d),
                  pl.BlockSpec((d, tf), lambda i, j: (0, j)),
                  pl.BlockSpec((tf, d), lambda i, j: (j, 0))],
        out_specs=pl.BlockSpec((tm, d), lambda i, j: (i, 0)),
        scratch_shapes=[pltpu.VMEM((tm, d), BF16), pltpu.VMEM((tm, d), F32)],
        compiler_params=_cparams("parallel", "arbitrary"),
        name="relu2_mlp",
    )(x2, gain.reshape(1, d), mod, mod, mod, w1, w2)


LOG2E = math.log2(math.e)


SCORE_BOUND = 60.0
BOUND_MARGIN = 1.02


def _needs_shift(dk, q_gain, k_gain):
    bound = dk * jnp.max(jnp.abs(q_gain)) * jnp.max(jnp.abs(k_gain)) * BOUND_MARGIN
    return jnp.logical_not(bound <= SCORE_BOUND)


def _shift_dispatch(needs_shift, attn, *args):
    return lax.cond(needs_shift, functools.partial(attn, True), functools.partial(attn, False), *args)


def _lane_tiles(x):
    return [x[:, LANES * c:LANES * (c + 1)] for c in range(x.shape[1] // LANES)]


def _flash_scratch(n, rows, t):
    return [pltpu.VMEM((n, rows, LANES), F32), pltpu.VMEM((n, rows, LANES), F32)]


def _flash(qs, k_at, v_at, first, i, t, rows_per_map, shift, scratch, window=None, masked_loop=False):
    n = len(qs)
    rows = qs[0].shape[0]
    n_diag = rows_per_map // t
    last = (i + 1) * n_diag - 1
    r = lax.broadcasted_iota(jnp.int32, (rows, t), 0)
    if rows != rows_per_map:
        r = jnp.where(r >= rows_per_map, r - rows_per_map, r)
    c_minus_r = lax.broadcasted_iota(jnp.int32, (rows, t), 1) - r

    assert masked_loop or window is None

    def visible(j):
        ahead = i * rows_per_map - j * t
        ok = c_minus_r <= ahead
        return ok if window is None else ok & (c_minus_r > ahead - window)

    def scores(h, j, ok):
        s = _dot_nt(qs[h], k_at(h, j))
        return s if ok is None else jnp.where(ok, s, NEG)

    def sweep(step, carry):
        if masked_loop:
            return lax.fori_loop(first, last + 1, lambda j, cr: step(j, cr, visible(j)), carry)
        carry = lax.fori_loop(first, i * n_diag, lambda j, cr: step(j, cr, None), carry)
        for d in range(n_diag):
            carry = step(i * n_diag + d, carry, visible(i * n_diag + d))
        return carry

    def max_step(j, ms, ok):
        out = []
        for h in range(n):
            m = ms[h]
            for s_c in _lane_tiles(scores(h, j, ok)):
                m = jnp.maximum(m, s_c)
            out.append(m)
        return tuple(out)

    if shift:
        ms = sweep(max_step, tuple(jnp.full((rows, LANES), NEG, F32) for _ in range(n)))
        ms = [jnp.broadcast_to(jnp.max(m, axis=-1, keepdims=True), (rows, LANES)) for m in ms]

    l_ref, acc_ref = scratch
    for h in range(n):
        l_ref[h] = jnp.zeros((rows, LANES), F32)
        acc_ref[h] = jnp.zeros((rows, LANES), F32)

    def sum_step(j, p_prev, ok):
        p_new = []
        for h in range(n):
            acc_ref[h] += _dot(p_prev[h], v_at(h, jnp.maximum(j - 1, 0)))
            ps = _lane_tiles(scores(h, j, ok))
            ps = [jnp.exp2(s_c - ms[h]) if shift else jnp.exp2(s_c) for s_c in ps]
            l = l_ref[h]
            for p_c in ps:
                l = l + p_c
            l_ref[h] = l
            p_new.append(jnp.concatenate(ps, axis=1).astype(BF16))
        return tuple(p_new)

    p = sweep(sum_step, tuple(jnp.zeros((rows, t), BF16) for _ in range(n)))
    return [(acc_ref[h] + _dot(p[h], v_at(h, last))) / jnp.sum(l_ref[h], axis=-1, keepdims=True)
            for h in range(n)]


def _hy_in_kernel(seq, x_ref, gain_ref, sc_ref, sh_ref, w_ref, qn_ref, kvn_ref, qg_ref, kg_ref,
                  c_ref, su_ref, sd_ref,
                  lat_ref, q_ref, ks0_ref, ks1_ref, kw_ref, pg_ref, kcmp_ref, vcmp_ref, vsw_ref):
    tm = x_ref.shape[0]
    h = _norm_mod(x_ref[...], gain_ref[...], sc_ref[0], sh_ref[0]).astype(BF16)
    tab = (c_ref[...], su_ref[...], sd_ref[...])

    def cols(a, b):
        return _dot(h, w_ref[:, a:b])

    def prep(x, g):
        return _rope(_rms64(x, g), tab, NSA_ROT // 2)

    lat_ref[:, 0:256] = _rms(cols(0, 256), qn_ref[...]).astype(BF16)
    lat_ref[:, 256:512] = _rms(cols(256, 512), kvn_ref[...]).astype(BF16)
    for ch in range(2):
        y = cols(512 + MXU_COLS * ch, 512 + MXU_COLS * (ch + 1))
        for e in range(2):
            q_ref[:, MXU_COLS * ch + LANES * e:MXU_COLS * ch + LANES * (e + 1)] = prep(
                y[:, LANES * e:LANES * (e + 1)], qg_ref[...]).astype(BF16)
    y = cols(1024, 1280)
    kslc = prep(y[:, 0:LANES], kg_ref[1])
    lane = _lane((tm, LANES))
    t_idx = (pl.program_id(0) * tm + lax.broadcasted_iota(jnp.int32, (tm, LANES), 0)) & (seq - 1)
    blk = t_idx >> 6
    ks0_ref[...] = jnp.where(lane < 64, kslc, jnp.where(lane - 64 == blk, 1.0, 0.0)).astype(BF16)
    ks1_ref[...] = jnp.where(lane >= 64, kslc, jnp.where(lane == blk, 1.0, 0.0)).astype(BF16)
    kw_ref[...] = prep(y[:, LANES:2 * LANES], kg_ref[2]).astype(BF16)
    pg_ref[...] = cols(1280, 1536)
    y = cols(1536, 1792)
    kcmp_ref[...] = y[:, 0:LANES].astype(BF16)
    vcmp_ref[...] = y[:, LANES:2 * LANES].astype(BF16)
    vsw_ref[...] = cols(1792, 2048).astype(BF16)


def _hy_in(x2, gain, mod, layer, w, q_norm, kv_norm, q_gain, k_gain, tab, nb):
    m, d = x2.shape
    tm = ROW_TILE
    seq = m // nb
    assert seq & (seq - 1) == 0 and seq // NSA_SLC_LEN == 32 and seq % tm == 0
    row = lambda wd: pl.BlockSpec((tm, wd), lambda i: (i, 0))
    full = lambda a: pl.BlockSpec(a.shape, lambda i: (0,) * a.ndim)
    bf = lambda wd: jax.ShapeDtypeStruct((m, wd), BF16)
    widths = (512, 512, LANES, LANES, LANES, 2 * LANES, LANES, LANES, 2 * LANES)
    shapes = tuple(jax.ShapeDtypeStruct((m, wd), F32) if k == 5 else bf(wd) for k, wd in enumerate(widths))
    gain = gain.reshape(1, d)
    return pl.pallas_call(
        functools.partial(_hy_in_kernel, seq),
        out_shape=shapes,
        grid=(m // tm,),
        in_specs=[row(d), full(gain), _mod_spec(layer, 1, nb, seq, tm, d), _mod_spec(layer, 0, nb, seq, tm, d),
                  full(w), full(q_norm), full(kv_norm), full(q_gain), full(k_gain),
                  row(LANES), row(LANES), row(LANES)],
        out_specs=tuple(row(wd) for wd in widths),
        compiler_params=_cparams("parallel"),
        name="hy_in_proj",
    )(x2, gain, mod, mod, w, q_norm, kv_norm, q_gain, k_gain, *tab)


def _mla_prep_kernel(lat_ref, kpe_ref, wq_ref, wk_ref, wv_ref, qg_ref, kg_ref, c_ref, su_ref, sd_ref, ones_ref,
                     q_ref, k_ref, v_ref):
    cq, ckv = lat_ref[:, 0:256], lat_ref[:, 256:512]
    kpe = kpe_ref[...]
    tab = (c_ref[...], su_ref[...], sd_ref[...])
    v_ref[...] = _dot(ckv, wv_ref[...]).astype(BF16)

    def head(x, g):
        return _rope(_rms_seg(x, g, ones_ref[...], MLA_DK), tab, MLA_ROPE // 2)

    for ch in range(MLA_HEADS // 2):
        cs = slice(MXU_COLS * ch, MXU_COLS * (ch + 1))
        q2, k2 = _dot(cq, wq_ref[:, cs]), _dot(ckv, wk_ref[:, cs])
        for e in range(2):
            sl = slice(MXU_COLS * ch + LANES * e, MXU_COLS * ch + LANES * (e + 1))
            es = slice(LANES * e, LANES * (e + 1))
            q_ref[:, sl] = head(q2[:, es], qg_ref[...]).astype(BF16)
            k_ref[:, sl] = head(k2[:, es] + kpe, kg_ref[...]).astype(BF16)


def _mla_prep(lat, pg, wq, wk, wv, q_gain, k_gain, tab):
    m = lat.shape[0]
    tm = ROW_TILE
    ones = _seg_ones(LANES)
    full = lambda a: pl.BlockSpec(a.shape, lambda i: (0, 0))
    row = lambda w: pl.BlockSpec((tm, w), lambda i: (i, 0))
    return pl.pallas_call(
        _mla_prep_kernel,
        out_shape=(jax.ShapeDtypeStruct((m, 1024), BF16), jax.ShapeDtypeStruct((m, 1024), BF16),
                   jax.ShapeDtypeStruct((m, 512), BF16)),
        grid=(m // tm,),
        in_specs=[row(512), row(LANES), full(wq), full(wk), full(wv), full(q_gain), full(k_gain),
                  row(LANES), row(LANES), row(LANES), full(ones)],
        out_specs=(row(1024), row(1024), row(512)),
        compiler_params=_cparams("parallel"),
        name="mla_prep",
    )(lat, pg, wq, wk, wv, q_gain, k_gain, *tab, ones)


def _mla_attn_kernel(shift, q_ref, k_ref, v_ref, o_ref, *scratch):
    i = pl.program_id(1)
    t, tq = ATT_TILE, Q_TILE
    lo = _lane((tq, LANES)) < 64

    def rows(j):
        return pl.ds(pl.multiple_of(j * t, t), t)

    def tile(h):
        return slice(LANES * h, LANES * (h + 1))

    outs = _flash([q_ref[0, :, tile(h)] for h in range(MLA_HEADS)],
                  lambda h, j: k_ref[0, rows(j), tile(h)],
                  lambda h, j: v_ref[0, rows(j), tile(h // 2)], 0, i, t, tq, shift, scratch)
    for p in range(MLA_HEADS // 2):
        o_ref[0, :, tile(p)] = jnp.where(lo, outs[2 * p], outs[2 * p + 1]).astype(BF16)


def _mla_attn(shift, q, k, v):
    b, s, _ = q.shape
    t, tq = ATT_TILE, Q_TILE
    return pl.pallas_call(
        functools.partial(_mla_attn_kernel, shift),
        out_shape=jax.ShapeDtypeStruct((b, s, 512), BF16),
        grid=(b, s // tq),
        in_specs=[pl.BlockSpec((1, tq, 1024), lambda bi, i: (bi, i, 0)),
                  pl.BlockSpec((1, s, 1024), lambda bi, i: (bi, 0, 0)),
                  pl.BlockSpec((1, s, 512), lambda bi, i: (bi, 0, 0))],
        out_specs=pl.BlockSpec((1, tq, 512), lambda bi, i: (bi, i, 0)),
        scratch_shapes=_flash_scratch(MLA_HEADS, tq, t),
        compiler_params=_cparams("parallel", "arbitrary"),
        name="mla_attention",
    )(q, k, v)


def _diff_in_kernel(x_ref, gain_ref, sc_ref, sh_ref, w_ref, g_ref, c_ref, su_ref, sd_ref, q_ref, k_ref, v_ref):
    h = _norm_mod(x_ref[...], gain_ref[...], sc_ref[0], sh_ref[0]).astype(BF16)
    tab = (c_ref[...], su_ref[...], sd_ref[...])
    n = DIFF_HEADS * LANES
    for which, out in ((0, q_ref), (1, k_ref)):
        g = g_ref[which]
        for ch in range(n // MXU_COLS):
            y = _dot(h, w_ref[:, which * n + MXU_COLS * ch:which * n + MXU_COLS * (ch + 1)])
            for e in range(2):
                out[:, MXU_COLS * ch + LANES * e:MXU_COLS * ch + LANES * (e + 1)] = _rope(
                    _rms64(y[:, LANES * e:LANES * (e + 1)], g), tab, DIFF_ROT // 2).astype(BF16)
    for ch in range(n // MXU_COLS):
        cs = slice(MXU_COLS * ch, MXU_COLS * (ch + 1))
        v_ref[:, cs] = _dot(h, w_ref[:, 2 * n + MXU_COLS * ch:2 * n + MXU_COLS * (ch + 1)]).astype(BF16)


def _diff_in(x2, gain, mod, layer, w, gains, tab, nb):
    m, d = x2.shape
    tm = ROW_TILE
    seq = m // nb
    row = lambda wd: pl.BlockSpec((tm, wd), lambda i: (i, 0))
    full = lambda a: pl.BlockSpec(a.shape, lambda i: (0,) * a.ndim)
    shp = jax.ShapeDtypeStruct((m, DIFF_HEADS * LANES), BF16)
    gain = gain.reshape(1, d)
    return pl.pallas_call(
        _diff_in_kernel,
        out_shape=(shp, shp, shp),
        grid=(m // tm,),
        in_specs=[row(d), full(gain), _mod_spec(layer, 1, nb, seq, tm, d), _mod_spec(layer, 0, nb, seq, tm, d),
                  full(w), full(gains), row(LANES), row(LANES), row(LANES)],
        out_specs=(row(DIFF_HEADS * LANES),) * 3,
        compiler_params=_cparams("parallel"),
        name="diff_in_proj",
    )(x2, gain, mod, mod, w, gains, *tab)


def _diff_attn_kernel(lam_init, shift, q_ref, k_ref, v_ref, lam_ref, sg_ref, x_ref, g_ref, w_ref,
                      o_ref, y_ref, *scratch):
    i = pl.program_id(1)
    t = ATT_TILE
    lo = _lane((t, LANES)) < 64
    lam = lam_ref[...]
    lmb = (jnp.exp(jnp.sum(lam[0:1] * lam[1:2], axis=-1, keepdims=True))
           - jnp.exp(jnp.sum(lam[2:3] * lam[3:4], axis=-1, keepdims=True)) + lam_init)

    def rows(j):
        return pl.ds(pl.multiple_of(j * t, t), t)

    def tile(h):
        return slice(LANES * h, LANES * (h + 1))

    def both_maps(h):
        qt = q_ref[0, :, tile(h)]
        zero = jnp.zeros_like(qt)
        return jnp.concatenate([jnp.where(lo, qt, zero), jnp.where(lo, zero, qt)], axis=0)

    outs = _flash([both_maps(h) for h in range(DIFF_HEADS)],
                  lambda h, j: k_ref[0, rows(j), tile(h)],
                  lambda h, j: v_ref[0, rows(j), tile(h)], 0, i, t, t, shift, scratch)
    for h in range(DIFF_HEADS):
        o = outs[h][:t] - lmb * outs[h][t:]
        o = o * lax.rsqrt(jnp.mean(o * o, axis=-1, keepdims=True) + EPS) * sg_ref[...]
        y_ref[:, tile(h)] = (o * (1.0 - lam_init)).astype(BF16)
    o_ref[0] = x_ref[0] + g_ref[0] * _dot(y_ref[...], w_ref[...])


def _gate_spec(layer, nb, d):
    return pl.BlockSpec((1, 1, d), lambda bi, i: (layer * nb + bi, 0, 2))


def _diff_attn(lam_init, layer, shift, q, k, v, lam, sub_gain, x3, mod, w_out):
    b, s, n = q.shape
    d = x3.shape[-1]
    t = ATT_TILE
    full = pl.BlockSpec((1, s, n), lambda bi, i: (bi, 0, 0))
    return pl.pallas_call(
        functools.partial(_diff_attn_kernel, lam_init, shift),
        out_shape=jax.ShapeDtypeStruct((b, s, d), F32),
        grid=(b, s // t),
        in_specs=[pl.BlockSpec((1, t, n), lambda bi, i: (bi, i, 0)), full, full,
                  pl.BlockSpec(lam.shape, lambda bi, i: (0, 0)),
                  pl.BlockSpec((1, LANES), lambda bi, i: (0, 0)),
                  pl.BlockSpec((1, t, d), lambda bi, i: (bi, i, 0)), _gate_spec(layer, b, d),
                  pl.BlockSpec(w_out.shape, lambda bi, i: (0, 0))],
        out_specs=pl.BlockSpec((1, t, d), lambda bi, i: (bi, i, 0)),
        scratch_shapes=[pltpu.VMEM((t, n), BF16)] + _flash_scratch(DIFF_HEADS, 2 * t, t),
        compiler_params=_cparams("parallel", "arbitrary"),
        name="diff_attention",
    )(q, k, v, lam, sub_gain.reshape(1, LANES), x3, mod, w_out)


def _nsa_compress_kernel(tk_ref, tv_ref, pos_ref, w1a_ref, w1b_ref, w2_ref, kg_ref,
                         c_ref, su_ref, sd_ref, kc_ref, vc_ref):
    n_rows = tk_ref.shape[1]
    for j, (t_ref, out) in enumerate(((tk_ref, kc_ref), (tv_ref, vc_ref))):
        tok = t_ref[0]
        p_hi, p_lo = _split_bf16(pos_ref[j])
        w1a, w1b = w1a_ref[j], w1b_ref[j]
        bias = (_dot(p_hi[0], w1a) + _dot(p_lo[0], w1a) + _dot(p_hi[1], w1b) + _dot(p_lo[1], w1b))[0:1]
        hid = _dot(tok, w1a) + pltpu.roll(_dot(tok, w1b), n_rows - 1, axis=0) + bias
        act = jax.nn.gelu(hid, approximate=True)
        cmp = _dot(act.astype(BF16), w2_ref[j])
        if j == 0:
            cmp = _rope(_rms64(cmp, kg_ref[...]), (c_ref[0], su_ref[0], sd_ref[0]), NSA_ROT // 2)
        out[0] = cmp.astype(BF16)


def _nsa_compress(tk, tv, pos, w1a, w1b, w2, k_gain0, tab):
    b, nr, w = tk.shape
    full = lambda a: pl.BlockSpec(a.shape, lambda bi: (0,) * a.ndim)
    per_b = lambda a: pl.BlockSpec((1,) + a.shape[1:], lambda bi: (bi,) + (0,) * (a.ndim - 1))
    shp = jax.ShapeDtypeStruct((b, nr, LANES), BF16)
    return pl.pallas_call(
        _nsa_compress_kernel,
        out_shape=(shp, shp),
        grid=(b,),
        in_specs=[per_b(tk), per_b(tv), full(pos), full(w1a), full(w1b), full(w2), full(k_gain0),
                  per_b(tab[0]), per_b(tab[1]), per_b(tab[2])],
        out_specs=(pl.BlockSpec((1, nr, LANES), lambda bi: (bi, 0, 0)),) * 2,
        compiler_params=_cparams("parallel"),
        name="nsa_compress",
    )(tk, tv, pos, w1a, w1b, w2, k_gain0, *tab)


def _nsa_attn_kernel(shift, q_ref, kc_ref, vc_ref, ks0_ref, ks1_ref, vs_ref, kw_ref, vw_ref, gate_ref, cov_ref,
                     x_ref, g_ref, ymla_ref, wmla_ref, wnsa_ref, o_ref, y_ref, *scratch):
    i = pl.program_id(1)
    t, tq = ATT_TILE, Q_TILE
    n_blk = 32
    q0 = i * tq
    lane = _lane((tq, LANES))
    lo = lane < 64
    qpos = q0 + lax.broadcasted_iota(jnp.int32, (tq, LANES), 0)
    cmp_ok = NSA_CMP_STRIDE * lane + (NSA_CMP_LEN - 1) <= qpos
    gates = _sigmoid(gate_ref[0])

    def rows(j):
        return pl.ds(pl.multiple_of(j * t, t), t)

    def gate(branch, h):
        col = branch * NSA_HEADS + h
        return gates[:, col:col + 1]

    out_ref, flash_scratch = scratch[0], scratch[1:]
    kc, vc = kc_ref[0], vc_ref[0]
    qts = [q_ref[0, :, LANES * hg:LANES * (hg + 1)] for hg in range(NSA_HPG)]
    owns = [lo, jnp.logical_not(lo)]
    qs = [jnp.where(own, qt, jnp.zeros_like(qt)) for own in owns for qt in qts]

    win = _flash(qs, lambda h, j: kw_ref[0, rows(j), :], lambda h, j: vw_ref[0, rows(j), :],
                 jnp.maximum((i * tq - NSA_WINDOW) // t, 0), i, t, tq, shift, flash_scratch, window=NSA_WINDOW,
                 masked_loop=True)
    for h in range(NSA_HEADS):
        out_ref[h] = gate(2, h) * win[h]

    q_sel = []
    for g in range(NSA_GROUPS):
        own = owns[g]

        p_sum = jnp.zeros((tq, LANES), F32)
        for hg in range(NSA_HPG):
            h = g * NSA_HPG + hg
            sc = jnp.where(cmp_ok, _dot_nt(qs[h], kc), NEG)
            p = jnp.where(cmp_ok, jnp.exp2(sc - jnp.max(sc, axis=-1, keepdims=True)), 0.0)
            l = jnp.sum(p, axis=-1, keepdims=True)
            p = p * jnp.where(l > 0.0, 1.0 / l, 0.0)
            p_sum = p_sum + p
            out_ref[h] += gate(0, h) * _dot(p.astype(BF16), vc)
        p_hi, p_lo = _split_bf16(p_sum)
        imp = (_dot_nt(cov_ref[...], p_hi) + _dot_nt(cov_ref[...], p_lo))[0:n_blk]
        blk = lax.broadcasted_iota(jnp.int32, (n_blk, tq), 0)
        jt = (q0 + lax.broadcasted_iota(jnp.int32, (n_blk, tq), 1)) >> 6
        allowed = blk <= jt
        forced = allowed & ((blk == 0) | (blk >= jt - 1))
        imp = jnp.where(forced, FORCED, jnp.where(allowed, imp, NEG))
        rank = jnp.zeros((n_blk, tq), jnp.int32)
        for jp in range(n_blk):
            other = imp[jp:jp + 1, :]
            ahead = (other > imp) | ((other == imp) & (blk > jp))
            rank = rank + ahead.astype(jnp.int32)
        pen_t = jnp.where((rank < NSA_TOP_N) & allowed, 0.0, NEG)
        pen_t = jnp.concatenate([pen_t, jnp.zeros((LANES - n_blk, tq), F32)], axis=0)
        pen = jnp.transpose(pen_t)
        if g == 0:
            pen = pltpu.roll(pen, 64, axis=1)
        pen = pen.astype(BF16)
        q_sel += [jnp.where(own, qt, pen) for qt in qts]

    slc = _flash(q_sel, lambda h, j: (ks0_ref if h < NSA_HPG else ks1_ref)[0, rows(j), :],
                 lambda h, j: vs_ref[0, rows(j), :], 0, i, t, tq, shift, flash_scratch, masked_loop=True)
    for hg in range(NSA_HPG):
        h0, h1 = hg, NSA_HPG + hg
        y_ref[:, LANES * hg:LANES * (hg + 1)] = jnp.where(
            lo, out_ref[h0] + gate(1, h0) * slc[h0], out_ref[h1] + gate(1, h1) * slc[h1]).astype(BF16)
    y = _dot(ymla_ref[0], wmla_ref[...]) + _dot(y_ref[...], wnsa_ref[...])
    o_ref[0] = x_ref[0] + g_ref[0] * y


def _nsa_attn(layer, shift, q, kc, vc, kslc0, kslc1, kwin, vsw, pg, cover_t, x3, mod, y_mla, w_mla, w_nsa):
    b, s, _ = q.shape
    d = x3.shape[-1]
    t, tq = ATT_TILE, Q_TILE
    tile3 = lambda w: pl.BlockSpec((1, tq, w), lambda bi, i: (bi, i, 0))
    full2 = lambda a: pl.BlockSpec(a.shape, lambda bi, i: (0, 0))
    seq = pl.BlockSpec((1, s, LANES), lambda bi, i: (bi, 0, 0))
    seq_hi = pl.BlockSpec((1, s, LANES), lambda bi, i: (bi, 0, 1))
    cmp = pl.BlockSpec((1, LANES, LANES), lambda bi, i: (bi, 0, 0))
    return pl.pallas_call(
        functools.partial(_nsa_attn_kernel, shift),
        out_shape=jax.ShapeDtypeStruct((b, s, d), F32),
        grid=(b, s // tq),
        in_specs=[tile3(512), cmp, cmp,
                  seq, seq, seq, seq, seq_hi,
                  pl.BlockSpec((1, tq, LANES), lambda bi, i: (bi, i, 1)),
                  full2(cover_t),
                  tile3(d), _gate_spec(layer, b, d), tile3(512), full2(w_mla), full2(w_nsa)],
        out_specs=tile3(d),
        scratch_shapes=([pltpu.VMEM((tq, 512), BF16), pltpu.VMEM((NSA_HEADS, tq, LANES), F32)]
                        + _flash_scratch(NSA_HEADS, tq, t)),
        compiler_params=_cparams("parallel", "arbitrary"),
        name="nsa_attention",
    )(q, kc, vc, kslc0, kslc1, vsw, kwin, vsw, pg, cover_t, x3, mod, y_mla, w_mla, w_nsa)


def _pad_lanes(a, width):
    return jnp.pad(a, [(0, 0)] * (a.ndim - 1) + [(0, width - a.shape[-1])])


def _hy_in_weight(w):
    d = w.shape[0]
    cq, ckv, kpe, nq, nkv, gate = jnp.split(w, [256, 512, 544, 1056, 1824], axis=1)
    nq = nq.reshape(d, NSA_GROUPS, NSA_HPG, NSA_DH).transpose(0, 2, 1, 3).reshape(d, 512)
    kcmp, vcmp, kslc, vslc, kwin, vwin = jnp.split(nkv, 6, axis=1)
    return jnp.concatenate([cq, ckv, nq, kslc, kwin, _pad_lanes(kpe, LANES), _pad_lanes(gate, LANES),
                            kcmp, vcmp, vslc, vwin], axis=1).astype(BF16)


def _cover_t():
    nc, ns = 127, 32
    c_start = np.arange(nc) * NSA_CMP_STRIDE
    c_end = c_start + NSA_CMP_LEN - 1
    j_start = np.arange(ns) * NSA_SLC_LEN
    cover = ((c_start[:, None] <= j_start[None, :] + NSA_SLC_LEN - 1) & (c_end[:, None] >= j_start[None, :]))
    out = np.zeros((LANES, LANES), np.float32)
    out[:ns, :nc] = cover.T
    return jnp.asarray(out, BF16)


def _compress_weights(cmp_pos, w1, w2):
    half = NSA_CMP_STRIDE
    w1 = w1.reshape(2, 2, half, NSA_DH, NSA_CMP_HIDDEN)
    eye = jnp.eye(NSA_GROUPS, dtype=F32)
    w1e = jnp.einsum('jcldn,gh->jclgdhn', w1, eye).reshape(2, 2, half * LANES, NSA_GROUPS * NSA_CMP_HIDDEN)
    w2e = jnp.einsum('jnd,gh->jgnhd', w2, eye).reshape(2, NSA_GROUPS * NSA_CMP_HIDDEN, LANES)
    pos = cmp_pos.reshape(2, 2, half, 1, NSA_DH)
    pos = jnp.broadcast_to(pos, (2, 2, half, NSA_GROUPS, NSA_DH)).reshape(2, 2, 1, half * LANES)
    pos = jnp.broadcast_to(pos, (2, 2, 8, half * LANES))
    return pos, w1e[:, 0].astype(BF16), w1e[:, 1].astype(BF16), w2e.astype(BF16)


def kernel(x, c, positions, ada_w, ada_b, norm_mix, norm_mlp, mlp_w1, mlp_w2, hy_w_in, hy_w_out, mla_q_norm, mla_w_uq, mla_kv_norm, mla_w_ukv, mla_q_gain, mla_k_gain, nsa_q_gain, nsa_k_gain, nsa_cmp_pos, nsa_cmp_w1, nsa_cmp_w2, diff_w_qkv, diff_w_out, diff_q_gain, diff_k_gain, diff_lambda, diff_sub_gain):
    nb, seq, d = x.shape
    depth = ada_w.shape[0]
    m = nb * seq
    n_cmp = seq // NSA_CMP_STRIDE

    mod = _adaln(c, ada_w, ada_b).reshape(depth * nb, 1, 6 * d)
    tab_mla, tab_64 = _rope_tables(positions.reshape(m, 1))
    pos_c = jnp.pad(positions[:, NSA_CMP_LEN - 1::NSA_CMP_STRIDE], ((0, 0), (0, 1)))
    _, tab_cmp = _rope_tables(pos_c.reshape(nb * n_cmp, 1))
    tab_cmp = tuple(a.reshape(nb, n_cmp, LANES) for a in tab_cmp)
    cover_t = _cover_t()
    seq3 = lambda a: a.reshape(nb, seq, -1)

    x2 = x.reshape(m, d)
    for i in range(depth):
        j = i // 2
        if i % 2 == 0:
            qg_s = nsa_q_gain[j] * (NSA_DH ** -0.5 * LOG2E)
            qg = jnp.tile(qg_s, 2).reshape(1, LANES)
            kg = jnp.tile(nsa_k_gain[j], (1, 2)).reshape(3, 1, LANES)
            lat, q_nsa, kslc0, kslc1, kwin, pg, kcmp, vcmp, vsw = _hy_in(
                x2, norm_mix[i], mod, i, _hy_in_weight(hy_w_in[j]),
                mla_q_norm[j].reshape(1, -1), mla_kv_norm[j].reshape(1, -1), qg, kg, tab_64, nb)

            wq = _pad_lanes(mla_w_uq[j].reshape(MLA_Q_RANK, MLA_HEADS, MLA_DK), LANES).reshape(MLA_Q_RANK, -1)
            wkv = mla_w_ukv[j].reshape(MLA_KV_RANK, MLA_HEADS, MLA_NOPE + MLA_V)
            wk = jnp.pad(wkv[..., :MLA_NOPE], ((0, 0), (0, 0), (MLA_ROPE, LANES - MLA_DK))).reshape(MLA_KV_RANK, -1)
            wv = wkv[..., MLA_NOPE:].reshape(MLA_KV_RANK, -1)
            q_mla, k_mla, v_mla = _mla_prep(
                lat, pg, wq.astype(BF16), wk.astype(BF16), wv.astype(BF16),
                _pad_lanes(mla_q_gain[j] * LOG2E, LANES).reshape(1, LANES),
                _pad_lanes(mla_k_gain[j] * MLA_DK ** 0.5, LANES).reshape(1, LANES),
                tab_mla)
            y_mla = _shift_dispatch(
                _needs_shift(MLA_DK, mla_q_gain[j] * (MLA_DK ** -0.5 * LOG2E), mla_k_gain[j]), _mla_attn,
                seq3(q_mla), seq3(k_mla), seq3(v_mla))

            pos_e, w1a, w1b, w2e = _compress_weights(nsa_cmp_pos[j], nsa_cmp_w1[j], nsa_cmp_w2[j])
            chunks = lambda a: a.reshape(nb, n_cmp, NSA_CMP_STRIDE * LANES)
            kc, vc = _nsa_compress(chunks(kcmp), chunks(vcmp), pos_e, w1a, w1b, w2e, kg[0], tab_cmp)
            w_out = hy_w_out[j]
            w_nsa = w_out[512:].reshape(NSA_GROUPS, NSA_HPG, NSA_DH, d).transpose(1, 0, 2, 3).reshape(512, d)
            x2 = _shift_dispatch(
                _needs_shift(NSA_DH, qg_s, nsa_k_gain[j][1:]), functools.partial(_nsa_attn, i),
                seq3(q_nsa), kc, vc, seq3(kslc0), seq3(kslc1), seq3(kwin), seq3(vsw), seq3(pg), cover_t,
                seq3(x2), mod, y_mla, w_out[:512].astype(BF16), w_nsa.astype(BF16)).reshape(m, d)
        else:
            lam_init = 0.8 - 0.6 * math.exp(-0.3 * i)
            gains = jnp.stack([jnp.tile(diff_q_gain[j] * (DIFF_DH ** -0.5 * LOG2E), 2), jnp.tile(diff_k_gain[j], 2)])
            q, k, v = _diff_in(x2, norm_mix[i], mod, i, diff_w_qkv[j].astype(BF16),
                               gains.reshape(2, 1, LANES), tab_64, nb)
            x2 = _shift_dispatch(
                _needs_shift(DIFF_DH, gains[0], gains[1]), functools.partial(_diff_attn, lam_init, i),
                seq3(q), seq3(k), seq3(v), diff_lambda[j], diff_sub_gain[j],
                seq3(x2), mod, diff_w_out[j].astype(BF16)).reshape(m, d)
        x2 = _mlp(x2, norm_mlp[i], mod, i, mlp_w1[i].astype(BF16), mlp_w2[i].astype(BF16), nb)
    return x2.reshape(nb, seq, d)
```

```python
import functools
import math

import numpy as np
import jax
import jax.numpy as jnp
from jax import lax
from jax.experimental import pallas as pl
from jax.experimental.pallas import tpu as pltpu

F32 = jnp.float32
BF16 = jnp.bfloat16

LANES = 128
VMEM_LIMIT = 52 * 1024 * 1024

ROPE_THETA = 500000.0
EPS = 1e-6
NEG = -1e30
FORCED = 1e9

MLA_HEADS, MLA_NOPE, MLA_ROPE, MLA_V = 8, 64, 32, 64
MLA_Q_RANK, MLA_KV_RANK = 256, 256
MLA_DK = MLA_ROPE + MLA_NOPE
NSA_HEADS, NSA_GROUPS, NSA_DH = 8, 2, 64
NSA_HPG = NSA_HEADS // NSA_GROUPS
NSA_ROT = NSA_DH // 4
NSA_CMP_LEN, NSA_CMP_STRIDE, NSA_CMP_HIDDEN = 32, 16, 128
NSA_SLC_LEN, NSA_TOP_N, NSA_WINDOW = 64, 16, 512
DIFF_HEADS, DIFF_DH = 8, 64
DIFF_ROT = DIFF_DH // 4

ROW_TILE = 512
ATT_TILE = 256
Q_TILE = 256
MXU_COLS = 256


def _cparams(*sem):
    return pltpu.CompilerParams(dimension_semantics=sem, vmem_limit_bytes=VMEM_LIMIT)


def _split_bf16(x):
    hi = x.astype(BF16)
    lo = (x - hi.astype(F32)).astype(BF16)
    return hi, lo


def _dot(a, b):
    return jnp.dot(a, b, preferred_element_type=F32)


def _dot_nt(a, b):
    return lax.dot_general(a, b, (((1,), (1,)), ((), ())), preferred_element_type=F32)


def _sigmoid(x):
    return 1.0 / (1.0 + jnp.exp(-x))


def _lane(shape):
    return lax.broadcasted_iota(jnp.int32, shape, 1)


def _adaln_kernel(c_ref, w_ref, b_ref, o_ref):
    c = c_ref[...]
    cond = c * _sigmoid(c)
    c_hi, c_lo = _split_bf16(cond)
    w_hi, w_lo = _split_bf16(w_ref[0])
    o_ref[0] = _dot(c_hi, w_hi) + _dot(c_hi, w_lo) + _dot(c_lo, w_hi) + b_ref[0]


def _adaln(c, ada_w, ada_b):
    depth, d, n = ada_w.shape
    b = c.shape[0]
    tn = 1536
    return pl.pallas_call(
        _adaln_kernel,
        out_shape=jax.ShapeDtypeStruct((depth, b, n), F32),
        grid=(depth, n // tn),
        in_specs=[pl.BlockSpec((b, d), lambda i, j: (0, 0)),
                  pl.BlockSpec((1, d, tn), lambda i, j: (i, 0, j)),
                  pl.BlockSpec((1, 1, tn), lambda i, j: (i, 0, j))],
        out_specs=pl.BlockSpec((1, b, tn), lambda i, j: (i, 0, j)),
        compiler_params=_cparams("parallel", "parallel"),
        name="adaln",
    )(c, ada_w, ada_b.reshape(depth, 1, n))


def _rope_table_kernel(pos_ref, fa_ref, ma_ref, fb_ref, mb_ref, ca_ref, ua_ref, da_ref, cb_ref, ub_ref, db_ref):
    pos = pos_ref[...].astype(F32)
    for f_ref, m_ref, c_ref, u_ref, d_ref in ((fa_ref, ma_ref, ca_ref, ua_ref, da_ref),
                                              (fb_ref, mb_ref, cb_ref, ub_ref, db_ref)):
        ang = pos * f_ref[...]
        sin = jnp.sin(ang)
        c_ref[...] = jnp.cos(ang)
        u_ref[...] = sin * m_ref[0:1]
        d_ref[...] = sin * m_ref[1:2]


def _lane_freqs(rot, seg):
    half = rot // 2
    lane = np.arange(LANES)
    inv = ROPE_THETA ** (-(np.arange(half, dtype=np.float32) / np.float32(half)))
    f = np.where(lane % seg < rot, inv.astype(np.float32)[lane % half], 0.0)
    return jnp.asarray(f.reshape(1, LANES), F32)


def _rope_masks(rot, seg):
    half = rot // 2
    lane = np.arange(LANES) % seg
    up = ((lane >= half) & (lane < rot)).astype(np.float32)
    dn = -(lane < half).astype(np.float32)
    return jnp.asarray(np.stack([up, dn]))


def _rope_tables(pos_col):
    rows = pos_col.shape[0]
    tr = min(rows, 2048)
    spec = pl.BlockSpec((tr, LANES), lambda i: (i, 0))
    vec = pl.BlockSpec((1, LANES), lambda i: (0, 0))
    msk = pl.BlockSpec((2, LANES), lambda i: (0, 0))
    shp = jax.ShapeDtypeStruct((rows, LANES), F32)
    out = pl.pallas_call(
        _rope_table_kernel,
        out_shape=(shp,) * 6,
        grid=(rows // tr,),
        in_specs=[pl.BlockSpec((tr, 1), lambda i: (i, 0)), vec, msk, vec, msk],
        out_specs=(spec,) * 6,
        compiler_params=_cparams("parallel"),
        name="rope_tables",
    )(pos_col, _lane_freqs(MLA_ROPE, LANES), _rope_masks(MLA_ROPE, LANES),
      _lane_freqs(NSA_ROT, NSA_DH), _rope_masks(NSA_ROT, NSA_DH))
    return out[:3], out[3:]


def _rope(y, tab, half):
    c, s_up, s_dn = tab
    up = pltpu.roll(y, half, axis=1)
    dn = pltpu.roll(y, LANES - half, axis=1)
    return y * c + up * s_up + dn * s_dn


def _rms64(x, gain):
    lo = _lane(x.shape) < 64
    x2 = x * x
    s_lo = jnp.sum(jnp.where(lo, x2, 0.0), axis=-1, keepdims=True)
    s_hi = jnp.sum(jnp.where(lo, 0.0, x2), axis=-1, keepdims=True)
    r = jnp.where(lo, lax.rsqrt(s_lo * (1.0 / 64) + EPS), lax.rsqrt(s_hi * (1.0 / 64) + EPS))
    return x * (r * gain)


def _seg_ones(seg):
    lane = np.arange(LANES) // seg
    return jnp.asarray(lane[:, None] == lane[None, :], BF16)


def _rms_seg(x, gain_root, ones, n):
    ss = _dot((x * x).astype(BF16), ones)
    return x * (lax.rsqrt(ss + n * EPS) * gain_root)


def _rms(x, gain):
    return x * lax.rsqrt(jnp.mean(x * x, axis=-1, keepdims=True) + EPS) * gain


def _norm_mod(x, gain, sc, sh):
    return _rms(x, gain) * (1.0 + sc) + sh


def _mod_spec(layer, k, nb, rows_per_batch, tm, d):
    per = rows_per_batch // tm
    return pl.BlockSpec((1, 1, d), lambda i, *_: (layer * nb + i // per, 0, k))


def _mlp_kernel(x_ref, gain_ref, sc_ref, sh_ref, g_ref, w1_ref, w2_ref, o_ref, h_scr, acc_scr):
    j = pl.program_id(1)

    @pl.when(j == 0)
    def _():
        h_scr[...] = _norm_mod(x_ref[...], gain_ref[...], sc_ref[0], sh_ref[0]).astype(BF16)
        acc_scr[...] = jnp.zeros_like(acc_scr)

    a = jnp.maximum(_dot(h_scr[...], w1_ref[...].astype(BF16)), 0.0)
    acc_scr[...] += _dot((a * a).astype(BF16), w2_ref[...].astype(BF16))

    @pl.when(j == pl.num_programs(1) - 1)
    def _():
        o_ref[...] = x_ref[...] + g_ref[0] * acc_scr[...]


def _mlp(x2, gain, mod, layer, w1, w2, nb):
    m, d = x2.shape
    ff = w1.shape[1]
    tm, tf = 1024, 1024
    s = m // nb
    return pl.pallas_call(
        _mlp_kernel,
        out_shape=jax.ShapeDtypeStruct((m, d), F32),
        grid=(m // tm, ff // tf),
        in_specs=[pl.BlockSpec((tm, d), lambda i, j: (i, 0)),
                  pl.BlockSpec((1, d), lambda i, j: (0, 0)),
                  _mod_spec(layer, 4, nb, s, tm, d),
                  _mod_spec(layer, 3, nb, s, tm, d),
                  _mod_spec(layer, 5, nb, s, tm, d),
                  pl.BlockSpec((d, tf), lambda i, j: (0, j)),
                  pl.BlockSpec((tf, d), lambda i, j: (j, 0))],
        out_specs=pl.BlockSpec((tm, d), lambda i, j: (i, 0)),
        scratch_shapes=[pltpu.VMEM((tm, d), BF16), pltpu.VMEM((tm, d), F32)],
        compiler_params=_cparams("parallel", "arbitrary"),
        name="relu2_mlp",
    )(x2, gain.reshape(1, d), mod, mod, mod, w1, w2)


LOG2E = math.log2(math.e)


SCORE_BOUND = 60.0
BOUND_MARGIN = 1.02


def _needs_shift(dk, q_gain, k_gain):
    bound = dk * jnp.max(jnp.abs(q_gain)) * jnp.max(jnp.abs(k_gain)) * BOUND_MARGIN
    return jnp.logical_not(bound <= SCORE_BOUND)


def _shift_dispatch(needs_shift, attn, *args):
    return lax.cond(needs_shift, functools.partial(attn, True), functools.partial(attn, False), *args)


def _lane_tiles(x):
    return [x[:, LANES * c:LANES * (c + 1)] for c in range(x.shape[1] // LANES)]


def _flash_scratch(n, rows, t):
    return [pltpu.VMEM((n, rows, LANES), F32), pltpu.VMEM((n, rows, LANES), F32)]


def _flash(qs, k_at, v_at, first, i, t, rows_per_map, shift, scratch, window=None, masked_loop=False):
    n = len(qs)
    rows = qs[0].shape[0]
    n_diag = rows_per_map // t
    last = (i + 1) * n_diag - 1
    r = lax.broadcasted_iota(jnp.int32, (rows, t), 0)
    if rows != rows_per_map:
        r = jnp.where(r >= rows_per_map, r - rows_per_map, r)
    c_minus_r = lax.broadcasted_iota(jnp.int32, (rows, t), 1) - r

    assert masked_loop or window is None

    def visible(j):
        ahead = i * rows_per_map - j * t
        ok = c_minus_r <= ahead
        return ok if window is None else ok & (c_minus_r > ahead - window)

    def scores(h, j, ok):
        s = _dot_nt(qs[h], k_at(h, j))
        return s if ok is None else jnp.where(ok, s, NEG)

    def sweep(step, carry):
        if masked_loop:
            return lax.fori_loop(first, last + 1, lambda j, cr: step(j, cr, visible(j)), carry)
        carry = lax.fori_loop(first, i * n_diag, lambda j, cr: step(j, cr, None), carry)
        for d in range(n_diag):
            carry = step(i * n_diag + d, carry, visible(i * n_diag + d))
        return carry

    def max_step(j, ms, ok):
        out = []
        for h in range(n):
            m = ms[h]
            for s_c in _lane_tiles(scores(h, j, ok)):
                m = jnp.maximum(m, s_c)
            out.append(m)
        return tuple(out)

    if shift:
        ms = sweep(max_step, tuple(jnp.full((rows, LANES), NEG, F32) for _ in range(n)))
        ms = [jnp.broadcast_to(jnp.max(m, axis=-1, keepdims=True), (rows, LANES)) for m in ms]

    l_ref, acc_ref = scratch
    for h in range(n):
        l_ref[h] = jnp.zeros((rows, LANES), F32)
        acc_ref[h] = jnp.zeros((rows, LANES), F32)

    def sum_step(j, p_prev, ok):
        p_new = []
        for h in range(n):
            acc_ref[h] += _dot(p_prev[h], v_at(h, jnp.maximum(j - 1, 0)))
            ps = _lane_tiles(scores(h, j, ok))
            ps = [jnp.exp2(s_c - ms[h]) if shift else jnp.exp2(s_c) for s_c in ps]
            l = l_ref[h]
            for p_c in ps:
                l = l + p_c
            l_ref[h] = l
            p_new.append(jnp.concatenate(ps, axis=1).astype(BF16))
        return tuple(p_new)

    p = sweep(sum_step, tuple(jnp.zeros((rows, t), BF16) for _ in range(n)))
    return [(acc_ref[h] + _dot(p[h], v_at(h, last))) / jnp.sum(l_ref[h], axis=-1, keepdims=True)
            for h in range(n)]


def _hy_in_kernel(seq, x_ref, gain_ref, sc_ref, sh_ref, w_ref, qn_ref, kvn_ref, qg_ref, kg_ref,
                  c_ref, su_ref, sd_ref,
                  lat_ref, q_ref, ks0_ref, ks1_ref, kw_ref, pg_ref, kcmp_ref, vcmp_ref, vsw_ref):
    tm = x_ref.shape[0]
    h = _norm_mod(x_ref[...], gain_ref[...], sc_ref[0], sh_ref[0]).astype(BF16)
    tab = (c_ref[...], su_ref[...], sd_ref[...])

    def cols(a, b):
        return _dot(h, w_ref[:, a:b])

    def prep(x, g):
        return _rope(_rms64(x, g), tab, NSA_ROT // 2)

    lat_ref[:, 0:256] = _rms(cols(0, 256), qn_ref[...]).astype(BF16)
    lat_ref[:, 256:512] = _rms(cols(256, 512), kvn_ref[...]).astype(BF16)
    for ch in range(2):
        y = cols(512 + MXU_COLS * ch, 512 + MXU_COLS * (ch + 1))
        for e in range(2):
            q_ref[:, MXU_COLS * ch + LANES * e:MXU_COLS * ch + LANES * (e + 1)] = prep(
                y[:, LANES * e:LANES * (e + 1)], qg_ref[...]).astype(BF16)
    y = cols(1024, 1280)
    kslc = prep(y[:, 0:LANES], kg_ref[1])
    lane = _lane((tm, LANES))
    t_idx = (pl.program_id(0) * tm + lax.broadcasted_iota(jnp.int32, (tm, LANES), 0)) & (seq - 1)
    blk = t_idx >> 6
    ks0_ref[...] = jnp.where(lane < 64, kslc, jnp.where(lane - 64 == blk, 1.0, 0.0)).astype(BF16)
    ks1_ref[...] = jnp.where(lane >= 64, kslc, jnp.where(lane == blk, 1.0, 0.0)).astype(BF16)
    kw_ref[...] = prep(y[:, LANES:2 * LANES], kg_ref[2]).astype(BF16)
    pg_ref[...] = cols(1280, 1536)
    y = cols(1536, 1792)
    kcmp_ref[...] = y[:, 0:LANES].astype(BF16)
    vcmp_ref[...] = y[:, LANES:2 * LANES].astype(BF16)
    vsw_ref[...] = cols(1792, 2048).astype(BF16)


def _hy_in(x2, gain, mod, layer, w, q_norm, kv_norm, q_gain, k_gain, tab, nb):
    m, d = x2.shape
    tm = ROW_TILE
    seq = m // nb
    assert seq & (seq - 1) == 0 and seq // NSA_SLC_LEN <= 32 and seq % tm == 0
    row = lambda wd: pl.BlockSpec((tm, wd), lambda i: (i, 0))
    full = lambda a: pl.BlockSpec(a.shape, lambda i: (0,) * a.ndim)
    bf = lambda wd: jax.ShapeDtypeStruct((m, wd), BF16)
    widths = (512, 512, LANES, LANES, LANES, 2 * LANES, LANES, LANES, 2 * LANES)
    shapes = tuple(jax.ShapeDtypeStruct((m, wd), F32) if k == 5 else bf(wd) for k, wd in enumerate(widths))
    gain = gain.reshape(1, d)
    return pl.pallas_call(
        functools.partial(_hy_in_kernel, seq),
        out_shape=shapes,
        grid=(m // tm,),
        in_specs=[row(d), full(gain), _mod_spec(layer, 1, nb, seq, tm, d), _mod_spec(layer, 0, nb, seq, tm, d),
                  full(w), full(q_norm), full(kv_norm), full(q_gain), full(k_gain),
                  row(LANES), row(LANES), row(LANES)],
        out_specs=tuple(row(wd) for wd in widths),
        compiler_params=_cparams("parallel"),
        name="hy_in_proj",
    )(x2, gain, mod, mod, w, q_norm, kv_norm, q_gain, k_gain, *tab)


def _mla_prep_kernel(lat_ref, kpe_ref, wq_ref, wk_ref, wv_ref, qg_ref, kg_ref, c_ref, su_ref, sd_ref, ones_ref,
                     q_ref, k_ref, v_ref):
    cq, ckv = lat_ref[:, 0:256], lat_ref[:, 256:512]
    kpe = kpe_ref[...]
    tab = (c_ref[...], su_ref[...], sd_ref[...])
    v_ref[...] = _dot(ckv, wv_ref[...]).astype(BF16)

    def head(x, g):
        return _rope(_rms_seg(x, g, ones_ref[...], MLA_DK), tab, MLA_ROPE // 2)

    for ch in range(MLA_HEADS // 2):
        cs = slice(MXU_COLS * ch, MXU_COLS * (ch + 1))
        q2, k2 = _dot(cq, wq_ref[:, cs]), _dot(ckv, wk_ref[:, cs])
        for e in range(2):
            sl = slice(MXU_COLS * ch + LANES * e, MXU_COLS * ch + LANES * (e + 1))
            es = slice(LANES * e, LANES * (e + 1))
            q_ref[:, sl] = head(q2[:, es], qg_ref[...]).astype(BF16)
            k_ref[:, sl] = head(k2[:, es] + kpe, kg_ref[...]).astype(BF16)


def _mla_prep(lat, pg, wq, wk, wv, q_gain, k_gain, tab):
    m = lat.shape[0]
    tm = ROW_TILE
    ones = _seg_ones(LANES)
    full = lambda a: pl.BlockSpec(a.shape, lambda i: (0, 0))
    row = lambda w: pl.BlockSpec((tm, w), lambda i: (i, 0))
    return pl.pallas_call(
        _mla_prep_kernel,
        out_shape=(jax.ShapeDtypeStruct((m, 1024), BF16), jax.ShapeDtypeStruct((m, 1024), BF16),
                   jax.ShapeDtypeStruct((m, 512), BF16)),
        grid=(m // tm,),
        in_specs=[row(512), row(LANES), full(wq), full(wk), full(wv), full(q_gain), full(k_gain),
                  row(LANES), row(LANES), row(LANES), full(ones)],
        out_specs=(row(1024), row(1024), row(512)),
        compiler_params=_cparams("parallel"),
        name="mla_prep",
    )(lat, pg, wq, wk, wv, q_gain, k_gain, *tab, ones)


def _mla_attn_kernel(shift, q_ref, k_ref, v_ref, o_ref, *scratch):
    i = pl.program_id(1)
    t, tq = ATT_TILE, Q_TILE
    lo = _lane((tq, LANES)) < 64

    def rows(j):
        return pl.ds(pl.multiple_of(j * t, t), t)

    def tile(h):
        return slice(LANES * h, LANES * (h + 1))

    outs = _flash([q_ref[0, :, tile(h)] for h in range(MLA_HEADS)],
                  lambda h, j: k_ref[0, rows(j), tile(h)],
                  lambda h, j: v_ref[0, rows(j), tile(h // 2)], 0, i, t, tq, shift, scratch)
    for p in range(MLA_HEADS // 2):
        o_ref[0, :, tile(p)] = jnp.where(lo, outs[2 * p], outs[2 * p + 1]).astype(BF16)


def _mla_attn(shift, q, k, v):
    b, s, _ = q.shape
    t, tq = ATT_TILE, Q_TILE
    return pl.pallas_call(
        functools.partial(_mla_attn_kernel, shift),
        out_shape=jax.ShapeDtypeStruct((b, s, 512), BF16),
        grid=(b, s // tq),
        in_specs=[pl.BlockSpec((1, tq, 1024), lambda bi, i: (bi, i, 0)),
                  pl.BlockSpec((1, s, 1024), lambda bi, i: (bi, 0, 0)),
                  pl.BlockSpec((1, s, 512), lambda bi, i: (bi, 0, 0))],
        out_specs=pl.BlockSpec((1, tq, 512), lambda bi, i: (bi, i, 0)),
        scratch_shapes=_flash_scratch(MLA_HEADS, tq, t),
        compiler_params=_cparams("parallel", "arbitrary"),
        name="mla_attention",
    )(q, k, v)


def _diff_in_kernel(x_ref, gain_ref, sc_ref, sh_ref, w_ref, g_ref, c_ref, su_ref, sd_ref, q_ref, k_ref, v_ref):
    h = _norm_mod(x_ref[...], gain_ref[...], sc_ref[0], sh_ref[0]).astype(BF16)
    tab = (c_ref[...], su_ref[...], sd_ref[...])
    n = DIFF_HEADS * LANES
    for which, out in ((0, q_ref), (1, k_ref)):
        g = g_ref[which]
        for ch in range(n // MXU_COLS):
            y = _dot(h, w_ref[:, which * n + MXU_COLS * ch:which * n + MXU_COLS * (ch + 1)])
            for e in range(MXU_COLS // LANES):
                out[:, MXU_COLS * ch + LANES * e:MXU_COLS * ch + LANES * (e + 1)] = _rope(
                    _rms64(y[:, LANES * e:LANES * (e + 1)], g), tab, DIFF_ROT // 2).astype(BF16)
    for ch in range(n // MXU_COLS):
        cs = slice(MXU_COLS * ch, MXU_COLS * (ch + 1))
        v_ref[:, cs] = _dot(h, w_ref[:, 2 * n + MXU_COLS * ch:2 * n + MXU_COLS * (ch + 1)]).astype(BF16)


def _diff_in(x2, gain, mod, layer, w, gains, tab, nb):
    m, d = x2.shape
    tm = ROW_TILE
    seq = m // nb
    row = lambda wd: pl.BlockSpec((tm, wd), lambda i: (i, 0))
    full = lambda a: pl.BlockSpec(a.shape, lambda i: (0,) * a.ndim)
    shp = jax.ShapeDtypeStruct((m, DIFF_HEADS * LANES), BF16)
    gain = gain.reshape(1, d)
    return pl.pallas_call(
        _diff_in_kernel,
        out_shape=(shp, shp, shp),
        grid=(m // tm,),
        in_specs=[row(d), full(gain), _mod_spec(layer, 1, nb, seq, tm, d), _mod_spec(layer, 0, nb, seq, tm, d),
                  full(w), full(gains), row(LANES), row(LANES), row(LANES)],
        out_specs=(row(DIFF_HEADS * LANES),) * 3,
        compiler_params=_cparams("parallel"),
        name="diff_in_proj",
    )(x2, gain, mod, mod, w, gains, *tab)


def _diff_attn_kernel(lam_init, shift, q_ref, k_ref, v_ref, lam_ref, sg_ref, x_ref, g_ref, w_ref,
                      o_ref, y_ref, *scratch):
    i = pl.program_id(1)
    t = ATT_TILE
    lo = _lane((t, LANES)) < 64
    lam = lam_ref[...]
    lmb = (jnp.exp(jnp.sum(lam[0:1] * lam[1:2], axis=-1, keepdims=True))
           - jnp.exp(jnp.sum(lam[2:3] * lam[3:4], axis=-1, keepdims=True)) + lam_init)

    def rows(j):
        return pl.ds(pl.multiple_of(j * t, t), t)

    def tile(h):
        return slice(LANES * h, LANES * (h + 1))

    def both_maps(h):
        qt = q_ref[0, :, tile(h)]
        zero = jnp.zeros_like(qt)
        return jnp.concatenate([jnp.where(lo, qt, zero), jnp.where(lo, zero, qt)], axis=0)

    outs = _flash([both_maps(h) for h in range(DIFF_HEADS)],
                  lambda h, j: k_ref[0, rows(j), tile(h)],
                  lambda h, j: v_ref[0, rows(j), tile(h)], 0, i, t, t, shift, scratch)
    for h in range(DIFF_HEADS):
        o = outs[h][:t] - lmb * outs[h][t:]
        o = o * lax.rsqrt(jnp.mean(o * o, axis=-1, keepdims=True) + EPS) * sg_ref[...]
        y_ref[:, tile(h)] = (o * (1.0 - lam_init)).astype(BF16)
    o_ref[0] = x_ref[0] + g_ref[0] * _dot(y_ref[...], w_ref[...])


def _gate_spec(layer, nb, d):
    return pl.BlockSpec((1, 1, d), lambda bi, i: (layer * nb + bi, 0, 2))


def _diff_attn(lam_init, layer, shift, q, k, v, lam, sub_gain, x3, mod, w_out):
    b, s, n = q.shape
    d = x3.shape[-1]
    t = ATT_TILE
    full = pl.BlockSpec((1, s, n), lambda bi, i: (bi, 0, 0))
    return pl.pallas_call(
        functools.partial(_diff_attn_kernel, lam_init, shift),
        out_shape=jax.ShapeDtypeStruct((b, s, d), F32),
        grid=(b, s // t),
        in_specs=[pl.BlockSpec((1, t, n), lambda bi, i: (bi, i, 0)), full, full,
                  pl.BlockSpec(lam.shape, lambda bi, i: (0, 0)),
                  pl.BlockSpec((1, LANES), lambda bi, i: (0, 0)),
                  pl.BlockSpec((1, t, d), lambda bi, i: (bi, i, 0)), _gate_spec(layer, b, d),
                  pl.BlockSpec(w_out.shape, lambda bi, i: (0, 0))],
        out_specs=pl.BlockSpec((1, t, d), lambda bi, i: (bi, i, 0)),
        scratch_shapes=[pltpu.VMEM((t, n), BF16)] + _flash_scratch(DIFF_HEADS, 2 * t, t),
        compiler_params=_cparams("parallel", "arbitrary"),
        name="diff_attention",
    )(q, k, v, lam, sub_gain.reshape(1, LANES), x3, mod, w_out)


def _nsa_compress_kernel(tk_ref, tv_ref, pos_ref, w1a_ref, w1b_ref, w2_ref, kg_ref,
                         c_ref, su_ref, sd_ref, kc_ref, vc_ref):
    n_rows = tk_ref.shape[1]
    for j, (t_ref, out) in enumerate(((tk_ref, kc_ref), (tv_ref, vc_ref))):
        tok = t_ref[0]
        p_hi, p_lo = _split_bf16(pos_ref[j])
        w1a, w1b = w1a_ref[j], w1b_ref[j]
        bias = (_dot(p_hi[0], w1a) + _dot(p_lo[0], w1a) + _dot(p_hi[1], w1b) + _dot(p_lo[1], w1b))[0:1]
        hid = _dot(tok, w1a) + pltpu.roll(_dot(tok, w1b), n_rows - 1, axis=0) + bias
        act = jax.nn.gelu(hid, approximate=True)
        cmp = _dot(act.astype(BF16), w2_ref[j])
        if j == 0:
            cmp = _rope(_rms64(cmp, kg_ref[...]), (c_ref[0], su_ref[0], sd_ref[0]), NSA_ROT // 2)
        out[0] = cmp.astype(BF16)


def _nsa_compress(tk, tv, pos, w1a, w1b, w2, k_gain0, tab):
    b, nr, w = tk.shape
    full = lambda a: pl.BlockSpec(a.shape, lambda bi: (0,) * a.ndim)
    per_b = lambda a: pl.BlockSpec((1,) + a.shape[1:], lambda bi: (bi,) + (0,) * (a.ndim - 1))
    shp = jax.ShapeDtypeStruct((b, nr, LANES), BF16)
    return pl.pallas_call(
        _nsa_compress_kernel,
        out_shape=(shp, shp),
        grid=(b,),
        in_specs=[per_b(tk), per_b(tv), full(pos), full(w1a), full(w1b), full(w2), full(k_gain0),
                  per_b(tab[0]), per_b(tab[1]), per_b(tab[2])],
        out_specs=(pl.BlockSpec((1, nr, LANES), lambda bi: (bi, 0, 0)),) * 2,
        compiler_params=_cparams("parallel"),
        name="nsa_compress",
    )(tk, tv, pos, w1a, w1b, w2, k_gain0, *tab)


def _nsa_attn_kernel(shift, q_ref, kc_ref, vc_ref, ks0_ref, ks1_ref, vs_ref, kw_ref, vw_ref, gate_ref, cov_ref,
                     x_ref, g_ref, ymla_ref, wmla_ref, wnsa_ref, o_ref, y_ref, *scratch):
    i = pl.program_id(1)
    t, tq = ATT_TILE, Q_TILE
    n_blk = ks0_ref.shape[1] // NSA_SLC_LEN
    q0 = i * tq
    lane = _lane((tq, LANES))
    lo = lane < 64
    qpos = q0 + lax.broadcasted_iota(jnp.int32, (tq, LANES), 0)
    cmp_ok = NSA_CMP_STRIDE * lane + (NSA_CMP_LEN - 1) <= qpos
    gates = _sigmoid(gate_ref[0])

    def rows(j):
        return pl.ds(pl.multiple_of(j * t, t), t)

    def gate(branch, h):
        col = branch * NSA_HEADS + h
        return gates[:, col:col + 1]

    out_ref, flash_scratch = scratch[0], scratch[1:]
    kc, vc = kc_ref[0], vc_ref[0]
    qts = [q_ref[0, :, LANES * hg:LANES * (hg + 1)] for hg in range(NSA_HPG)]
    owns = [lo, jnp.logical_not(lo)]
    qs = [jnp.where(own, qt, jnp.zeros_like(qt)) for own in owns for qt in qts]

    win = _flash(qs, lambda h, j: kw_ref[0, rows(j), :], lambda h, j: vw_ref[0, rows(j), :],
                 jnp.maximum((i * tq - NSA_WINDOW) // t, 0), i, t, tq, shift, flash_scratch, window=NSA_WINDOW,
                 masked_loop=True)
    for h in range(NSA_HEADS):
        out_ref[h] = gate(2, h) * win[h]

    q_sel = []
    for g in range(NSA_GROUPS):
        own = owns[g]

        p_sum = jnp.zeros((tq, LANES), F32)
        for hg in range(NSA_HPG):
            h = g * NSA_HPG + hg
            sc = jnp.where(cmp_ok, _dot_nt(qs[h], kc), NEG)
            p = jnp.where(cmp_ok, jnp.exp2(sc - jnp.max(sc, axis=-1, keepdims=True)), 0.0)
            l = jnp.sum(p, axis=-1, keepdims=True)
            p = p * jnp.where(l > 0.0, 1.0 / l, 0.0)
            p_sum = p_sum + p
            out_ref[h] += gate(0, h) * _dot(p.astype(BF16), vc)
        p_hi, p_lo = _split_bf16(p_sum)
        imp = (_dot_nt(cov_ref[...], p_hi) + _dot_nt(cov_ref[...], p_lo))[0:n_blk]
        blk = lax.broadcasted_iota(jnp.int32, (n_blk, tq), 0)
        jt = (q0 + lax.broadcasted_iota(jnp.int32, (n_blk, tq), 1)) >> 6
        allowed = blk <= jt
        forced = allowed & ((blk == 0) | (blk >= jt - 1))
        imp = jnp.where(forced, FORCED, jnp.where(allowed, imp, NEG))
        rank = jnp.zeros((n_blk, tq), jnp.int32)
        for jp in range(n_blk):
            other = imp[jp:jp + 1, :]
            ahead = (other > imp) | ((other == imp) & (blk > jp))
            rank = rank + ahead.astype(jnp.int32)
        pen_t = jnp.where((rank < NSA_TOP_N) & allowed, 0.0, NEG)
        pen_t = jnp.concatenate([pen_t, jnp.zeros((LANES - n_blk, tq), F32)], axis=0)
        pen = jnp.transpose(pen_t)
        if g == 0:
            pen = pltpu.roll(pen, 64, axis=1)
        pen = pen.astype(BF16)
        q_sel += [jnp.where(own, qt, pen) for qt in qts]

    slc = _flash(q_sel, lambda h, j: (ks0_ref if h < NSA_HPG else ks1_ref)[0, rows(j), :],
                 lambda h, j: vs_ref[0, rows(j), :], 0, i, t, tq, shift, flash_scratch, masked_loop=True)
    for hg in range(NSA_HPG):
        h0, h1 = hg, NSA_HPG + hg
        y_ref[:, LANES * hg:LANES * (hg + 1)] = jnp.where(
            lo, out_ref[h0] + gate(1, h0) * slc[h0], out_ref[h1] + gate(1, h1) * slc[h1]).astype(BF16)
    y = _dot(ymla_ref[0], wmla_ref[...]) + _dot(y_ref[...], wnsa_ref[...])
    o_ref[0] = x_ref[0] + g_ref[0] * y


def _nsa_attn(layer, shift, q, kc, vc, kslc0, kslc1, kwin, vsw, pg, cover_t, x3, mod, y_mla, w_mla, w_nsa):
    b, s, _ = q.shape
    d = x3.shape[-1]
    t, tq = ATT_TILE, Q_TILE
    tile3 = lambda w: pl.BlockSpec((1, tq, w), lambda bi, i: (bi, i, 0))
    full2 = lambda a: pl.BlockSpec(a.shape, lambda bi, i: (0, 0))
    seq = pl.BlockSpec((1, s, LANES), lambda bi, i: (bi, 0, 0))
    seq_hi = pl.BlockSpec((1, s, LANES), lambda bi, i: (bi, 0, 1))
    cmp = pl.BlockSpec((1, LANES, LANES), lambda bi, i: (bi, 0, 0))
    return pl.pallas_call(
        functools.partial(_nsa_attn_kernel, shift),
        out_shape=jax.ShapeDtypeStruct((b, s, d), F32),
        grid=(b, s // tq),
        in_specs=[tile3(512), cmp, cmp,
                  seq, seq, seq, seq, seq_hi,
                  pl.BlockSpec((1, tq, LANES), lambda bi, i: (bi, i, 1)),
                  full2(cover_t),
                  tile3(d), _gate_spec(layer, b, d), tile3(512), full2(w_mla), full2(w_nsa)],
        out_specs=tile3(d),
        scratch_shapes=([pltpu.VMEM((tq, 512), BF16), pltpu.VMEM((NSA_HEADS, tq, LANES), F32)]
                        + _flash_scratch(NSA_HEADS, tq, t)),
        compiler_params=_cparams("parallel", "arbitrary"),
        name="nsa_attention",
    )(q, kc, vc, kslc0, kslc1, vsw, kwin, vsw, pg, cover_t, x3, mod, y_mla, w_mla, w_nsa)


def _pad_lanes(a, width):
    return jnp.pad(a, [(0, 0)] * (a.ndim - 1) + [(0, width - a.shape[-1])])


def _hy_in_weight(w):
    d = w.shape[0]
    cq, ckv, kpe, nq, nkv, gate = jnp.split(w, [256, 512, 544, 1056, 1824], axis=1)
    nq = nq.reshape(d, NSA_GROUPS, NSA_HPG, NSA_DH).transpose(0, 2, 1, 3).reshape(d, 512)
    kcmp, vcmp, kslc, vslc, kwin, vwin = jnp.split(nkv, 6, axis=1)
    return jnp.concatenate([cq, ckv, nq, kslc, kwin, _pad_lanes(kpe, LANES), _pad_lanes(gate, LANES),
                            kcmp, vcmp, vslc, vwin], axis=1).astype(BF16)


def _cover_t(seq):
    nc, ns = (seq - NSA_CMP_LEN) // NSA_CMP_STRIDE + 1, seq // NSA_SLC_LEN
    assert nc < LANES and ns <= 32 and seq % Q_TILE == 0
    c_start = np.arange(nc) * NSA_CMP_STRIDE
    c_end = c_start + NSA_CMP_LEN - 1
    j_start = np.arange(ns) * NSA_SLC_LEN
    cover = ((c_start[:, None] <= j_start[None, :] + NSA_SLC_LEN - 1) & (c_end[:, None] >= j_start[None, :]))
    out = np.zeros((LANES, LANES), np.float32)
    out[:ns, :nc] = cover.T
    return jnp.asarray(out, BF16)


def _compress_weights(cmp_pos, w1, w2):
    half = NSA_CMP_STRIDE
    w1 = w1.reshape(2, 2, half, NSA_DH, NSA_CMP_HIDDEN)
    eye = jnp.eye(NSA_GROUPS, dtype=F32)
    w1e = jnp.einsum('jcldn,gh->jclgdhn', w1, eye).reshape(2, 2, half * LANES, NSA_GROUPS * NSA_CMP_HIDDEN)
    w2e = jnp.einsum('jnd,gh->jgnhd', w2, eye).reshape(2, NSA_GROUPS * NSA_CMP_HIDDEN, LANES)
    pos = cmp_pos.reshape(2, 2, half, 1, NSA_DH)
    pos = jnp.broadcast_to(pos, (2, 2, half, NSA_GROUPS, NSA_DH)).reshape(2, 2, 1, half * LANES)
    pos = jnp.broadcast_to(pos, (2, 2, 8, half * LANES))
    return pos, w1e[:, 0].astype(BF16), w1e[:, 1].astype(BF16), w2e.astype(BF16)


def kernel(x, c, positions, ada_w, ada_b, norm_mix, norm_mlp, mlp_w1, mlp_w2, hy_w_in, hy_w_out, mla_q_norm, mla_w_uq, mla_kv_norm, mla_w_ukv, mla_q_gain, mla_k_gain, nsa_q_gain, nsa_k_gain, nsa_cmp_pos, nsa_cmp_w1, nsa_cmp_w2, diff_w_qkv, diff_w_out, diff_q_gain, diff_k_gain, diff_lambda, diff_sub_gain):
    nb, seq, d = x.shape
    depth = ada_w.shape[0]
    m = nb * seq
    n_cmp = seq // NSA_CMP_STRIDE

    mod = _adaln(c, ada_w, ada_b).reshape(depth * nb, 1, 6 * d)
    tab_mla, tab_64 = _rope_tables(positions.reshape(m, 1))
    pos_c = jnp.pad(positions[:, NSA_CMP_LEN - 1::NSA_CMP_STRIDE], ((0, 0), (0, 1)))
    _, tab_cmp = _rope_tables(pos_c.reshape(nb * n_cmp, 1))
    tab_cmp = tuple(a.reshape(nb, n_cmp, LANES) for a in tab_cmp)
    cover_t = _cover_t(seq)
    seq3 = lambda a: a.reshape(nb, seq, -1)

    x2 = x.reshape(m, d)
    for i in range(depth):
        j = i // 2
        if i % 2 == 0:
            qg_s = nsa_q_gain[j] * (NSA_DH ** -0.5 * LOG2E)
            qg = jnp.tile(qg_s, 2).reshape(1, LANES)
            kg = jnp.tile(nsa_k_gain[j], (1, 2)).reshape(3, 1, LANES)
            lat, q_nsa, kslc0, kslc1, kwin, pg, kcmp, vcmp, vsw = _hy_in(
                x2, norm_mix[i], mod, i, _hy_in_weight(hy_w_in[j]),
                mla_q_norm[j].reshape(1, -1), mla_kv_norm[j].reshape(1, -1), qg, kg, tab_64, nb)

            wq = _pad_lanes(mla_w_uq[j].reshape(MLA_Q_RANK, MLA_HEADS, MLA_DK), LANES).reshape(MLA_Q_RANK, -1)
            wkv = mla_w_ukv[j].reshape(MLA_KV_RANK, MLA_HEADS, MLA_NOPE + MLA_V)
            wk = jnp.pad(wkv[..., :MLA_NOPE], ((0, 0), (0, 0), (MLA_ROPE, LANES - MLA_DK))).reshape(MLA_KV_RANK, -1)
            wv = wkv[..., MLA_NOPE:].reshape(MLA_KV_RANK, -1)
            q_mla, k_mla, v_mla = _mla_prep(
                lat, pg, wq.astype(BF16), wk.astype(BF16), wv.astype(BF16),
                _pad_lanes(mla_q_gain[j] * LOG2E, LANES).reshape(1, LANES),
                _pad_lanes(mla_k_gain[j] * MLA_DK ** 0.5, LANES).reshape(1, LANES),
                tab_mla)
            y_mla = _shift_dispatch(
                _needs_shift(MLA_DK, mla_q_gain[j] * (MLA_DK ** -0.5 * LOG2E), mla_k_gain[j]), _mla_attn,
                seq3(q_mla), seq3(k_mla), seq3(v_mla))

            pos_e, w1a, w1b, w2e = _compress_weights(nsa_cmp_pos[j], nsa_cmp_w1[j], nsa_cmp_w2[j])
            chunks = lambda a: a.reshape(nb, n_cmp, NSA_CMP_STRIDE * LANES)
            kc, vc = _nsa_compress(chunks(kcmp), chunks(vcmp), pos_e, w1a, w1b, w2e, kg[0], tab_cmp)
            w_out = hy_w_out[j]
            w_nsa = w_out[512:].reshape(NSA_GROUPS, NSA_HPG, NSA_DH, d).transpose(1, 0, 2, 3).reshape(512, d)
            x2 = _shift_dispatch(
                _needs_shift(NSA_DH, qg_s, nsa_k_gain[j][1:]), functools.partial(_nsa_attn, i),
                seq3(q_nsa), kc, vc, seq3(kslc0), seq3(kslc1), seq3(kwin), seq3(vsw), seq3(pg), cover_t,
                seq3(x2), mod, y_mla, w_out[:512].astype(BF16), w_nsa.astype(BF16)).reshape(m, d)
        else:
            lam_init = 0.8 - 0.6 * math.exp(-0.3 * i)
            gains = jnp.stack([jnp.tile(diff_q_gain[j] * (DIFF_DH ** -0.5 * LOG2E), 2), jnp.tile(diff_k_gain[j], 2)])
            q, k, v = _diff_in(x2, norm_mix[i], mod, i, diff_w_qkv[j].astype(BF16),
                               gains.reshape(2, 1, LANES), tab_64, nb)
            x2 = _shift_dispatch(
                _needs_shift(DIFF_DH, gains[0], gains[1]), functools.partial(_diff_attn, lam_init, i),
                seq3(q), seq3(k), seq3(v), diff_lambda[j], diff_sub_gain[j],
                seq3(x2), mod, diff_w_out[j].astype(BF16)).reshape(m, d)
        x2 = _mlp(x2, norm_mlp[i], mod, i, mlp_w1[i], mlp_w2[i], nb)
    return x2.reshape(nb, seq, d)
```

```python
import functools
import math

import numpy as np
import jax
import jax.numpy as jnp
from jax import lax
from jax.experimental import pallas as pl
from jax.experimental.pallas import tpu as pltpu

F32 = jnp.float32
BF16 = jnp.bfloat16

LANES = 128
VMEM_LIMIT = 52 * 1024 * 1024

ROPE_THETA = 500000.0
EPS = 1e-6
NEG = -1e30
FORCED = 1e9

MLA_HEADS, MLA_NOPE, MLA_ROPE, MLA_V = 8, 64, 32, 64
MLA_Q_RANK, MLA_KV_RANK = 256, 256
MLA_DK = MLA_ROPE + MLA_NOPE
NSA_HEADS, NSA_GROUPS, NSA_DH = 8, 2, 64
NSA_HPG = NSA_HEADS // NSA_GROUPS
NSA_ROT = NSA_DH // 4
NSA_CMP_LEN, NSA_CMP_STRIDE, NSA_CMP_HIDDEN = 32, 16, 128
NSA_SLC_LEN, NSA_TOP_N, NSA_WINDOW = 64, 16, 512
DIFF_HEADS, DIFF_DH = 8, 64
DIFF_ROT = DIFF_DH // 4

ROW_TILE = 512
ATT_TILE = 256
Q_TILE = 256
MXU_COLS = 256


def _cparams(*sem):
    return pltpu.CompilerParams(dimension_semantics=sem, vmem_limit_bytes=VMEM_LIMIT)


def _split_bf16(x):
    hi = x.astype(BF16)
    lo = (x - hi.astype(F32)).astype(BF16)
    return hi, lo


def _dot(a, b):
    return jnp.dot(a, b, preferred_element_type=F32)


def _dot_nt(a, b):
    return lax.dot_general(a, b, (((1,), (1,)), ((), ())), preferred_element_type=F32)


def _sigmoid(x):
    return 1.0 / (1.0 + jnp.exp(-x))


def _lane(shape):
    return lax.broadcasted_iota(jnp.int32, shape, 1)


def _adaln_kernel(c_ref, w_ref, b_ref, o_ref):
    c = c_ref[...]
    cond = c * _sigmoid(c)
    c_hi, c_lo = _split_bf16(cond)
    w_hi, w_lo = _split_bf16(w_ref[0])
    o_ref[0] = _dot(c_hi, w_hi) + _dot(c_hi, w_lo) + _dot(c_lo, w_hi) + b_ref[0]


def _adaln(c, ada_w, ada_b):
    depth, d, n = ada_w.shape
    b = c.shape[0]
    tn = 1536
    return pl.pallas_call(
        _adaln_kernel,
        out_shape=jax.ShapeDtypeStruct((depth, b, n), F32),
        grid=(depth, n // tn),
        in_specs=[pl.BlockSpec((b, d), lambda i, j: (0, 0)),
                  pl.BlockSpec((1, d, tn), lambda i, j: (i, 0, j)),
                  pl.BlockSpec((1, 1, tn), lambda i, j: (i, 0, j))],
        out_specs=pl.BlockSpec((1, b, tn), lambda i, j: (i, 0, j)),
        compiler_params=_cparams("parallel", "parallel"),
        name="adaln",
    )(c, ada_w, ada_b.reshape(depth, 1, n))


def _rope_table_kernel(pos_ref, fa_ref, ma_ref, fb_ref, mb_ref, ca_ref, ua_ref, da_ref, cb_ref, ub_ref, db_ref):
    pos = pos_ref[...].astype(F32)
    for f_ref, m_ref, c_ref, u_ref, d_ref in ((fa_ref, ma_ref, ca_ref, ua_ref, da_ref),
                                              (fb_ref, mb_ref, cb_ref, ub_ref, db_ref)):
        ang = pos * f_ref[...]
        sin = jnp.sin(ang)
        c_ref[...] = jnp.cos(ang)
        u_ref[...] = sin * m_ref[0:1]
        d_ref[...] = sin * m_ref[1:2]


def _lane_freqs(rot, seg):
    half = rot // 2
    lane = np.arange(LANES)
    inv = ROPE_THETA ** (-(np.arange(half, dtype=np.float32) / np.float32(half)))
    f = np.where(lane % seg < rot, inv.astype(np.float32)[lane % half], 0.0)
    return jnp.asarray(f.reshape(1, LANES), F32)


def _rope_masks(rot, seg):
    half = rot // 2
    lane = np.arange(LANES) % seg
    up = ((lane >= half) & (lane < rot)).astype(np.float32)
    dn = -(lane < half).astype(np.float32)
    return jnp.asarray(np.stack([up, dn]))


def _rope_tables(pos_col):
    rows = pos_col.shape[0]
    tr = min(rows, 2048)
    spec = pl.BlockSpec((tr, LANES), lambda i: (i, 0))
    vec = pl.BlockSpec((1, LANES), lambda i: (0, 0))
    msk = pl.BlockSpec((2, LANES), lambda i: (0, 0))
    shp = jax.ShapeDtypeStruct((rows, LANES), F32)
    out = pl.pallas_call(
        _rope_table_kernel,
        out_shape=(shp,) * 6,
        grid=(rows // tr,),
        in_specs=[pl.BlockSpec((tr, 1), lambda i: (i, 0)), vec, msk, vec, msk],
        out_specs=(spec,) * 6,
        compiler_params=_cparams("parallel"),
        name="rope_tables",
    )(pos_col, _lane_freqs(MLA_ROPE, LANES), _rope_masks(MLA_ROPE, LANES),
      _lane_freqs(NSA_ROT, NSA_DH), _rope_masks(NSA_ROT, NSA_DH))
    return out[:3], out[3:]


def _rope(y, tab, half):
    c, s_up, s_dn = tab
    up = pltpu.roll(y, half, axis=1)
    dn = pltpu.roll(y, LANES - half, axis=1)
    return y * c + up * s_up + dn * s_dn


def _rms64(x, gain):
    lo = _lane(x.shape) < 64
    x2 = x * x
    s_lo = jnp.sum(jnp.where(lo, x2, 0.0), axis=-1, keepdims=True)
    s_hi = jnp.sum(jnp.where(lo, 0.0, x2), axis=-1, keepdims=True)
    r = jnp.where(lo, lax.rsqrt(s_lo * (1.0 / 64) + EPS), lax.rsqrt(s_hi * (1.0 / 64) + EPS))
    return x * (r * gain)


def _seg_ones(seg):
    lane = np.arange(LANES) // seg
    return jnp.asarray(lane[:, None] == lane[None, :], BF16)


def _rms_seg(x, gain_root, ones, n):
    ss = _dot((x * x).astype(BF16), ones)
    return x * (lax.rsqrt(ss + n * EPS) * gain_root)


def _rms(x, gain):
    return x * lax.rsqrt(jnp.mean(x * x, axis=-1, keepdims=True) + EPS) * gain


def _norm_mod(x, gain, sc, sh):
    return _rms(x, gain) * (1.0 + sc) + sh


def _mod_spec(layer, k, nb, rows_per_batch, tm, d):
    per = rows_per_batch // tm
    return pl.BlockSpec((1, 1, d), lambda i, *_: (layer * nb + i // per, 0, k))


def _mlp_kernel(x_ref, gain_ref, sc_ref, sh_ref, g_ref, w1_ref, w2_ref, o_ref, h_scr, acc_scr):
    j = pl.program_id(1)

    @pl.when(j == 0)
    def _():
        h_scr[...] = _norm_mod(x_ref[...], gain_ref[...], sc_ref[0], sh_ref[0]).astype(BF16)
        acc_scr[...] = jnp.zeros_like(acc_scr)

    a = jnp.maximum(_dot(h_scr[...], w1_ref[0].astype(BF16)), 0.0)
    acc_scr[...] += _dot((a * a).astype(BF16), w2_ref[0].astype(BF16))

    @pl.when(j == pl.num_programs(1) - 1)
    def _():
        o_ref[...] = x_ref[...] + g_ref[0] * acc_scr[...]


def _mlp(x2, gain, mod, layer, w1, w2, nb):
    m, d = x2.shape
    ff = w1.shape[2]
    tm, tf = 1024, 1024
    s = m // nb
    return pl.pallas_call(
        _mlp_kernel,
        out_shape=jax.ShapeDtypeStruct((m, d), F32),
        grid=(m // tm, ff // tf),
        in_specs=[pl.BlockSpec((tm, d), lambda i, j: (i, 0)),
                  pl.BlockSpec((1, d), lambda i, j: (0, 0)),
                  _mod_spec(layer, 4, nb, s, tm, d),
                  _mod_spec(layer, 3, nb, s, tm, d),
                  _mod_spec(layer, 5, nb, s, tm, d),
                  pl.BlockSpec((1, d, tf), lambda i, j: (layer, 0, j)),
                  pl.BlockSpec((1, tf, d), lambda i, j: (layer, j, 0))],
        out_specs=pl.BlockSpec((tm, d), lambda i, j: (i, 0)),
        scratch_shapes=[pltpu.VMEM((tm, d), BF16), pltpu.VMEM((tm, d), F32)],
        compiler_params=_cparams("parallel", "arbitrary"),
        name="relu2_mlp",
    )(x2, gain.reshape(1, d), mod, mod, mod, w1, w2)


LOG2E = math.log2(math.e)


SCORE_BOUND = 60.0
BOUND_MARGIN = 1.02


def _needs_shift(dk, q_gain, k_gain):
    bound = dk * jnp.max(jnp.abs(q_gain)) * jnp.max(jnp.abs(k_gain)) * BOUND_MARGIN
    return jnp.logical_not(bound <= SCORE_BOUND)


def _shift_dispatch(needs_shift, attn, *args):
    return lax.cond(needs_shift, functools.partial(attn, True), functools.partial(attn, False), *args)


def _lane_tiles(x):
    return [x[:, LANES * c:LANES * (c + 1)] for c in range(x.shape[1] // LANES)]


def _flash_scratch(n, rows, t):
    return [pltpu.VMEM((n, rows, LANES), F32), pltpu.VMEM((n, rows, LANES), F32)]


def _flash(qs, k_at, v_at, first, i, t, rows_per_map, shift, scratch, window=None, masked_loop=False):
    n = len(qs)
    rows = qs[0].shape[0]
    n_diag = rows_per_map // t
    last = (i + 1) * n_diag - 1
    r = lax.broadcasted_iota(jnp.int32, (rows, t), 0)
    if rows != rows_per_map:
        r = jnp.where(r >= rows_per_map, r - rows_per_map, r)
    c_minus_r = lax.broadcasted_iota(jnp.int32, (rows, t), 1) - r

    assert masked_loop or window is None

    def visible(j):
        ahead = i * rows_per_map - j * t
        ok = c_minus_r <= ahead
        return ok if window is None else ok & (c_minus_r > ahead - window)

    def scores(h, j, ok):
        s = _dot_nt(qs[h], k_at(h, j))
        return s if ok is None else jnp.where(ok, s, NEG)

    def sweep(step, carry):
        if masked_loop:
            return lax.fori_loop(first, last + 1, lambda j, cr: step(j, cr, visible(j)), carry)
        carry = lax.fori_loop(first, i * n_diag, lambda j, cr: step(j, cr, None), carry)
        for d in range(n_diag):
            carry = step(i * n_diag + d, carry, visible(i * n_diag + d))
        return carry

    def max_step(j, ms, ok):
        out = []
        for h in range(n):
            m = ms[h]
            for s_c in _lane_tiles(scores(h, j, ok)):
                m = jnp.maximum(m, s_c)
            out.append(m)
        return tuple(out)

    if shift:
        ms = sweep(max_step, tuple(jnp.full((rows, LANES), NEG, F32) for _ in range(n)))
        ms = [jnp.broadcast_to(jnp.max(m, axis=-1, keepdims=True), (rows, LANES)) for m in ms]

    l_ref, acc_ref = scratch
    for h in range(n):
        l_ref[h] = jnp.zeros((rows, LANES), F32)
        acc_ref[h] = jnp.zeros((rows, LANES), F32)

    def sum_step(j, p_prev, ok):
        p_new = []
        for h in range(n):
            acc_ref[h] += _dot(p_prev[h], v_at(h, jnp.maximum(j - 1, 0)))
            ps = _lane_tiles(scores(h, j, ok))
            ps = [jnp.exp2(s_c - ms[h]) if shift else jnp.exp2(s_c) for s_c in ps]
            l = l_ref[h]
            for p_c in ps:
                l = l + p_c
            l_ref[h] = l
            p_new.append(jnp.concatenate(ps, axis=1).astype(BF16))
        return tuple(p_new)

    p = sweep(sum_step, tuple(jnp.zeros((rows, t), BF16) for _ in range(n)))
    return [(acc_ref[h] + _dot(p[h], v_at(h, last))) / jnp.sum(l_ref[h], axis=-1, keepdims=True)
            for h in range(n)]


def _hy_in_kernel(seq, x_ref, gain_ref, sc_ref, sh_ref, w_ref, qn_ref, kvn_ref, qg_ref, kg_ref,
                  c_ref, su_ref, sd_ref,
                  lat_ref, q_ref, ks0_ref, ks1_ref, kw_ref, pg_ref, kcmp_ref, vcmp_ref, vsw_ref):
    tm = x_ref.shape[0]
    h = _norm_mod(x_ref[...], gain_ref[...], sc_ref[0], sh_ref[0]).astype(BF16)
    tab = (c_ref[...], su_ref[...], sd_ref[...])

    def cols(a, b):
        return _dot(h, w_ref[:, a:b])

    def prep(x, g):
        return _rope(_rms64(x, g), tab, NSA_ROT // 2)

    lat_ref[:, 0:256] = _rms(cols(0, 256), qn_ref[...]).astype(BF16)
    lat_ref[:, 256:512] = _rms(cols(256, 512), kvn_ref[...]).astype(BF16)
    for ch in range(2):
        y = cols(512 + MXU_COLS * ch, 512 + MXU_COLS * (ch + 1))
        for e in range(2):
            q_ref[:, MXU_COLS * ch + LANES * e:MXU_COLS * ch + LANES * (e + 1)] = prep(
                y[:, LANES * e:LANES * (e + 1)], qg_ref[...]).astype(BF16)
    y = cols(1024, 1280)
    kslc = prep(y[:, 0:LANES], kg_ref[1])
    lane = _lane((tm, LANES))
    t_idx = (pl.program_id(0) * tm + lax.broadcasted_iota(jnp.int32, (tm, LANES), 0)) & (seq - 1)
    blk = t_idx >> 6
    ks0_ref[...] = jnp.where(lane < 64, kslc, jnp.where(lane - 64 == blk, 1.0, 0.0)).astype(BF16)
    ks1_ref[...] = jnp.where(lane >= 64, kslc, jnp.where(lane == blk, 1.0, 0.0)).astype(BF16)
    kw_ref[...] = prep(y[:, LANES:2 * LANES], kg_ref[2]).astype(BF16)
    pg_ref[...] = cols(1280, 1536)
    y = cols(1536, 1792)
    kcmp_ref[...] = y[:, 0:LANES].astype(BF16)
    vcmp_ref[...] = y[:, LANES:2 * LANES].astype(BF16)
    vsw_ref[...] = cols(1792, 2048).astype(BF16)


def _hy_in(x2, gain, mod, layer, w, q_norm, kv_norm, q_gain, k_gain, tab, nb):
    m, d = x2.shape
    tm = ROW_TILE
    seq = m // nb
    assert seq & (seq - 1) == 0 and seq // NSA_SLC_LEN <= 32 and seq % tm == 0
    row = lambda wd: pl.BlockSpec((tm, wd), lambda i: (i, 0))
    full = lambda a: pl.BlockSpec(a.shape, lambda i: (0,) * a.ndim)
    bf = lambda wd: jax.ShapeDtypeStruct((m, wd), BF16)
    widths = (512, 512, LANES, LANES, LANES, 2 * LANES, LANES, LANES, 2 * LANES)
    shapes = tuple(jax.ShapeDtypeStruct((m, wd), F32) if k == 5 else bf(wd) for k, wd in enumerate(widths))
    gain = gain.reshape(1, d)
    return pl.pallas_call(
        functools.partial(_hy_in_kernel, seq),
        out_shape=shapes,
        grid=(m // tm,),
        in_specs=[row(d), full(gain), _mod_spec(layer, 1, nb, seq, tm, d), _mod_spec(layer, 0, nb, seq, tm, d),
                  full(w), full(q_norm), full(kv_norm), full(q_gain), full(k_gain),
                  row(LANES), row(LANES), row(LANES)],
        out_specs=tuple(row(wd) for wd in widths),
        compiler_params=_cparams("parallel"),
        name="hy_in_proj",
    )(x2, gain, mod, mod, w, q_norm, kv_norm, q_gain, k_gain, *tab)


def _mla_prep_kernel(lat_ref, kpe_ref, wq_ref, wk_ref, wv_ref, qg_ref, kg_ref, c_ref, su_ref, sd_ref, ones_ref,
                     q_ref, k_ref, v_ref):
    cq, ckv = lat_ref[:, 0:256], lat_ref[:, 256:512]
    kpe = kpe_ref[...]
    tab = (c_ref[...], su_ref[...], sd_ref[...])
    v_ref[...] = _dot(ckv, wv_ref[...]).astype(BF16)

    def head(x, g):
        return _rope(_rms_seg(x, g, ones_ref[...], MLA_DK), tab, MLA_ROPE // 2)

    for ch in range(MLA_HEADS // 2):
        cs = slice(MXU_COLS * ch, MXU_COLS * (ch + 1))
        q2, k2 = _dot(cq, wq_ref[:, cs]), _dot(ckv, wk_ref[:, cs])
        for e in range(2):
            sl = slice(MXU_COLS * ch + LANES * e, MXU_COLS * ch + LANES * (e + 1))
            es = slice(LANES * e, LANES * (e + 1))
            q_ref[:, sl] = head(q2[:, es], qg_ref[...]).astype(BF16)
            k_ref[:, sl] = head(k2[:, es] + kpe, kg_ref[...]).astype(BF16)


def _mla_prep(lat, pg, wq, wk, wv, q_gain, k_gain, tab):
    m = lat.shape[0]
    tm = ROW_TILE
    ones = _seg_ones(LANES)
    full = lambda a: pl.BlockSpec(a.shape, lambda i: (0, 0))
    row = lambda w: pl.BlockSpec((tm, w), lambda i: (i, 0))
    return pl.pallas_call(
        _mla_prep_kernel,
        out_shape=(jax.ShapeDtypeStruct((m, 1024), BF16), jax.ShapeDtypeStruct((m, 1024), BF16),
                   jax.ShapeDtypeStruct((m, 512), BF16)),
        grid=(m // tm,),
        in_specs=[row(512), row(LANES), full(wq), full(wk), full(wv), full(q_gain), full(k_gain),
                  row(LANES), row(LANES), row(LANES), full(ones)],
        out_specs=(row(1024), row(1024), row(512)),
        compiler_params=_cparams("parallel"),
        name="mla_prep",
    )(lat, pg, wq, wk, wv, q_gain, k_gain, *tab, ones)


def _mla_attn_kernel(shift, q_ref, k_ref, v_ref, o_ref, *scratch):
    i = pl.program_id(1)
    t, tq = ATT_TILE, Q_TILE
    lo = _lane((tq, LANES)) < 64

    def rows(j):
        return pl.ds(pl.multiple_of(j * t, t), t)

    def tile(h):
        return slice(LANES * h, LANES * (h + 1))

    outs = _flash([q_ref[0, :, tile(h)] for h in range(MLA_HEADS)],
                  lambda h, j: k_ref[0, rows(j), tile(h)],
                  lambda h, j: v_ref[0, rows(j), tile(h // 2)], 0, i, t, tq, shift, scratch)
    for p in range(MLA_HEADS // 2):
        o_ref[0, :, tile(p)] = jnp.where(lo, outs[2 * p], outs[2 * p + 1]).astype(BF16)


def _mla_attn(shift, q, k, v):
    b, s, _ = q.shape
    t, tq = ATT_TILE, Q_TILE
    return pl.pallas_call(
        functools.partial(_mla_attn_kernel, shift),
        out_shape=jax.ShapeDtypeStruct((b, s, 512), BF16),
        grid=(b, s // tq),
        in_specs=[pl.BlockSpec((1, tq, 1024), lambda bi, i: (bi, i, 0)),
                  pl.BlockSpec((1, s, 1024), lambda bi, i: (bi, 0, 0)),
                  pl.BlockSpec((1, s, 512), lambda bi, i: (bi, 0, 0))],
        out_specs=pl.BlockSpec((1, tq, 512), lambda bi, i: (bi, i, 0)),
        scratch_shapes=_flash_scratch(MLA_HEADS, tq, t),
        compiler_params=_cparams("parallel", "arbitrary"),
        name="mla_attention",
    )(q, k, v)


def _diff_in_kernel(x_ref, gain_ref, sc_ref, sh_ref, w_ref, g_ref, c_ref, su_ref, sd_ref, q_ref, k_ref, v_ref):
    h = _norm_mod(x_ref[...], gain_ref[...], sc_ref[0], sh_ref[0]).astype(BF16)
    tab = (c_ref[...], su_ref[...], sd_ref[...])
    n = DIFF_HEADS * LANES
    for which, out in ((0, q_ref), (1, k_ref)):
        g = g_ref[which]
        for ch in range(n // MXU_COLS):
            y = _dot(h, w_ref[:, which * n + MXU_COLS * ch:which * n + MXU_COLS * (ch + 1)])
            for e in range(MXU_COLS // LANES):
                out[:, MXU_COLS * ch + LANES * e:MXU_COLS * ch + LANES * (e + 1)] = _rope(
                    _rms64(y[:, LANES * e:LANES * (e + 1)], g), tab, DIFF_ROT // 2).astype(BF16)
    for ch in range(n // MXU_COLS):
        cs = slice(MXU_COLS * ch, MXU_COLS * (ch + 1))
        v_ref[:, cs] = _dot(h, w_ref[:, 2 * n + MXU_COLS * ch:2 * n + MXU_COLS * (ch + 1)]).astype(BF16)


def _diff_in(x2, gain, mod, layer, w, gains, tab, nb):
    m, d = x2.shape
    tm = ROW_TILE
    seq = m // nb
    row = lambda wd: pl.BlockSpec((tm, wd), lambda i: (i, 0))
    full = lambda a: pl.BlockSpec(a.shape, lambda i: (0,) * a.ndim)
    shp = jax.ShapeDtypeStruct((m, DIFF_HEADS * LANES), BF16)
    gain = gain.reshape(1, d)
    return pl.pallas_call(
        _diff_in_kernel,
        out_shape=(shp, shp, shp),
        grid=(m // tm,),
        in_specs=[row(d), full(gain), _mod_spec(layer, 1, nb, seq, tm, d), _mod_spec(layer, 0, nb, seq, tm, d),
                  full(w), full(gains), row(LANES), row(LANES), row(LANES)],
        out_specs=(row(DIFF_HEADS * LANES),) * 3,
        compiler_params=_cparams("parallel"),
        name="diff_in_proj",
    )(x2, gain, mod, mod, w, gains, *tab)


def _diff_attn_kernel(lam_init, shift, q_ref, k_ref, v_ref, lam_ref, sg_ref, x_ref, g_ref, w_ref,
                      o_ref, y_ref, *scratch):
    i = pl.program_id(1)
    t = ATT_TILE
    lo = _lane((t, LANES)) < 64
    lam = lam_ref[...]
    lmb = (jnp.exp(jnp.sum(lam[0:1] * lam[1:2], axis=-1, keepdims=True))
           - jnp.exp(jnp.sum(lam[2:3] * lam[3:4], axis=-1, keepdims=True)) + lam_init)

    def rows(j):
        return pl.ds(pl.multiple_of(j * t, t), t)

    def tile(h):
        return slice(LANES * h, LANES * (h + 1))

    def both_maps(h):
        qt = q_ref[0, :, tile(h)]
        zero = jnp.zeros_like(qt)
        return jnp.concatenate([jnp.where(lo, qt, zero), jnp.where(lo, zero, qt)], axis=0)

    outs = _flash([both_maps(h) for h in range(DIFF_HEADS)],
                  lambda h, j: k_ref[0, rows(j), tile(h)],
                  lambda h, j: v_ref[0, rows(j), tile(h)], 0, i, t, t, shift, scratch)
    for h in range(DIFF_HEADS):
        o = outs[h][:t] - lmb * outs[h][t:]
        o = o * lax.rsqrt(jnp.mean(o * o, axis=-1, keepdims=True) + EPS) * sg_ref[...]
        y_ref[:, tile(h)] = (o * (1.0 - lam_init)).astype(BF16)
    o_ref[0] = x_ref[0] + g_ref[0] * _dot(y_ref[...], w_ref[...])


def _gate_spec(layer, nb, d):
    return pl.BlockSpec((1, 1, d), lambda bi, i: (layer * nb + bi, 0, 2))


def _diff_attn(lam_init, layer, shift, q, k, v, lam, sub_gain, x3, mod, w_out):
    b, s, n = q.shape
    d = x3.shape[-1]
    t = ATT_TILE
    full = pl.BlockSpec((1, s, n), lambda bi, i: (bi, 0, 0))
    return pl.pallas_call(
        functools.partial(_diff_attn_kernel, lam_init, shift),
        out_shape=jax.ShapeDtypeStruct((b, s, d), F32),
        grid=(b, s // t),
        in_specs=[pl.BlockSpec((1, t, n), lambda bi, i: (bi, i, 0)), full, full,
                  pl.BlockSpec(lam.shape, lambda bi, i: (0, 0)),
                  pl.BlockSpec((1, LANES), lambda bi, i: (0, 0)),
                  pl.BlockSpec((1, t, d), lambda bi, i: (bi, i, 0)), _gate_spec(layer, b, d),
                  pl.BlockSpec(w_out.shape, lambda bi, i: (0, 0))],
        out_specs=pl.BlockSpec((1, t, d), lambda bi, i: (bi, i, 0)),
        scratch_shapes=[pltpu.VMEM((t, n), BF16)] + _flash_scratch(DIFF_HEADS, 2 * t, t),
        compiler_params=_cparams("parallel", "arbitrary"),
        name="diff_attention",
    )(q, k, v, lam, sub_gain.reshape(1, LANES), x3, mod, w_out)


def _nsa_compress_kernel(tk_ref, tv_ref, pos_ref, w1a_ref, w1b_ref, w2_ref, kg_ref,
                         c_ref, su_ref, sd_ref, kc_ref, vc_ref):
    n_rows = tk_ref.shape[1]
    for j, (t_ref, out) in enumerate(((tk_ref, kc_ref), (tv_ref, vc_ref))):
        tok = t_ref[0]
        p_hi, p_lo = _split_bf16(pos_ref[j])
        w1a, w1b = w1a_ref[j], w1b_ref[j]
        bias = (_dot(p_hi[0], w1a) + _dot(p_lo[0], w1a) + _dot(p_hi[1], w1b) + _dot(p_lo[1], w1b))[0:1]
        hid = _dot(tok, w1a) + pltpu.roll(_dot(tok, w1b), n_rows - 1, axis=0) + bias
        act = jax.nn.gelu(hid, approximate=True)
        cmp = _dot(act.astype(BF16), w2_ref[j])
        if j == 0:
            cmp = _rope(_rms64(cmp, kg_ref[...]), (c_ref[0], su_ref[0], sd_ref[0]), NSA_ROT // 2)
        out[0] = cmp.astype(BF16)


def _nsa_compress(tk, tv, pos, w1a, w1b, w2, k_gain0, tab):
    b, nr, w = tk.shape
    full = lambda a: pl.BlockSpec(a.shape, lambda bi: (0,) * a.ndim)
    per_b = lambda a: pl.BlockSpec((1,) + a.shape[1:], lambda bi: (bi,) + (0,) * (a.ndim - 1))
    shp = jax.ShapeDtypeStruct((b, nr, LANES), BF16)
    return pl.pallas_call(
        _nsa_compress_kernel,
        out_shape=(shp, shp),
        grid=(b,),
        in_specs=[per_b(tk), per_b(tv), full(pos), full(w1a), full(w1b), full(w2), full(k_gain0),
                  per_b(tab[0]), per_b(tab[1]), per_b(tab[2])],
        out_specs=(pl.BlockSpec((1, nr, LANES), lambda bi: (bi, 0, 0)),) * 2,
        compiler_params=_cparams("parallel"),
        name="nsa_compress",
    )(tk, tv, pos, w1a, w1b, w2, k_gain0, *tab)


def _nsa_attn_kernel(shift, q_ref, kc_ref, vc_ref, ks0_ref, ks1_ref, vs_ref, kw_ref, vw_ref, gate_ref, cov_ref,
                     x_ref, g_ref, ymla_ref, wmla_ref, wnsa_ref, o_ref, y_ref, *scratch):
    i = pl.program_id(1)
    t, tq = ATT_TILE, Q_TILE
    n_blk = ks0_ref.shape[1] // NSA_SLC_LEN
    q0 = i * tq
    lane = _lane((tq, LANES))
    lo = lane < 64
    qpos = q0 + lax.broadcasted_iota(jnp.int32, (tq, LANES), 0)
    cmp_ok = NSA_CMP_STRIDE * lane + (NSA_CMP_LEN - 1) <= qpos
    gates = _sigmoid(gate_ref[0])

    def rows(j):
        return pl.ds(pl.multiple_of(j * t, t), t)

    def gate(branch, h):
        col = branch * NSA_HEADS + h
        return gates[:, col:col + 1]

    out_ref, flash_scratch = scratch[0], scratch[1:]
    kc, vc = kc_ref[0], vc_ref[0]
    qts = [q_ref[0, :, LANES * hg:LANES * (hg + 1)] for hg in range(NSA_HPG)]
    owns = [lo, jnp.logical_not(lo)]
    qs = [jnp.where(own, qt, jnp.zeros_like(qt)) for own in owns for qt in qts]

    win = _flash(qs, lambda h, j: kw_ref[0, rows(j), :], lambda h, j: vw_ref[0, rows(j), :],
                 jnp.maximum((i * tq - NSA_WINDOW) // t, 0), i, t, tq, shift, flash_scratch, window=NSA_WINDOW,
                 masked_loop=True)
    for h in range(NSA_HEADS):
        out_ref[h] = gate(2, h) * win[h]

    q_sel = []
    for g in range(NSA_GROUPS):
        own = owns[g]

        p_sum = jnp.zeros((tq, LANES), F32)
        for hg in range(NSA_HPG):
            h = g * NSA_HPG + hg
            sc = jnp.where(cmp_ok, _dot_nt(qs[h], kc), NEG)
            p = jnp.where(cmp_ok, jnp.exp2(sc - jnp.max(sc, axis=-1, keepdims=True)), 0.0)
            l = jnp.sum(p, axis=-1, keepdims=True)
            p = p * jnp.where(l > 0.0, 1.0 / l, 0.0)
            p_sum = p_sum + p
            out_ref[h] += gate(0, h) * _dot(p.astype(BF16), vc)
        p_hi, p_lo = _split_bf16(p_sum)
        imp = (_dot_nt(cov_ref[...], p_hi) + _dot_nt(cov_ref[...], p_lo))[0:n_blk]
        blk = lax.broadcasted_iota(jnp.int32, (n_blk, tq), 0)
        jt = (q0 + lax.broadcasted_iota(jnp.int32, (n_blk, tq), 1)) >> 6
        allowed = blk <= jt
        forced = allowed & ((blk == 0) | (blk >= jt - 1))
        imp = jnp.where(forced, FORCED, jnp.where(allowed, imp, NEG))
        rank = jnp.zeros((n_blk, tq), jnp.int32)
        for jp in range(n_blk):
            other = imp[jp:jp + 1, :]
            ahead = (other > imp) | ((other == imp) & (blk > jp))
            rank = rank + ahead.astype(jnp.int32)
        pen_t = jnp.where((rank < NSA_TOP_N) & allowed, 0.0, NEG)
        pen_t = jnp.concatenate([pen_t, jnp.zeros((LANES - n_blk, tq), F32)], axis=0)
        pen = jnp.transpose(pen_t)
        if g == 0:
            pen = pltpu.roll(pen, 64, axis=1)
        pen = pen.astype(BF16)
        q_sel += [jnp.where(own, qt, pen) for qt in qts]

    slc = _flash(q_sel, lambda h, j: (ks0_ref if h < NSA_HPG else ks1_ref)[0, rows(j), :],
                 lambda h, j: vs_ref[0, rows(j), :], 0, i, t, tq, shift, flash_scratch, masked_loop=True)
    for hg in range(NSA_HPG):
        h0, h1 = hg, NSA_HPG + hg
        y_ref[:, LANES * hg:LANES * (hg + 1)] = jnp.where(
            lo, out_ref[h0] + gate(1, h0) * slc[h0], out_ref[h1] + gate(1, h1) * slc[h1]).astype(BF16)
    y = _dot(ymla_ref[0], wmla_ref[...]) + _dot(y_ref[...], wnsa_ref[...])
    o_ref[0] = x_ref[0] + g_ref[0] * y


def _nsa_attn(layer, shift, q, kc, vc, kslc0, kslc1, kwin, vsw, pg, cover_t, x3, mod, y_mla, w_mla, w_nsa):
    b, s, _ = q.shape
    d = x3.shape[-1]
    t, tq = ATT_TILE, Q_TILE
    tile3 = lambda w: pl.BlockSpec((1, tq, w), lambda bi, i: (bi, i, 0))
    full2 = lambda a: pl.BlockSpec(a.shape, lambda bi, i: (0, 0))
    seq = pl.BlockSpec((1, s, LANES), lambda bi, i: (bi, 0, 0))
    seq_hi = pl.BlockSpec((1, s, LANES), lambda bi, i: (bi, 0, 1))
    cmp = pl.BlockSpec((1, LANES, LANES), lambda bi, i: (bi, 0, 0))
    return pl.pallas_call(
        functools.partial(_nsa_attn_kernel, shift),
        out_shape=jax.ShapeDtypeStruct((b, s, d), F32),
        grid=(b, s // tq),
        in_specs=[tile3(512), cmp, cmp,
                  seq, seq, seq, seq, seq_hi,
                  pl.BlockSpec((1, tq, LANES), lambda bi, i: (bi, i, 1)),
                  full2(cover_t),
                  tile3(d), _gate_spec(layer, b, d), tile3(512), full2(w_mla), full2(w_nsa)],
        out_specs=tile3(d),
        scratch_shapes=([pltpu.VMEM((tq, 512), BF16), pltpu.VMEM((NSA_HEADS, tq, LANES), F32)]
                        + _flash_scratch(NSA_HEADS, tq, t)),
        compiler_params=_cparams("parallel", "arbitrary"),
        name="nsa_attention",
    )(q, kc, vc, kslc0, kslc1, vsw, kwin, vsw, pg, cover_t, x3, mod, y_mla, w_mla, w_nsa)


def _pad_lanes(a, width):
    return jnp.pad(a, [(0, 0)] * (a.ndim - 1) + [(0, width - a.shape[-1])])


def _hy_in_weight(w):
    d = w.shape[0]
    cq, ckv, kpe, nq, nkv, gate = jnp.split(w, [256, 512, 544, 1056, 1824], axis=1)
    nq = nq.reshape(d, NSA_GROUPS, NSA_HPG, NSA_DH).transpose(0, 2, 1, 3).reshape(d, 512)
    kcmp, vcmp, kslc, vslc, kwin, vwin = jnp.split(nkv, 6, axis=1)
    return jnp.concatenate([cq, ckv, nq, kslc, kwin, _pad_lanes(kpe, LANES), _pad_lanes(gate, LANES),
                            kcmp, vcmp, vslc, vwin], axis=1).astype(BF16)


def _cover_t(seq):
    nc, ns = (seq - NSA_CMP_LEN) // NSA_CMP_STRIDE + 1, seq // NSA_SLC_LEN
    assert nc < LANES and ns <= 32 and seq % Q_TILE == 0
    c_start = np.arange(nc) * NSA_CMP_STRIDE
    c_end = c_start + NSA_CMP_LEN - 1
    j_start = np.arange(ns) * NSA_SLC_LEN
    cover = ((c_start[:, None] <= j_start[None, :] + NSA_SLC_LEN - 1) & (c_end[:, None] >= j_start[None, :]))
    out = np.zeros((LANES, LANES), np.float32)
    out[:ns, :nc] = cover.T
    return jnp.asarray(out, BF16)


def _compress_weights(cmp_pos, w1, w2):
    half = NSA_CMP_STRIDE
    w1 = w1.reshape(2, 2, half, NSA_DH, NSA_CMP_HIDDEN)
    eye = jnp.eye(NSA_GROUPS, dtype=F32)
    w1e = jnp.einsum('jcldn,gh->jclgdhn', w1, eye).reshape(2, 2, half * LANES, NSA_GROUPS * NSA_CMP_HIDDEN)
    w2e = jnp.einsum('jnd,gh->jgnhd', w2, eye).reshape(2, NSA_GROUPS * NSA_CMP_HIDDEN, LANES)
    pos = cmp_pos.reshape(2, 2, half, 1, NSA_DH)
    pos = jnp.broadcast_to(pos, (2, 2, half, NSA_GROUPS, NSA_DH)).reshape(2, 2, 1, half * LANES)
    pos = jnp.broadcast_to(pos, (2, 2, 8, half * LANES))
    return pos, w1e[:, 0].astype(BF16), w1e[:, 1].astype(BF16), w2e.astype(BF16)


def kernel(x, c, positions, ada_w, ada_b, norm_mix, norm_mlp, mlp_w1, mlp_w2, hy_w_in, hy_w_out, mla_q_norm, mla_w_uq, mla_kv_norm, mla_w_ukv, mla_q_gain, mla_k_gain, nsa_q_gain, nsa_k_gain, nsa_cmp_pos, nsa_cmp_w1, nsa_cmp_w2, diff_w_qkv, diff_w_out, diff_q_gain, diff_k_gain, diff_lambda, diff_sub_gain):
    nb, seq, d = x.shape
    depth = ada_w.shape[0]
    m = nb * seq
    n_cmp = seq // NSA_CMP_STRIDE

    mod = _adaln(c, ada_w, ada_b).reshape(depth * nb, 1, 6 * d)
    tab_mla, tab_64 = _rope_tables(positions.reshape(m, 1))
    pos_c = jnp.pad(positions[:, NSA_CMP_LEN - 1::NSA_CMP_STRIDE], ((0, 0), (0, 1)))
    _, tab_cmp = _rope_tables(pos_c.reshape(nb * n_cmp, 1))
    tab_cmp = tuple(a.reshape(nb, n_cmp, LANES) for a in tab_cmp)
    cover_t = _cover_t(seq)
    seq3 = lambda a: a.reshape(nb, seq, -1)

    x2 = x.reshape(m, d)
    for i in range(depth):
        j = i // 2
        if i % 2 == 0:
            qg_s = nsa_q_gain[j] * (NSA_DH ** -0.5 * LOG2E)
            qg = jnp.tile(qg_s, 2).reshape(1, LANES)
            kg = jnp.tile(nsa_k_gain[j], (1, 2)).reshape(3, 1, LANES)
            lat, q_nsa, kslc0, kslc1, kwin, pg, kcmp, vcmp, vsw = _hy_in(
                x2, norm_mix[i], mod, i, _hy_in_weight(hy_w_in[j]),
                mla_q_norm[j].reshape(1, -1), mla_kv_norm[j].reshape(1, -1), qg, kg, tab_64, nb)

            wq = _pad_lanes(mla_w_uq[j].reshape(MLA_Q_RANK, MLA_HEADS, MLA_DK), LANES).reshape(MLA_Q_RANK, -1)
            wkv = mla_w_ukv[j].reshape(MLA_KV_RANK, MLA_HEADS, MLA_NOPE + MLA_V)
            wk = jnp.pad(wkv[..., :MLA_NOPE], ((0, 0), (0, 0), (MLA_ROPE, LANES - MLA_DK))).reshape(MLA_KV_RANK, -1)
            wv = wkv[..., MLA_NOPE:].reshape(MLA_KV_RANK, -1)
            q_mla, k_mla, v_mla = _mla_prep(
                lat, pg, wq.astype(BF16), wk.astype(BF16), wv.astype(BF16),
                _pad_lanes(mla_q_gain[j] * LOG2E, LANES).reshape(1, LANES),
                _pad_lanes(mla_k_gain[j] * MLA_DK ** 0.5, LANES).reshape(1, LANES),
                tab_mla)
            y_mla = _shift_dispatch(
                _needs_shift(MLA_DK, mla_q_gain[j] * (MLA_DK ** -0.5 * LOG2E), mla_k_gain[j]), _mla_attn,
                seq3(q_mla), seq3(k_mla), seq3(v_mla))

            pos_e, w1a, w1b, w2e = _compress_weights(nsa_cmp_pos[j], nsa_cmp_w1[j], nsa_cmp_w2[j])
            chunks = lambda a: a.reshape(nb, n_cmp, NSA_CMP_STRIDE * LANES)
            kc, vc = _nsa_compress(chunks(kcmp), chunks(vcmp), pos_e, w1a, w1b, w2e, kg[0], tab_cmp)
            w_out = hy_w_out[j]
            w_nsa = w_out[512:].reshape(NSA_GROUPS, NSA_HPG, NSA_DH, d).transpose(1, 0, 2, 3).reshape(512, d)
            x2 = _shift_dispatch(
                _needs_shift(NSA_DH, qg_s, nsa_k_gain[j][1:]), functools.partial(_nsa_attn, i),
                seq3(q_nsa), kc, vc, seq3(kslc0), seq3(kslc1), seq3(kwin), seq3(vsw), seq3(pg), cover_t,
                seq3(x2), mod, y_mla, w_out[:512].astype(BF16), w_nsa.astype(BF16)).reshape(m, d)
        else:
            lam_init = 0.8 - 0.6 * math.exp(-0.3 * i)
            gains = jnp.stack([jnp.tile(diff_q_gain[j] * (DIFF_DH ** -0.5 * LOG2E), 2), jnp.tile(diff_k_gain[j], 2)])
            q, k, v = _diff_in(x2, norm_mix[i], mod, i, diff_w_qkv[j].astype(BF16),
                               gains.reshape(2, 1, LANES), tab_64, nb)
            x2 = _shift_dispatch(
                _needs_shift(DIFF_DH, gains[0], gains[1]), functools.partial(_diff_attn, lam_init, i),
                seq3(q), seq3(k), seq3(v), diff_lambda[j], diff_sub_gain[j],
                seq3(x2), mod, diff_w_out[j].astype(BF16)).reshape(m, d)
        x2 = _mlp(x2, norm_mlp[i], mod, i, mlp_w1, mlp_w2, nb)
    return x2.reshape(nb, seq, d)
```

```python
import functools
import math

import numpy as np
import jax
import jax.numpy as jnp
from jax import lax
from jax.experimental import pallas as pl
from jax.experimental.pallas import tpu as pltpu

F32 = jnp.float32
BF16 = jnp.bfloat16

LANES = 128
VMEM_LIMIT = 52 * 1024 * 1024

ROPE_THETA = 500000.0
EPS = 1e-6
NEG = -1e30
FORCED = 1e9

MLA_HEADS, MLA_NOPE, MLA_ROPE, MLA_V = 8, 64, 32, 64
MLA_Q_RANK, MLA_KV_RANK = 256, 256
MLA_DK = MLA_ROPE + MLA_NOPE
NSA_HEADS, NSA_GROUPS, NSA_DH = 8, 2, 64
NSA_HPG = NSA_HEADS // NSA_GROUPS
NSA_ROT = NSA_DH // 4
NSA_CMP_LEN, NSA_CMP_STRIDE, NSA_CMP_HIDDEN = 32, 16, 128
NSA_SLC_LEN, NSA_TOP_N, NSA_WINDOW = 64, 16, 512
DIFF_HEADS, DIFF_DH = 8, 64
DIFF_ROT = DIFF_DH // 4

ROW_TILE = 512
ATT_TILE = 256
Q_TILE = 256
MXU_COLS = 256


def _cparams(*sem):
    return pltpu.CompilerParams(dimension_semantics=sem, vmem_limit_bytes=VMEM_LIMIT)


def _split_bf16(x):
    hi = x.astype(BF16)
    lo = (x - hi.astype(F32)).astype(BF16)
    return hi, lo


def _dot(a, b):
    return jnp.dot(a, b, preferred_element_type=F32)


def _dot_nt(a, b):
    return lax.dot_general(a, b, (((1,), (1,)), ((), ())), preferred_element_type=F32)


def _sigmoid(x):
    return 1.0 / (1.0 + jnp.exp(-x))


def _lane(shape):
    return lax.broadcasted_iota(jnp.int32, shape, 1)


def _adaln_kernel(c_ref, w_ref, b_ref, o_ref):
    c = c_ref[...]
    cond = c * _sigmoid(c)
    c_hi, c_lo = _split_bf16(cond)
    w_hi, w_lo = _split_bf16(w_ref[0])
    o_ref[0] = _dot(c_hi, w_hi) + _dot(c_hi, w_lo) + _dot(c_lo, w_hi) + b_ref[0]


def _adaln(c, ada_w, ada_b):
    depth, d, n = ada_w.shape
    b = c.shape[0]
    tn = 1536
    return pl.pallas_call(
        _adaln_kernel,
        out_shape=jax.ShapeDtypeStruct((depth, b, n), F32),
        grid=(depth, n // tn),
        in_specs=[pl.BlockSpec((b, d), lambda i, j: (0, 0)),
                  pl.BlockSpec((1, d, tn), lambda i, j: (i, 0, j)),
                  pl.BlockSpec((1, 1, tn), lambda i, j: (i, 0, j))],
        out_specs=pl.BlockSpec((1, b, tn), lambda i, j: (i, 0, j)),
        compiler_params=_cparams("parallel", "parallel"),
        name="adaln",
    )(c, ada_w, ada_b.reshape(depth, 1, n))


TABLE_SHIFT = 32


def _rope_table_kernel(pos_ref, f_ref, ma_ref, mb_ref, ca_ref, ua_ref, da_ref, cb_ref, ub_ref, db_ref):
    ang = pos_ref[...].astype(F32) * f_ref[...]
    cos, sin = jnp.cos(ang), jnp.sin(ang)
    for m_ref, c_ref, u_ref, d_ref, shift in ((ma_ref, ca_ref, ua_ref, da_ref, 0),
                                              (mb_ref, cb_ref, ub_ref, db_ref, TABLE_SHIFT)):
        c = pltpu.roll(cos, LANES - shift, axis=1) if shift else cos
        s = pltpu.roll(sin, LANES - shift, axis=1) if shift else sin
        c_ref[...] = jnp.where(m_ref[2:3] > 0.0, c, 1.0)
        u_ref[...] = s * m_ref[0:1]
        d_ref[...] = s * m_ref[1:2]


def _lane_freqs(rot, seg):
    half = rot // 2
    lane = np.arange(LANES)
    inv = ROPE_THETA ** (-(np.arange(half, dtype=np.float32) / np.float32(half)))
    f = np.where(lane % seg < rot, inv.astype(np.float32)[lane % half], 0.0)
    return f.reshape(1, LANES).astype(np.float32)


def _rope_masks(rot, seg):
    half = rot // 2
    lane = np.arange(LANES) % seg
    up = ((lane >= half) & (lane < rot)).astype(np.float32)
    dn = -(lane < half).astype(np.float32)
    return jnp.asarray(np.stack([up, dn, (lane < rot).astype(np.float32)]))


def _rope_tables(pos_col):
    rows = pos_col.shape[0]
    tr = min(rows, 2048)
    spec = pl.BlockSpec((tr, LANES), lambda i: (i, 0))
    vec = pl.BlockSpec((1, LANES), lambda i: (0, 0))
    msk = pl.BlockSpec((3, LANES), lambda i: (0, 0))
    shp = jax.ShapeDtypeStruct((rows, LANES), F32)
    f_a, f_b = _lane_freqs(MLA_ROPE, LANES), np.roll(_lane_freqs(NSA_ROT, NSA_DH), TABLE_SHIFT, axis=1)
    assert not np.any((f_a != 0) & (f_b != 0))
    out = pl.pallas_call(
        _rope_table_kernel,
        out_shape=(shp,) * 6,
        grid=(rows // tr,),
        in_specs=[pl.BlockSpec((tr, 1), lambda i: (i, 0)), vec, msk, msk],
        out_specs=(spec,) * 6,
        compiler_params=_cparams("parallel"),
        name="rope_tables",
    )(pos_col, jnp.asarray(f_a + f_b, F32), _rope_masks(MLA_ROPE, LANES), _rope_masks(NSA_ROT, NSA_DH))
    return out[:3], out[3:]


def _rope(y, tab, half):
    c, s_up, s_dn = tab
    up = pltpu.roll(y, half, axis=1)
    dn = pltpu.roll(y, LANES - half, axis=1)
    return y * c + up * s_up + dn * s_dn


def _rms64(x, gain):
    lo = _lane(x.shape) < 64
    x2 = x * x
    s_lo = jnp.sum(jnp.where(lo, x2, 0.0), axis=-1, keepdims=True)
    s_hi = jnp.sum(jnp.where(lo, 0.0, x2), axis=-1, keepdims=True)
    r = jnp.where(lo, lax.rsqrt(s_lo * (1.0 / 64) + EPS), lax.rsqrt(s_hi * (1.0 / 64) + EPS))
    return x * (r * gain)


def _seg_ones(seg):
    lane = np.arange(LANES) // seg
    return jnp.asarray(lane[:, None] == lane[None, :], BF16)


def _rms_seg(x, gain_root, ones, n):
    ss = _dot((x * x).astype(BF16), ones)
    return x * (lax.rsqrt(ss + n * EPS) * gain_root)


def _rms(x, gain):
    return x * lax.rsqrt(jnp.mean(x * x, axis=-1, keepdims=True) + EPS) * gain


def _norm_mod(x, gain, sc, sh):
    return _rms(x, gain) * (1.0 + sc) + sh


def _mod_spec(layer, k, nb, rows_per_batch, tm, d):
    per = rows_per_batch // tm
    return pl.BlockSpec((1, 1, d), lambda i, *_: (layer * nb + i // per, 0, k))


def _mlp_kernel(x_ref, gain_ref, sc_ref, sh_ref, g_ref, w1_ref, w2_ref, o_ref, h_scr, acc_scr):
    j = pl.program_id(1)

    @pl.when(j == 0)
    def _():
        h_scr[...] = _norm_mod(x_ref[...], gain_ref[...], sc_ref[0], sh_ref[0]).astype(BF16)
        acc_scr[...] = jnp.zeros_like(acc_scr)

    a = jnp.maximum(_dot(h_scr[...], w1_ref[0].astype(BF16)), 0.0)
    acc_scr[...] += _dot((a * a).astype(BF16), w2_ref[0].astype(BF16))

    @pl.when(j == pl.num_programs(1) - 1)
    def _():
        o_ref[...] = x_ref[...] + g_ref[0] * acc_scr[...]


def _mlp(x2, gain, mod, layer, w1, w2, nb):
    m, d = x2.shape
    ff = w1.shape[2]
    tm, tf = 1024, 1024
    s = m // nb
    return pl.pallas_call(
        _mlp_kernel,
        out_shape=jax.ShapeDtypeStruct((m, d), F32),
        grid=(m // tm, ff // tf),
        in_specs=[pl.BlockSpec((tm, d), lambda i, j: (i, 0)),
                  pl.BlockSpec((1, d), lambda i, j: (0, 0)),
                  _mod_spec(layer, 4, nb, s, tm, d),
                  _mod_spec(layer, 3, nb, s, tm, d),
                  _mod_spec(layer, 5, nb, s, tm, d),
                  pl.BlockSpec((1, d, tf), lambda i, j: (layer, 0, j)),
                  pl.BlockSpec((1, tf, d), lambda i, j: (layer, j, 0))],
        out_specs=pl.BlockSpec((tm, d), lambda i, j: (i, 0)),
        scratch_shapes=[pltpu.VMEM((tm, d), BF16), pltpu.VMEM((tm, d), F32)],
        compiler_params=_cparams("parallel", "arbitrary"),
        name="relu2_mlp",
    )(x2, gain.reshape(1, d), mod, mod, mod, w1, w2)


LOG2E = math.log2(math.e)


SCORE_BOUND = 60.0
BOUND_MARGIN = 1.02


def _needs_shift(dk, q_gain, k_gain):
    bound = dk * jnp.max(jnp.abs(q_gain)) * jnp.max(jnp.abs(k_gain)) * BOUND_MARGIN
    return jnp.logical_not(bound <= SCORE_BOUND)


def _shift_dispatch(needs_shift, attn, *args):
    return lax.cond(needs_shift, functools.partial(attn, True), functools.partial(attn, False), *args)


def _lane_tiles(x):
    return [x[:, LANES * c:LANES * (c + 1)] for c in range(x.shape[1] // LANES)]


def _flash_scratch(n, rows, t):
    return [pltpu.VMEM((n, rows, LANES), F32), pltpu.VMEM((n, rows, LANES), F32)]


def _flash(qs, k_at, v_at, first, i, t, rows_per_map, shift, scratch, window=None, masked_loop=False):
    n = len(qs)
    rows = qs[0].shape[0]
    n_diag = rows_per_map // t
    last = (i + 1) * n_diag - 1
    r = lax.broadcasted_iota(jnp.int32, (rows, t), 0)
    if rows != rows_per_map:
        r = jnp.where(r >= rows_per_map, r - rows_per_map, r)
    c_minus_r = lax.broadcasted_iota(jnp.int32, (rows, t), 1) - r

    assert masked_loop or window is None

    def visible(j):
        ahead = i * rows_per_map - j * t
        ok = c_minus_r <= ahead
        return ok if window is None else ok & (c_minus_r > ahead - window)

    def scores(h, j, ok):
        s = _dot_nt(qs[h], k_at(h, j))
        return s if ok is None else jnp.where(ok, s, NEG)

    def sweep(step, carry):
        if masked_loop:
            return lax.fori_loop(first, last + 1, lambda j, cr: step(j, cr, visible(j)), carry)
        carry = lax.fori_loop(first, i * n_diag, lambda j, cr: step(j, cr, None), carry)
        for d in range(n_diag):
            carry = step(i * n_diag + d, carry, visible(i * n_diag + d))
        return carry

    def max_step(j, ms, ok):
        out = []
        for h in range(n):
            m = ms[h]
            for s_c in _lane_tiles(scores(h, j, ok)):
                m = jnp.maximum(m, s_c)
            out.append(m)
        return tuple(out)

    if shift:
        ms = sweep(max_step, tuple(jnp.full((rows, LANES), NEG, F32) for _ in range(n)))
        ms = [jnp.broadcast_to(jnp.max(m, axis=-1, keepdims=True), (rows, LANES)) for m in ms]

    l_ref, acc_ref = scratch
    for h in range(n):
        l_ref[h] = jnp.zeros((rows, LANES), F32)
        acc_ref[h] = jnp.zeros((rows, LANES), F32)

    def sum_step(j, p_prev, ok):
        p_new = []
        for h in range(n):
            acc_ref[h] += _dot(p_prev[h], v_at(h, jnp.maximum(j - 1, 0)))
            ps = _lane_tiles(scores(h, j, ok))
            ps = [jnp.exp2(s_c - ms[h]) if shift else jnp.exp2(s_c) for s_c in ps]
            l = l_ref[h]
            for p_c in ps:
                l = l + p_c
            l_ref[h] = l
            p_new.append(jnp.concatenate(ps, axis=1).astype(BF16))
        return tuple(p_new)

    p = sweep(sum_step, tuple(jnp.zeros((rows, t), BF16) for _ in range(n)))
    return [(acc_ref[h] + _dot(p[h], v_at(h, last))) / jnp.sum(l_ref[h], axis=-1, keepdims=True)
            for h in range(n)]


def _hy_in_kernel(seq, x_ref, gain_ref, sc_ref, sh_ref, w_ref, qn_ref, kvn_ref, qg_ref, kg_ref,
                  c_ref, su_ref, sd_ref,
                  lat_ref, q_ref, ks0_ref, ks1_ref, kw_ref, pg_ref, kcmp_ref, vcmp_ref, vsw_ref):
    tm = x_ref.shape[0]
    h = _norm_mod(x_ref[...], gain_ref[...], sc_ref[0], sh_ref[0]).astype(BF16)
    tab = (c_ref[...], su_ref[...], sd_ref[...])

    def cols(a, b):
        return _dot(h, w_ref[:, a:b])

    def prep(x, g):
        return _rope(_rms64(x, g), tab, NSA_ROT // 2)

    lat_ref[:, 0:256] = _rms(cols(0, 256), qn_ref[...]).astype(BF16)
    lat_ref[:, 256:512] = _rms(cols(256, 512), kvn_ref[...]).astype(BF16)
    for ch in range(2):
        y = cols(512 + MXU_COLS * ch, 512 + MXU_COLS * (ch + 1))
        for e in range(2):
            q_ref[:, MXU_COLS * ch + LANES * e:MXU_COLS * ch + LANES * (e + 1)] = prep(
                y[:, LANES * e:LANES * (e + 1)], qg_ref[...]).astype(BF16)
    y = cols(1024, 1280)
    kslc = prep(y[:, 0:LANES], kg_ref[1])
    lane = _lane((tm, LANES))
    t_idx = (pl.program_id(0) * tm + lax.broadcasted_iota(jnp.int32, (tm, LANES), 0)) & (seq - 1)
    blk = t_idx >> 6
    ks0_ref[...] = jnp.where(lane < 64, kslc, jnp.where(lane - 64 == blk, 1.0, 0.0)).astype(BF16)
    ks1_ref[...] = jnp.where(lane >= 64, kslc, jnp.where(lane == blk, 1.0, 0.0)).astype(BF16)
    kw_ref[...] = prep(y[:, LANES:2 * LANES], kg_ref[2]).astype(BF16)
    pg_ref[...] = cols(1280, 1536)
    y = cols(1536, 1792)
    kcmp_ref[...] = y[:, 0:LANES].astype(BF16)
    vcmp_ref[...] = y[:, LANES:2 * LANES].astype(BF16)
    vsw_ref[...] = cols(1792, 2048).astype(BF16)


def _hy_in(x2, gain, mod, layer, w, q_norm, kv_norm, q_gain, k_gain, tab, nb):
    m, d = x2.shape
    tm = ROW_TILE
    seq = m // nb
    assert seq & (seq - 1) == 0 and seq // NSA_SLC_LEN <= 32 and seq % tm == 0
    row = lambda wd: pl.BlockSpec((tm, wd), lambda i: (i, 0))
    full = lambda a: pl.BlockSpec(a.shape, lambda i: (0,) * a.ndim)
    bf = lambda wd: jax.ShapeDtypeStruct((m, wd), BF16)
    widths = (512, 512, LANES, LANES, LANES, 2 * LANES, LANES, LANES, 2 * LANES)
    shapes = tuple(jax.ShapeDtypeStruct((m, wd), F32) if k == 5 else bf(wd) for k, wd in enumerate(widths))
    gain = gain.reshape(1, d)
    return pl.pallas_call(
        functools.partial(_hy_in_kernel, seq),
        out_shape=shapes,
        grid=(m // tm,),
        in_specs=[row(d), full(gain), _mod_spec(layer, 1, nb, seq, tm, d), _mod_spec(layer, 0, nb, seq, tm, d),
                  full(w), full(q_norm), full(kv_norm), full(q_gain), full(k_gain),
                  row(LANES), row(LANES), row(LANES)],
        out_specs=tuple(row(wd) for wd in widths),
        compiler_params=_cparams("parallel"),
        name="hy_in_proj",
    )(x2, gain, mod, mod, w, q_norm, kv_norm, q_gain, k_gain, *tab)


def _mla_prep_kernel(lat_ref, kpe_ref, wq_ref, wk_ref, wv_ref, qg_ref, kg_ref, c_ref, su_ref, sd_ref, ones_ref,
                     q_ref, k_ref, v_ref):
    cq, ckv = lat_ref[:, 0:256], lat_ref[:, 256:512]
    kpe = kpe_ref[...]
    tab = (c_ref[...], su_ref[...], sd_ref[...])
    v_ref[...] = _dot(ckv, wv_ref[...]).astype(BF16)

    def head(x, g):
        return _rope(_rms_seg(x, g, ones_ref[...], MLA_DK), tab, MLA_ROPE // 2)

    for ch in range(MLA_HEADS // 2):
        cs = slice(MXU_COLS * ch, MXU_COLS * (ch + 1))
        q2, k2 = _dot(cq, wq_ref[:, cs]), _dot(ckv, wk_ref[:, cs])
        for e in range(2):
            sl = slice(MXU_COLS * ch + LANES * e, MXU_COLS * ch + LANES * (e + 1))
            es = slice(LANES * e, LANES * (e + 1))
            q_ref[:, sl] = head(q2[:, es], qg_ref[...]).astype(BF16)
            k_ref[:, sl] = head(k2[:, es] + kpe, kg_ref[...]).astype(BF16)


def _mla_prep(lat, pg, wq, wk, wv, q_gain, k_gain, tab):
    m = lat.shape[0]
    tm = ROW_TILE
    ones = _seg_ones(LANES)
    full = lambda a: pl.BlockSpec(a.shape, lambda i: (0, 0))
    row = lambda w: pl.BlockSpec((tm, w), lambda i: (i, 0))
    return pl.pallas_call(
        _mla_prep_kernel,
        out_shape=(jax.ShapeDtypeStruct((m, 1024), BF16), jax.ShapeDtypeStruct((m, 1024), BF16),
                   jax.ShapeDtypeStruct((m, 512), BF16)),
        grid=(m // tm,),
        in_specs=[row(512), row(LANES), full(wq), full(wk), full(wv), full(q_gain), full(k_gain),
                  row(LANES), row(LANES), row(LANES), full(ones)],
        out_specs=(row(1024), row(1024), row(512)),
        compiler_params=_cparams("parallel"),
        name="mla_prep",
    )(lat, pg, wq, wk, wv, q_gain, k_gain, *tab, ones)


def _mla_attn_kernel(shift, q_ref, k_ref, v_ref, o_ref, *scratch):
    i = pl.program_id(1)
    t, tq = ATT_TILE, Q_TILE
    lo = _lane((tq, LANES)) < 64

    def rows(j):
        return pl.ds(pl.multiple_of(j * t, t), t)

    def tile(h):
        return slice(LANES * h, LANES * (h + 1))

    outs = _flash([q_ref[0, :, tile(h)] for h in range(MLA_HEADS)],
                  lambda h, j: k_ref[0, rows(j), tile(h)],
                  lambda h, j: v_ref[0, rows(j), tile(h // 2)], 0, i, t, tq, shift, scratch)
    for p in range(MLA_HEADS // 2):
        o_ref[0, :, tile(p)] = jnp.where(lo, outs[2 * p], outs[2 * p + 1]).astype(BF16)


def _mla_attn(shift, q, k, v):
    b, s, _ = q.shape
    t, tq = ATT_TILE, Q_TILE
    return pl.pallas_call(
        functools.partial(_mla_attn_kernel, shift),
        out_shape=jax.ShapeDtypeStruct((b, s, 512), BF16),
        grid=(b, s // tq),
        in_specs=[pl.BlockSpec((1, tq, 1024), lambda bi, i: (bi, i, 0)),
                  pl.BlockSpec((1, s, 1024), lambda bi, i: (bi, 0, 0)),
                  pl.BlockSpec((1, s, 512), lambda bi, i: (bi, 0, 0))],
        out_specs=pl.BlockSpec((1, tq, 512), lambda bi, i: (bi, i, 0)),
        scratch_shapes=_flash_scratch(MLA_HEADS, tq, t),
        compiler_params=_cparams("parallel", "arbitrary"),
        name="mla_attention",
    )(q, k, v)


def _diff_in_kernel(x_ref, gain_ref, sc_ref, sh_ref, w_ref, g_ref, c_ref, su_ref, sd_ref, q_ref, k_ref, v_ref):
    h = _norm_mod(x_ref[...], gain_ref[...], sc_ref[0], sh_ref[0]).astype(BF16)
    tab = (c_ref[...], su_ref[...], sd_ref[...])
    n = DIFF_HEADS * LANES
    for which, out in ((0, q_ref), (1, k_ref)):
        g = g_ref[which]
        for ch in range(n // MXU_COLS):
            y = _dot(h, w_ref[:, which * n + MXU_COLS * ch:which * n + MXU_COLS * (ch + 1)])
            for e in range(MXU_COLS // LANES):
                out[:, MXU_COLS * ch + LANES * e:MXU_COLS * ch + LANES * (e + 1)] = _rope(
                    _rms64(y[:, LANES * e:LANES * (e + 1)], g), tab, DIFF_ROT // 2).astype(BF16)
    for ch in range(n // MXU_COLS):
        cs = slice(MXU_COLS * ch, MXU_COLS * (ch + 1))
        v_ref[:, cs] = _dot(h, w_ref[:, 2 * n + MXU_COLS * ch:2 * n + MXU_COLS * (ch + 1)]).astype(BF16)


def _diff_in(x2, gain, mod, layer, w, gains, tab, nb):
    m, d = x2.shape
    tm = ROW_TILE
    seq = m // nb
    row = lambda wd: pl.BlockSpec((tm, wd), lambda i: (i, 0))
    full = lambda a: pl.BlockSpec(a.shape, lambda i: (0,) * a.ndim)
    shp = jax.ShapeDtypeStruct((m, DIFF_HEADS * LANES), BF16)
    gain = gain.reshape(1, d)
    return pl.pallas_call(
        _diff_in_kernel,
        out_shape=(shp, shp, shp),
        grid=(m // tm,),
        in_specs=[row(d), full(gain), _mod_spec(layer, 1, nb, seq, tm, d), _mod_spec(layer, 0, nb, seq, tm, d),
                  full(w), full(gains), row(LANES), row(LANES), row(LANES)],
        out_specs=(row(DIFF_HEADS * LANES),) * 3,
        compiler_params=_cparams("parallel"),
        name="diff_in_proj",
    )(x2, gain, mod, mod, w, gains, *tab)


def _diff_attn_kernel(lam_init, shift, q_ref, k_ref, v_ref, lam_ref, sg_ref, x_ref, g_ref, w_ref,
                      o_ref, y_ref, *scratch):
    i = pl.program_id(1)
    t = ATT_TILE
    lo = _lane((t, LANES)) < 64
    lam = lam_ref[...]
    lmb = (jnp.exp(jnp.sum(lam[0:1] * lam[1:2], axis=-1, keepdims=True))
           - jnp.exp(jnp.sum(lam[2:3] * lam[3:4], axis=-1, keepdims=True)) + lam_init)

    def rows(j):
        return pl.ds(pl.multiple_of(j * t, t), t)

    def tile(h):
        return slice(LANES * h, LANES * (h + 1))

    def both_maps(h):
        qt = q_ref[0, :, tile(h)]
        zero = jnp.zeros_like(qt)
        return jnp.concatenate([jnp.where(lo, qt, zero), jnp.where(lo, zero, qt)], axis=0)

    outs = _flash([both_maps(h) for h in range(DIFF_HEADS)],
                  lambda h, j: k_ref[0, rows(j), tile(h)],
                  lambda h, j: v_ref[0, rows(j), tile(h)], 0, i, t, t, shift, scratch)
    for h in range(DIFF_HEADS):
        o = outs[h][:t] - lmb * outs[h][t:]
        o = o * lax.rsqrt(jnp.mean(o * o, axis=-1, keepdims=True) + EPS) * sg_ref[...]
        y_ref[:, tile(h)] = (o * (1.0 - lam_init)).astype(BF16)
    o_ref[0] = x_ref[0] + g_ref[0] * _dot(y_ref[...], w_ref[...])


def _gate_spec(layer, nb, d):
    return pl.BlockSpec((1, 1, d), lambda bi, i: (layer * nb + bi, 0, 2))


def _diff_attn(lam_init, layer, shift, q, k, v, lam, sub_gain, x3, mod, w_out):
    b, s, n = q.shape
    d = x3.shape[-1]
    t = ATT_TILE
    full = pl.BlockSpec((1, s, n), lambda bi, i: (bi, 0, 0))
    return pl.pallas_call(
        functools.partial(_diff_attn_kernel, lam_init, shift),
        out_shape=jax.ShapeDtypeStruct((b, s, d), F32),
        grid=(b, s // t),
        in_specs=[pl.BlockSpec((1, t, n), lambda bi, i: (bi, i, 0)), full, full,
                  pl.BlockSpec(lam.shape, lambda bi, i: (0, 0)),
                  pl.BlockSpec((1, LANES), lambda bi, i: (0, 0)),
                  pl.BlockSpec((1, t, d), lambda bi, i: (bi, i, 0)), _gate_spec(layer, b, d),
                  pl.BlockSpec(w_out.shape, lambda bi, i: (0, 0))],
        out_specs=pl.BlockSpec((1, t, d), lambda bi, i: (bi, i, 0)),
        scratch_shapes=[pltpu.VMEM((t, n), BF16)] + _flash_scratch(DIFF_HEADS, 2 * t, t),
        compiler_params=_cparams("parallel", "arbitrary"),
        name="diff_attention",
    )(q, k, v, lam, sub_gain.reshape(1, LANES), x3, mod, w_out)


def _nsa_compress_kernel(tk_ref, tv_ref, pos_ref, w1a_ref, w1b_ref, w2_ref, kg_ref,
                         c_ref, su_ref, sd_ref, kc_ref, vc_ref):
    n_rows = tk_ref.shape[1]
    for j, (t_ref, out) in enumerate(((tk_ref, kc_ref), (tv_ref, vc_ref))):
        tok = t_ref[0]
        p_hi, p_lo = _split_bf16(pos_ref[j])
        w1a, w1b = w1a_ref[j], w1b_ref[j]
        bias = (_dot(p_hi[0], w1a) + _dot(p_lo[0], w1a) + _dot(p_hi[1], w1b) + _dot(p_lo[1], w1b))[0:1]
        hid = _dot(tok, w1a) + pltpu.roll(_dot(tok, w1b), n_rows - 1, axis=0) + bias
        act = jax.nn.gelu(hid, approximate=True)
        cmp = _dot(act.astype(BF16), w2_ref[j])
        if j == 0:
            cmp = _rope(_rms64(cmp, kg_ref[...]), (c_ref[0], su_ref[0], sd_ref[0]), NSA_ROT // 2)
        out[0] = cmp.astype(BF16)


def _nsa_compress(tk, tv, pos, w1a, w1b, w2, k_gain0, tab):
    b, nr, w = tk.shape
    full = lambda a: pl.BlockSpec(a.shape, lambda bi: (0,) * a.ndim)
    per_b = lambda a: pl.BlockSpec((1,) + a.shape[1:], lambda bi: (bi,) + (0,) * (a.ndim - 1))
    shp = jax.ShapeDtypeStruct((b, nr, LANES), BF16)
    return pl.pallas_call(
        _nsa_compress_kernel,
        out_shape=(shp, shp),
        grid=(b,),
        in_specs=[per_b(tk), per_b(tv), full(pos), full(w1a), full(w1b), full(w2), full(k_gain0),
                  per_b(tab[0]), per_b(tab[1]), per_b(tab[2])],
        out_specs=(pl.BlockSpec((1, nr, LANES), lambda bi: (bi, 0, 0)),) * 2,
        compiler_params=_cparams("parallel"),
        name="nsa_compress",
    )(tk, tv, pos, w1a, w1b, w2, k_gain0, *tab)


def _nsa_attn_kernel(shift, q_ref, kc_ref, vc_ref, ks0_ref, ks1_ref, vs_ref, kw_ref, vw_ref, gate_ref, cov_ref,
                     x_ref, g_ref, ymla_ref, wmla_ref, wnsa_ref, o_ref, y_ref, *scratch):
    i = pl.program_id(1)
    t, tq = ATT_TILE, Q_TILE
    n_blk = ks0_ref.shape[1] // NSA_SLC_LEN
    q0 = i * tq
    lane = _lane((tq, LANES))
    lo = lane < 64
    qpos = q0 + lax.broadcasted_iota(jnp.int32, (tq, LANES), 0)
    cmp_ok = NSA_CMP_STRIDE * lane + (NSA_CMP_LEN - 1) <= qpos
    gates = _sigmoid(gate_ref[0])

    def rows(j):
        return pl.ds(pl.multiple_of(j * t, t), t)

    def gate(branch, h):
        col = branch * NSA_HEADS + h
        return gates[:, col:col + 1]

    out_ref, flash_scratch = scratch[0], scratch[1:]
    kc, vc = kc_ref[0], vc_ref[0]
    qts = [q_ref[0, :, LANES * hg:LANES * (hg + 1)] for hg in range(NSA_HPG)]
    owns = [lo, jnp.logical_not(lo)]
    qs = [jnp.where(own, qt, jnp.zeros_like(qt)) for own in owns for qt in qts]

    win = _flash(qs, lambda h, j: kw_ref[0, rows(j), :], lambda h, j: vw_ref[0, rows(j), :],
                 jnp.maximum((i * tq - NSA_WINDOW) // t, 0), i, t, tq, shift, flash_scratch, window=NSA_WINDOW,
                 masked_loop=True)
    for h in range(NSA_HEADS):
        out_ref[h] = gate(2, h) * win[h]

    q_sel = []
    for g in range(NSA_GROUPS):
        own = owns[g]

        p_sum = jnp.zeros((tq, LANES), F32)
        for hg in range(NSA_HPG):
            h = g * NSA_HPG + hg
            sc = jnp.where(cmp_ok, _dot_nt(qs[h], kc), NEG)
            p = jnp.where(cmp_ok, jnp.exp2(sc - jnp.max(sc, axis=-1, keepdims=True)), 0.0)
            l = jnp.sum(p, axis=-1, keepdims=True)
            p = p * jnp.where(l > 0.0, 1.0 / l, 0.0)
            p_sum = p_sum + p
            out_ref[h] += gate(0, h) * _dot(p.astype(BF16), vc)
        p_hi, p_lo = _split_bf16(p_sum)
        imp = (_dot_nt(cov_ref[...], p_hi) + _dot_nt(cov_ref[...], p_lo))[0:n_blk]
        blk = lax.broadcasted_iota(jnp.int32, (n_blk, tq), 0)
        jt = (q0 + lax.broadcasted_iota(jnp.int32, (n_blk, tq), 1)) >> 6
        allowed = blk <= jt
        forced = allowed & ((blk == 0) | (blk >= jt - 1))
        imp = jnp.where(forced, FORCED, jnp.where(allowed, imp, NEG))
        rank = jnp.zeros((n_blk, tq), jnp.int32)
        for jp in range(n_blk):
            other = imp[jp:jp + 1, :]
            ahead = (other > imp) | ((other == imp) & (blk > jp))
            rank = rank + ahead.astype(jnp.int32)
        pen_t = jnp.where((rank < NSA_TOP_N) & allowed, 0.0, NEG)
        pen_t = jnp.concatenate([pen_t, jnp.zeros((LANES - n_blk, tq), F32)], axis=0)
        pen = jnp.transpose(pen_t)
        if g == 0:
            pen = pltpu.roll(pen, 64, axis=1)
        pen = pen.astype(BF16)
        q_sel += [jnp.where(own, qt, pen) for qt in qts]

    slc = _flash(q_sel, lambda h, j: (ks0_ref if h < NSA_HPG else ks1_ref)[0, rows(j), :],
                 lambda h, j: vs_ref[0, rows(j), :], 0, i, t, tq, shift, flash_scratch, masked_loop=True)
    for hg in range(NSA_HPG):
        h0, h1 = hg, NSA_HPG + hg
        y_ref[:, LANES * hg:LANES * (hg + 1)] = jnp.where(
            lo, out_ref[h0] + gate(1, h0) * slc[h0], out_ref[h1] + gate(1, h1) * slc[h1]).astype(BF16)
    y = _dot(ymla_ref[0], wmla_ref[...]) + _dot(y_ref[...], wnsa_ref[...])
    o_ref[0] = x_ref[0] + g_ref[0] * y


def _nsa_attn(layer, shift, q, kc, vc, kslc0, kslc1, kwin, vsw, pg, cover_t, x3, mod, y_mla, w_mla, w_nsa):
    b, s, _ = q.shape
    d = x3.shape[-1]
    t, tq = ATT_TILE, Q_TILE
    tile3 = lambda w: pl.BlockSpec((1, tq, w), lambda bi, i: (bi, i, 0))
    full2 = lambda a: pl.BlockSpec(a.shape, lambda bi, i: (0, 0))
    seq = pl.BlockSpec((1, s, LANES), lambda bi, i: (bi, 0, 0))
    seq_hi = pl.BlockSpec((1, s, LANES), lambda bi, i: (bi, 0, 1))
    cmp = pl.BlockSpec((1, LANES, LANES), lambda bi, i: (bi, 0, 0))
    return pl.pallas_call(
        functools.partial(_nsa_attn_kernel, shift),
        out_shape=jax.ShapeDtypeStruct((b, s, d), F32),
        grid=(b, s // tq),
        in_specs=[tile3(512), cmp, cmp,
                  seq, seq, seq, seq, seq_hi,
                  pl.BlockSpec((1, tq, LANES), lambda bi, i: (bi, i, 1)),
                  full2(cover_t),
                  tile3(d), _gate_spec(layer, b, d), tile3(512), full2(w_mla), full2(w_nsa)],
        out_specs=tile3(d),
        scratch_shapes=([pltpu.VMEM((tq, 512), BF16), pltpu.VMEM((NSA_HEADS, tq, LANES), F32)]
                        + _flash_scratch(NSA_HEADS, tq, t)),
        compiler_params=_cparams("parallel", "arbitrary"),
        name="nsa_attention",
    )(q, kc, vc, kslc0, kslc1, vsw, kwin, vsw, pg, cover_t, x3, mod, y_mla, w_mla, w_nsa)


def _pad_lanes(a, width):
    return jnp.pad(a, [(0, 0)] * (a.ndim - 1) + [(0, width - a.shape[-1])])


def _hy_in_weight(w):
    d = w.shape[0]
    cq, ckv, kpe, nq, nkv, gate = jnp.split(w, [256, 512, 544, 1056, 1824], axis=1)
    nq = nq.reshape(d, NSA_GROUPS, NSA_HPG, NSA_DH).transpose(0, 2, 1, 3).reshape(d, 512)
    kcmp, vcmp, kslc, vslc, kwin, vwin = jnp.split(nkv, 6, axis=1)
    return jnp.concatenate([cq, ckv, nq, kslc, kwin, _pad_lanes(kpe, LANES), _pad_lanes(gate, LANES),
                            kcmp, vcmp, vslc, vwin], axis=1).astype(BF16)


def _cover_t(seq):
    nc, ns = (seq - NSA_CMP_LEN) // NSA_CMP_STRIDE + 1, seq // NSA_SLC_LEN
    assert nc < LANES and ns <= 32 and seq % Q_TILE == 0
    c_start = np.arange(nc) * NSA_CMP_STRIDE
    c_end = c_start + NSA_CMP_LEN - 1
    j_start = np.arange(ns) * NSA_SLC_LEN
    cover = ((c_start[:, None] <= j_start[None, :] + NSA_SLC_LEN - 1) & (c_end[:, None] >= j_start[None, :]))
    out = np.zeros((LANES, LANES), np.float32)
    out[:ns, :nc] = cover.T
    return jnp.asarray(out, BF16)


def _compress_weights(cmp_pos, w1, w2):
    half = NSA_CMP_STRIDE
    w1 = w1.reshape(2, 2, half, NSA_DH, NSA_CMP_HIDDEN)
    eye = jnp.eye(NSA_GROUPS, dtype=F32)
    w1e = jnp.einsum('jcldn,gh->jclgdhn', w1, eye).reshape(2, 2, half * LANES, NSA_GROUPS * NSA_CMP_HIDDEN)
    w2e = jnp.einsum('jnd,gh->jgnhd', w2, eye).reshape(2, NSA_GROUPS * NSA_CMP_HIDDEN, LANES)
    pos = cmp_pos.reshape(2, 2, half, 1, NSA_DH)
    pos = jnp.broadcast_to(pos, (2, 2, half, NSA_GROUPS, NSA_DH)).reshape(2, 2, 1, half * LANES)
    pos = jnp.broadcast_to(pos, (2, 2, 8, half * LANES))
    return pos, w1e[:, 0].astype(BF16), w1e[:, 1].astype(BF16), w2e.astype(BF16)


def kernel(x, c, positions, ada_w, ada_b, norm_mix, norm_mlp, mlp_w1, mlp_w2, hy_w_in, hy_w_out, mla_q_norm, mla_w_uq, mla_kv_norm, mla_w_ukv, mla_q_gain, mla_k_gain, nsa_q_gain, nsa_k_gain, nsa_cmp_pos, nsa_cmp_w1, nsa_cmp_w2, diff_w_qkv, diff_w_out, diff_q_gain, diff_k_gain, diff_lambda, diff_sub_gain):
    nb, seq, d = x.shape
    depth = ada_w.shape[0]
    m = nb * seq
    n_cmp = seq // NSA_CMP_STRIDE

    mod = _adaln(c, ada_w, ada_b).reshape(depth * nb, 1, 6 * d)
    tab_mla, tab_64 = _rope_tables(positions.reshape(m, 1))
    pos_c = jnp.pad(positions[:, NSA_CMP_LEN - 1::NSA_CMP_STRIDE], ((0, 0), (0, 1)))
    _, tab_cmp = _rope_tables(pos_c.reshape(nb * n_cmp, 1))
    tab_cmp = tuple(a.reshape(nb, n_cmp, LANES) for a in tab_cmp)
    cover_t = _cover_t(seq)
    seq3 = lambda a: a.reshape(nb, seq, -1)

    x2 = x.reshape(m, d)
    for i in range(depth):
        j = i // 2
        if i % 2 == 0:
            qg_s = nsa_q_gain[j] * (NSA_DH ** -0.5 * LOG2E)
            qg = jnp.tile(qg_s, 2).reshape(1, LANES)
            kg = jnp.tile(nsa_k_gain[j], (1, 2)).reshape(3, 1, LANES)
            lat, q_nsa, kslc0, kslc1, kwin, pg, kcmp, vcmp, vsw = _hy_in(
                x2, norm_mix[i], mod, i, _hy_in_weight(hy_w_in[j]),
                mla_q_norm[j].reshape(1, -1), mla_kv_norm[j].reshape(1, -1), qg, kg, tab_64, nb)

            wq = _pad_lanes(mla_w_uq[j].reshape(MLA_Q_RANK, MLA_HEADS, MLA_DK), LANES).reshape(MLA_Q_RANK, -1)
            wkv = mla_w_ukv[j].reshape(MLA_KV_RANK, MLA_HEADS, MLA_NOPE + MLA_V)
            wk = jnp.pad(wkv[..., :MLA_NOPE], ((0, 0), (0, 0), (MLA_ROPE, LANES - MLA_DK))).reshape(MLA_KV_RANK, -1)
            wv = wkv[..., MLA_NOPE:].reshape(MLA_KV_RANK, -1)
            q_mla, k_mla, v_mla = _mla_prep(
                lat, pg, wq.astype(BF16), wk.astype(BF16), wv.astype(BF16),
                _pad_lanes(mla_q_gain[j] * LOG2E, LANES).reshape(1, LANES),
                _pad_lanes(mla_k_gain[j] * MLA_DK ** 0.5, LANES).reshape(1, LANES),
                tab_mla)
            y_mla = _shift_dispatch(
                _needs_shift(MLA_DK, mla_q_gain[j] * (MLA_DK ** -0.5 * LOG2E), mla_k_gain[j]), _mla_attn,
                seq3(q_mla), seq3(k_mla), seq3(v_mla))

            pos_e, w1a, w1b, w2e = _compress_weights(nsa_cmp_pos[j], nsa_cmp_w1[j], nsa_cmp_w2[j])
            chunks = lambda a: a.reshape(nb, n_cmp, NSA_CMP_STRIDE * LANES)
            kc, vc = _nsa_compress(chunks(kcmp), chunks(vcmp), pos_e, w1a, w1b, w2e, kg[0], tab_cmp)
            w_out = hy_w_out[j]
            w_nsa = w_out[512:].reshape(NSA_GROUPS, NSA_HPG, NSA_DH, d).transpose(1, 0, 2, 3).reshape(512, d)
            x2 = _shift_dispatch(
                _needs_shift(NSA_DH, qg_s, nsa_k_gain[j][1:]), functools.partial(_nsa_attn, i),
                seq3(q_nsa), kc, vc, seq3(kslc0), seq3(kslc1), seq3(kwin), seq3(vsw), seq3(pg), cover_t,
                seq3(x2), mod, y_mla, w_out[:512].astype(BF16), w_nsa.astype(BF16)).reshape(m, d)
        else:
            lam_init = 0.8 - 0.6 * math.exp(-0.3 * i)
            gains = jnp.stack([jnp.tile(diff_q_gain[j] * (DIFF_DH ** -0.5 * LOG2E), 2), jnp.tile(diff_k_gain[j], 2)])
            q, k, v = _diff_in(x2, norm_mix[i], mod, i, diff_w_qkv[j].astype(BF16),
                               gains.reshape(2, 1, LANES), tab_64, nb)
            x2 = _shift_dispatch(
                _needs_shift(DIFF_DH, gains[0], gains[1]), functools.partial(_diff_attn, lam_init, i),
                seq3(q), seq3(k), seq3(v), diff_lambda[j], diff_sub_gain[j],
                seq3(x2), mod, diff_w_out[j].astype(BF16)).reshape(m, d)
        x2 = _mlp(x2, norm_mlp[i], mod, i, mlp_w1, mlp_w2, nb)
    return x2.reshape(nb, seq, d)
```

```python
import functools
import math

import numpy as np
import jax
import jax.numpy as jnp
from jax import lax
from jax.experimental import pallas as pl
from jax.experimental.pallas import tpu as pltpu

F32 = jnp.float32
BF16 = jnp.bfloat16

LANES = 128
VMEM_LIMIT = 52 * 1024 * 1024

ROPE_THETA = 500000.0
EPS = 1e-6
NEG = -1e30
FORCED = 1e9

MLA_HEADS, MLA_NOPE, MLA_ROPE, MLA_V = 8, 64, 32, 64
MLA_Q_RANK, MLA_KV_RANK = 256, 256
MLA_DK = MLA_ROPE + MLA_NOPE
NSA_HEADS, NSA_GROUPS, NSA_DH = 8, 2, 64
NSA_HPG = NSA_HEADS // NSA_GROUPS
NSA_ROT = NSA_DH // 4
NSA_CMP_LEN, NSA_CMP_STRIDE, NSA_CMP_HIDDEN = 32, 16, 128
NSA_SLC_LEN, NSA_TOP_N, NSA_WINDOW = 64, 16, 512
DIFF_HEADS, DIFF_DH = 8, 64
DIFF_ROT = DIFF_DH // 4

ROW_TILE = 512
ATT_TILE = 256
Q_TILE = 256
MXU_COLS = 256


def _cparams(*sem):
    return pltpu.CompilerParams(dimension_semantics=sem, vmem_limit_bytes=VMEM_LIMIT)


def _split_bf16(x):
    hi = x.astype(BF16)
    lo = (x - hi.astype(F32)).astype(BF16)
    return hi, lo


def _dot(a, b):
    return jnp.dot(a, b, preferred_element_type=F32)


def _dot_nt(a, b):
    return lax.dot_general(a, b, (((1,), (1,)), ((), ())), preferred_element_type=F32)


def _sigmoid(x):
    return 1.0 / (1.0 + jnp.exp(-x))


def _lane(shape):
    return lax.broadcasted_iota(jnp.int32, shape, 1)


def _adaln_kernel(c_ref, w_ref, b_ref, o_ref):
    c = c_ref[...]
    cond = c * _sigmoid(c)
    c_hi, c_lo = _split_bf16(cond)
    w_hi, w_lo = _split_bf16(w_ref[0])
    o_ref[0] = _dot(c_hi, w_hi) + _dot(c_hi, w_lo) + _dot(c_lo, w_hi) + b_ref[0]


def _adaln(c, ada_w, ada_b):
    depth, d, n = ada_w.shape
    b = c.shape[0]
    tn = 1536
    return pl.pallas_call(
        _adaln_kernel,
        out_shape=jax.ShapeDtypeStruct((depth, b, n), F32),
        grid=(depth, n // tn),
        in_specs=[pl.BlockSpec((b, d), lambda i, j: (0, 0)),
                  pl.BlockSpec((1, d, tn), lambda i, j: (i, 0, j)),
                  pl.BlockSpec((1, 1, tn), lambda i, j: (i, 0, j))],
        out_specs=pl.BlockSpec((1, b, tn), lambda i, j: (i, 0, j)),
        compiler_params=_cparams("parallel", "parallel"),
        name="adaln",
    )(c, ada_w, ada_b.reshape(depth, 1, n))


TABLE_SHIFT = 32


def _rope_table_kernel(pos_ref, f_ref, ma_ref, mb_ref, ca_ref, ua_ref, da_ref, cb_ref, ub_ref, db_ref):
    ang = pos_ref[...].astype(F32) * f_ref[...]
    cos, sin = jnp.cos(ang), jnp.sin(ang)
    for m_ref, c_ref, u_ref, d_ref, shift in ((ma_ref, ca_ref, ua_ref, da_ref, 0),
                                              (mb_ref, cb_ref, ub_ref, db_ref, TABLE_SHIFT)):
        c = pltpu.roll(cos, LANES - shift, axis=1) if shift else cos
        s = pltpu.roll(sin, LANES - shift, axis=1) if shift else sin
        c_ref[...] = jnp.where(m_ref[2:3] > 0.0, c, 1.0)
        u_ref[...] = s * m_ref[0:1]
        d_ref[...] = s * m_ref[1:2]


def _lane_freqs(rot, seg):
    half = rot // 2
    lane = np.arange(LANES)
    inv = ROPE_THETA ** (-(np.arange(half, dtype=np.float32) / np.float32(half)))
    f = np.where(lane % seg < rot, inv.astype(np.float32)[lane % half], 0.0)
    return f.reshape(1, LANES).astype(np.float32)


def _rope_masks(rot, seg):
    half = rot // 2
    lane = np.arange(LANES) % seg
    up = ((lane >= half) & (lane < rot)).astype(np.float32)
    dn = -(lane < half).astype(np.float32)
    return jnp.asarray(np.stack([up, dn, (lane < rot).astype(np.float32)]))


def _rope_tables(pos_col):
    rows = pos_col.shape[0]
    tr = min(rows, 2048)
    spec = pl.BlockSpec((tr, LANES), lambda i: (i, 0))
    vec = pl.BlockSpec((1, LANES), lambda i: (0, 0))
    msk = pl.BlockSpec((3, LANES), lambda i: (0, 0))
    shp = jax.ShapeDtypeStruct((rows, LANES), F32)
    f_a, f_b = _lane_freqs(MLA_ROPE, LANES), np.roll(_lane_freqs(NSA_ROT, NSA_DH), TABLE_SHIFT, axis=1)
    assert not np.any((f_a != 0) & (f_b != 0))
    out = pl.pallas_call(
        _rope_table_kernel,
        out_shape=(shp,) * 6,
        grid=(rows // tr,),
        in_specs=[pl.BlockSpec((tr, 1), lambda i: (i, 0)), vec, msk, msk],
        out_specs=(spec,) * 6,
        compiler_params=_cparams("parallel"),
        name="rope_tables",
    )(pos_col, jnp.asarray(f_a + f_b, F32), _rope_masks(MLA_ROPE, LANES), _rope_masks(NSA_ROT, NSA_DH))
    return out[:3], out[3:]


def _rope(y, tab, half):
    c, s_up, s_dn = tab
    up = pltpu.roll(y, half, axis=1)
    dn = pltpu.roll(y, LANES - half, axis=1)
    return y * c + up * s_up + dn * s_dn


def _rms64(x, gain):
    lo = _lane(x.shape) < 64
    x2 = x * x
    s_lo = jnp.sum(jnp.where(lo, x2, 0.0), axis=-1, keepdims=True)
    s_hi = jnp.sum(jnp.where(lo, 0.0, x2), axis=-1, keepdims=True)
    r = jnp.where(lo, lax.rsqrt(s_lo * (1.0 / 64) + EPS), lax.rsqrt(s_hi * (1.0 / 64) + EPS))
    return x * (r * gain)


def _seg_ones(seg):
    lane = np.arange(LANES) // seg
    return jnp.asarray(lane[:, None] == lane[None, :], BF16)


def _rms_seg(x, gain_root, ones, n):
    ss = _dot((x * x).astype(BF16), ones)
    return x * (lax.rsqrt(ss + n * EPS) * gain_root)


def _rms(x, gain):
    return x * lax.rsqrt(jnp.mean(x * x, axis=-1, keepdims=True) + EPS) * gain


def _norm_mod(x, gain, sc, sh):
    return _rms(x, gain) * (1.0 + sc) + sh


def _mod_spec(layer, k, nb, rows_per_batch, tm, d):
    per = rows_per_batch // tm
    return pl.BlockSpec((1, 1, d), lambda i, *_: (layer * nb + i // per, 0, k))


def _mlp_kernel(x_ref, gain_ref, sc_ref, sh_ref, g_ref, w1_ref, w2_ref, o_ref, h_scr, acc_scr):
    j = pl.program_id(1)

    @pl.when(j == 0)
    def _():
        h_scr[...] = _norm_mod(x_ref[...], gain_ref[...], sc_ref[0], sh_ref[0]).astype(BF16)
        acc_scr[...] = jnp.zeros_like(acc_scr)

    a = jnp.maximum(_dot(h_scr[...], w1_ref[0].astype(BF16)), 0.0)
    acc_scr[...] += _dot((a * a).astype(BF16), w2_ref[0].astype(BF16))

    @pl.when(j == pl.num_programs(1) - 1)
    def _():
        o_ref[...] = x_ref[...] + g_ref[0] * acc_scr[...]


def _mlp(x2, gain, mod, layer, w1, w2, nb):
    m, d = x2.shape
    ff = w1.shape[2]
    tm, tf = 1024, 1024
    s = m // nb
    return pl.pallas_call(
        _mlp_kernel,
        out_shape=jax.ShapeDtypeStruct((m, d), F32),
        grid=(m // tm, ff // tf),
        in_specs=[pl.BlockSpec((tm, d), lambda i, j: (i, 0)),
                  pl.BlockSpec((1, d), lambda i, j: (0, 0)),
                  _mod_spec(layer, 4, nb, s, tm, d),
                  _mod_spec(layer, 3, nb, s, tm, d),
                  _mod_spec(layer, 5, nb, s, tm, d),
                  pl.BlockSpec((1, d, tf), lambda i, j: (layer, 0, j)),
                  pl.BlockSpec((1, tf, d), lambda i, j: (layer, j, 0))],
        out_specs=pl.BlockSpec((tm, d), lambda i, j: (i, 0)),
        scratch_shapes=[pltpu.VMEM((tm, d), BF16), pltpu.VMEM((tm, d), F32)],
        compiler_params=_cparams("parallel", "arbitrary"),
        name="relu2_mlp",
    )(x2, gain.reshape(1, d), mod, mod, mod, w1, w2)


LOG2E = math.log2(math.e)


SCORE_BOUND = 60.0
BOUND_MARGIN = 1.02


def _needs_shift(dk, q_gain, k_gain):
    bound = dk * jnp.max(jnp.abs(q_gain)) * jnp.max(jnp.abs(k_gain)) * BOUND_MARGIN
    return jnp.logical_not(bound <= SCORE_BOUND)


def _shift_dispatch(needs_shift, attn, *args):
    return lax.cond(needs_shift, functools.partial(attn, True), functools.partial(attn, False), *args)


def _lane_tiles(x):
    return [x[:, LANES * c:LANES * (c + 1)] for c in range(x.shape[1] // LANES)]


def _flash_scratch(n, rows, t):
    return [pltpu.VMEM((n, rows, LANES), F32), pltpu.VMEM((n, rows, LANES), F32)]


def _flash(qs, k_at, v_at, first, i, t, rows_per_map, shift, scratch, window=None, masked_loop=False):
    n = len(qs)
    rows = qs[0].shape[0]
    n_diag = rows_per_map // t
    last = (i + 1) * n_diag - 1
    r = lax.broadcasted_iota(jnp.int32, (rows, t), 0)
    if rows != rows_per_map:
        r = jnp.where(r >= rows_per_map, r - rows_per_map, r)
    c_minus_r = lax.broadcasted_iota(jnp.int32, (rows, t), 1) - r

    assert masked_loop or window is None

    def visible(j):
        ahead = i * rows_per_map - j * t
        ok = c_minus_r <= ahead
        return ok if window is None else ok & (c_minus_r > ahead - window)

    def scores(h, j, ok):
        s = _dot_nt(qs[h], k_at(h, j))
        return s if ok is None else jnp.where(ok, s, NEG)

    def sweep(step, carry):
        if masked_loop:
            return lax.fori_loop(first, last + 1, lambda j, cr: step(j, cr, visible(j)), carry)
        carry = lax.fori_loop(first, i * n_diag, lambda j, cr: step(j, cr, None), carry)
        for d in range(n_diag):
            carry = step(i * n_diag + d, carry, visible(i * n_diag + d))
        return carry

    def max_step(j, ms, ok):
        out = []
        for h in range(n):
            m = ms[h]
            for s_c in _lane_tiles(scores(h, j, ok)):
                m = jnp.maximum(m, s_c)
            out.append(m)
        return tuple(out)

    if shift:
        ms = sweep(max_step, tuple(jnp.full((rows, LANES), NEG, F32) for _ in range(n)))
        ms = [jnp.broadcast_to(jnp.max(m, axis=-1, keepdims=True), (rows, LANES)) for m in ms]

    l_ref, acc_ref = scratch
    for h in range(n):
        l_ref[h] = jnp.zeros((rows, LANES), F32)
        acc_ref[h] = jnp.zeros((rows, LANES), F32)

    def sum_step(j, p_prev, ok):
        p_new = []
        for h in range(n):
            acc_ref[h] += _dot(p_prev[h], v_at(h, jnp.maximum(j - 1, 0)))
            ps = _lane_tiles(scores(h, j, ok))
            ps = [jnp.exp2(s_c - ms[h]) if shift else jnp.exp2(s_c) for s_c in ps]
            l = l_ref[h]
            for p_c in ps:
                l = l + p_c
            l_ref[h] = l
            p_new.append(jnp.concatenate(ps, axis=1).astype(BF16))
        return tuple(p_new)

    p = sweep(sum_step, tuple(jnp.zeros((rows, t), BF16) for _ in range(n)))
    return [(acc_ref[h] + _dot(p[h], v_at(h, last))) / jnp.sum(l_ref[h], axis=-1, keepdims=True)
            for h in range(n)]


def _hy_in_kernel(seq, x_ref, gain_ref, sc_ref, sh_ref, w_ref, qn_ref, kvn_ref, qg_ref, kg_ref,
                  c_ref, su_ref, sd_ref,
                  lat_ref, q_ref, ks0_ref, ks1_ref, kw_ref, pg_ref, kcmp_ref, vcmp_ref, vsw_ref):
    tm = x_ref.shape[0]
    h = _norm_mod(x_ref[...], gain_ref[...], sc_ref[0], sh_ref[0]).astype(BF16)
    tab = (c_ref[...], su_ref[...], sd_ref[...])

    def cols(a, b):
        return _dot(h, w_ref[:, a:b])

    def prep(x, g):
        return _rope(_rms64(x, g), tab, NSA_ROT // 2)

    lat_ref[:, 0:256] = _rms(cols(0, 256), qn_ref[...]).astype(BF16)
    lat_ref[:, 256:512] = _rms(cols(256, 512), kvn_ref[...]).astype(BF16)
    for ch in range(2):
        y = cols(512 + MXU_COLS * ch, 512 + MXU_COLS * (ch + 1))
        for e in range(2):
            q_ref[:, MXU_COLS * ch + LANES * e:MXU_COLS * ch + LANES * (e + 1)] = prep(
                y[:, LANES * e:LANES * (e + 1)], qg_ref[...]).astype(BF16)
    y = cols(1024, 1280)
    kslc = prep(y[:, 0:LANES], kg_ref[1])
    lane = _lane((tm, LANES))
    t_idx = (pl.program_id(0) * tm + lax.broadcasted_iota(jnp.int32, (tm, LANES), 0)) & (seq - 1)
    blk = t_idx >> 6
    ks0_ref[...] = jnp.where(lane < 64, kslc, jnp.where(lane - 64 == blk, 1.0, 0.0)).astype(BF16)
    ks1_ref[...] = jnp.where(lane >= 64, kslc, jnp.where(lane == blk, 1.0, 0.0)).astype(BF16)
    kw_ref[...] = prep(y[:, LANES:2 * LANES], kg_ref[2]).astype(BF16)
    pg_ref[...] = cols(1280, 1536)
    y = cols(1536, 1792)
    kcmp_ref[...] = y[:, 0:LANES]
    vcmp_ref[...] = y[:, LANES:2 * LANES]
    vsw_ref[...] = cols(1792, 2048).astype(BF16)


def _hy_in(x2, gain, mod, layer, w, q_norm, kv_norm, q_gain, k_gain, tab, nb):
    m, d = x2.shape
    tm = ROW_TILE
    seq = m // nb
    assert seq & (seq - 1) == 0 and seq // NSA_SLC_LEN <= 32 and seq % tm == 0
    row = lambda wd: pl.BlockSpec((tm, wd), lambda i: (i, 0))
    full = lambda a: pl.BlockSpec(a.shape, lambda i: (0,) * a.ndim)
    bf = lambda wd: jax.ShapeDtypeStruct((m, wd), BF16)
    widths = (512, 512, LANES, LANES, LANES, 2 * LANES, LANES, LANES, 2 * LANES)
    shapes = tuple(jax.ShapeDtypeStruct((m, wd), F32) if k in (5, 6, 7) else bf(wd) for k, wd in enumerate(widths))
    gain = gain.reshape(1, d)
    return pl.pallas_call(
        functools.partial(_hy_in_kernel, seq),
        out_shape=shapes,
        grid=(m // tm,),
        in_specs=[row(d), full(gain), _mod_spec(layer, 1, nb, seq, tm, d), _mod_spec(layer, 0, nb, seq, tm, d),
                  full(w), full(q_norm), full(kv_norm), full(q_gain), full(k_gain),
                  row(LANES), row(LANES), row(LANES)],
        out_specs=tuple(row(wd) for wd in widths),
        compiler_params=_cparams("parallel"),
        name="hy_in_proj",
    )(x2, gain, mod, mod, w, q_norm, kv_norm, q_gain, k_gain, *tab)


def _mla_prep_kernel(lat_ref, kpe_ref, wq_ref, wk_ref, wv_ref, qg_ref, kg_ref, c_ref, su_ref, sd_ref, ones_ref,
                     q_ref, k_ref, v_ref):
    cq, ckv = lat_ref[:, 0:256], lat_ref[:, 256:512]
    kpe = kpe_ref[...]
    tab = (c_ref[...], su_ref[...], sd_ref[...])
    v_ref[...] = _dot(ckv, wv_ref[...]).astype(BF16)

    def head(x, g):
        return _rope(_rms_seg(x, g, ones_ref[...], MLA_DK), tab, MLA_ROPE // 2)

    for ch in range(MLA_HEADS // 2):
        cs = slice(MXU_COLS * ch, MXU_COLS * (ch + 1))
        q2, k2 = _dot(cq, wq_ref[:, cs]), _dot(ckv, wk_ref[:, cs])
        for e in range(2):
            sl = slice(MXU_COLS * ch + LANES * e, MXU_COLS * ch + LANES * (e + 1))
            es = slice(LANES * e, LANES * (e + 1))
            q_ref[:, sl] = head(q2[:, es], qg_ref[...]).astype(BF16)
            k_ref[:, sl] = head(k2[:, es] + kpe, kg_ref[...]).astype(BF16)


def _mla_prep(lat, pg, wq, wk, wv, q_gain, k_gain, tab):
    m = lat.shape[0]
    tm = ROW_TILE
    ones = _seg_ones(LANES)
    full = lambda a: pl.BlockSpec(a.shape, lambda i: (0, 0))
    row = lambda w: pl.BlockSpec((tm, w), lambda i: (i, 0))
    return pl.pallas_call(
        _mla_prep_kernel,
        out_shape=(jax.ShapeDtypeStruct((m, 1024), BF16), jax.ShapeDtypeStruct((m, 1024), BF16),
                   jax.ShapeDtypeStruct((m, 512), BF16)),
        grid=(m // tm,),
        in_specs=[row(512), row(LANES), full(wq), full(wk), full(wv), full(q_gain), full(k_gain),
                  row(LANES), row(LANES), row(LANES), full(ones)],
        out_specs=(row(1024), row(1024), row(512)),
        compiler_params=_cparams("parallel"),
        name="mla_prep",
    )(lat, pg, wq, wk, wv, q_gain, k_gain, *tab, ones)


def _mla_attn_kernel(shift, q_ref, k_ref, v_ref, o_ref, *scratch):
    i = pl.program_id(1)
    t, tq = ATT_TILE, Q_TILE
    lo = _lane((tq, LANES)) < 64

    def rows(j):
        return pl.ds(pl.multiple_of(j * t, t), t)

    def tile(h):
        return slice(LANES * h, LANES * (h + 1))

    outs = _flash([q_ref[0, :, tile(h)] for h in range(MLA_HEADS)],
                  lambda h, j: k_ref[0, rows(j), tile(h)],
                  lambda h, j: v_ref[0, rows(j), tile(h // 2)], 0, i, t, tq, shift, scratch)
    for p in range(MLA_HEADS // 2):
        o_ref[0, :, tile(p)] = jnp.where(lo, outs[2 * p], outs[2 * p + 1]).astype(BF16)


def _mla_attn(shift, q, k, v):
    b, s, _ = q.shape
    t, tq = ATT_TILE, Q_TILE
    return pl.pallas_call(
        functools.partial(_mla_attn_kernel, shift),
        out_shape=jax.ShapeDtypeStruct((b, s, 512), BF16),
        grid=(b, s // tq),
        in_specs=[pl.BlockSpec((1, tq, 1024), lambda bi, i: (bi, i, 0)),
                  pl.BlockSpec((1, s, 1024), lambda bi, i: (bi, 0, 0)),
                  pl.BlockSpec((1, s, 512), lambda bi, i: (bi, 0, 0))],
        out_specs=pl.BlockSpec((1, tq, 512), lambda bi, i: (bi, i, 0)),
        scratch_shapes=_flash_scratch(MLA_HEADS, tq, t),
        compiler_params=_cparams("parallel", "arbitrary"),
        name="mla_attention",
    )(q, k, v)


def _diff_in_kernel(x_ref, gain_ref, sc_ref, sh_ref, w_ref, g_ref, c_ref, su_ref, sd_ref, q_ref, k_ref, v_ref):
    h = _norm_mod(x_ref[...], gain_ref[...], sc_ref[0], sh_ref[0]).astype(BF16)
    tab = (c_ref[...], su_ref[...], sd_ref[...])
    n = DIFF_HEADS * LANES
    for which, out in ((0, q_ref), (1, k_ref)):
        g = g_ref[which]
        for ch in range(n // MXU_COLS):
            y = _dot(h, w_ref[:, which * n + MXU_COLS * ch:which * n + MXU_COLS * (ch + 1)])
            for e in range(MXU_COLS // LANES):
                out[:, MXU_COLS * ch + LANES * e:MXU_COLS * ch + LANES * (e + 1)] = _rope(
                    _rms64(y[:, LANES * e:LANES * (e + 1)], g), tab, DIFF_ROT // 2).astype(BF16)
    for ch in range(n // MXU_COLS):
        cs = slice(MXU_COLS * ch, MXU_COLS * (ch + 1))
        v_ref[:, cs] = _dot(h, w_ref[:, 2 * n + MXU_COLS * ch:2 * n + MXU_COLS * (ch + 1)]).astype(BF16)


def _diff_in(x2, gain, mod, layer, w, gains, tab, nb):
    m, d = x2.shape
    tm = ROW_TILE
    seq = m // nb
    row = lambda wd: pl.BlockSpec((tm, wd), lambda i: (i, 0))
    full = lambda a: pl.BlockSpec(a.shape, lambda i: (0,) * a.ndim)
    shp = jax.ShapeDtypeStruct((m, DIFF_HEADS * LANES), BF16)
    gain = gain.reshape(1, d)
    return pl.pallas_call(
        _diff_in_kernel,
        out_shape=(shp, shp, shp),
        grid=(m // tm,),
        in_specs=[row(d), full(gain), _mod_spec(layer, 1, nb, seq, tm, d), _mod_spec(layer, 0, nb, seq, tm, d),
                  full(w), full(gains), row(LANES), row(LANES), row(LANES)],
        out_specs=(row(DIFF_HEADS * LANES),) * 3,
        compiler_params=_cparams("parallel"),
        name="diff_in_proj",
    )(x2, gain, mod, mod, w, gains, *tab)


def _diff_attn_kernel(lam_init, shift, q_ref, k_ref, v_ref, lam_ref, sg_ref, x_ref, g_ref, w_ref,
                      o_ref, y_ref, *scratch):
    i = pl.program_id(1)
    t = ATT_TILE
    lo = _lane((t, LANES)) < 64
    lam = lam_ref[...]
    lmb = (jnp.exp(jnp.sum(lam[0:1] * lam[1:2], axis=-1, keepdims=True))
           - jnp.exp(jnp.sum(lam[2:3] * lam[3:4], axis=-1, keepdims=True)) + lam_init)

    def rows(j):
        return pl.ds(pl.multiple_of(j * t, t), t)

    def tile(h):
        return slice(LANES * h, LANES * (h + 1))

    def both_maps(h):
        qt = q_ref[0, :, tile(h)]
        zero = jnp.zeros_like(qt)
        return jnp.concatenate([jnp.where(lo, qt, zero), jnp.where(lo, zero, qt)], axis=0)

    outs = _flash([both_maps(h) for h in range(DIFF_HEADS)],
                  lambda h, j: k_ref[0, rows(j), tile(h)],
                  lambda h, j: v_ref[0, rows(j), tile(h)], 0, i, t, t, shift, scratch)
    for h in range(DIFF_HEADS):
        o = outs[h][:t] - lmb * outs[h][t:]
        o = o * lax.rsqrt(jnp.mean(o * o, axis=-1, keepdims=True) + EPS) * sg_ref[...]
        y_ref[:, tile(h)] = (o * (1.0 - lam_init)).astype(BF16)
    o_ref[0] = x_ref[0] + g_ref[0] * _dot(y_ref[...], w_ref[...])


def _gate_spec(layer, nb, d):
    return pl.BlockSpec((1, 1, d), lambda bi, i: (layer * nb + bi, 0, 2))


def _diff_attn(lam_init, layer, shift, q, k, v, lam, sub_gain, x3, mod, w_out):
    b, s, n = q.shape
    d = x3.shape[-1]
    t = ATT_TILE
    full = pl.BlockSpec((1, s, n), lambda bi, i: (bi, 0, 0))
    return pl.pallas_call(
        functools.partial(_diff_attn_kernel, lam_init, shift),
        out_shape=jax.ShapeDtypeStruct((b, s, d), F32),
        grid=(b, s // t),
        in_specs=[pl.BlockSpec((1, t, n), lambda bi, i: (bi, i, 0)), full, full,
                  pl.BlockSpec(lam.shape, lambda bi, i: (0, 0)),
                  pl.BlockSpec((1, LANES), lambda bi, i: (0, 0)),
                  pl.BlockSpec((1, t, d), lambda bi, i: (bi, i, 0)), _gate_spec(layer, b, d),
                  pl.BlockSpec(w_out.shape, lambda bi, i: (0, 0))],
        out_specs=pl.BlockSpec((1, t, d), lambda bi, i: (bi, i, 0)),
        scratch_shapes=[pltpu.VMEM((t, n), BF16)] + _flash_scratch(DIFF_HEADS, 2 * t, t),
        compiler_params=_cparams("parallel", "arbitrary"),
        name="diff_attention",
    )(q, k, v, lam, sub_gain.reshape(1, LANES), x3, mod, w_out)


def _nsa_compress_kernel(tk_ref, tv_ref, pos_ref, w1a_ref, w1b_ref, w2_ref, kg_ref,
                         c_ref, su_ref, sd_ref, kc_ref, vc_ref):
    n_rows = tk_ref.shape[1] // NSA_CMP_STRIDE
    for j, (t_ref, out) in enumerate(((tk_ref, kc_ref), (tv_ref, vc_ref))):
        p_hi, p_lo = _split_bf16(pos_ref[j])
        w1a, w1b = w1a_ref[j], w1b_ref[j]
        bias = (_dot(p_hi[0], w1a) + _dot(p_lo[0], w1a) + _dot(p_hi[1], w1b) + _dot(p_lo[1], w1b))[0:1]
        first = jnp.zeros((n_rows, NSA_GROUPS * NSA_CMP_HIDDEN), F32)
        second = jnp.zeros((n_rows, NSA_GROUPS * NSA_CMP_HIDDEN), F32)
        for l in range(NSA_CMP_STRIDE):
            tok = t_ref[0, pl.ds(l, n_rows, stride=NSA_CMP_STRIDE), :].astype(BF16)
            first = first + _dot(tok, w1a[LANES * l:LANES * (l + 1)])
            second = second + _dot(tok, w1b[LANES * l:LANES * (l + 1)])
        hid = first + pltpu.roll(second, n_rows - 1, axis=0) + bias
        act = jax.nn.gelu(hid, approximate=True)
        cmp = _dot(act.astype(BF16), w2_ref[j])
        if j == 0:
            cmp = _rope(_rms64(cmp, kg_ref[...]), (c_ref[0], su_ref[0], sd_ref[0]), NSA_ROT // 2)
        out[0] = cmp.astype(BF16)


def _nsa_compress(tk, tv, pos, w1a, w1b, w2, k_gain0, tab):
    b, s, w = tk.shape
    nr = s // NSA_CMP_STRIDE
    full = lambda a: pl.BlockSpec(a.shape, lambda bi: (0,) * a.ndim)
    per_b = lambda a: pl.BlockSpec((1,) + a.shape[1:], lambda bi: (bi,) + (0,) * (a.ndim - 1))
    shp = jax.ShapeDtypeStruct((b, nr, LANES), BF16)
    return pl.pallas_call(
        _nsa_compress_kernel,
        out_shape=(shp, shp),
        grid=(b,),
        in_specs=[per_b(tk), per_b(tv), full(pos), full(w1a), full(w1b), full(w2), full(k_gain0),
                  per_b(tab[0]), per_b(tab[1]), per_b(tab[2])],
        out_specs=(pl.BlockSpec((1, nr, LANES), lambda bi: (bi, 0, 0)),) * 2,
        compiler_params=_cparams("parallel"),
        name="nsa_compress",
    )(tk, tv, pos, w1a, w1b, w2, k_gain0, *tab)


def _nsa_attn_kernel(shift, q_ref, kc_ref, vc_ref, ks0_ref, ks1_ref, vs_ref, kw_ref, vw_ref, gate_ref, cov_ref,
                     x_ref, g_ref, ymla_ref, wmla_ref, wnsa_ref, o_ref, y_ref, *scratch):
    i = pl.program_id(1)
    t, tq = ATT_TILE, Q_TILE
    n_blk = ks0_ref.shape[1] // NSA_SLC_LEN
    q0 = i * tq
    lane = _lane((tq, LANES))
    lo = lane < 64
    qpos = q0 + lax.broadcasted_iota(jnp.int32, (tq, LANES), 0)
    cmp_ok = NSA_CMP_STRIDE * lane + (NSA_CMP_LEN - 1) <= qpos
    gates = _sigmoid(gate_ref[0])

    def rows(j):
        return pl.ds(pl.multiple_of(j * t, t), t)

    def gate(branch, h):
        col = branch * NSA_HEADS + h
        return gates[:, col:col + 1]

    out_ref, flash_scratch = scratch[0], scratch[1:]
    kc, vc = kc_ref[0], vc_ref[0]
    qts = [q_ref[0, :, LANES * hg:LANES * (hg + 1)] for hg in range(NSA_HPG)]
    owns = [lo, jnp.logical_not(lo)]
    qs = [jnp.where(own, qt, jnp.zeros_like(qt)) for own in owns for qt in qts]

    win = _flash(qs, lambda h, j: kw_ref[0, rows(j), :], lambda h, j: vw_ref[0, rows(j), :],
                 jnp.maximum((i * tq - NSA_WINDOW) // t, 0), i, t, tq, shift, flash_scratch, window=NSA_WINDOW,
                 masked_loop=True)
    for h in range(NSA_HEADS):
        out_ref[h] = gate(2, h) * win[h]

    q_sel = []
    for g in range(NSA_GROUPS):
        own = owns[g]

        p_sum = jnp.zeros((tq, LANES), F32)
        for hg in range(NSA_HPG):
            h = g * NSA_HPG + hg
            sc = jnp.where(cmp_ok, _dot_nt(qs[h], kc), NEG)
            p = jnp.where(cmp_ok, jnp.exp2(sc - jnp.max(sc, axis=-1, keepdims=True)), 0.0)
            l = jnp.sum(p, axis=-1, keepdims=True)
            p = p * jnp.where(l > 0.0, 1.0 / l, 0.0)
            p_sum = p_sum + p
            out_ref[h] += gate(0, h) * _dot(p.astype(BF16), vc)
        p_hi, p_lo = _split_bf16(p_sum)
        imp = (_dot_nt(cov_ref[...], p_hi) + _dot_nt(cov_ref[...], p_lo))[0:n_blk]
        blk = lax.broadcasted_iota(jnp.int32, (n_blk, tq), 0)
        jt = (q0 + lax.broadcasted_iota(jnp.int32, (n_blk, tq), 1)) >> 6
        allowed = blk <= jt
        forced = allowed & ((blk == 0) | (blk >= jt - 1))
        imp = jnp.where(forced, FORCED, jnp.where(allowed, imp, NEG))
        rank = jnp.zeros((n_blk, tq), jnp.int32)
        for jp in range(n_blk):
            other = imp[jp:jp + 1, :]
            ahead = (other > imp) | ((other == imp) & (blk > jp))
            rank = rank + ahead.astype(jnp.int32)
        pen_t = jnp.where((rank < NSA_TOP_N) & allowed, 0.0, NEG)
        pen_t = jnp.concatenate([pen_t, jnp.zeros((LANES - n_blk, tq), F32)], axis=0)
        pen = jnp.transpose(pen_t)
        if g == 0:
            pen = pltpu.roll(pen, 64, axis=1)
        pen = pen.astype(BF16)
        q_sel += [jnp.where(own, qt, pen) for qt in qts]

    slc = _flash(q_sel, lambda h, j: (ks0_ref if h < NSA_HPG else ks1_ref)[0, rows(j), :],
                 lambda h, j: vs_ref[0, rows(j), :], 0, i, t, tq, shift, flash_scratch, masked_loop=True)
    for hg in range(NSA_HPG):
        h0, h1 = hg, NSA_HPG + hg
        y_ref[:, LANES * hg:LANES * (hg + 1)] = jnp.where(
            lo, out_ref[h0] + gate(1, h0) * slc[h0], out_ref[h1] + gate(1, h1) * slc[h1]).astype(BF16)
    y = _dot(ymla_ref[0], wmla_ref[...]) + _dot(y_ref[...], wnsa_ref[...])
    o_ref[0] = x_ref[0] + g_ref[0] * y


def _nsa_attn(layer, shift, q, kc, vc, kslc0, kslc1, kwin, vsw, pg, cover_t, x3, mod, y_mla, w_mla, w_nsa):
    b, s, _ = q.shape
    d = x3.shape[-1]
    t, tq = ATT_TILE, Q_TILE
    tile3 = lambda w: pl.BlockSpec((1, tq, w), lambda bi, i: (bi, i, 0))
    full2 = lambda a: pl.BlockSpec(a.shape, lambda bi, i: (0, 0))
    seq = pl.BlockSpec((1, s, LANES), lambda bi, i: (bi, 0, 0))
    seq_hi = pl.BlockSpec((1, s, LANES), lambda bi, i: (bi, 0, 1))
    cmp = pl.BlockSpec((1, LANES, LANES), lambda bi, i: (bi, 0, 0))
    return pl.pallas_call(
        functools.partial(_nsa_attn_kernel, shift),
        out_shape=jax.ShapeDtypeStruct((b, s, d), F32),
        grid=(b, s // tq),
        in_specs=[tile3(512), cmp, cmp,
                  seq, seq, seq, seq, seq_hi,
                  pl.BlockSpec((1, tq, LANES), lambda bi, i: (bi, i, 1)),
                  full2(cover_t),
                  tile3(d), _gate_spec(layer, b, d), tile3(512), full2(w_mla), full2(w_nsa)],
        out_specs=tile3(d),
        scratch_shapes=([pltpu.VMEM((tq, 512), BF16), pltpu.VMEM((NSA_HEADS, tq, LANES), F32)]
                        + _flash_scratch(NSA_HEADS, tq, t)),
        compiler_params=_cparams("parallel", "arbitrary"),
        name="nsa_attention",
    )(q, kc, vc, kslc0, kslc1, vsw, kwin, vsw, pg, cover_t, x3, mod, y_mla, w_mla, w_nsa)


def _pad_lanes(a, width):
    return jnp.pad(a, [(0, 0)] * (a.ndim - 1) + [(0, width - a.shape[-1])])


def _hy_in_weight(w):
    d = w.shape[0]
    cq, ckv, kpe, nq, nkv, gate = jnp.split(w, [256, 512, 544, 1056, 1824], axis=1)
    nq = nq.reshape(d, NSA_GROUPS, NSA_HPG, NSA_DH).transpose(0, 2, 1, 3).reshape(d, 512)
    kcmp, vcmp, kslc, vslc, kwin, vwin = jnp.split(nkv, 6, axis=1)
    return jnp.concatenate([cq, ckv, nq, kslc, kwin, _pad_lanes(kpe, LANES), _pad_lanes(gate, LANES),
                            kcmp, vcmp, vslc, vwin], axis=1).astype(BF16)


def _cover_t(seq):
    nc, ns = (seq - NSA_CMP_LEN) // NSA_CMP_STRIDE + 1, seq // NSA_SLC_LEN
    assert nc < LANES and ns <= 32 and seq % Q_TILE == 0
    c_start = np.arange(nc) * NSA_CMP_STRIDE
    c_end = c_start + NSA_CMP_LEN - 1
    j_start = np.arange(ns) * NSA_SLC_LEN
    cover = ((c_start[:, None] <= j_start[None, :] + NSA_SLC_LEN - 1) & (c_end[:, None] >= j_start[None, :]))
    out = np.zeros((LANES, LANES), np.float32)
    out[:ns, :nc] = cover.T
    return jnp.asarray(out, BF16)


def _compress_weights(cmp_pos, w1, w2):
    half = NSA_CMP_STRIDE
    w1 = w1.reshape(2, 2, half, NSA_DH, NSA_CMP_HIDDEN)
    eye = jnp.eye(NSA_GROUPS, dtype=F32)
    w1e = jnp.einsum('jcldn,gh->jclgdhn', w1, eye).reshape(2, 2, half * LANES, NSA_GROUPS * NSA_CMP_HIDDEN)
    w2e = jnp.einsum('jnd,gh->jgnhd', w2, eye).reshape(2, NSA_GROUPS * NSA_CMP_HIDDEN, LANES)
    pos = cmp_pos.reshape(2, 2, half, 1, NSA_DH)
    pos = jnp.broadcast_to(pos, (2, 2, half, NSA_GROUPS, NSA_DH)).reshape(2, 2, 1, half * LANES)
    pos = jnp.broadcast_to(pos, (2, 2, 8, half * LANES))
    return pos, w1e[:, 0].astype(BF16), w1e[:, 1].astype(BF16), w2e.astype(BF16)


def kernel(x, c, positions, ada_w, ada_b, norm_mix, norm_mlp, mlp_w1, mlp_w2, hy_w_in, hy_w_out, mla_q_norm, mla_w_uq, mla_kv_norm, mla_w_ukv, mla_q_gain, mla_k_gain, nsa_q_gain, nsa_k_gain, nsa_cmp_pos, nsa_cmp_w1, nsa_cmp_w2, diff_w_qkv, diff_w_out, diff_q_gain, diff_k_gain, diff_lambda, diff_sub_gain):
    nb, seq, d = x.shape
    depth = ada_w.shape[0]
    m = nb * seq
    n_cmp = seq // NSA_CMP_STRIDE

    mod = _adaln(c, ada_w, ada_b).reshape(depth * nb, 1, 6 * d)
    tab_mla, tab_64 = _rope_tables(positions.reshape(m, 1))
    pos_c = jnp.pad(positions[:, NSA_CMP_LEN - 1::NSA_CMP_STRIDE], ((0, 0), (0, 1)))
    _, tab_cmp = _rope_tables(pos_c.reshape(nb * n_cmp, 1))
    tab_cmp = tuple(a.reshape(nb, n_cmp, LANES) for a in tab_cmp)
    cover_t = _cover_t(seq)
    seq3 = lambda a: a.reshape(nb, seq, -1)

    x2 = x.reshape(m, d)
    for i in range(depth):
        j = i // 2
        if i % 2 == 0:
            qg_s = nsa_q_gain[j] * (NSA_DH ** -0.5 * LOG2E)
            qg = jnp.tile(qg_s, 2).reshape(1, LANES)
            kg = jnp.tile(nsa_k_gain[j], (1, 2)).reshape(3, 1, LANES)
            lat, q_nsa, kslc0, kslc1, kwin, pg, kcmp, vcmp, vsw = _hy_in(
                x2, norm_mix[i], mod, i, _hy_in_weight(hy_w_in[j]),
                mla_q_norm[j].reshape(1, -1), mla_kv_norm[j].reshape(1, -1), qg, kg, tab_64, nb)

            wq = _pad_lanes(mla_w_uq[j].reshape(MLA_Q_RANK, MLA_HEADS, MLA_DK), LANES).reshape(MLA_Q_RANK, -1)
            wkv = mla_w_ukv[j].reshape(MLA_KV_RANK, MLA_HEADS, MLA_NOPE + MLA_V)
            wk = jnp.pad(wkv[..., :MLA_NOPE], ((0, 0), (0, 0), (MLA_ROPE, LANES - MLA_DK))).reshape(MLA_KV_RANK, -1)
            wv = wkv[..., MLA_NOPE:].reshape(MLA_KV_RANK, -1)
            q_mla, k_mla, v_mla = _mla_prep(
                lat, pg, wq.astype(BF16), wk.astype(BF16), wv.astype(BF16),
                _pad_lanes(mla_q_gain[j] * LOG2E, LANES).reshape(1, LANES),
                _pad_lanes(mla_k_gain[j] * MLA_DK ** 0.5, LANES).reshape(1, LANES),
                tab_mla)
            y_mla = _shift_dispatch(
                _needs_shift(MLA_DK, mla_q_gain[j] * (MLA_DK ** -0.5 * LOG2E), mla_k_gain[j]), _mla_attn,
                seq3(q_mla), seq3(k_mla), seq3(v_mla))

            pos_e, w1a, w1b, w2e = _compress_weights(nsa_cmp_pos[j], nsa_cmp_w1[j], nsa_cmp_w2[j])
            kc, vc = _nsa_compress(seq3(kcmp), seq3(vcmp), pos_e, w1a, w1b, w2e, kg[0], tab_cmp)
            w_out = hy_w_out[j]
            w_nsa = w_out[512:].reshape(NSA_GROUPS, NSA_HPG, NSA_DH, d).transpose(1, 0, 2, 3).reshape(512, d)
            x2 = _shift_dispatch(
                _needs_shift(NSA_DH, qg_s, nsa_k_gain[j][1:]), functools.partial(_nsa_attn, i),
                seq3(q_nsa), kc, vc, seq3(kslc0), seq3(kslc1), seq3(kwin), seq3(vsw), seq3(pg), cover_t,
                seq3(x2), mod, y_mla, w_out[:512].astype(BF16), w_nsa.astype(BF16)).reshape(m, d)
        else:
            lam_init = 0.8 - 0.6 * math.exp(-0.3 * i)
            gains = jnp.stack([jnp.tile(diff_q_gain[j] * (DIFF_DH ** -0.5 * LOG2E), 2), jnp.tile(diff_k_gain[j], 2)])
            q, k, v = _diff_in(x2, norm_mix[i], mod, i, diff_w_qkv[j].astype(BF16),
                               gains.reshape(2, 1, LANES), tab_64, nb)
            x2 = _shift_dispatch(
                _needs_shift(DIFF_DH, gains[0], gains[1]), functools.partial(_diff_attn, lam_init, i),
                seq3(q), seq3(k), seq3(v), diff_lambda[j], diff_sub_gain[j],
                seq3(x2), mod, diff_w_out[j].astype(BF16)).reshape(m, d)
        x2 = _mlp(x2, norm_mlp[i], mod, i, mlp_w1, mlp_w2, nb)
    return x2.reshape(nb, seq, d)
```

```python
import functools
import math

import numpy as np
import jax
import jax.numpy as jnp
from jax import lax
from jax.experimental import pallas as pl
from jax.experimental.pallas import tpu as pltpu

F32 = jnp.float32
BF16 = jnp.bfloat16

LANES = 128
VMEM_LIMIT = 52 * 1024 * 1024

ROPE_THETA = 500000.0
EPS = 1e-6
NEG = -1e30
FORCED = 1e9

MLA_HEADS, MLA_NOPE, MLA_ROPE, MLA_V = 8, 64, 32, 64
MLA_Q_RANK, MLA_KV_RANK = 256, 256
MLA_DK = MLA_ROPE + MLA_NOPE
NSA_HEADS, NSA_GROUPS, NSA_DH = 8, 2, 64
NSA_HPG = NSA_HEADS // NSA_GROUPS
NSA_ROT = NSA_DH // 4
NSA_CMP_LEN, NSA_CMP_STRIDE, NSA_CMP_HIDDEN = 32, 16, 128
NSA_SLC_LEN, NSA_TOP_N, NSA_WINDOW = 64, 16, 512
DIFF_HEADS, DIFF_DH = 8, 64
DIFF_ROT = DIFF_DH // 4

ROW_TILE = 512
ATT_TILE = 256
Q_TILE = 256
MXU_COLS = 256


def _cparams(*sem):
    return pltpu.CompilerParams(dimension_semantics=sem, vmem_limit_bytes=VMEM_LIMIT)


def _split_bf16(x):
    hi = x.astype(BF16)
    lo = (x - hi.astype(F32)).astype(BF16)
    return hi, lo


def _dot(a, b):
    return jnp.dot(a, b, preferred_element_type=F32)


def _dot_nt(a, b):
    return lax.dot_general(a, b, (((1,), (1,)), ((), ())), preferred_element_type=F32)


def _sigmoid(x):
    return 1.0 / (1.0 + jnp.exp(-x))


def _lane(shape):
    return lax.broadcasted_iota(jnp.int32, shape, 1)


def _adaln_kernel(c_ref, w_ref, b_ref, o_ref):
    c = c_ref[...]
    cond = c * _sigmoid(c)
    c_hi, c_lo = _split_bf16(cond)
    w_hi, w_lo = _split_bf16(w_ref[0])
    o_ref[0] = _dot(c_hi, w_hi) + _dot(c_hi, w_lo) + _dot(c_lo, w_hi) + b_ref[0]


def _adaln(c, ada_w, ada_b):
    depth, d, n = ada_w.shape
    b = c.shape[0]
    tn = 1536
    return pl.pallas_call(
        _adaln_kernel,
        out_shape=jax.ShapeDtypeStruct((depth, b, n), F32),
        grid=(depth, n // tn),
        in_specs=[pl.BlockSpec((b, d), lambda i, j: (0, 0)),
                  pl.BlockSpec((1, d, tn), lambda i, j: (i, 0, j)),
                  pl.BlockSpec((1, 1, tn), lambda i, j: (i, 0, j))],
        out_specs=pl.BlockSpec((1, b, tn), lambda i, j: (i, 0, j)),
        compiler_params=_cparams("parallel", "parallel"),
        name="adaln",
    )(c, ada_w, ada_b.reshape(depth, 1, n))


TABLE_SHIFT = 32


def _rope_table_kernel(pos_ref, f_ref, ma_ref, mb_ref, ca_ref, ua_ref, da_ref, cb_ref, ub_ref, db_ref):
    ang = pos_ref[...].astype(F32) * f_ref[...]
    cos, sin = jnp.cos(ang), jnp.sin(ang)
    for m_ref, c_ref, u_ref, d_ref, shift in ((ma_ref, ca_ref, ua_ref, da_ref, 0),
                                              (mb_ref, cb_ref, ub_ref, db_ref, TABLE_SHIFT)):
        c = pltpu.roll(cos, LANES - shift, axis=1) if shift else cos
        s = pltpu.roll(sin, LANES - shift, axis=1) if shift else sin
        c_ref[...] = jnp.where(m_ref[2:3] > 0.0, c, 1.0)
        u_ref[...] = s * m_ref[0:1]
        d_ref[...] = s * m_ref[1:2]


def _lane_freqs(rot, seg):
    half = rot // 2
    lane = np.arange(LANES)
    inv = ROPE_THETA ** (-(np.arange(half, dtype=np.float32) / np.float32(half)))
    f = np.where(lane % seg < rot, inv.astype(np.float32)[lane % half], 0.0)
    return f.reshape(1, LANES).astype(np.float32)


def _rope_masks(rot, seg):
    half = rot // 2
    lane = np.arange(LANES) % seg
    up = ((lane >= half) & (lane < rot)).astype(np.float32)
    dn = -(lane < half).astype(np.float32)
    return jnp.asarray(np.stack([up, dn, (lane < rot).astype(np.float32)]))


def _rope_tables(pos_col):
    rows = pos_col.shape[0]
    tr = min(rows, 2048)
    spec = pl.BlockSpec((tr, LANES), lambda i: (i, 0))
    vec = pl.BlockSpec((1, LANES), lambda i: (0, 0))
    msk = pl.BlockSpec((3, LANES), lambda i: (0, 0))
    shp = jax.ShapeDtypeStruct((rows, LANES), F32)
    f_a, f_b = _lane_freqs(MLA_ROPE, LANES), np.roll(_lane_freqs(NSA_ROT, NSA_DH), TABLE_SHIFT, axis=1)
    assert not np.any((f_a != 0) & (f_b != 0))
    out = pl.pallas_call(
        _rope_table_kernel,
        out_shape=(shp,) * 6,
        grid=(rows // tr,),
        in_specs=[pl.BlockSpec((tr, 1), lambda i: (i, 0)), vec, msk, msk],
        out_specs=(spec,) * 6,
        compiler_params=_cparams("parallel"),
        name="rope_tables",
    )(pos_col, jnp.asarray(f_a + f_b, F32), _rope_masks(MLA_ROPE, LANES), _rope_masks(NSA_ROT, NSA_DH))
    return out[:3], out[3:]


def _rope(y, tab, half):
    c, s_up, s_dn = tab
    up = pltpu.roll(y, half, axis=1)
    dn = pltpu.roll(y, LANES - half, axis=1)
    return y * c + up * s_up + dn * s_dn


def _rms64(x, gain):
    lo = _lane(x.shape) < 64
    x2 = x * x
    s_lo = jnp.sum(jnp.where(lo, x2, 0.0), axis=-1, keepdims=True)
    s_hi = jnp.sum(jnp.where(lo, 0.0, x2), axis=-1, keepdims=True)
    r = jnp.where(lo, lax.rsqrt(s_lo * (1.0 / 64) + EPS), lax.rsqrt(s_hi * (1.0 / 64) + EPS))
    return x * (r * gain)


def _seg_ones(seg):
    lane = np.arange(LANES) // seg
    return jnp.asarray(lane[:, None] == lane[None, :], BF16)


def _rms_seg(x, gain_root, ones, n):
    ss = _dot((x * x).astype(BF16), ones)
    return x * (lax.rsqrt(ss + n * EPS) * gain_root)


def _rms(x, gain):
    return x * lax.rsqrt(jnp.mean(x * x, axis=-1, keepdims=True) + EPS) * gain


def _norm_mod(x, gain, sc, sh):
    return _rms(x, gain) * (1.0 + sc) + sh


def _mod_spec(layer, k, nb, rows_per_batch, tm, d):
    per = rows_per_batch // tm
    return pl.BlockSpec((1, 1, d), lambda i, *_: (layer * nb + i // per, 0, k))


def _mlp_kernel(x_ref, gain_ref, sc_ref, sh_ref, g_ref, w1_ref, w2_ref, o_ref, h_scr, acc_scr):
    j = pl.program_id(1)

    @pl.when(j == 0)
    def _():
        h_scr[...] = _norm_mod(x_ref[...], gain_ref[...], sc_ref[0], sh_ref[0]).astype(BF16)
        acc_scr[...] = jnp.zeros_like(acc_scr)

    a = jnp.maximum(_dot(h_scr[...], w1_ref[0].astype(BF16)), 0.0)
    acc_scr[...] += _dot((a * a).astype(BF16), w2_ref[0].astype(BF16))

    @pl.when(j == pl.num_programs(1) - 1)
    def _():
        o_ref[...] = x_ref[...] + g_ref[0] * acc_scr[...]


def _mlp(x2, gain, mod, layer, w1, w2, nb):
    m, d = x2.shape
    ff = w1.shape[2]
    tm, tf = 1024, 1024
    s = m // nb
    return pl.pallas_call(
        _mlp_kernel,
        out_shape=jax.ShapeDtypeStruct((m, d), F32),
        grid=(m // tm, ff // tf),
        in_specs=[pl.BlockSpec((tm, d), lambda i, j: (i, 0)),
                  pl.BlockSpec((1, d), lambda i, j: (0, 0)),
                  _mod_spec(layer, 4, nb, s, tm, d),
                  _mod_spec(layer, 3, nb, s, tm, d),
                  _mod_spec(layer, 5, nb, s, tm, d),
                  pl.BlockSpec((1, d, tf), lambda i, j: (layer, 0, j)),
                  pl.BlockSpec((1, tf, d), lambda i, j: (layer, j, 0))],
        out_specs=pl.BlockSpec((tm, d), lambda i, j: (i, 0)),
        scratch_shapes=[pltpu.VMEM((tm, d), BF16), pltpu.VMEM((tm, d), F32)],
        compiler_params=_cparams("parallel", "arbitrary"),
        name="relu2_mlp",
    )(x2, gain.reshape(1, d), mod, mod, mod, w1, w2)


LOG2E = math.log2(math.e)


SCORE_BOUND = 60.0
BOUND_MARGIN = 1.02


def _needs_shift(dk, q_gain, k_gain):
    bound = dk * jnp.max(jnp.abs(q_gain)) * jnp.max(jnp.abs(k_gain)) * BOUND_MARGIN
    return jnp.logical_not(bound <= SCORE_BOUND)


def _shift_dispatch(needs_shift, attn, *args):
    return lax.cond(needs_shift, functools.partial(attn, True), functools.partial(attn, False), *args)


def _lane_tiles(x):
    return [x[:, LANES * c:LANES * (c + 1)] for c in range(x.shape[1] // LANES)]


def _flash_scratch(n, rows, t):
    return [pltpu.VMEM((n, rows, LANES), F32), pltpu.VMEM((n, rows, LANES), F32)]


def _flash(qs, k_at, v_at, first, i, t, rows_per_map, shift, scratch, window=None, masked_loop=False):
    n = len(qs)
    rows = qs[0].shape[0]
    n_diag = rows_per_map // t
    last = (i + 1) * n_diag - 1
    r = lax.broadcasted_iota(jnp.int32, (rows, t), 0)
    if rows != rows_per_map:
        r = jnp.where(r >= rows_per_map, r - rows_per_map, r)
    c_minus_r = lax.broadcasted_iota(jnp.int32, (rows, t), 1) - r

    assert masked_loop or window is None

    def visible(j):
        ahead = i * rows_per_map - j * t
        ok = c_minus_r <= ahead
        return ok if window is None else ok & (c_minus_r > ahead - window)

    def scores(h, j, ok):
        s = _dot_nt(qs[h], k_at(h, j))
        return s if ok is None else jnp.where(ok, s, NEG)

    def sweep(step, carry):
        if masked_loop:
            return lax.fori_loop(first, last + 1, lambda j, cr: step(j, cr, visible(j)), carry)
        carry = lax.fori_loop(first, i * n_diag, lambda j, cr: step(j, cr, None), carry)
        for d in range(n_diag):
            carry = step(i * n_diag + d, carry, visible(i * n_diag + d))
        return carry

    def max_step(j, ms, ok):
        out = []
        for h in range(n):
            m = ms[h]
            for s_c in _lane_tiles(scores(h, j, ok)):
                m = jnp.maximum(m, s_c)
            out.append(m)
        return tuple(out)

    if shift:
        ms = sweep(max_step, tuple(jnp.full((rows, LANES), NEG, F32) for _ in range(n)))
        ms = [jnp.broadcast_to(jnp.max(m, axis=-1, keepdims=True), (rows, LANES)) for m in ms]

    l_ref, acc_ref = scratch
    for h in range(n):
        l_ref[h] = jnp.zeros((rows, LANES), F32)
        acc_ref[h] = jnp.zeros((rows, LANES), F32)

    def sum_step(j, p_prev, ok):
        p_new = []
        for h in range(n):
            acc_ref[h] += _dot(p_prev[h], v_at(h, jnp.maximum(j - 1, 0)))
            ps = _lane_tiles(scores(h, j, ok))
            ps = [jnp.exp2(s_c - ms[h]) if shift else jnp.exp2(s_c) for s_c in ps]
            l = l_ref[h]
            for p_c in ps:
                l = l + p_c
            l_ref[h] = l
            p_new.append(jnp.concatenate(ps, axis=1).astype(BF16))
        return tuple(p_new)

    p = sweep(sum_step, tuple(jnp.zeros((rows, t), BF16) for _ in range(n)))
    return [(acc_ref[h] + _dot(p[h], v_at(h, last))) / jnp.sum(l_ref[h], axis=-1, keepdims=True)
            for h in range(n)]


def _hy_in_kernel(seq, x_ref, gain_ref, sc_ref, sh_ref, w_ref, qn_ref, kvn_ref, qg_ref, kg_ref,
                  c_ref, su_ref, sd_ref,
                  lat_ref, q_ref, ks0_ref, ks1_ref, kw_ref, pg_ref, kcmp_ref, vcmp_ref, vsw_ref):
    tm = x_ref.shape[0]
    h = _norm_mod(x_ref[...], gain_ref[...], sc_ref[0], sh_ref[0]).astype(BF16)
    tab = (c_ref[...], su_ref[...], sd_ref[...])

    def cols(a, b):
        return _dot(h, w_ref[:, a:b])

    def prep(x, g):
        return _rope(_rms64(x, g), tab, NSA_ROT // 2)

    lat_ref[:, 0:256] = _rms(cols(0, 256), qn_ref[...]).astype(BF16)
    lat_ref[:, 256:512] = _rms(cols(256, 512), kvn_ref[...]).astype(BF16)
    for ch in range(2):
        y = cols(512 + MXU_COLS * ch, 512 + MXU_COLS * (ch + 1))
        for e in range(2):
            q_ref[:, MXU_COLS * ch + LANES * e:MXU_COLS * ch + LANES * (e + 1)] = prep(
                y[:, LANES * e:LANES * (e + 1)], qg_ref[...]).astype(BF16)
    y = cols(1024, 1280)
    kslc = prep(y[:, 0:LANES], kg_ref[1])
    lane = _lane((tm, LANES))
    t_idx = (pl.program_id(0) * tm + lax.broadcasted_iota(jnp.int32, (tm, LANES), 0)) & (seq - 1)
    blk = t_idx >> 6
    ks0_ref[...] = jnp.where(lane < 64, kslc, jnp.where(lane - 64 == blk, 1.0, 0.0)).astype(BF16)
    ks1_ref[...] = jnp.where(lane >= 64, kslc, jnp.where(lane == blk, 1.0, 0.0)).astype(BF16)
    kw_ref[...] = prep(y[:, LANES:2 * LANES], kg_ref[2]).astype(BF16)
    pg_ref[...] = cols(1280, 1536)
    y = cols(1536, 1792)
    kcmp_ref[...] = y[:, 0:LANES]
    vcmp_ref[...] = y[:, LANES:2 * LANES]
    vsw_ref[...] = cols(1792, 2048).astype(BF16)


def _hy_in(x2, gain, mod, layer, w, q_norm, kv_norm, q_gain, k_gain, tab, nb):
    m, d = x2.shape
    tm = ROW_TILE
    seq = m // nb
    assert seq & (seq - 1) == 0 and seq // NSA_SLC_LEN <= 32 and seq % tm == 0
    row = lambda wd: pl.BlockSpec((tm, wd), lambda i: (i, 0))
    full = lambda a: pl.BlockSpec(a.shape, lambda i: (0,) * a.ndim)
    bf = lambda wd: jax.ShapeDtypeStruct((m, wd), BF16)
    widths = (512, 512, LANES, LANES, LANES, 2 * LANES, LANES, LANES, 2 * LANES)
    shapes = tuple(jax.ShapeDtypeStruct((m, wd), F32) if k in (5, 6, 7) else bf(wd) for k, wd in enumerate(widths))
    gain = gain.reshape(1, d)
    return pl.pallas_call(
        functools.partial(_hy_in_kernel, seq),
        out_shape=shapes,
        grid=(m // tm,),
        in_specs=[row(d), full(gain), _mod_spec(layer, 1, nb, seq, tm, d), _mod_spec(layer, 0, nb, seq, tm, d),
                  full(w), full(q_norm), full(kv_norm), full(q_gain), full(k_gain),
                  row(LANES), row(LANES), row(LANES)],
        out_specs=tuple(row(wd) for wd in widths),
        compiler_params=_cparams("parallel"),
        name="hy_in_proj",
    )(x2, gain, mod, mod, w, q_norm, kv_norm, q_gain, k_gain, *tab)


def _mla_prep_kernel(lat_ref, kpe_ref, wq_ref, wk_ref, wv_ref, qg_ref, kg_ref, c_ref, su_ref, sd_ref, ones_ref,
                     q_ref, k_ref, v_ref):
    cq, ckv = lat_ref[:, 0:256], lat_ref[:, 256:512]
    kpe = kpe_ref[...]
    tab = (c_ref[...], su_ref[...], sd_ref[...])
    v_ref[...] = _dot(ckv, wv_ref[...]).astype(BF16)

    def head(x, g):
        return _rope(_rms_seg(x, g, ones_ref[...], MLA_DK), tab, MLA_ROPE // 2)

    for ch in range(MLA_HEADS // 2):
        cs = slice(MXU_COLS * ch, MXU_COLS * (ch + 1))
        q2, k2 = _dot(cq, wq_ref[:, cs]), _dot(ckv, wk_ref[:, cs])
        for e in range(2):
            sl = slice(MXU_COLS * ch + LANES * e, MXU_COLS * ch + LANES * (e + 1))
            es = slice(LANES * e, LANES * (e + 1))
            q_ref[:, sl] = head(q2[:, es], qg_ref[...]).astype(BF16)
            k_ref[:, sl] = head(k2[:, es] + kpe, kg_ref[...]).astype(BF16)


def _mla_prep(lat, pg, wq, wk, wv, q_gain, k_gain, tab):
    m = lat.shape[0]
    tm = ROW_TILE
    ones = _seg_ones(LANES)
    full = lambda a: pl.BlockSpec(a.shape, lambda i: (0, 0))
    row = lambda w: pl.BlockSpec((tm, w), lambda i: (i, 0))
    return pl.pallas_call(
        _mla_prep_kernel,
        out_shape=(jax.ShapeDtypeStruct((m, 1024), BF16), jax.ShapeDtypeStruct((m, 1024), BF16),
                   jax.ShapeDtypeStruct((m, 512), BF16)),
        grid=(m // tm,),
        in_specs=[row(512), row(LANES), full(wq), full(wk), full(wv), full(q_gain), full(k_gain),
                  row(LANES), row(LANES), row(LANES), full(ones)],
        out_specs=(row(1024), row(1024), row(512)),
        compiler_params=_cparams("parallel"),
        name="mla_prep",
    )(lat, pg, wq, wk, wv, q_gain, k_gain, *tab, ones)


def _mla_attn_kernel(shift, q_ref, k_ref, v_ref, o_ref, *scratch):
    i = pl.program_id(1)
    t, tq = ATT_TILE, Q_TILE
    lo = _lane((tq, LANES)) < 64

    def rows(j):
        return pl.ds(pl.multiple_of(j * t, t), t)

    def tile(h):
        return slice(LANES * h, LANES * (h + 1))

    outs = _flash([q_ref[0, :, tile(h)] for h in range(MLA_HEADS)],
                  lambda h, j: k_ref[0, rows(j), tile(h)],
                  lambda h, j: v_ref[0, rows(j), tile(h // 2)], 0, i, t, tq, shift, scratch)
    for p in range(MLA_HEADS // 2):
        o_ref[0, :, tile(p)] = jnp.where(lo, outs[2 * p], outs[2 * p + 1]).astype(BF16)


def _mla_attn(shift, q, k, v):
    b, s, _ = q.shape
    t, tq = ATT_TILE, Q_TILE
    return pl.pallas_call(
        functools.partial(_mla_attn_kernel, shift),
        out_shape=jax.ShapeDtypeStruct((b, s, 512), BF16),
        grid=(b, s // tq),
        in_specs=[pl.BlockSpec((1, tq, 1024), lambda bi, i: (bi, i, 0)),
                  pl.BlockSpec((1, s, 1024), lambda bi, i: (bi, 0, 0)),
                  pl.BlockSpec((1, s, 512), lambda bi, i: (bi, 0, 0))],
        out_specs=pl.BlockSpec((1, tq, 512), lambda bi, i: (bi, i, 0)),
        scratch_shapes=_flash_scratch(MLA_HEADS, tq, t),
        compiler_params=_cparams("parallel", "arbitrary"),
        name="mla_attention",
    )(q, k, v)


def _diff_in_kernel(x_ref, gain_ref, sc_ref, sh_ref, w_ref, g_ref, c_ref, su_ref, sd_ref, q_ref, k_ref, v_ref):
    h = _norm_mod(x_ref[...], gain_ref[...], sc_ref[0], sh_ref[0]).astype(BF16)
    tab = (c_ref[...], su_ref[...], sd_ref[...])
    n = DIFF_HEADS * LANES
    for which, out in ((0, q_ref), (1, k_ref)):
        g = g_ref[which]
        for ch in range(n // MXU_COLS):
            y = _dot(h, w_ref[:, which * n + MXU_COLS * ch:which * n + MXU_COLS * (ch + 1)])
            for e in range(MXU_COLS // LANES):
                out[:, MXU_COLS * ch + LANES * e:MXU_COLS * ch + LANES * (e + 1)] = _rope(
                    _rms64(y[:, LANES * e:LANES * (e + 1)], g), tab, DIFF_ROT // 2).astype(BF16)
    for ch in range(n // MXU_COLS):
        cs = slice(MXU_COLS * ch, MXU_COLS * (ch + 1))
        v_ref[:, cs] = _dot(h, w_ref[:, 2 * n + MXU_COLS * ch:2 * n + MXU_COLS * (ch + 1)]).astype(BF16)


def _diff_in(x2, gain, mod, layer, w, gains, tab, nb):
    m, d = x2.shape
    tm = ROW_TILE
    seq = m // nb
    row = lambda wd: pl.BlockSpec((tm, wd), lambda i: (i, 0))
    full = lambda a: pl.BlockSpec(a.shape, lambda i: (0,) * a.ndim)
    shp = jax.ShapeDtypeStruct((m, DIFF_HEADS * LANES), BF16)
    gain = gain.reshape(1, d)
    return pl.pallas_call(
        _diff_in_kernel,
        out_shape=(shp, shp, shp),
        grid=(m // tm,),
        in_specs=[row(d), full(gain), _mod_spec(layer, 1, nb, seq, tm, d), _mod_spec(layer, 0, nb, seq, tm, d),
                  full(w), full(gains), row(LANES), row(LANES), row(LANES)],
        out_specs=(row(DIFF_HEADS * LANES),) * 3,
        compiler_params=_cparams("parallel"),
        name="diff_in_proj",
    )(x2, gain, mod, mod, w, gains, *tab)


def _diff_attn_kernel(lam_init, shift, q_ref, k_ref, v_ref, lam_ref, sg_ref, x_ref, g_ref, w_ref,
                      o_ref, y_ref, *scratch):
    i = pl.program_id(1)
    t = ATT_TILE
    lo = _lane((t, LANES)) < 64
    lam = lam_ref[...]
    lmb = (jnp.exp(jnp.sum(lam[0:1] * lam[1:2], axis=-1, keepdims=True))
           - jnp.exp(jnp.sum(lam[2:3] * lam[3:4], axis=-1, keepdims=True)) + lam_init)

    def rows(j):
        return pl.ds(pl.multiple_of(j * t, t), t)

    def tile(h):
        return slice(LANES * h, LANES * (h + 1))

    def both_maps(h):
        qt = q_ref[0, :, tile(h)]
        zero = jnp.zeros_like(qt)
        return jnp.concatenate([jnp.where(lo, qt, zero), jnp.where(lo, zero, qt)], axis=0)

    outs = _flash([both_maps(h) for h in range(DIFF_HEADS)],
                  lambda h, j: k_ref[0, rows(j), tile(h)],
                  lambda h, j: v_ref[0, rows(j), tile(h)], 0, i, t, t, shift, scratch)
    for h in range(DIFF_HEADS):
        o = outs[h][:t] - lmb * outs[h][t:]
        o = o * lax.rsqrt(jnp.mean(o * o, axis=-1, keepdims=True) + EPS) * sg_ref[...]
        y_ref[:, tile(h)] = (o * (1.0 - lam_init)).astype(BF16)
    o_ref[0] = x_ref[0] + g_ref[0] * _dot(y_ref[...], w_ref[...])


def _gate_spec(layer, nb, d):
    return pl.BlockSpec((1, 1, d), lambda bi, i: (layer * nb + bi, 0, 2))


def _diff_attn(lam_init, layer, shift, q, k, v, lam, sub_gain, x3, mod, w_out):
    b, s, n = q.shape
    d = x3.shape[-1]
    t = ATT_TILE
    full = pl.BlockSpec((1, s, n), lambda bi, i: (bi, 0, 0))
    return pl.pallas_call(
        functools.partial(_diff_attn_kernel, lam_init, shift),
        out_shape=jax.ShapeDtypeStruct((b, s, d), F32),
        grid=(b, s // t),
        in_specs=[pl.BlockSpec((1, t, n), lambda bi, i: (bi, i, 0)), full, full,
                  pl.BlockSpec(lam.shape, lambda bi, i: (0, 0)),
                  pl.BlockSpec((1, LANES), lambda bi, i: (0, 0)),
                  pl.BlockSpec((1, t, d), lambda bi, i: (bi, i, 0)), _gate_spec(layer, b, d),
                  pl.BlockSpec(w_out.shape, lambda bi, i: (0, 0))],
        out_specs=pl.BlockSpec((1, t, d), lambda bi, i: (bi, i, 0)),
        scratch_shapes=[pltpu.VMEM((t, n), BF16)] + _flash_scratch(DIFF_HEADS, 2 * t, t),
        compiler_params=_cparams("parallel", "arbitrary"),
        name="diff_attention",
    )(q, k, v, lam, sub_gain.reshape(1, LANES), x3, mod, w_out)


def _nsa_compress_kernel(tk_ref, tv_ref, pos_ref, w1a_ref, w1b_ref, w2_ref, kg_ref,
                         c_ref, su_ref, sd_ref, kc_ref, vc_ref):
    n_rows = tk_ref.shape[1] // NSA_CMP_STRIDE
    for j, (t_ref, out) in enumerate(((tk_ref, kc_ref), (tv_ref, vc_ref))):
        p_hi, p_lo = _split_bf16(pos_ref[j])
        w1a, w1b = w1a_ref[j], w1b_ref[j]
        bias = (_dot(p_hi[0], w1a) + _dot(p_lo[0], w1a) + _dot(p_hi[1], w1b) + _dot(p_lo[1], w1b))[0:1]
        first = jnp.zeros((n_rows, NSA_GROUPS * NSA_CMP_HIDDEN), F32)
        second = jnp.zeros((n_rows, NSA_GROUPS * NSA_CMP_HIDDEN), F32)
        for l in range(NSA_CMP_STRIDE):
            tok = t_ref[0, pl.ds(l, n_rows, stride=NSA_CMP_STRIDE), :].astype(BF16)
            first = first + _dot(tok, w1a[LANES * l:LANES * (l + 1)])
            second = second + _dot(tok, w1b[LANES * l:LANES * (l + 1)])
        hid = first + pltpu.roll(second, n_rows - 1, axis=0) + bias
        act = jax.nn.gelu(hid, approximate=True)
        cmp = _dot(act.astype(BF16), w2_ref[j])
        if j == 0:
            cmp = _rope(_rms64(cmp, kg_ref[...]), (c_ref[0], su_ref[0], sd_ref[0]), NSA_ROT // 2)
        out[0] = cmp.astype(BF16)


def _nsa_compress(tk, tv, pos, w1a, w1b, w2, k_gain0, tab):
    b, s, w = tk.shape
    nr = s // NSA_CMP_STRIDE
    full = lambda a: pl.BlockSpec(a.shape, lambda bi: (0,) * a.ndim)
    per_b = lambda a: pl.BlockSpec((1,) + a.shape[1:], lambda bi: (bi,) + (0,) * (a.ndim - 1))
    shp = jax.ShapeDtypeStruct((b, nr, LANES), BF16)
    return pl.pallas_call(
        _nsa_compress_kernel,
        out_shape=(shp, shp),
        grid=(b,),
        in_specs=[per_b(tk), per_b(tv), full(pos), full(w1a), full(w1b), full(w2), full(k_gain0),
                  per_b(tab[0]), per_b(tab[1]), per_b(tab[2])],
        out_specs=(pl.BlockSpec((1, nr, LANES), lambda bi: (bi, 0, 0)),) * 2,
        compiler_params=_cparams("parallel"),
        name="nsa_compress",
    )(tk, tv, pos, w1a, w1b, w2, k_gain0, *tab)


def _nsa_attn_kernel(shift, q_ref, kc_ref, vc_ref, ks0_ref, ks1_ref, vs_ref, kw_ref, vw_ref, gate_ref, cov_ref,
                     x_ref, g_ref, ymla_ref, wmla_ref, wnsa_ref, o_ref, y_ref, *scratch):
    i = pl.program_id(1)
    t, tq = ATT_TILE, Q_TILE
    n_blk = ks0_ref.shape[1] // NSA_SLC_LEN
    q0 = i * tq
    lane = _lane((tq, LANES))
    lo = lane < 64
    qpos = q0 + lax.broadcasted_iota(jnp.int32, (tq, LANES), 0)
    cmp_ok = NSA_CMP_STRIDE * lane + (NSA_CMP_LEN - 1) <= qpos
    gates = _sigmoid(gate_ref[0])

    def rows(j):
        return pl.ds(pl.multiple_of(j * t, t), t)

    def gate(branch, h):
        col = branch * NSA_HEADS + h
        return gates[:, col:col + 1]

    out_ref, flash_scratch = scratch[0], scratch[1:]
    kc, vc = kc_ref[0], vc_ref[0]
    qts = [q_ref[0, :, LANES * hg:LANES * (hg + 1)] for hg in range(NSA_HPG)]
    owns = [lo, jnp.logical_not(lo)]
    qs = [jnp.where(own, qt, jnp.zeros_like(qt)) for own in owns for qt in qts]

    win = _flash(qs, lambda h, j: kw_ref[0, rows(j), :], lambda h, j: vw_ref[0, rows(j), :],
                 jnp.maximum((i * tq - NSA_WINDOW) // t, 0), i, t, tq, shift, flash_scratch, window=NSA_WINDOW,
                 masked_loop=True)
    for h in range(NSA_HEADS):
        out_ref[h] = gate(2, h) * win[h]

    q_sel = []
    for g in range(NSA_GROUPS):
        own = owns[g]

        p_sum = jnp.zeros((tq, LANES), F32)
        for hg in range(NSA_HPG):
            h = g * NSA_HPG + hg
            sc = jnp.where(cmp_ok, _dot_nt(qs[h], kc), NEG)
            p = jnp.where(cmp_ok, jnp.exp2(sc - jnp.max(sc, axis=-1, keepdims=True)), 0.0)
            l = jnp.sum(p, axis=-1, keepdims=True)
            p = p * jnp.where(l > 0.0, 1.0 / l, 0.0)
            p_sum = p_sum + p
            out_ref[h] += gate(0, h) * _dot(p.astype(BF16), vc)
        p_hi, p_lo = _split_bf16(p_sum)
        imp = (_dot_nt(cov_ref[...], p_hi) + _dot_nt(cov_ref[...], p_lo))[0:n_blk]
        blk = lax.broadcasted_iota(jnp.int32, (n_blk, tq), 0)
        jt = (q0 + lax.broadcasted_iota(jnp.int32, (n_blk, tq), 1)) >> 6
        allowed = blk <= jt
        forced = allowed & ((blk == 0) | (blk >= jt - 1))
        imp = jnp.where(forced, FORCED, jnp.where(allowed, imp, NEG))
        rank = jnp.zeros((n_blk, tq), jnp.int32)
        for jp in range(n_blk):
            other = imp[jp:jp + 1, :]
            ahead = (other > imp) | ((other == imp) & (blk > jp))
            rank = rank + ahead.astype(jnp.int32)
        pen_t = jnp.where((rank < NSA_TOP_N) & allowed, 0.0, NEG)
        pen_t = jnp.concatenate([pen_t, jnp.zeros((LANES - n_blk, tq), F32)], axis=0)
        pen = jnp.transpose(pen_t)
        if g == 0:
            pen = pltpu.roll(pen, 64, axis=1)
        pen = pen.astype(BF16)
        q_sel += [jnp.where(own, qt, pen) for qt in qts]

    slc = _flash(q_sel, lambda h, j: (ks0_ref if h < NSA_HPG else ks1_ref)[0, rows(j), :],
                 lambda h, j: vs_ref[0, rows(j), :], 0, i, t, tq, shift, flash_scratch, masked_loop=True)
    for hg in range(NSA_HPG):
        h0, h1 = hg, NSA_HPG + hg
        y_ref[:, LANES * hg:LANES * (hg + 1)] = jnp.where(
            lo, out_ref[h0] + gate(1, h0) * slc[h0], out_ref[h1] + gate(1, h1) * slc[h1]).astype(BF16)
    y = _dot(ymla_ref[0], wmla_ref[...]) + _dot(y_ref[...], wnsa_ref[...])
    o_ref[0] = x_ref[0] + g_ref[0] * y


def _nsa_attn(layer, shift, q, kc, vc, kslc0, kslc1, kwin, vsw, pg, cover_t, x3, mod, y_mla, w_mla, w_nsa):
    b, s, _ = q.shape
    d = x3.shape[-1]
    t, tq = ATT_TILE, Q_TILE
    tile3 = lambda w: pl.BlockSpec((1, tq, w), lambda bi, i: (bi, i, 0))
    full2 = lambda a: pl.BlockSpec(a.shape, lambda bi, i: (0, 0))
    seq = pl.BlockSpec((1, s, LANES), lambda bi, i: (bi, 0, 0))
    seq_hi = pl.BlockSpec((1, s, LANES), lambda bi, i: (bi, 0, 1))
    cmp = pl.BlockSpec((1, LANES, LANES), lambda bi, i: (bi, 0, 0))
    return pl.pallas_call(
        functools.partial(_nsa_attn_kernel, shift),
        out_shape=jax.ShapeDtypeStruct((b, s, d), F32),
        grid=(b, s // tq),
        in_specs=[tile3(512), cmp, cmp,
                  seq, seq, seq, seq, seq_hi,
                  pl.BlockSpec((1, tq, LANES), lambda bi, i: (bi, i, 1)),
                  full2(cover_t),
                  tile3(d), _gate_spec(layer, b, d), tile3(512), full2(w_mla), full2(w_nsa)],
        out_specs=tile3(d),
        scratch_shapes=([pltpu.VMEM((tq, 512), BF16), pltpu.VMEM((NSA_HEADS, tq, LANES), F32)]
                        + _flash_scratch(NSA_HEADS, tq, t)),
        compiler_params=_cparams("parallel", "arbitrary"),
        name="nsa_attention",
    )(q, kc, vc, kslc0, kslc1, vsw, kwin, vsw, pg, cover_t, x3, mod, y_mla, w_mla, w_nsa)


def _pad_lanes(a, width):
    return jnp.pad(a, [(0, 0)] * (a.ndim - 1) + [(0, width - a.shape[-1])])


def _hy_in_weight(w):
    d = w.shape[0]
    cq, ckv, kpe, nq, nkv, gate = jnp.split(w, [256, 512, 544, 1056, 1824], axis=1)
    nq = nq.reshape(d, NSA_GROUPS, NSA_HPG, NSA_DH).transpose(0, 2, 1, 3).reshape(d, 512)
    kcmp, vcmp, kslc, vslc, kwin, vwin = jnp.split(nkv, 6, axis=1)
    return jnp.concatenate([cq, ckv, nq, kslc, kwin, _pad_lanes(kpe, LANES), _pad_lanes(gate, LANES),
                            kcmp, vcmp, vslc, vwin], axis=1).astype(BF16)


def _cover_t(seq):
    nc, ns = (seq - NSA_CMP_LEN) // NSA_CMP_STRIDE + 1, seq // NSA_SLC_LEN
    assert nc < LANES and ns <= 32 and seq % Q_TILE == 0
    c_start = np.arange(nc) * NSA_CMP_STRIDE
    c_end = c_start + NSA_CMP_LEN - 1
    j_start = np.arange(ns) * NSA_SLC_LEN
    cover = ((c_start[:, None] <= j_start[None, :] + NSA_SLC_LEN - 1) & (c_end[:, None] >= j_start[None, :]))
    out = np.zeros((LANES, LANES), np.float32)
    out[:ns, :nc] = cover.T
    return jnp.asarray(out, BF16)


def _compress_weights(cmp_pos, w1, w2):
    half = NSA_CMP_STRIDE
    assert NSA_GROUPS == 2
    w1 = w1.reshape(2, 2, half, NSA_DH, NSA_CMP_HIDDEN)
    eye = jnp.eye(NSA_GROUPS, dtype=F32)
    zero = jnp.zeros_like(w1)
    w1e = jnp.concatenate([jnp.concatenate([w1, zero], axis=-1), jnp.concatenate([zero, w1], axis=-1)], axis=-2)
    w1e = w1e.reshape(2, 2, half * LANES, NSA_GROUPS * NSA_CMP_HIDDEN)
    w2e = jnp.einsum('jnd,gh->jgnhd', w2, eye).reshape(2, NSA_GROUPS * NSA_CMP_HIDDEN, LANES)
    pos = cmp_pos.reshape(2, 2, half, 1, NSA_DH)
    pos = jnp.broadcast_to(pos, (2, 2, half, NSA_GROUPS, NSA_DH)).reshape(2, 2, 1, half * LANES)
    pos = jnp.broadcast_to(pos, (2, 2, 8, half * LANES))
    return pos, w1e[:, 0].astype(BF16), w1e[:, 1].astype(BF16), w2e.astype(BF16)


def kernel(x, c, positions, ada_w, ada_b, norm_mix, norm_mlp, mlp_w1, mlp_w2, hy_w_in, hy_w_out, mla_q_norm, mla_w_uq, mla_kv_norm, mla_w_ukv, mla_q_gain, mla_k_gain, nsa_q_gain, nsa_k_gain, nsa_cmp_pos, nsa_cmp_w1, nsa_cmp_w2, diff_w_qkv, diff_w_out, diff_q_gain, diff_k_gain, diff_lambda, diff_sub_gain):
    nb, seq, d = x.shape
    depth = ada_w.shape[0]
    m = nb * seq
    n_cmp = seq // NSA_CMP_STRIDE

    mod = _adaln(c, ada_w, ada_b).reshape(depth * nb, 1, 6 * d)
    tab_mla, tab_64 = _rope_tables(positions.reshape(m, 1))
    pos_c = jnp.pad(positions[:, NSA_CMP_LEN - 1::NSA_CMP_STRIDE], ((0, 0), (0, 1)))
    _, tab_cmp = _rope_tables(pos_c.reshape(nb * n_cmp, 1))
    tab_cmp = tuple(a.reshape(nb, n_cmp, LANES) for a in tab_cmp)
    cover_t = _cover_t(seq)
    seq3 = lambda a: a.reshape(nb, seq, -1)

    x2 = x.reshape(m, d)
    for i in range(depth):
        j = i // 2
        if i % 2 == 0:
            qg_s = nsa_q_gain[j] * (NSA_DH ** -0.5 * LOG2E)
            qg = jnp.tile(qg_s, 2).reshape(1, LANES)
            kg = jnp.tile(nsa_k_gain[j], (1, 2)).reshape(3, 1, LANES)
            lat, q_nsa, kslc0, kslc1, kwin, pg, kcmp, vcmp, vsw = _hy_in(
                x2, norm_mix[i], mod, i, _hy_in_weight(hy_w_in[j]),
                mla_q_norm[j].reshape(1, -1), mla_kv_norm[j].reshape(1, -1), qg, kg, tab_64, nb)

            wq = _pad_lanes(mla_w_uq[j].reshape(MLA_Q_RANK, MLA_HEADS, MLA_DK), LANES).reshape(MLA_Q_RANK, -1)
            wkv = mla_w_ukv[j].reshape(MLA_KV_RANK, MLA_HEADS, MLA_NOPE + MLA_V)
            wk = jnp.pad(wkv[..., :MLA_NOPE], ((0, 0), (0, 0), (MLA_ROPE, LANES - MLA_DK))).reshape(MLA_KV_RANK, -1)
            wv = wkv[..., MLA_NOPE:].reshape(MLA_KV_RANK, -1)
            q_mla, k_mla, v_mla = _mla_prep(
                lat, pg, wq.astype(BF16), wk.astype(BF16), wv.astype(BF16),
                _pad_lanes(mla_q_gain[j] * LOG2E, LANES).reshape(1, LANES),
                _pad_lanes(mla_k_gain[j] * MLA_DK ** 0.5, LANES).reshape(1, LANES),
                tab_mla)
            y_mla = _shift_dispatch(
                _needs_shift(MLA_DK, mla_q_gain[j] * (MLA_DK ** -0.5 * LOG2E), mla_k_gain[j]), _mla_attn,
                seq3(q_mla), seq3(k_mla), seq3(v_mla))

            pos_e, w1a, w1b, w2e = _compress_weights(nsa_cmp_pos[j], nsa_cmp_w1[j], nsa_cmp_w2[j])
            kc, vc = _nsa_compress(seq3(kcmp), seq3(vcmp), pos_e, w1a, w1b, w2e, kg[0], tab_cmp)
            w_out = hy_w_out[j]
            w_nsa = w_out[512:].reshape(NSA_GROUPS, NSA_HPG, NSA_DH, d).transpose(1, 0, 2, 3).reshape(512, d)
            x2 = _shift_dispatch(
                _needs_shift(NSA_DH, qg_s, nsa_k_gain[j][1:]), functools.partial(_nsa_attn, i),
                seq3(q_nsa), kc, vc, seq3(kslc0), seq3(kslc1), seq3(kwin), seq3(vsw), seq3(pg), cover_t,
                seq3(x2), mod, y_mla, w_out[:512].astype(BF16), w_nsa.astype(BF16)).reshape(m, d)
        else:
            lam_init = 0.8 - 0.6 * math.exp(-0.3 * i)
            gains = jnp.stack([jnp.tile(diff_q_gain[j] * (DIFF_DH ** -0.5 * LOG2E), 2), jnp.tile(diff_k_gain[j], 2)])
            q, k, v = _diff_in(x2, norm_mix[i], mod, i, diff_w_qkv[j].astype(BF16),
                               gains.reshape(2, 1, LANES), tab_64, nb)
            x2 = _shift_dispatch(
                _needs_shift(DIFF_DH, gains[0], gains[1]), functools.partial(_diff_attn, lam_init, i),
                seq3(q), seq3(k), seq3(v), diff_lambda[j], diff_sub_gain[j],
                seq3(x2), mod, diff_w_out[j].astype(BF16)).reshape(m, d)
        x2 = _mlp(x2, norm_mlp[i], mod, i, mlp_w1, mlp_w2, nb)
    return x2.reshape(nb, seq, d)
```
